```python
import math
import jax, jax.numpy as jnp
from jax import lax
import numpy as np

D_MODEL = 1024
BATCH = 4
SEQ = 4096
DEPTH = 1
DEC_BATCH = 8
DEC_SEQ = 2048
PAST_LEN = 128

N_META = 16
N_ATT_HEADS = 8
ATT_DH = 64
ATT_DV = 2 * ATT_DH
ATT_QK = N_ATT_HEADS * 2 * ATT_DH
ATT_V = N_ATT_HEADS * ATT_DV
Q_BLOCK = 128
NUM_BUCKETS = 32
MAX_DISTANCE = 128
SSM_HEADS = 16
SSM_HEADDIM = 64
SSM_INNER = SSM_HEADS * SSM_HEADDIM
SSM_GROUPS = 2
SSM_STATE = 64
SSM_CONV = 7
SSM_CONV_DIM = SSM_INNER + 2 * SSM_GROUPS * SSM_STATE
CHUNK = 128
D_MIX = ATT_V + SSM_INNER
D_IN_PROJ = 2 * ATT_QK + ATT_V + SSM_INNER + SSM_CONV_DIM + 2 * SSM_HEADS
D_FF = 2816
EPS = 1e-6

kernel_name = "hymba_diffattn_ssd_macaron_encoder"


def rmsnorm(x, w):
    x32 = x.astype(jnp.float32)
    y = x32 * lax.rsqrt(jnp.mean(x32 * x32, axis=-1, keepdims=True) + EPS)
    return (y * w.astype(jnp.float32)).astype(x.dtype)


def swiglu(u, wg, wu, wd):
    return (jax.nn.silu(u @ wg) * (u @ wu)) @ wd


def t5_bucket(rel):
    half = NUM_BUCKETS // 2
    max_exact = half // 2
    ret = jnp.where(rel > 0, half, 0)
    n = jnp.abs(rel)
    nf = jnp.maximum(n, 1).astype(jnp.float32)
    large = max_exact + (jnp.log(nf / max_exact) / math.log(MAX_DISTANCE / max_exact)
                         * (half - max_exact)).astype(jnp.int32)
    large = jnp.minimum(large, half - 1)
    return ret + jnp.where(n < max_exact, n, large)


def diff_attention(q, k, v, rel_bias, lq1, lk1, lq2, lk2, subln_w, layer):
    b, L, _ = q.shape
    s = L - N_META
    q = q.reshape(b, L, N_ATT_HEADS, 2, ATT_DH)
    k = k.reshape(b, L, N_ATT_HEADS, 2, ATT_DH)
    v = v.reshape(b, L, N_ATT_HEADS, ATT_DV)
    lam_init = 0.8 - 0.6 * math.exp(-0.3 * layer)
    lam = (jnp.exp(jnp.sum(lq1.astype(jnp.float32) * lk1.astype(jnp.float32)))
           - jnp.exp(jnp.sum(lq2.astype(jnp.float32) * lk2.astype(jnp.float32))) + lam_init)
    k_pos = jnp.arange(L, dtype=jnp.int32)
    scale = ATT_DH ** -0.5

    def attend(qb, q_pos):
        bias = rel_bias[t5_bucket(k_pos[None, :] - q_pos[:, None])]
        bias = jnp.transpose(bias, (2, 0, 1)).astype(jnp.float32)
        logits = jnp.einsum('bqhtd,bkhtd->bthqk', qb, k).astype(jnp.float32) * scale + bias
        p = jax.nn.softmax(logits, axis=-1)
        a = (p[:, 0] - lam * p[:, 1]).astype(v.dtype)
        return jnp.einsum('bhqk,bkhd->bqhd', a, v)

    out_meta = attend(q[:, :N_META], jnp.arange(N_META, dtype=jnp.int32))
    nblk = s // Q_BLOCK
    q_blocks = q[:, N_META:].reshape(b, nblk, Q_BLOCK, N_ATT_HEADS, 2, ATT_DH).transpose(1, 0, 2, 3, 4, 5)
    base = jnp.arange(Q_BLOCK, dtype=jnp.int32)
    out_real = lax.map(lambda a: attend(a[0], N_META + a[1] * Q_BLOCK + base),
                       (q_blocks, jnp.arange(nblk, dtype=jnp.int32)))
    out_real = out_real.transpose(1, 0, 2, 3, 4).reshape(b, s, N_ATT_HEADS, ATT_DV)
    out = jnp.concatenate([out_meta, out_real], axis=1)
    out = rmsnorm(out, subln_w) * (1.0 - lam_init)
    return out.reshape(b, L, ATT_V)


def centred_dwconv(u, w, bias):
    c = u.shape[-1]
    out = lax.conv_general_dilated(u, w[:, None, :], window_strides=(1,),
                                   padding=[(SSM_CONV // 2, SSM_CONV // 2)],
                                   dimension_numbers=('NWC', 'WIO', 'NWC'),
                                   feature_group_count=c)
    return out + bias


def ssd_chunked(x, dt, A, B, C, h0, cs):
    b, l, h, p = x.shape
    n = B.shape[-1]
    c = l // cs
    x = x.reshape(b, c, cs, h, p)
    dt = dt.reshape(b, c, cs, h)
    B = B.reshape(b, c, cs, h, n)
    C = C.reshape(b, c, cs, h, n)
    a_cum = jnp.cumsum(dt * A, axis=2)
    seg = a_cum[:, :, :, None, :] - a_cum[:, :, None, :, :]
    lower = jnp.tril(jnp.ones((cs, cs), dtype=bool))[None, None, :, :, None]
    decay = jnp.exp(jnp.where(lower, seg, -jnp.inf))
    scores = jnp.einsum('bclhn,bcshn->bclsh', C, B) * decay * dt[:, :, None, :, :]
    y_diag = jnp.einsum('bclsh,bcshp->bclhp', scores, x)
    decay_states = jnp.exp(a_cum[:, :, -1:, :] - a_cum)
    states = jnp.einsum('bclhn,bclh,bclhp->bchpn', B, decay_states * dt, x)
    chunk_decay = jnp.exp(a_cum[:, :, -1, :])

    def step(hs, inp):
        st, d = inp
        return hs * d[:, :, None, None] + st, hs

    h_last, h_prev = lax.scan(step, h0, (states.transpose(1, 0, 2, 3, 4), chunk_decay.transpose(1, 0, 2)))
    h_prev = h_prev.transpose(1, 0, 2, 3, 4)
    y_off = jnp.einsum('bclhn,bchpn,bclh->bclhp', C, h_prev, jnp.exp(a_cum))
    return (y_diag + y_off).reshape(b, l, h, p), h_last


def ssd_direction(x, dt, A, B, C, reverse):
    b, L, h, p = x.shape
    if reverse:
        x, dt, B, C = x[:, ::-1], dt[:, ::-1], B[:, ::-1], C[:, ::-1]
        first, cs1, cs2 = L - N_META, CHUNK, N_META
    else:
        first, cs1, cs2 = N_META, N_META, CHUNK
    h0 = jnp.zeros((b, h, p, B.shape[-1]), jnp.float32)
    y1, h1 = ssd_chunked(x[:, :first], dt[:, :first], A, B[:, :first], C[:, :first], h0, cs1)
    y2, _ = ssd_chunked(x[:, first:], dt[:, first:], A, B[:, first:], C[:, first:], h1, cs2)
    y = jnp.concatenate([y1, y2], axis=1)
    return y[:, ::-1] if reverse else y


def ssd_mixer(z, xbc, dt_raw, conv_w, conv_b, dt_bias_f, dt_bias_b, a_log_f, a_log_b, d_skip, norm_w):
    b, L, _ = z.shape
    xbc = jax.nn.silu(centred_dwconv(xbc, conv_w, conv_b)).astype(jnp.float32)
    xs = xbc[..., :SSM_INNER].reshape(b, L, SSM_HEADS, SSM_HEADDIM)
    hpg = SSM_HEADS // SSM_GROUPS
    Bm = jnp.repeat(xbc[..., SSM_INNER:SSM_INNER + SSM_GROUPS * SSM_STATE].reshape(b, L, SSM_GROUPS, SSM_STATE), hpg, axis=2)
    Cm = jnp.repeat(xbc[..., SSM_INNER + SSM_GROUPS * SSM_STATE:].reshape(b, L, SSM_GROUPS, SSM_STATE), hpg, axis=2)
    dt_raw = dt_raw.astype(jnp.float32)
    dt_f = jax.nn.softplus(dt_raw[..., :SSM_HEADS] + dt_bias_f.astype(jnp.float32))
    dt_b = jax.nn.softplus(dt_raw[..., SSM_HEADS:] + dt_bias_b.astype(jnp.float32))
    A_f = -jnp.exp(a_log_f.astype(jnp.float32))
    A_b = -jnp.exp(a_log_b.astype(jnp.float32))
    y = (ssd_direction(xs, dt_f, A_f, Bm, Cm, False)
         + ssd_direction(xs, dt_b, A_b, Bm, Cm, True)
         + xs * d_skip.astype(jnp.float32)[:, None])
    y = y.reshape(b, L, SSM_INNER).astype(z.dtype)
    return rmsnorm(y * jax.nn.silu(z), norm_w)


def encoder(x, meta_tokens, ffn1_norm_w, ffn1_w_gate, ffn1_w_up, ffn1_w_down, mix_norm_w, w_in,
            rel_bias, lambda_q1, lambda_k1, lambda_q2, lambda_k2, attn_subln_w, conv_w, conv_b,
            dt_bias_fwd, dt_bias_bwd, a_log_fwd, a_log_bwd, ssm_d, ssm_norm_w, w_out,
            ffn2_norm_w, ffn2_w_gate, ffn2_w_up, ffn2_w_down, final_norm_w):
    b = x.shape[0]
    meta = jnp.broadcast_to(meta_tokens[None].astype(x.dtype), (b, N_META, D_MODEL))
    h = jnp.concatenate([meta, x], axis=1)
    o1, o2, o3, o4, o5 = ATT_QK, 2 * ATT_QK, 2 * ATT_QK + ATT_V, 2 * ATT_QK + ATT_V + SSM_INNER, 2 * ATT_QK + ATT_V + SSM_INNER + SSM_CONV_DIM
    for layer in range(DEPTH):
        h = h + 0.5 * swiglu(rmsnorm(h, ffn1_norm_w[layer]), ffn1_w_gate[layer], ffn1_w_up[layer], ffn1_w_down[layer])
        u = rmsnorm(h, mix_norm_w[layer])
        proj = u @ w_in[layer]
        q, k, v, z, xbc, dt_raw = jnp.split(proj, [o1, o2, o3, o4, o5], axis=-1)
        att = diff_attention(q, k, v, rel_bias, lambda_q1[layer], lambda_k1[layer], lambda_q2[layer],
                             lambda_k2[layer], attn_subln_w[layer], layer)
        ssm = ssd_mixer(z, xbc, dt_raw, conv_w[layer], conv_b[layer], dt_bias_fwd[layer], dt_bias_bwd[layer],
                        a_log_fwd[layer], a_log_bwd[layer], ssm_d[layer], ssm_norm_w[layer])
        h = h + jnp.concatenate([att, ssm], axis=-1) @ w_out[layer]
        h = h + 0.5 * swiglu(rmsnorm(h, ffn2_norm_w[layer]), ffn2_w_gate[layer], ffn2_w_up[layer], ffn2_w_down[layer])
    h = rmsnorm(h, final_norm_w)
    return h[:, N_META:]


def setup_inputs(seed: int = 0) -> dict:
    key = jax.random.key(seed)
    ks = jax.random.split(key, 32)
    f32 = jnp.float32

    def nrm(k, shape, scale):
        return jax.random.normal(k, shape, f32) * scale

    def gain(k, shape):
        return 1.0 + 0.02 * jax.random.normal(k, shape, f32)

    def dt_bias(k):
        u = jax.random.uniform(k, (DEPTH, SSM_HEADS), f32)
        dt = jnp.exp(u * (math.log(0.1) - math.log(0.001)) + math.log(0.001))
        return dt + jnp.log(-jnp.expm1(-dt))

    def a_log(k):
        return jnp.log(jax.random.uniform(k, (DEPTH, SSM_HEADS), f32, 1.0, 16.0))

    return {
        "x_prompt": nrm(ks[0], (BATCH, SEQ, D_MODEL), 1.0),
        "x_sample": nrm(ks[1], (DEC_BATCH, DEC_SEQ, D_MODEL), 1.0),
        "meta_tokens": nrm(ks[2], (N_META, D_MODEL), 1.0),
        "ffn1_norm_w": gain(ks[3], (DEPTH, D_MODEL)),
        "ffn1_w_gate": nrm(ks[4], (DEPTH, D_MODEL, D_FF), D_MODEL ** -0.5),
        "ffn1_w_up": nrm(ks[5], (DEPTH, D_MODEL, D_FF), D_MODEL ** -0.5),
        "ffn1_w_down": nrm(ks[6], (DEPTH, D_FF, D_MODEL), D_FF ** -0.5),
        "mix_norm_w": gain(ks[7], (DEPTH, D_MODEL)),
        "w_in": nrm(ks[8], (DEPTH, D_MODEL, D_IN_PROJ), D_MODEL ** -0.5),
        "rel_bias": nrm(ks[9], (NUM_BUCKETS, N_ATT_HEADS), 0.5),
        "lambda_q1": nrm(ks[10], (DEPTH, ATT_DH), 0.1),
        "lambda_k1": nrm(ks[11], (DEPTH, ATT_DH), 0.1),
        "lambda_q2": nrm(ks[12], (DEPTH, ATT_DH), 0.1),
        "lambda_k2": nrm(ks[13], (DEPTH, ATT_DH), 0.1),
        "attn_subln_w": gain(ks[14], (DEPTH, ATT_DV)),
        "conv_w": nrm(ks[15], (DEPTH, SSM_CONV, SSM_CONV_DIM), SSM_CONV ** -0.5),
        "conv_b": nrm(ks[16], (DEPTH, SSM_CONV_DIM), 0.02),
        "dt_bias_fwd": dt_bias(ks[17]),
        "dt_bias_bwd": dt_bias(ks[18]),
        "a_log_fwd": a_log(ks[19]),
        "a_log_bwd": a_log(ks[20]),
        "ssm_d": gain(ks[21], (DEPTH, SSM_HEADS)),
        "ssm_norm_w": gain(ks[22], (DEPTH, SSM_INNER)),
        "w_out": nrm(ks[23], (DEPTH, D_MIX, D_MODEL), D_MIX ** -0.5),
        "ffn2_norm_w": gain(ks[24], (DEPTH, D_MODEL)),
        "ffn2_w_gate": nrm(ks[25], (DEPTH, D_MODEL, D_FF), D_MODEL ** -0.5),
        "ffn2_w_up": nrm(ks[26], (DEPTH, D_MODEL, D_FF), D_MODEL ** -0.5),
        "ffn2_w_down": nrm(ks[27], (DEPTH, D_FF, D_MODEL), D_FF ** -0.5),
        "final_norm_w": gain(ks[28], (D_MODEL,)),
    }


def reference(x_prompt, x_sample, meta_tokens, ffn1_norm_w, ffn1_w_gate, ffn1_w_up, ffn1_w_down, mix_norm_w,
              w_in, rel_bias, lambda_q1, lambda_k1, lambda_q2, lambda_k2, attn_subln_w, conv_w, conv_b,
              dt_bias_fwd, dt_bias_bwd, a_log_fwd, a_log_bwd, ssm_d, ssm_norm_w, w_out,
              ffn2_norm_w, ffn2_w_gate, ffn2_w_up, ffn2_w_down, final_norm_w):
    weights = (meta_tokens, ffn1_norm_w, ffn1_w_gate, ffn1_w_up, ffn1_w_down, mix_norm_w, w_in,
               rel_bias, lambda_q1, lambda_k1, lambda_q2, lambda_k2, attn_subln_w, conv_w, conv_b,
               dt_bias_fwd, dt_bias_bwd, a_log_fwd, a_log_bwd, ssm_d, ssm_norm_w, w_out,
               ffn2_norm_w, ffn2_w_gate, ffn2_w_up, ffn2_w_down, final_norm_w)
    y_prompt = encoder(x_prompt, *weights)
    y_sample = encoder(x_sample, *weights)
    return (y_prompt, y_sample)
```

```python
import functools
import math

import jax
import jax.numpy as jnp
from jax import lax
from jax.experimental import pallas as pl
from jax.experimental.pallas import tpu as pltpu

F32 = jnp.float32
BF16 = jnp.bfloat16

D_MODEL = 1024
N_META = 16
N_ATT_HEADS = 8
ATT_DH = 64
ATT_DV = 128
ATT_QK = 1024
ATT_V = 1024
NUM_BUCKETS = 32
MAX_DISTANCE = 128
SSM_HEADS = 16
SSM_HEADDIM = 64
SSM_INNER = 1024
SSM_GROUPS = 2
SSM_STATE = 64
SSM_CONV = 7
SSM_CONV_DIM = 1280
D_FF = 2816
EPS = 1e-6
LAYER = 0
LAM_INIT = 0.8 - 0.6 * math.exp(-0.3 * LAYER)
LOG2E = math.log2(math.e)
Q_SCALE = ATT_DH ** -0.5 * LOG2E
NEG_BIG = -1e30

LANES = 128
BF16_ROWS = 16
VMEM_LIMIT = 56 * 1024 * 1024

FF_TILE = 256
N_FF = D_FF // FF_TILE
DT_PAD = LANES
D_IN_PAD = 2 * ATT_QK + ATT_V + SSM_INNER + SSM_CONV_DIM + DT_PAD
T5_BAND = 91

ROW_TILE = 512
ATT_TQ = 256
ATT_TK = 512
SSD_CHUNK = 128
HALO = BF16_ROWS


def _rmsnorm(x, w):
    ms = jnp.mean(x * x, axis=-1, keepdims=True)
    return x * lax.rsqrt(ms + EPS) * w


def _resident(shape):
    nd = len(shape)
    return pl.BlockSpec(shape, lambda *_: (0,) * nd, pipeline_mode=pl.Buffered(1))


def _ffn_kernel(*refs, has_mix, has_final):
    it = iter(refs)
    h_ref = next(it)
    if has_mix:
        att_ref, ssm_ref, wo_ref = next(it), next(it), next(it)
    nw_ref, wg_ref, wu_ref, wd_ref = next(it), next(it), next(it), next(it)
    fw_ref = next(it) if has_final else None
    o_ref = next(it)

    h = h_ref[...]
    if has_mix:
        h = (h + jnp.dot(att_ref[...], wo_ref[0], preferred_element_type=F32)
             + jnp.dot(ssm_ref[...], wo_ref[1], preferred_element_type=F32))
    u = _rmsnorm(h, nw_ref[...]).astype(BF16)
    acc = jnp.zeros_like(h)
    for j in range(N_FF):
        g = jnp.dot(u, wg_ref[j], preferred_element_type=F32)
        up = jnp.dot(u, wu_ref[j], preferred_element_type=F32)
        a = (g * jax.nn.sigmoid(g) * up).astype(BF16)
        acc = acc + jnp.dot(a, wd_ref[j], preferred_element_type=F32)
    h = h + 0.5 * acc
    if has_final:
        h = _rmsnorm(h, fw_ref[...])
    o_ref[...] = h


def _ffn_call(h, norm_w, wg, wu, wd, mix=None, final_w=None):
    n = h.shape[0]
    tm = min(ROW_TILE, n)
    assert n % tm == 0
    row = lambda width: pl.BlockSpec((tm, width), lambda i: (i, 0))
    args, specs = [h], [row(D_MODEL)]
    if mix is not None:
        att, ssm, wo = mix
        args += [att, ssm, wo]
        specs += [row(ATT_V), row(SSM_INNER), _resident(wo.shape)]
    args += [norm_w, wg, wu, wd]
    specs += [_resident(norm_w.shape), _resident(wg.shape), _resident(wu.shape), _resident(wd.shape)]
    if final_w is not None:
        args.append(final_w)
        specs.append(_resident(final_w.shape))
    return pl.pallas_call(
        functools.partial(_ffn_kernel, has_mix=mix is not None, has_final=final_w is not None),
        grid=(n // tm,),
        in_specs=specs,
        out_specs=row(D_MODEL),
        out_shape=jax.ShapeDtypeStruct((n, D_MODEL), F32),
        compiler_params=pltpu.CompilerParams(dimension_semantics=("arbitrary",), vmem_limit_bytes=VMEM_LIMIT),
        name="ffn_mix" if mix is not None else "ffn",
    )(*args)


_IN_SEGS = (("q", 0, ATT_QK), ("k", ATT_QK, ATT_QK), ("v", 2 * ATT_QK, ATT_V), ("z", 2 * ATT_QK + ATT_V, SSM_INNER),
            ("xbc", 2 * ATT_QK + ATT_V + SSM_INNER, SSM_CONV_DIM), ("dt", D_IN_PAD - DT_PAD, DT_PAD))


def _inproj_kernel(h_ref, nw_ref, win_ref, q_ref, k_ref, v_ref, z_ref, xbc_ref, dt_ref):
    u = _rmsnorm(h_ref[...], nw_ref[...]).astype(BF16)
    outs = dict(q=q_ref, k=k_ref, v=v_ref, z=z_ref, xbc=xbc_ref, dt=dt_ref)
    for name, c0, width in _IN_SEGS:
        o_ref = outs[name]
        step = 512 if width % 512 == 0 else (256 if width % 256 == 0 else LANES)
        for s in range(0, width, step):
            r = jnp.dot(u, win_ref[:, c0 + s:c0 + s + step], preferred_element_type=F32)
            if name == "q":
                r = r * Q_SCALE
            o_ref[:, s:s + step] = r.astype(o_ref.dtype)


def _inproj_call(h, norm_w, win):
    n = h.shape[0]
    tm = min(ROW_TILE, n)
    assert n % tm == 0
    row = lambda width: pl.BlockSpec((tm, width), lambda i: (i, 0))
    widths = (ATT_QK, ATT_QK, ATT_V, SSM_INNER, SSM_CONV_DIM, DT_PAD)
    dtypes = (BF16, BF16, BF16, BF16, BF16, F32)
    return pl.pallas_call(
        _inproj_kernel,
        grid=(n // tm,),
        in_specs=[row(D_MODEL), _resident(norm_w.shape), _resident(win.shape)],
        out_specs=[row(w) for w in widths],
        out_shape=[jax.ShapeDtypeStruct((n, w), dt) for w, dt in zip(widths, dtypes)],
        compiler_params=pltpu.CompilerParams(dimension_semantics=("arbitrary",), vmem_limit_bytes=VMEM_LIMIT),
        name="inproj",
    )(h, norm_w, win)


def _t5_bias(rel, rb_ref, head):
    half = NUM_BUCKETS // 2
    max_exact = half // 2
    ret = jnp.where(rel > 0, half, 0)
    n = jnp.abs(rel)
    nf = jnp.maximum(n, 1).astype(F32)
    large = max_exact + (jnp.log(nf / max_exact) / math.log(MAX_DISTANCE / max_exact)
                         * (half - max_exact)).astype(jnp.int32)
    large = jnp.minimum(large, half - 1)
    bucket = ret + jnp.where(n < max_exact, n, large)
    val = jnp.zeros(rel.shape, F32)
    for jb in range(NUM_BUCKETS):
        val = jnp.where(bucket == jb, rb_ref[jb, head], val)
    return val * LOG2E


def _bias_kernel(rb_ref, tab_ref, mtab_ref, *, tq, tk):
    head = pl.program_id(0)
    r = tk // tq
    n_near = r + 2
    row = lax.broadcasted_iota(jnp.int32, (tq, tk), 0)
    col = lax.broadcasted_iota(jnp.int32, (tq, tk), 1)
    for t in range(n_near):
        tab_ref[0, t] = _t5_bias(col - row + (t - r) * tq, rb_ref, head)
    far_left = rb_ref[NUM_BUCKETS // 2 - 1, head] * LOG2E
    far_right = rb_ref[NUM_BUCKETS - 1, head] * LOG2E
    tab_ref[0, n_near] = jnp.full((tq, tk), far_left, F32)
    tab_ref[0, n_near + 1] = jnp.full((tq, tk), far_right, F32)
    mrow = lax.broadcasted_iota(jnp.int32, (tq, LANES), 0)
    mcol = lax.broadcasted_iota(jnp.int32, (tq, LANES), 1)
    valid = mcol < N_META
    mtab_ref[0, 0] = jnp.where(valid, _t5_bias(mcol - N_META - mrow, rb_ref, head), NEG_BIG)
    mtab_ref[0, 1] = jnp.where(valid, far_left, NEG_BIG)


def _bias_call(rel_bias, tq, tk):
    assert tk % tq == 0 and tq >= T5_BAND + 1
    nt = tk // tq + 4
    return pl.pallas_call(
        functools.partial(_bias_kernel, tq=tq, tk=tk),
        grid=(N_ATT_HEADS,),
        in_specs=[pl.BlockSpec(memory_space=pltpu.SMEM)],
        out_specs=[pl.BlockSpec((1, nt, tq, tk), lambda h: (h, 0, 0, 0)),
                   pl.BlockSpec((1, 2, tq, LANES), lambda h: (h, 0, 0, 0))],
        out_shape=[jax.ShapeDtypeStruct((N_ATT_HEADS, nt, tq, tk), F32),
                   jax.ShapeDtypeStruct((N_ATT_HEADS, 2, tq, LANES), F32)],
        compiler_params=pltpu.CompilerParams(dimension_semantics=("arbitrary",)),
        name="t5_bias",
    )(rel_bias)


def _attn_kernel(lam_ref, q_ref, k_ref, v_ref, km_ref, vm_ref, tab_ref, mtab_ref, sw_ref, o_ref,
                 m_scr, l_scr, acc_scr, *, tq, tk, nkc):
    qi = pl.program_id(2)
    r = tk // tq
    n_near = r + 2
    q = q_ref[...]
    lane = lax.broadcasted_iota(jnp.int32, (tq, LANES), 1)
    zero = jnp.zeros_like(q)
    q2 = jnp.concatenate([jnp.where(lane < ATT_DH, q, zero), jnp.where(lane >= ATT_DH, q, zero)], axis=0)
    nt_dims = (((1,), (1,)), ((), ()))

    m_scr[...] = jnp.full(m_scr.shape, NEG_BIG, F32)
    l_scr[...] = jnp.zeros(l_scr.shape, F32)
    acc_scr[...] = jnp.zeros(acc_scr.shape, F32)

    def online_step(s, vblk):
        width = s.shape[1]
        m_prev = m_scr[...]
        m_new = jnp.maximum(m_prev, jnp.max(s, axis=1, keepdims=True))
        alpha = jnp.exp2(m_prev - m_new)
        p = jnp.exp2(s - pltpu.repeat(m_new, width // LANES, axis=1))
        l_scr[...] = alpha * l_scr[...] + jnp.sum(p, axis=1, keepdims=True)
        acc_scr[...] = alpha * acc_scr[...] + jnp.dot(p.astype(BF16), vblk, preferred_element_type=F32)
        m_scr[...] = m_new

    def body(j, carry):
        start = pl.multiple_of(j * tk, tk)
        kc = k_ref[pl.ds(start, tk), :]
        vc = v_ref[pl.ds(start, tk), :]
        s = lax.dot_general(q2, kc, nt_dims, preferred_element_type=F32)
        du = j * r - qi
        idx = jnp.where(du <= -(r + 1), n_near, jnp.where(du >= 2, n_near + 1, du + r))
        bias = tab_ref[0, idx]
        s = (s.reshape(2, tq, tk) + bias[None]).reshape(2 * tq, tk)
        online_step(s, vc)
        return carry

    lax.fori_loop(0, nkc, body, 0)

    sm = lax.dot_general(q2, km_ref[...], nt_dims, preferred_element_type=F32)
    mb = mtab_ref[0, jnp.minimum(qi, 1)]
    sm = (sm.reshape(2, tq, LANES) + mb[None]).reshape(2 * tq, LANES)
    online_step(sm, vm_ref[...])

    o = acc_scr[...] / l_scr[...]
    lv = lam_ref[...]
    lam = (jnp.exp(jnp.sum(lv[0:1] * lv[1:2], axis=1, keepdims=True))
           - jnp.exp(jnp.sum(lv[2:3] * lv[3:4], axis=1, keepdims=True)) + LAM_INIT)
    out = o[:tq] - lam * o[tq:]
    out = _rmsnorm(out, sw_ref[...]) * (1.0 - LAM_INIT)
    o_ref[...] = out.astype(o_ref.dtype)


def _attn_call(lamv, q, k, v, km, vm, tab, mtab, subw, batch, seq, tq, tk):
    n = q.shape[0]
    assert n == batch * seq and seq % tk == 0 and seq % tq == 0
    nq = seq // tq
    nt = tab.shape[1]
    return pl.pallas_call(
        functools.partial(_attn_kernel, tq=tq, tk=tk, nkc=seq // tk),
        grid=(N_ATT_HEADS, batch, nq),
        in_specs=[
            pl.BlockSpec(lamv.shape, lambda h, b, i: (0, 0)),
            pl.BlockSpec((tq, LANES), lambda h, b, i: (b * nq + i, h)),
            pl.BlockSpec((seq, LANES), lambda h, b, i: (b, h)),
            pl.BlockSpec((seq, LANES), lambda h, b, i: (b, h)),
            pl.BlockSpec((LANES, LANES), lambda h, b, i: (0, h)),
            pl.BlockSpec((LANES, LANES), lambda h, b, i: (0, h)),
            pl.BlockSpec((1, nt, tq, tk), lambda h, b, i: (h, 0, 0, 0)),
            pl.BlockSpec((1, 2, tq, LANES), lambda h, b, i: (h, 0, 0, 0)),
            pl.BlockSpec(subw.shape, lambda h, b, i: (0, 0)),
        ],
        out_specs=pl.BlockSpec((tq, LANES), lambda h, b, i: (b * nq + i, h)),
        out_shape=jax.ShapeDtypeStruct((n, ATT_V), BF16),
        scratch_shapes=[pltpu.VMEM((2 * tq, LANES), F32)] * 3,
        compiler_params=pltpu.CompilerParams(dimension_semantics=("arbitrary",) * 3, vmem_limit_bytes=VMEM_LIMIT),
        name="diff_attn",
    )(lamv, q, k, v, km, vm, tab, mtab, subw)


def _split3(x):
    hi = x.astype(BF16)
    r1 = x - hi.astype(F32)
    mid = r1.astype(BF16)
    lo = (r1 - mid.astype(F32)).astype(BF16)
    return hi, mid, lo


def _cumsum_rows(a):
    rows = a.shape[0]
    r_i = lax.broadcasted_iota(jnp.int32, (rows, rows), 0)
    c_i = lax.broadcasted_iota(jnp.int32, (rows, rows), 1)
    tri = jnp.where(c_i <= r_i, 1.0, 0.0).astype(BF16)
    out = None
    for term in _split3(a):
        part = jnp.dot(tri, term, preferred_element_type=F32)
        out = part if out is None else out + part
    return out


def _expand_heads(w, first_lane):
    src = lax.broadcasted_iota(jnp.int32, (LANES, SSM_INNER), 0)
    dst = lax.broadcasted_iota(jnp.int32, (LANES, SSM_INNER), 1)
    sel = jnp.where(src - first_lane == dst // SSM_HEADDIM, 1.0, 0.0).astype(BF16)
    hi = w.astype(BF16)
    lo = (w - hi.astype(F32)).astype(BF16)
    return jnp.dot(hi, sel, preferred_element_type=F32) + jnp.dot(lo, sel, preferred_element_type=F32)


def _softplus(x):
    return jnp.maximum(x, 0.0) + jnp.log1p(jnp.exp(-jnp.abs(x)))


def _group_block_mask():
    row = lax.broadcasted_iota(jnp.int32, (LANES, SSM_INNER), 0)
    col = lax.broadcasted_iota(jnp.int32, (LANES, SSM_INNER), 1)
    return row // SSM_STATE == col // (SSM_INNER // SSM_GROUPS)


def _conv_silu(win_ref, cw_ref, cb_ref, rows):
    acc = jnp.broadcast_to(cb_ref[...], (rows, SSM_CONV_DIM))
    for j in range(SSM_CONV):
        acc = acc + cw_ref[j:j + 1, :] * win_ref[pl.ds(HALO - SSM_CONV // 2 + j, rows), :]
    return acc * jax.nn.sigmoid(acc)


def _ssd_kernel(z_ref, xc_ref, xl_ref, xr_ref, dt_ref, mx_ref, mdt_ref, cw_ref, cb_ref, dtb_ref, alog_ref,
                dsk_ref, nw_ref, o_ref, xs_scr, hbs_scr, hf_scr, hb_scr, win_scr, *, cs, nc):
    ph = pl.program_id(1)
    t = pl.program_id(2)
    fwd0, bwd0 = 0, SSM_HEADS
    a_row = -jnp.exp(alog_ref[...])

    def decay_terms(dt_raw):
        dt = _softplus(dt_raw + dtb_ref[...])
        a = dt * a_row
        return dt, a, _cumsum_rows(a)

    @pl.when(ph == 0)
    def _():
        cc = nc - 1 - t

        @pl.when(t == 0)
        def _():
            hb_scr[...] = jnp.zeros(hb_scr.shape, F32)

        left = jnp.where(cc == 0, mx_ref[...], xl_ref[...])
        right = jnp.where(cc == nc - 1, jnp.zeros_like(xr_ref[...]), xr_ref[...])
        win_scr[0:HALO, :] = left.astype(F32)
        win_scr[HALO:HALO + cs, :] = xc_ref[...].astype(F32)
        win_scr[HALO + cs:HALO + cs + HALO, :] = right.astype(F32)
        xbc = _conv_silu(win_scr, cw_ref, cb_ref, cs)
        xs_scr[cc] = xbc.astype(BF16)

        dt, a, cum = decay_terms(dt_ref[...])
        eb = cum - a
        w_b = jnp.exp(eb) * dt
        xw = (xbc[:, :SSM_INNER] * _expand_heads(w_b, bwd0)).astype(BF16)
        bm_t = xbc[:, SSM_INNER:SSM_INNER + LANES].T.astype(BF16)
        upd = jnp.dot(bm_t, xw, preferred_element_type=F32)
        upd = jnp.where(_group_block_mask(), upd, 0.0)
        tot = jnp.broadcast_to(jnp.exp(cum[cs - 1:cs, :]), (8, LANES))
        dec = _expand_heads(tot, bwd0)[0:1, :]
        hb = hb_scr[...]
        hbs_scr[cc] = hb.astype(BF16)
        hb_scr[...] = hb * dec + upd

    @pl.when(ph == 1)
    def _():
        cc = t

        @pl.when(t == 0)
        def _():
            win_scr[0:HALO, :] = jnp.zeros((HALO, SSM_CONV_DIM), F32)
            win_scr[HALO:2 * HALO, :] = mx_ref[...].astype(F32)
            win_scr[2 * HALO:3 * HALO, :] = xc_ref[0:HALO, :].astype(F32)
            xm = _conv_silu(win_scr, cw_ref, cb_ref, N_META)
            dtm, am, cumm = decay_terms(mdt_ref[...])
            w_m = jnp.exp(cumm[N_META - 1:N_META, :] - cumm) * dtm
            xwm = (xm[:, :SSM_INNER] * _expand_heads(w_m, fwd0)).astype(BF16)
            bmm_t = xm[:, SSM_INNER:SSM_INNER + LANES].T.astype(BF16)
            init = jnp.dot(bmm_t, xwm, preferred_element_type=F32)
            hf_scr[...] = jnp.where(_group_block_mask(), init, 0.0)

        xbc = xs_scr[cc]
        x_bf = xbc[:, :SSM_INNER]
        bm = xbc[:, SSM_INNER:SSM_INNER + LANES]
        cm = xbc[:, SSM_INNER + LANES:SSM_INNER + 2 * LANES]
        x = x_bf.astype(F32)

        dt, a, cum = decay_terms(dt_ref[...])
        eb = cum - a
        dt_t, cum_t, eb_t = dt.T, cum.T, eb.T

        lane = lax.broadcasted_iota(jnp.int32, (cs, LANES), 1)
        zc = jnp.zeros_like(cm)
        nt_dims = (((1,), (1,)), ((), ()))
        g_mats = [lax.dot_general(jnp.where(lane // SSM_STATE == g, cm, zc), bm, nt_dims,
                                  preferred_element_type=F32) for g in range(SSM_GROUPS)]

        l_i = lax.broadcasted_iota(jnp.int32, (cs, cs), 0)
        s_i = lax.broadcasted_iota(jnp.int32, (cs, cs), 1)
        lower = s_i <= l_i
        diag = s_i == l_i
        hpg = SSM_HEADS // SSM_GROUPS
        zx = jnp.zeros((cs, LANES), BF16)
        pieces = []
        for hp in range(SSM_HEADS // 2):
            w_pair = []
            for h in (2 * hp, 2 * hp + 1):
                arg_f = cum[:, fwd0 + h:fwd0 + h + 1] - cum_t[fwd0 + h:fwd0 + h + 1, :]
                arg_b = eb_t[bwd0 + h:bwd0 + h + 1, :] - eb[:, bwd0 + h:bwd0 + h + 1]
                e = jnp.exp(jnp.minimum(jnp.where(lower, arg_f, arg_b), 0.0))
                dt_f_row = dt_t[fwd0 + h:fwd0 + h + 1, :]
                dt_b_row = dt_t[bwd0 + h:bwd0 + h + 1, :]
                m = e * jnp.where(lower, dt_f_row, dt_b_row) + jnp.where(diag, dt_b_row, 0.0)
                w_pair.append((g_mats[h // hpg] * m).astype(BF16))
            xp = x_bf[:, hp * LANES:(hp + 1) * LANES]
            rhs = jnp.concatenate([jnp.where(lane < SSM_HEADDIM, xp, zx),
                                   jnp.where(lane >= SSM_HEADDIM, xp, zx)], axis=0)
            pieces.append(jnp.dot(jnp.concatenate(w_pair, axis=1), rhs, preferred_element_type=F32))
        y = jnp.concatenate(pieces, axis=1)

        hf = hf_scr[...]
        y = y + _expand_heads(jnp.exp(cum), fwd0) * jnp.dot(cm, hf.astype(BF16), preferred_element_type=F32)
        d_b = jnp.exp(cum[cs - 1:cs, :] - eb)
        y = y + _expand_heads(d_b, bwd0) * jnp.dot(cm, hbs_scr[cc], preferred_element_type=F32)
        y = y + x * dsk_ref[...]

        w_f = jnp.exp(cum[cs - 1:cs, :] - cum) * dt
        xw = (x * _expand_heads(w_f, fwd0)).astype(BF16)
        upd = jnp.dot(bm.astype(F32).T.astype(BF16), xw, preferred_element_type=F32)
        upd = jnp.where(_group_block_mask(), upd, 0.0)
        tot = jnp.broadcast_to(jnp.exp(cum[cs - 1:cs, :]), (8, LANES))
        hf_scr[...] = hf * _expand_heads(tot, fwd0)[0:1, :] + upd

        zf = z_ref[...].astype(F32)
        y = y * (zf * jax.nn.sigmoid(zf))
        o_ref[...] = _rmsnorm(y, nw_ref[...]).astype(o_ref.dtype)


def _ssd_call(z, xbc, dt, mxbc, mdt, cw, cb, dtb, alog, dskip, nw, batch, seq, cs):
    n = z.shape[0]
    assert n == batch * seq and seq % cs == 0 and cs % HALO == 0
    nc = seq // cs
    hpc = cs // HALO
    n_halo = n // HALO

    def visit(ph, t):
        return (1 - ph) * (nc - 1 - t) + ph * t

    def ph0_chunk(ph, t):
        return (1 - ph) * (nc - 1 - t)

    const2 = lambda shape: pl.BlockSpec(shape, lambda b, ph, t: (0, 0))
    return pl.pallas_call(
        functools.partial(_ssd_kernel, cs=cs, nc=nc),
        grid=(batch, 2, nc),
        in_specs=[
            pl.BlockSpec((cs, SSM_INNER), lambda b, ph, t: (b * nc + ph * t, 0)),
            pl.BlockSpec((cs, SSM_CONV_DIM), lambda b, ph, t: (b * nc + ph0_chunk(ph, t), 0)),
            pl.BlockSpec((HALO, SSM_CONV_DIM),
                         lambda b, ph, t: (jnp.maximum((b * nc + ph0_chunk(ph, t)) * hpc - 1, 0), 0)),
            pl.BlockSpec((HALO, SSM_CONV_DIM),
                         lambda b, ph, t: (jnp.minimum((b * nc + ph0_chunk(ph, t) + 1) * hpc, n_halo - 1), 0)),
            pl.BlockSpec((cs, DT_PAD), lambda b, ph, t: (b * nc + visit(ph, t), 0)),
            const2(mxbc.shape), const2(mdt.shape), const2(cw.shape), const2(cb.shape), const2(dtb.shape),
            const2(alog.shape), const2(dskip.shape), const2(nw.shape),
        ],
        out_specs=pl.BlockSpec((cs, SSM_INNER), lambda b, ph, t: (b * nc + ph * t, 0)),
        out_shape=jax.ShapeDtypeStruct((n, SSM_INNER), BF16),
        scratch_shapes=[
            pltpu.VMEM((nc, cs, SSM_CONV_DIM), BF16),
            pltpu.VMEM((nc, LANES, SSM_INNER), BF16),
            pltpu.VMEM((LANES, SSM_INNER), F32),
            pltpu.VMEM((LANES, SSM_INNER), F32),
            pltpu.VMEM((cs + 2 * HALO, SSM_CONV_DIM), F32),
        ],
        compiler_params=pltpu.CompilerParams(dimension_semantics=("arbitrary",) * 3, vmem_limit_bytes=VMEM_LIMIT),
        name="bi_ssd",
    )(z, xbc, xbc, xbc, dt, mxbc, mdt, cw, cb, dtb, alog, dskip, nw)


def _prep_weights(ffn1_norm_w, ffn1_w_gate, ffn1_w_up, ffn1_w_down, mix_norm_w, w_in, lambda_q1, lambda_k1,
                  lambda_q2, lambda_k2, attn_subln_w, conv_w, conv_b, dt_bias_fwd, dt_bias_bwd, a_log_fwd,
                  a_log_bwd, ssm_d, ssm_norm_w, w_out, ffn2_norm_w, ffn2_w_gate, ffn2_w_up, ffn2_w_down,
                  final_norm_w):
    def ffn(norm_w, wg, wu, wd):
        chunk_cols = lambda w: w[0].astype(BF16).reshape(D_MODEL, N_FF, FF_TILE).transpose(1, 0, 2)
        return (norm_w[0][None, :], chunk_cols(wg), chunk_cols(wu),
                wd[0].astype(BF16).reshape(N_FF, FF_TILE, D_MODEL))

    pad_lanes = lambda v, width: jnp.pad(v, (0, width - v.shape[0]))[None, :]
    win = jnp.pad(w_in[0].astype(BF16), ((0, 0), (0, D_IN_PAD - w_in.shape[2])))
    return dict(
        ffn1=ffn(ffn1_norm_w, ffn1_w_gate, ffn1_w_up, ffn1_w_down),
        ffn2=ffn(ffn2_norm_w, ffn2_w_gate, ffn2_w_up, ffn2_w_down),
        mix_norm=mix_norm_w[0][None, :],
        win=win,
        lamv=jnp.stack([lambda_q1[0], lambda_k1[0], lambda_q2[0], lambda_k2[0]]),
        subw=attn_subln_w[0][None, :],
        cw=jnp.pad(conv_w[0], ((0, 8 - SSM_CONV), (0, 0))),
        cb=conv_b[0][None, :],
        dtb=pad_lanes(jnp.concatenate([dt_bias_fwd[0], dt_bias_bwd[0]]), DT_PAD),
        alog=pad_lanes(jnp.concatenate([a_log_fwd[0], a_log_bwd[0]]), DT_PAD),
        dskip=jnp.repeat(ssm_d[0], SSM_HEADDIM)[None, :],
        ssm_norm=ssm_norm_w[0][None, :],
        wo=w_out[0].astype(BF16).reshape(2, ATT_V, D_MODEL),
        final=final_norm_w[None, :],
    )


def _encode(x, w, meta_proj, tab, mtab):
    batch, seq, _ = x.shape
    km, vm, mxbc, mdt = meta_proj
    h0 = x.reshape(batch * seq, D_MODEL)
    h1 = _ffn_call(h0, *w["ffn1"])
    q, k, v, z, xbc, dt = _inproj_call(h1, w["mix_norm"], w["win"])
    att = _attn_call(w["lamv"], q, k, v, km, vm, tab, mtab, w["subw"], batch, seq, ATT_TQ, ATT_TK)
    ssm = _ssd_call(z, xbc, dt, mxbc, mdt, w["cw"], w["cb"], w["dtb"], w["alog"], w["dskip"], w["ssm_norm"],
                    batch, seq, SSD_CHUNK)
    y = _ffn_call(h1, *w["ffn2"], mix=(att, ssm, w["wo"]), final_w=w["final"])
    return y.reshape(batch, seq, D_MODEL)


def kernel(x_prompt, x_sample, meta_tokens, ffn1_norm_w, ffn1_w_gate, ffn1_w_up, ffn1_w_down, mix_norm_w, w_in, rel_bias, lambda_q1, lambda_k1, lambda_q2, lambda_k2, attn_subln_w, conv_w, conv_b, dt_bias_fwd, dt_bias_bwd, a_log_fwd, a_log_bwd, ssm_d, ssm_norm_w, w_out, ffn2_norm_w, ffn2_w_gate, ffn2_w_up, ffn2_w_down, final_norm_w):
    w = _prep_weights(ffn1_norm_w, ffn1_w_gate, ffn1_w_up, ffn1_w_down, mix_norm_w, w_in, lambda_q1, lambda_k1,
                      lambda_q2, lambda_k2, attn_subln_w, conv_w, conv_b, dt_bias_fwd, dt_bias_bwd, a_log_fwd,
                      a_log_bwd, ssm_d, ssm_norm_w, w_out, ffn2_norm_w, ffn2_w_gate, ffn2_w_up, ffn2_w_down,
                      final_norm_w)
    hm = _ffn_call(meta_tokens, *w["ffn1"])
    _, km, vm, _, mxbc, mdt = _inproj_call(hm, w["mix_norm"], w["win"])
    pad_rows = lambda a: jnp.pad(a, ((0, LANES - N_META), (0, 0)))
    meta_proj = (pad_rows(km), pad_rows(vm), mxbc, mdt)
    tab, mtab = _bias_call(rel_bias, ATT_TQ, ATT_TK)
    return (_encode(x_prompt, w, meta_proj, tab, mtab), _encode(x_sample, w, meta_proj, tab, mtab))
```

```python
import functools
import math

import jax
import jax.numpy as jnp
from jax import lax
from jax.experimental import pallas as pl
from jax.experimental.pallas import tpu as pltpu

F32 = jnp.float32
BF16 = jnp.bfloat16

D_MODEL = 1024
N_META = 16
N_ATT_HEADS = 8
ATT_DH = 64
ATT_DV = 128
ATT_QK = 1024
ATT_V = 1024
NUM_BUCKETS = 32
MAX_DISTANCE = 128
SSM_HEADS = 16
SSM_HEADDIM = 64
SSM_INNER = 1024
SSM_GROUPS = 2
SSM_STATE = 64
SSM_CONV = 7
SSM_CONV_DIM = 1280
D_FF = 2816
EPS = 1e-6
LAYER = 0
LAM_INIT = 0.8 - 0.6 * math.exp(-0.3 * LAYER)
LOG2E = math.log2(math.e)
Q_SCALE = ATT_DH ** -0.5 * LOG2E
NEG_BIG = -1e30

LANES = 128
BF16_ROWS = 16
VMEM_LIMIT = 56 * 1024 * 1024

FF_TILE = 256
N_FF = D_FF // FF_TILE
DT_PAD = LANES
D_IN_PAD = 2 * ATT_QK + SSM_INNER + SSM_CONV_DIM + DT_PAD
T5_BAND = 91

ROW_TILE = 512
ATT_TQ = 256
ATT_TK = 512
SSD_CHUNK = 128
HALO = BF16_ROWS


def _rmsnorm(x, w):
    ms = jnp.mean(x * x, axis=-1, keepdims=True)
    return x * lax.rsqrt(ms + EPS) * w


def _resident(shape):
    nd = len(shape)
    return pl.BlockSpec(shape, lambda *_: (0,) * nd, pipeline_mode=pl.Buffered(1))


def _ffn_kernel(*refs, has_mix, has_final):
    it = iter(refs)
    h_ref = next(it)
    if has_mix:
        att_ref, ssm_ref, wo_ref = next(it), next(it), next(it)
    nw_ref, wg_ref, wu_ref, wd_ref = next(it), next(it), next(it), next(it)
    fw_ref = next(it) if has_final else None
    o_ref = next(it)

    h = h_ref[...]
    if has_mix:
        h = (h + jnp.dot(att_ref[...], wo_ref[0], preferred_element_type=F32)
             + jnp.dot(ssm_ref[...], wo_ref[1], preferred_element_type=F32))
    u = _rmsnorm(h, nw_ref[...]).astype(BF16)
    acc = jnp.zeros_like(h)
    for j in range(N_FF):
        g = jnp.dot(u, wg_ref[j], preferred_element_type=F32)
        up = jnp.dot(u, wu_ref[j], preferred_element_type=F32)
        a = (g * jax.nn.sigmoid(g) * up).astype(BF16)
        acc = acc + jnp.dot(a, wd_ref[j], preferred_element_type=F32)
    h = h + 0.5 * acc
    if has_final:
        h = _rmsnorm(h, fw_ref[...])
    o_ref[...] = h


def _ffn_call(h, norm_w, wg, wu, wd, mix=None, final_w=None):
    n = h.shape[0]
    tm = min(ROW_TILE, n)
    assert n % tm == 0
    row = lambda width: pl.BlockSpec((tm, width), lambda i: (i, 0))
    args, specs = [h], [row(D_MODEL)]
    if mix is not None:
        att, ssm, wo = mix
        args += [att, ssm, wo]
        specs += [row(ATT_V), row(SSM_INNER), _resident(wo.shape)]
    args += [norm_w, wg, wu, wd]
    specs += [_resident(norm_w.shape), _resident(wg.shape), _resident(wu.shape), _resident(wd.shape)]
    if final_w is not None:
        args.append(final_w)
        specs.append(_resident(final_w.shape))
    return pl.pallas_call(
        functools.partial(_ffn_kernel, has_mix=mix is not None, has_final=final_w is not None),
        grid=(n // tm,),
        in_specs=specs,
        out_specs=row(D_MODEL),
        out_shape=jax.ShapeDtypeStruct((n, D_MODEL), F32),
        compiler_params=pltpu.CompilerParams(dimension_semantics=("arbitrary",), vmem_limit_bytes=VMEM_LIMIT),
        name="ffn_mix" if mix is not None else "ffn",
    )(*args)


_IN_SEGS = (("q", 0, ATT_QK), ("k", ATT_QK, ATT_QK), ("z", 2 * ATT_QK, SSM_INNER),
            ("xbc", 2 * ATT_QK + SSM_INNER, SSM_CONV_DIM), ("dt", D_IN_PAD - DT_PAD, DT_PAD))


def _inproj_kernel(h_ref, nw_ref, win_ref, wvt_ref, q_ref, k_ref, vt_ref, z_ref, xbc_ref, dt_ref):
    u = _rmsnorm(h_ref[...], nw_ref[...]).astype(BF16)
    outs = dict(q=q_ref, k=k_ref, z=z_ref, xbc=xbc_ref, dt=dt_ref)
    for name, c0, width in _IN_SEGS:
        o_ref = outs[name]
        step = 512 if width % 512 == 0 else (256 if width % 256 == 0 else LANES)
        for s in range(0, width, step):
            r = jnp.dot(u, win_ref[:, c0 + s:c0 + s + step], preferred_element_type=F32)
            if name == "q":
                r = r * Q_SCALE
            o_ref[:, s:s + step] = r.astype(o_ref.dtype)
    nt_dims = (((1,), (1,)), ((), ()))
    for s in range(0, ATT_V, 256):
        r = lax.dot_general(wvt_ref[s:s + 256, :], u, nt_dims, preferred_element_type=F32)
        vt_ref[0, s:s + 256, :] = r.astype(vt_ref.dtype)


def _inproj_call(h, norm_w, win, wvt):
    n = h.shape[0]
    tm = min(ATT_TK, n)
    assert n % tm == 0
    row = lambda width: pl.BlockSpec((tm, width), lambda i: (i, 0))
    widths = (ATT_QK, ATT_QK, SSM_INNER, SSM_CONV_DIM, DT_PAD)
    dtypes = (BF16, BF16, BF16, BF16, F32)
    shapes = [jax.ShapeDtypeStruct((n, w), dt) for w, dt in zip(widths, dtypes)]
    specs = [row(w) for w in widths]
    shapes.insert(2, jax.ShapeDtypeStruct((n // tm, ATT_V, tm), BF16))
    specs.insert(2, pl.BlockSpec((1, ATT_V, tm), lambda i: (i, 0, 0)))
    return pl.pallas_call(
        _inproj_kernel,
        grid=(n // tm,),
        in_specs=[row(D_MODEL), _resident(norm_w.shape), _resident(win.shape), _resident(wvt.shape)],
        out_specs=specs,
        out_shape=shapes,
        compiler_params=pltpu.CompilerParams(dimension_semantics=("arbitrary",), vmem_limit_bytes=VMEM_LIMIT),
        name="inproj",
    )(h, norm_w, win, wvt)


def _t5_bias(rel, rb_ref, head):
    half = NUM_BUCKETS // 2
    max_exact = half // 2
    ret = jnp.where(rel > 0, half, 0)
    n = jnp.abs(rel)
    nf = jnp.maximum(n, 1).astype(F32)
    large = max_exact + (jnp.log(nf / max_exact) / math.log(MAX_DISTANCE / max_exact)
                         * (half - max_exact)).astype(jnp.int32)
    large = jnp.minimum(large, half - 1)
    bucket = ret + jnp.where(n < max_exact, n, large)
    val = jnp.zeros(rel.shape, F32)
    for jb in range(NUM_BUCKETS):
        val = jnp.where(bucket == jb, rb_ref[jb, head], val)
    return val * LOG2E


def _bias_kernel(rb_ref, tab_ref, mtab_ref, *, tq, tk):
    head = pl.program_id(0)
    r = tk // tq
    n_near = r + 2
    krow = lax.broadcasted_iota(jnp.int32, (tk, tq), 0)
    qcol = lax.broadcasted_iota(jnp.int32, (tk, tq), 1)
    for t in range(n_near):
        tab_ref[0, t] = _t5_bias(krow - qcol + (t - r) * tq, rb_ref, head)
    far_left = rb_ref[NUM_BUCKETS // 2 - 1, head] * LOG2E
    far_right = rb_ref[NUM_BUCKETS - 1, head] * LOG2E
    tab_ref[0, n_near] = jnp.full((tk, tq), far_left, F32)
    tab_ref[0, n_near + 1] = jnp.full((tk, tq), far_right, F32)
    mrow = lax.broadcasted_iota(jnp.int32, (LANES, tq), 0)
    mcol = lax.broadcasted_iota(jnp.int32, (LANES, tq), 1)
    valid = mrow < N_META
    mtab_ref[0, 0] = jnp.where(valid, _t5_bias(mrow - N_META - mcol, rb_ref, head), NEG_BIG)
    mtab_ref[0, 1] = jnp.where(valid, far_left, NEG_BIG)


def _bias_call(rel_bias, tq, tk):
    assert tk % tq == 0 and tq >= T5_BAND + 1
    nt = tk // tq + 4
    return pl.pallas_call(
        functools.partial(_bias_kernel, tq=tq, tk=tk),
        grid=(N_ATT_HEADS,),
        in_specs=[pl.BlockSpec(memory_space=pltpu.SMEM)],
        out_specs=[pl.BlockSpec((1, nt, tk, tq), lambda h: (h, 0, 0, 0)),
                   pl.BlockSpec((1, 2, LANES, tq), lambda h: (h, 0, 0, 0))],
        out_shape=[jax.ShapeDtypeStruct((N_ATT_HEADS, nt, tk, tq), F32),
                   jax.ShapeDtypeStruct((N_ATT_HEADS, 2, LANES, tq), F32)],
        compiler_params=pltpu.CompilerParams(dimension_semantics=("arbitrary",)),
        name="t5_bias",
    )(rel_bias)


def _attn_kernel(lam_ref, q_ref, k_ref, vt_ref, km_ref, vmt_ref, tab_ref, mtab_ref, sw_ref, o_ref,
                 sa_ref, sb_ref, mca_ref, mcb_ref, m_scr, l_scr, acc_scr, *, tq, tk, nkc):
    qi = pl.program_id(2)
    r = tk // tq
    n_near = r + 2
    q = q_ref[...]
    lane = lax.broadcasted_iota(jnp.int32, (tq, LANES), 1)
    zero = jnp.zeros_like(q)
    q2 = jnp.concatenate([jnp.where(lane < ATT_DH, q, zero), jnp.where(lane >= ATT_DH, q, zero)], axis=0)
    nt_dims = (((1,), (1,)), ((), ()))

    def add_bias(s, b):
        return jnp.concatenate([s[:, :tq] + b, s[:, tq:] + b], axis=1)

    def logits(j):
        start = pl.multiple_of(j * tk, tk)
        s = lax.dot_general(k_ref[pl.ds(start, tk), :], q2, nt_dims, preferred_element_type=F32)
        du = j * r - qi
        idx = jnp.where(du <= -(r + 1), n_near, jnp.where(du >= 2, n_near + 1, du + r))
        return add_bias(s, tab_ref[0, idx])

    def produce(j, s_ref, mc_ref):
        s = logits(j)
        s_ref[...] = s
        mc_ref[...] = jnp.max(s, axis=0, keepdims=True)

    def consume(s, m_cur, vt, first=False):
        if first:
            m_new = m_cur
        else:
            m_prev = m_scr[...]
            m_new = jnp.maximum(m_prev, m_cur)
            alpha = jnp.exp2(m_prev - m_new)
        p = jnp.exp2(s - m_new)
        p_sum = jnp.sum(p, axis=0, keepdims=True)
        pv = jnp.dot(vt, p.astype(BF16), preferred_element_type=F32)
        if first:
            l_scr[...] = p_sum
            acc_scr[...] = pv
        else:
            l_scr[...] = alpha * l_scr[...] + p_sum
            acc_scr[...] = alpha * acc_scr[...] + pv
        m_scr[...] = m_new

    sm = lax.dot_general(km_ref[...], q2, nt_dims, preferred_element_type=F32)
    sm = add_bias(sm, mtab_ref[0, jnp.minimum(qi, 1)])
    produce(0, sa_ref, mca_ref)
    consume(sm, jnp.max(sm, axis=0, keepdims=True), vmt_ref[...], first=True)

    def body(i, carry):
        produce(2 * i + 1, sb_ref, mcb_ref)
        consume(sa_ref[...], mca_ref[...], vt_ref[2 * i])
        produce(2 * i + 2, sa_ref, mca_ref)
        consume(sb_ref[...], mcb_ref[...], vt_ref[2 * i + 1])
        return carry

    lax.fori_loop(0, nkc // 2 - 1, body, 0)
    produce(nkc - 1, sb_ref, mcb_ref)
    consume(sa_ref[...], mca_ref[...], vt_ref[nkc - 2])
    consume(sb_ref[...], mcb_ref[...], vt_ref[nkc - 1])

    o = acc_scr[...] / l_scr[...]
    lv = lam_ref[...]
    lam = (jnp.exp(jnp.sum(lv[0:1] * lv[1:2], axis=1, keepdims=True))
           - jnp.exp(jnp.sum(lv[2:3] * lv[3:4], axis=1, keepdims=True)) + LAM_INIT)
    out = o[:, :tq] - lam * o[:, tq:]
    ms = jnp.mean(out * out, axis=0, keepdims=True)
    out = out * lax.rsqrt(ms + EPS) * sw_ref[...] * (1.0 - LAM_INIT)
    o_ref[...] = out.T.astype(o_ref.dtype)


def _attn_call(lamv, q, k, vt, km, vmt, tab, mtab, subw_col, batch, seq, tq, tk):
    n = q.shape[0]
    assert n == batch * seq and seq % (2 * tk) == 0 and seq % tq == 0 and vt.shape[2] == tk
    nq = seq // tq
    nkc = seq // tk
    nt = tab.shape[1]
    return pl.pallas_call(
        functools.partial(_attn_kernel, tq=tq, tk=tk, nkc=nkc),
        grid=(N_ATT_HEADS, batch, nq),
        in_specs=[
            pl.BlockSpec(lamv.shape, lambda h, b, i: (0, 0)),
            pl.BlockSpec((tq, LANES), lambda h, b, i: (b * nq + i, h)),
            pl.BlockSpec((seq, LANES), lambda h, b, i: (b, h)),
            pl.BlockSpec((nkc, ATT_DV, tk), lambda h, b, i: (b, h, 0)),
            pl.BlockSpec((LANES, LANES), lambda h, b, i: (0, h)),
            pl.BlockSpec((ATT_DV, LANES), lambda h, b, i: (h, 0)),
            pl.BlockSpec((1, nt, tk, tq), lambda h, b, i: (h, 0, 0, 0)),
            pl.BlockSpec((1, 2, LANES, tq), lambda h, b, i: (h, 0, 0, 0)),
            pl.BlockSpec(subw_col.shape, lambda h, b, i: (0, 0)),
        ],
        out_specs=pl.BlockSpec((tq, LANES), lambda h, b, i: (b * nq + i, h)),
        out_shape=jax.ShapeDtypeStruct((n, ATT_V), BF16),
        scratch_shapes=[pltpu.VMEM((tk, 2 * tq), F32)] * 2 + [pltpu.VMEM((1, 2 * tq), F32)] * 4
        + [pltpu.VMEM((ATT_DV, 2 * tq), F32)],
        compiler_params=pltpu.CompilerParams(dimension_semantics=("arbitrary",) * 3, vmem_limit_bytes=VMEM_LIMIT),
        name="diff_attn",
    )(lamv, q, k, vt, km, vmt, tab, mtab, subw_col)


def _split3(x):
    hi = x.astype(BF16)
    r1 = x - hi.astype(F32)
    mid = r1.astype(BF16)
    lo = (r1 - mid.astype(F32)).astype(BF16)
    return hi, mid, lo


def _cumsum_rows(a):
    rows = a.shape[0]
    r_i = lax.broadcasted_iota(jnp.int32, (rows, rows), 0)
    c_i = lax.broadcasted_iota(jnp.int32, (rows, rows), 1)
    tri = jnp.where(c_i <= r_i, 1.0, 0.0).astype(BF16)
    out = None
    for term in _split3(a):
        part = jnp.dot(tri, term, preferred_element_type=F32)
        out = part if out is None else out + part
    return out


def _expand_heads(w, first_lane):
    src = lax.broadcasted_iota(jnp.int32, (LANES, SSM_INNER), 0)
    dst = lax.broadcasted_iota(jnp.int32, (LANES, SSM_INNER), 1)
    sel = jnp.where(src - first_lane == dst // SSM_HEADDIM, 1.0, 0.0).astype(BF16)
    hi = w.astype(BF16)
    lo = (w - hi.astype(F32)).astype(BF16)
    return jnp.dot(hi, sel, preferred_element_type=F32) + jnp.dot(lo, sel, preferred_element_type=F32)


def _softplus(x):
    return jnp.maximum(x, 0.0) + jnp.log1p(jnp.exp(-jnp.abs(x)))


def _group_block_mask():
    row = lax.broadcasted_iota(jnp.int32, (LANES, SSM_INNER), 0)
    col = lax.broadcasted_iota(jnp.int32, (LANES, SSM_INNER), 1)
    return row // SSM_STATE == col // (SSM_INNER // SSM_GROUPS)


def _conv_silu(win_ref, cw_ref, cb_ref, rows):
    acc = jnp.broadcast_to(cb_ref[...], (rows, SSM_CONV_DIM))
    for j in range(SSM_CONV):
        acc = acc + cw_ref[j:j + 1, :] * win_ref[pl.ds(HALO - SSM_CONV // 2 + j, rows), :]
    return acc * jax.nn.sigmoid(acc)


def _ssd_kernel(z_ref, xc_ref, xl_ref, xr_ref, dt_ref, mx_ref, mdt_ref, cw_ref, cb_ref, dtb_ref, alog_ref,
                dsk_ref, nw_ref, o_ref, xs_scr, hbs_scr, hf_scr, hb_scr, win_scr, *, cs, nc):
    ph = pl.program_id(1)
    t = pl.program_id(2)
    fwd0, bwd0 = 0, SSM_HEADS
    a_row = -jnp.exp(alog_ref[...])

    def decay_terms(dt_raw):
        dt = _softplus(dt_raw + dtb_ref[...])
        a = dt * a_row
        return dt, a, _cumsum_rows(a)

    @pl.when(ph == 0)
    def _():
        cc = nc - 1 - t

        @pl.when(t == 0)
        def _():
            hb_scr[...] = jnp.zeros(hb_scr.shape, F32)

        left = jnp.where(cc == 0, mx_ref[...], xl_ref[...])
        right = jnp.where(cc == nc - 1, jnp.zeros_like(xr_ref[...]), xr_ref[...])
        win_scr[0:HALO, :] = left.astype(F32)
        win_scr[HALO:HALO + cs, :] = xc_ref[...].astype(F32)
        win_scr[HALO + cs:HALO + cs + HALO, :] = right.astype(F32)
        xbc = _conv_silu(win_scr, cw_ref, cb_ref, cs)
        xs_scr[cc] = xbc.astype(BF16)

        dt, a, cum = decay_terms(dt_ref[...])
        eb = cum - a
        w_b = jnp.exp(eb) * dt
        xw = (xbc[:, :SSM_INNER] * _expand_heads(w_b, bwd0)).astype(BF16)
        bm_t = xbc[:, SSM_INNER:SSM_INNER + LANES].T.astype(BF16)
        upd = jnp.dot(bm_t, xw, preferred_element_type=F32)
        upd = jnp.where(_group_block_mask(), upd, 0.0)
        tot = jnp.broadcast_to(jnp.exp(cum[cs - 1:cs, :]), (8, LANES))
        dec = _expand_heads(tot, bwd0)[0:1, :]
        hb = hb_scr[...]
        hbs_scr[cc] = hb.astype(BF16)
        hb_scr[...] = hb * dec + upd

    @pl.when(ph == 1)
    def _():
        cc = t

        @pl.when(t == 0)
        def _():
            win_scr[0:HALO, :] = jnp.zeros((HALO, SSM_CONV_DIM), F32)
            win_scr[HALO:2 * HALO, :] = mx_ref[...].astype(F32)
            win_scr[2 * HALO:3 * HALO, :] = xc_ref[0:HALO, :].astype(F32)
            xm = _conv_silu(win_scr, cw_ref, cb_ref, N_META)
            dtm, am, cumm = decay_terms(mdt_ref[...])
            w_m = jnp.exp(cumm[N_META - 1:N_META, :] - cumm) * dtm
            xwm = (xm[:, :SSM_INNER] * _expand_heads(w_m, fwd0)).astype(BF16)
            bmm_t = xm[:, SSM_INNER:SSM_INNER + LANES].T.astype(BF16)
            init = jnp.dot(bmm_t, xwm, preferred_element_type=F32)
            hf_scr[...] = jnp.where(_group_block_mask(), init, 0.0)

        xbc = xs_scr[cc]
        x_bf = xbc[:, :SSM_INNER]
        bm = xbc[:, SSM_INNER:SSM_INNER + LANES]
        cm = xbc[:, SSM_INNER + LANES:SSM_INNER + 2 * LANES]
        x = x_bf.astype(F32)

        dt, a, cum = decay_terms(dt_ref[...])
        eb = cum - a
        dt_t, cum_t, eb_t = dt.T, cum.T, eb.T

        lane = lax.broadcasted_iota(jnp.int32, (cs, LANES), 1)
        zc = jnp.zeros_like(cm)
        nt_dims = (((1,), (1,)), ((), ()))
        g_mats = [lax.dot_general(jnp.where(lane // SSM_STATE == g, cm, zc), bm, nt_dims,
                                  preferred_element_type=F32) for g in range(SSM_GROUPS)]

        l_i = lax.broadcasted_iota(jnp.int32, (cs, cs), 0)
        s_i = lax.broadcasted_iota(jnp.int32, (cs, cs), 1)
        lower = s_i <= l_i
        diag = s_i == l_i
        hpg = SSM_HEADS // SSM_GROUPS
        zx = jnp.zeros((cs, LANES), BF16)
        pieces = []
        for hp in range(SSM_HEADS // 2):
            w_pair = []
            for h in (2 * hp, 2 * hp + 1):
                arg_f = cum[:, fwd0 + h:fwd0 + h + 1] - cum_t[fwd0 + h:fwd0 + h + 1, :]
                arg_b = eb_t[bwd0 + h:bwd0 + h + 1, :] - eb[:, bwd0 + h:bwd0 + h + 1]
                e = jnp.exp(jnp.minimum(jnp.where(lower, arg_f, arg_b), 0.0))
                dt_f_row = dt_t[fwd0 + h:fwd0 + h + 1, :]
                dt_b_row = dt_t[bwd0 + h:bwd0 + h + 1, :]
                m = e * jnp.where(lower, dt_f_row, dt_b_row) + jnp.where(diag, dt_b_row, 0.0)
                w_pair.append((g_mats[h // hpg] * m).astype(BF16))
            xp = x_bf[:, hp * LANES:(hp + 1) * LANES]
            rhs = jnp.concatenate([jnp.where(lane < SSM_HEADDIM, xp, zx),
                                   jnp.where(lane >= SSM_HEADDIM, xp, zx)], axis=0)
            pieces.append(jnp.dot(jnp.concatenate(w_pair, axis=1), rhs, preferred_element_type=F32))
        y = jnp.concatenate(pieces, axis=1)

        hf = hf_scr[...]
        y = y + _expand_heads(jnp.exp(cum), fwd0) * jnp.dot(cm, hf.astype(BF16), preferred_element_type=F32)
        d_b = jnp.exp(cum[cs - 1:cs, :] - eb)
        y = y + _expand_heads(d_b, bwd0) * jnp.dot(cm, hbs_scr[cc], preferred_element_type=F32)
        y = y + x * dsk_ref[...]

        w_f = jnp.exp(cum[cs - 1:cs, :] - cum) * dt
        xw = (x * _expand_heads(w_f, fwd0)).astype(BF16)
        upd = jnp.dot(bm.astype(F32).T.astype(BF16), xw, preferred_element_type=F32)
        upd = jnp.where(_group_block_mask(), upd, 0.0)
        tot = jnp.broadcast_to(jnp.exp(cum[cs - 1:cs, :]), (8, LANES))
        hf_scr[...] = hf * _expand_heads(tot, fwd0)[0:1, :] + upd

        zf = z_ref[...].astype(F32)
        y = y * (zf * jax.nn.sigmoid(zf))
        o_ref[...] = _rmsnorm(y, nw_ref[...]).astype(o_ref.dtype)


def _ssd_call(z, xbc, dt, mxbc, mdt, cw, cb, dtb, alog, dskip, nw, batch, seq, cs):
    n = z.shape[0]
    assert n == batch * seq and seq % cs == 0 and cs % HALO == 0
    nc = seq // cs
    hpc = cs // HALO
    n_halo = n // HALO

    def visit(ph, t):
        return (1 - ph) * (nc - 1 - t) + ph * t

    def ph0_chunk(ph, t):
        return (1 - ph) * (nc - 1 - t)

    const2 = lambda shape: pl.BlockSpec(shape, lambda b, ph, t: (0, 0))
    return pl.pallas_call(
        functools.partial(_ssd_kernel, cs=cs, nc=nc),
        grid=(batch, 2, nc),
        in_specs=[
            pl.BlockSpec((cs, SSM_INNER), lambda b, ph, t: (b * nc + ph * t, 0)),
            pl.BlockSpec((cs, SSM_CONV_DIM), lambda b, ph, t: (b * nc + ph0_chunk(ph, t), 0)),
            pl.BlockSpec((HALO, SSM_CONV_DIM),
                         lambda b, ph, t: (jnp.maximum((b * nc + ph0_chunk(ph, t)) * hpc - 1, 0), 0)),
            pl.BlockSpec((HALO, SSM_CONV_DIM),
                         lambda b, ph, t: (jnp.minimum((b * nc + ph0_chunk(ph, t) + 1) * hpc, n_halo - 1), 0)),
            pl.BlockSpec((cs, DT_PAD), lambda b, ph, t: (b * nc + visit(ph, t), 0)),
            const2(mxbc.shape), const2(mdt.shape), const2(cw.shape), const2(cb.shape), const2(dtb.shape),
            const2(alog.shape), const2(dskip.shape), const2(nw.shape),
        ],
        out_specs=pl.BlockSpec((cs, SSM_INNER), lambda b, ph, t: (b * nc + ph * t, 0)),
        out_shape=jax.ShapeDtypeStruct((n, SSM_INNER), BF16),
        scratch_shapes=[
            pltpu.VMEM((nc, cs, SSM_CONV_DIM), BF16),
            pltpu.VMEM((nc, LANES, SSM_INNER), BF16),
            pltpu.VMEM((LANES, SSM_INNER), F32),
            pltpu.VMEM((LANES, SSM_INNER), F32),
            pltpu.VMEM((cs + 2 * HALO, SSM_CONV_DIM), F32),
        ],
        compiler_params=pltpu.CompilerParams(dimension_semantics=("arbitrary",) * 3, vmem_limit_bytes=VMEM_LIMIT),
        name="bi_ssd",
    )(z, xbc, xbc, xbc, dt, mxbc, mdt, cw, cb, dtb, alog, dskip, nw)


def _prep_weights(ffn1_norm_w, ffn1_w_gate, ffn1_w_up, ffn1_w_down, mix_norm_w, w_in, lambda_q1, lambda_k1,
                  lambda_q2, lambda_k2, attn_subln_w, conv_w, conv_b, dt_bias_fwd, dt_bias_bwd, a_log_fwd,
                  a_log_bwd, ssm_d, ssm_norm_w, w_out, ffn2_norm_w, ffn2_w_gate, ffn2_w_up, ffn2_w_down,
                  final_norm_w):
    def ffn(norm_w, wg, wu, wd):
        chunk_cols = lambda w: w[0].astype(BF16).reshape(D_MODEL, N_FF, FF_TILE).transpose(1, 0, 2)
        return (norm_w[0][None, :], chunk_cols(wg), chunk_cols(wu),
                wd[0].astype(BF16).reshape(N_FF, FF_TILE, D_MODEL))

    pad_lanes = lambda v, width: jnp.pad(v, (0, width - v.shape[0]))[None, :]
    o_v, o_z = 2 * ATT_QK, 2 * ATT_QK + ATT_V
    wi = w_in[0].astype(BF16)
    win = jnp.pad(jnp.concatenate([wi[:, :o_v], wi[:, o_z:]], axis=1),
                  ((0, 0), (0, D_IN_PAD - (w_in.shape[2] - ATT_V))))
    return dict(
        ffn1=ffn(ffn1_norm_w, ffn1_w_gate, ffn1_w_up, ffn1_w_down),
        ffn2=ffn(ffn2_norm_w, ffn2_w_gate, ffn2_w_up, ffn2_w_down),
        mix_norm=mix_norm_w[0][None, :],
        win=win,
        wvt=wi[:, o_v:o_z].T,
        lamv=jnp.stack([lambda_q1[0], lambda_k1[0], lambda_q2[0], lambda_k2[0]]),
        subw_col=attn_subln_w[0][:, None],
        cw=jnp.pad(conv_w[0], ((0, 8 - SSM_CONV), (0, 0))),
        cb=conv_b[0][None, :],
        dtb=pad_lanes(jnp.concatenate([dt_bias_fwd[0], dt_bias_bwd[0]]), DT_PAD),
        alog=pad_lanes(jnp.concatenate([a_log_fwd[0], a_log_bwd[0]]), DT_PAD),
        dskip=jnp.repeat(ssm_d[0], SSM_HEADDIM)[None, :],
        ssm_norm=ssm_norm_w[0][None, :],
        wo=w_out[0].astype(BF16).reshape(2, ATT_V, D_MODEL),
        final=final_norm_w[None, :],
    )


def _encode(x, w, meta_proj, tab, mtab):
    batch, seq, _ = x.shape
    km, vmt, mxbc, mdt = meta_proj
    h0 = x.reshape(batch * seq, D_MODEL)
    h1 = _ffn_call(h0, *w["ffn1"])
    q, k, vt, z, xbc, dt = _inproj_call(h1, w["mix_norm"], w["win"], w["wvt"])
    att = _attn_call(w["lamv"], q, k, vt, km, vmt, tab, mtab, w["subw_col"], batch, seq, ATT_TQ, ATT_TK)
    ssm = _ssd_call(z, xbc, dt, mxbc, mdt, w["cw"], w["cb"], w["dtb"], w["alog"], w["dskip"], w["ssm_norm"],
                    batch, seq, SSD_CHUNK)
    y = _ffn_call(h1, *w["ffn2"], mix=(att, ssm, w["wo"]), final_w=w["final"])
    return y.reshape(batch, seq, D_MODEL)


def kernel(x_prompt, x_sample, meta_tokens, ffn1_norm_w, ffn1_w_gate, ffn1_w_up, ffn1_w_down, mix_norm_w, w_in, rel_bias, lambda_q1, lambda_k1, lambda_q2, lambda_k2, attn_subln_w, conv_w, conv_b, dt_bias_fwd, dt_bias_bwd, a_log_fwd, a_log_bwd, ssm_d, ssm_norm_w, w_out, ffn2_norm_w, ffn2_w_gate, ffn2_w_up, ffn2_w_down, final_norm_w):
    w = _prep_weights(ffn1_norm_w, ffn1_w_gate, ffn1_w_up, ffn1_w_down, mix_norm_w, w_in, lambda_q1, lambda_k1,
                      lambda_q2, lambda_k2, attn_subln_w, conv_w, conv_b, dt_bias_fwd, dt_bias_bwd, a_log_fwd,
                      a_log_bwd, ssm_d, ssm_norm_w, w_out, ffn2_norm_w, ffn2_w_gate, ffn2_w_up, ffn2_w_down,
                      final_norm_w)
    hm = _ffn_call(meta_tokens, *w["ffn1"])
    _, km, vmt, _, mxbc, mdt = _inproj_call(hm, w["mix_norm"], w["win"], w["wvt"])
    km = jnp.pad(km, ((0, LANES - N_META), (0, 0)))
    vmt = jnp.pad(vmt[0], ((0, 0), (0, LANES - N_META)))
    meta_proj = (km, vmt, mxbc, mdt)
    tab, mtab = _bias_call(rel_bias, ATT_TQ, ATT_TK)
    return (_encode(x_prompt, w, meta_proj, tab, mtab), _encode(x_sample, w, meta_proj, tab, mtab))
```

```python
import functools
import math

import jax
import jax.numpy as jnp
from jax import lax
from jax.experimental import pallas as pl
from jax.experimental.pallas import tpu as pltpu

F32 = jnp.float32
BF16 = jnp.bfloat16

D_MODEL = 1024
N_META = 16
N_ATT_HEADS = 8
ATT_DH = 64
ATT_DV = 128
ATT_QK = 1024
ATT_V = 1024
NUM_BUCKETS = 32
MAX_DISTANCE = 128
SSM_HEADS = 16
SSM_HEADDIM = 64
SSM_INNER = 1024
SSM_GROUPS = 2
SSM_STATE = 64
SSM_CONV = 7
SSM_CONV_DIM = 1280
D_FF = 2816
EPS = 1e-6
LAYER = 0
LAM_INIT = 0.8 - 0.6 * math.exp(-0.3 * LAYER)
LOG2E = math.log2(math.e)
Q_SCALE = ATT_DH ** -0.5 * LOG2E
NEG_BIG = -1e30

LANES = 128
BF16_ROWS = 16
VMEM_LIMIT = 56 * 1024 * 1024

FF_TILE = 256
N_FF = D_FF // FF_TILE
DT_PAD = LANES
D_IN_PAD = 2 * ATT_QK + SSM_INNER + SSM_CONV_DIM + DT_PAD
T5_BAND = 91

ROW_TILE = 512
ATT_TQ = 512
ATT_TK = 512
SSD_CHUNK = 128
HALO = BF16_ROWS
VT_ROWS = ATT_DV + BF16_ROWS


def _rmsnorm(x, w):
    ms = jnp.mean(x * x, axis=-1, keepdims=True)
    return x * lax.rsqrt(ms + EPS) * w


def _resident(shape):
    nd = len(shape)
    return pl.BlockSpec(shape, lambda *_: (0,) * nd, pipeline_mode=pl.Buffered(1))


def _ffn_kernel(*refs, has_mix, has_final):
    it = iter(refs)
    h_ref = next(it)
    if has_mix:
        att_ref, ssm_ref, wo_ref = next(it), next(it), next(it)
    nw_ref, wg_ref, wu_ref, wd_ref = next(it), next(it), next(it), next(it)
    fw_ref = next(it) if has_final else None
    o_ref = next(it)

    h = h_ref[...]
    if has_mix:
        h = (h + jnp.dot(att_ref[...], wo_ref[0], preferred_element_type=F32)
             + jnp.dot(ssm_ref[...], wo_ref[1], preferred_element_type=F32))
    u = _rmsnorm(h, nw_ref[...]).astype(BF16)
    acc = jnp.zeros_like(h)
    for j in range(N_FF):
        g = jnp.dot(u, wg_ref[j], preferred_element_type=F32)
        up = jnp.dot(u, wu_ref[j], preferred_element_type=F32)
        a = (g * jax.nn.sigmoid(g) * up).astype(BF16)
        acc = acc + jnp.dot(a, wd_ref[j], preferred_element_type=F32)
    h = h + 0.5 * acc
    if has_final:
        h = _rmsnorm(h, fw_ref[...])
    o_ref[...] = h


def _ffn_call(h, norm_w, wg, wu, wd, mix=None, final_w=None):
    n = h.shape[0]
    tm = min(ROW_TILE, n)
    assert n % tm == 0
    row = lambda width: pl.BlockSpec((tm, width), lambda i: (i, 0))
    args, specs = [h], [row(D_MODEL)]
    if mix is not None:
        att, ssm, wo = mix
        args += [att, ssm, wo]
        specs += [row(ATT_V), row(SSM_INNER), _resident(wo.shape)]
    args += [norm_w, wg, wu, wd]
    specs += [_resident(norm_w.shape), _resident(wg.shape), _resident(wu.shape), _resident(wd.shape)]
    if final_w is not None:
        args.append(final_w)
        specs.append(_resident(final_w.shape))
    return pl.pallas_call(
        functools.partial(_ffn_kernel, has_mix=mix is not None, has_final=final_w is not None),
        grid=(n // tm,),
        in_specs=specs,
        out_specs=row(D_MODEL),
        out_shape=jax.ShapeDtypeStruct((n, D_MODEL), F32),
        compiler_params=pltpu.CompilerParams(dimension_semantics=("arbitrary",), vmem_limit_bytes=VMEM_LIMIT),
        name="ffn_mix" if mix is not None else "ffn",
    )(*args)


_IN_SEGS = (("q", 0, ATT_QK), ("k", ATT_QK, ATT_QK), ("z", 2 * ATT_QK, SSM_INNER),
            ("xbc", 2 * ATT_QK + SSM_INNER, SSM_CONV_DIM), ("dt", D_IN_PAD - DT_PAD, DT_PAD))


def _inproj_kernel(h_ref, nw_ref, win_ref, wvt_ref, q_ref, k_ref, vt_ref, z_ref, xbc_ref, dt_ref):
    u = _rmsnorm(h_ref[...], nw_ref[...]).astype(BF16)
    outs = dict(q=q_ref, k=k_ref, z=z_ref, xbc=xbc_ref, dt=dt_ref)
    for name, c0, width in _IN_SEGS:
        o_ref = outs[name]
        step = 512 if width % 512 == 0 else (256 if width % 256 == 0 else LANES)
        for s in range(0, width, step):
            r = jnp.dot(u, win_ref[:, c0 + s:c0 + s + step], preferred_element_type=F32)
            if name == "q":
                r = r * Q_SCALE
            o_ref[:, s:s + step] = r.astype(o_ref.dtype)
    nt_dims = (((1,), (1,)), ((), ()))
    ones = jnp.ones((VT_ROWS - ATT_DV, u.shape[0]), vt_ref.dtype)
    for s in range(0, ATT_V, 256):
        r = lax.dot_general(wvt_ref[s:s + 256, :], u, nt_dims, preferred_element_type=F32).astype(vt_ref.dtype)
        for hh in range(256 // ATT_DV):
            head = s // ATT_DV + hh
            vt_ref[0, head * VT_ROWS:head * VT_ROWS + ATT_DV, :] = r[hh * ATT_DV:(hh + 1) * ATT_DV]
            vt_ref[0, head * VT_ROWS + ATT_DV:(head + 1) * VT_ROWS, :] = ones


def _inproj_call(h, norm_w, win, wvt):
    n = h.shape[0]
    tm = min(ATT_TK, n)
    assert n % tm == 0
    row = lambda width: pl.BlockSpec((tm, width), lambda i: (i, 0))
    widths = (ATT_QK, ATT_QK, SSM_INNER, SSM_CONV_DIM, DT_PAD)
    dtypes = (BF16, BF16, BF16, BF16, F32)
    shapes = [jax.ShapeDtypeStruct((n, w), dt) for w, dt in zip(widths, dtypes)]
    specs = [row(w) for w in widths]
    shapes.insert(2, jax.ShapeDtypeStruct((n // tm, N_ATT_HEADS * VT_ROWS, tm), BF16))
    specs.insert(2, pl.BlockSpec((1, N_ATT_HEADS * VT_ROWS, tm), lambda i: (i, 0, 0)))
    return pl.pallas_call(
        _inproj_kernel,
        grid=(n // tm,),
        in_specs=[row(D_MODEL), _resident(norm_w.shape), _resident(win.shape), _resident(wvt.shape)],
        out_specs=specs,
        out_shape=shapes,
        compiler_params=pltpu.CompilerParams(dimension_semantics=("arbitrary",), vmem_limit_bytes=VMEM_LIMIT),
        name="inproj",
    )(h, norm_w, win, wvt)


def _t5_bias(rel, rb_ref, head):
    half = NUM_BUCKETS // 2
    max_exact = half // 2
    ret = jnp.where(rel > 0, half, 0)
    n = jnp.abs(rel)
    nf = jnp.maximum(n, 1).astype(F32)
    large = max_exact + (jnp.log(nf / max_exact) / math.log(MAX_DISTANCE / max_exact)
                         * (half - max_exact)).astype(jnp.int32)
    large = jnp.minimum(large, half - 1)
    bucket = ret + jnp.where(n < max_exact, n, large)
    val = jnp.zeros(rel.shape, F32)
    for jb in range(NUM_BUCKETS):
        val = jnp.where(bucket == jb, rb_ref[jb, head], val)
    return val * LOG2E


def _bias_kernel(rb_ref, tab_ref, mtab_ref, *, tq, tk):
    head = pl.program_id(0)
    r = tk // tq
    n_near = r + 2
    krow = lax.broadcasted_iota(jnp.int32, (tk, tq), 0)
    qcol = lax.broadcasted_iota(jnp.int32, (tk, tq), 1)
    for t in range(n_near):
        tab_ref[0, t] = _t5_bias(krow - qcol + (t - r) * tq, rb_ref, head)
    far_left = rb_ref[NUM_BUCKETS // 2 - 1, head] * LOG2E
    far_right = rb_ref[NUM_BUCKETS - 1, head] * LOG2E
    tab_ref[0, n_near] = jnp.full((tk, tq), far_left, F32)
    tab_ref[0, n_near + 1] = jnp.full((tk, tq), far_right, F32)
    mrow = lax.broadcasted_iota(jnp.int32, (N_META, tq), 0)
    mcol = lax.broadcasted_iota(jnp.int32, (N_META, tq), 1)
    mtab_ref[0, 0] = _t5_bias(mrow - N_META - mcol, rb_ref, head)
    mtab_ref[0, 1] = jnp.full((N_META, tq), far_left, F32)


def _bias_call(rel_bias, tq, tk):
    assert tk % tq == 0 and tq >= T5_BAND + 1
    nt = tk // tq + 4
    return pl.pallas_call(
        functools.partial(_bias_kernel, tq=tq, tk=tk),
        grid=(N_ATT_HEADS,),
        in_specs=[pl.BlockSpec(memory_space=pltpu.SMEM)],
        out_specs=[pl.BlockSpec((1, nt, tk, tq), lambda h: (h, 0, 0, 0)),
                   pl.BlockSpec((1, 2, N_META, tq), lambda h: (h, 0, 0, 0))],
        out_shape=[jax.ShapeDtypeStruct((N_ATT_HEADS, nt, tk, tq), F32),
                   jax.ShapeDtypeStruct((N_ATT_HEADS, 2, N_META, tq), F32)],
        compiler_params=pltpu.CompilerParams(dimension_semantics=("arbitrary",)),
        name="t5_bias",
    )(rel_bias)


def _attn_kernel(lam_ref, q_ref, k_ref, vt_ref, km_ref, vmt_ref, tab_ref, mtab_ref, sw_ref, o_ref,
                 sa_ref, sb_ref, mca_ref, mcb_ref, m_scr, acc_scr, *, tq, tk, nkc):
    qi = pl.program_id(2)
    r = tk // tq
    n_near = r + 2
    q = q_ref[...]
    lane = lax.broadcasted_iota(jnp.int32, (tq, LANES), 1)
    zero = jnp.zeros_like(q)
    q2 = jnp.concatenate([jnp.where(lane < ATT_DH, q, zero), jnp.where(lane >= ATT_DH, q, zero)], axis=0)
    nt_dims = (((1,), (1,)), ((), ()))

    def add_bias(s, b):
        return jnp.concatenate([s[:, :tq] + b, s[:, tq:] + b], axis=1)

    def logits(j):
        start = pl.multiple_of(j * tk, tk)
        s = lax.dot_general(k_ref[pl.ds(start, tk), :], q2, nt_dims, preferred_element_type=F32)
        du = j * r - qi
        idx = jnp.where(du <= -(r + 1), n_near, jnp.where(du >= 2, n_near + 1, du + r))
        return add_bias(s, tab_ref[0, idx])

    def produce(j, s_ref, mc_ref):
        s = logits(j)
        s_ref[...] = s
        mc_ref[...] = jnp.max(s, axis=0, keepdims=True)

    def consume(s, m_cur, vt, first=False):
        if first:
            m_new = m_cur
        else:
            m_prev = m_scr[...]
            m_new = jnp.maximum(m_prev, m_cur)
            alpha = jnp.exp2(m_prev - m_new)
        p = jnp.exp2(s - m_new).astype(BF16)
        pv = jnp.dot(vt, p, preferred_element_type=F32)
        acc_scr[...] = pv if first else alpha * acc_scr[...] + pv
        m_scr[...] = m_new

    sm = lax.dot_general(km_ref[...], q2, nt_dims, preferred_element_type=F32)
    sm = add_bias(sm, mtab_ref[0, jnp.minimum(qi, 1)])
    produce(0, sa_ref, mca_ref)
    consume(sm, jnp.max(sm, axis=0, keepdims=True), vmt_ref[...], first=True)

    def body(i, carry):
        produce(2 * i + 1, sb_ref, mcb_ref)
        consume(sa_ref[...], mca_ref[...], vt_ref[2 * i])
        produce(2 * i + 2, sa_ref, mca_ref)
        consume(sb_ref[...], mcb_ref[...], vt_ref[2 * i + 1])
        return carry

    lax.fori_loop(0, nkc // 2 - 1, body, 0)
    produce(nkc - 1, sb_ref, mcb_ref)
    consume(sa_ref[...], mca_ref[...], vt_ref[nkc - 2])
    consume(sb_ref[...], mcb_ref[...], vt_ref[nkc - 1])

    acc = acc_scr[...]
    o = acc[:ATT_DV] / acc[ATT_DV:ATT_DV + 1]
    lv = lam_ref[...]
    lam = (jnp.exp(jnp.sum(lv[0:1] * lv[1:2], axis=1, keepdims=True))
           - jnp.exp(jnp.sum(lv[2:3] * lv[3:4], axis=1, keepdims=True)) + LAM_INIT)
    out = o[:, :tq] - lam * o[:, tq:]
    ms = jnp.mean(out * out, axis=0, keepdims=True)
    out = out * lax.rsqrt(ms + EPS) * sw_ref[...] * (1.0 - LAM_INIT)
    o_ref[...] = out.T.astype(o_ref.dtype)


def _attn_call(lamv, q, k, vt, km, vmt, tab, mtab, subw_col, batch, seq, tq, tk):
    n = q.shape[0]
    assert n == batch * seq and seq % (2 * tk) == 0 and seq % tq == 0 and vt.shape[2] == tk
    nq = seq // tq
    nkc = seq // tk
    nt = tab.shape[1]
    return pl.pallas_call(
        functools.partial(_attn_kernel, tq=tq, tk=tk, nkc=nkc),
        grid=(N_ATT_HEADS, batch, nq),
        in_specs=[
            pl.BlockSpec(lamv.shape, lambda h, b, i: (0, 0)),
            pl.BlockSpec((tq, LANES), lambda h, b, i: (b * nq + i, h)),
            pl.BlockSpec((seq, LANES), lambda h, b, i: (b, h)),
            pl.BlockSpec((nkc, VT_ROWS, tk), lambda h, b, i: (b, h, 0)),
            pl.BlockSpec((N_META, LANES), lambda h, b, i: (0, h)),
            pl.BlockSpec((VT_ROWS, N_META), lambda h, b, i: (h, 0)),
            pl.BlockSpec((1, nt, tk, tq), lambda h, b, i: (h, 0, 0, 0)),
            pl.BlockSpec((1, 2, N_META, tq), lambda h, b, i: (h, 0, 0, 0)),
            pl.BlockSpec(subw_col.shape, lambda h, b, i: (0, 0)),
        ],
        out_specs=pl.BlockSpec((tq, LANES), lambda h, b, i: (b * nq + i, h)),
        out_shape=jax.ShapeDtypeStruct((n, ATT_V), BF16),
        scratch_shapes=[pltpu.VMEM((tk, 2 * tq), F32)] * 2 + [pltpu.VMEM((1, 2 * tq), F32)] * 3
        + [pltpu.VMEM((VT_ROWS, 2 * tq), F32)],
        compiler_params=pltpu.CompilerParams(dimension_semantics=("arbitrary",) * 3, vmem_limit_bytes=VMEM_LIMIT),
        name="diff_attn",
    )(lamv, q, k, vt, km, vmt, tab, mtab, subw_col)


def _split3(x):
    hi = x.astype(BF16)
    r1 = x - hi.astype(F32)
    mid = r1.astype(BF16)
    lo = (r1 - mid.astype(F32)).astype(BF16)
    return hi, mid, lo


def _cumsum_rows(a):
    rows = a.shape[0]
    r_i = lax.broadcasted_iota(jnp.int32, (rows, rows), 0)
    c_i = lax.broadcasted_iota(jnp.int32, (rows, rows), 1)
    tri = jnp.where(c_i <= r_i, 1.0, 0.0).astype(BF16)
    out = None
    for term in _split3(a):
        part = jnp.dot(tri, term, preferred_element_type=F32)
        out = part if out is None else out + part
    return out


def _expand_rows(parts, sel_ref):
    masked = []
    for w, first in parts:
        lane = lax.broadcasted_iota(jnp.int32, w.shape, 1)
        masked.append(jnp.where((lane >= first) & (lane < first + SSM_HEADS), w, 0.0))
    stacked = jnp.concatenate(masked, axis=0)
    hi = stacked.astype(BF16)
    lo = (stacked - hi.astype(F32)).astype(BF16)
    sel = sel_ref[...]
    full = jnp.dot(hi, sel, preferred_element_type=F32) + jnp.dot(lo, sel, preferred_element_type=F32)
    outs, r0 = [], 0
    for w, _ in parts:
        outs.append(full[r0:r0 + w.shape[0]])
        r0 += w.shape[0]
    return outs


def _softplus(x):
    return jnp.maximum(x, 0.0) + jnp.log(1.0 + jnp.exp(-jnp.abs(x)))


GROUP_COLS = SSM_INNER // SSM_GROUPS


def _state_update(b_t, xw):
    return jnp.concatenate(
        [jnp.dot(b_t[g * SSM_STATE:(g + 1) * SSM_STATE], xw[:, g * GROUP_COLS:(g + 1) * GROUP_COLS],
                 preferred_element_type=F32) for g in range(SSM_GROUPS)], axis=0)


def _stack_decay(dec_row):
    return jnp.concatenate(
        [jnp.broadcast_to(dec_row[:, g * GROUP_COLS:(g + 1) * GROUP_COLS], (SSM_STATE, GROUP_COLS))
         for g in range(SSM_GROUPS)], axis=0)


def _conv_silu(win_ref, cw_ref, cb_ref, rows):
    acc = jnp.broadcast_to(cb_ref[...], (rows, SSM_CONV_DIM))
    for j in range(SSM_CONV):
        acc = acc + cw_ref[j:j + 1, :] * win_ref[pl.ds(HALO - SSM_CONV // 2 + j, rows), :]
    return acc * jax.nn.sigmoid(acc)


def _ssd_kernel(z_ref, xc_ref, xl_ref, xr_ref, dt_ref, mx_ref, mdt_ref, cw_ref, cb_ref, dtb_ref, alog_ref,
                dsk_ref, nw_ref, sel_ref, o_ref, xs_scr, dts_scr, cum_scr, hbs_scr, hf_scr, hb_scr, win_scr,
                *, cs, nc):
    ph = pl.program_id(1)
    t = pl.program_id(2)
    fwd0, bwd0 = 0, SSM_HEADS
    a_row = -jnp.exp(alog_ref[...])

    def decay_terms(dt_raw):
        dt = _softplus(dt_raw + dtb_ref[...])
        return dt, _cumsum_rows(dt * a_row)

    def bcast8(row):
        return jnp.broadcast_to(row, (8, LANES))

    @pl.when(ph == 0)
    def _():
        cc = nc - 1 - t

        @pl.when(t == 0)
        def _():
            hb_scr[...] = jnp.zeros(hb_scr.shape, F32)

        left = jnp.where(cc == 0, mx_ref[...], xl_ref[...])
        right = jnp.where(cc == nc - 1, jnp.zeros_like(xr_ref[...]), xr_ref[...])
        win_scr[0:HALO, :] = left.astype(F32)
        win_scr[HALO:HALO + cs, :] = xc_ref[...].astype(F32)
        win_scr[HALO + cs:HALO + cs + HALO, :] = right.astype(F32)
        xbc = _conv_silu(win_scr, cw_ref, cb_ref, cs)
        xs_scr[cc] = xbc.astype(BF16)

        dt, cum = decay_terms(dt_ref[...])
        dts_scr[cc] = dt
        cum_scr[cc] = cum
        eb = cum - dt * a_row
        w_b, dec = _expand_rows([(jnp.exp(eb) * dt, bwd0), (bcast8(jnp.exp(cum[cs - 1:cs, :])), bwd0)], sel_ref)
        xw = (xbc[:, :SSM_INNER] * w_b).astype(BF16)
        bm_t = xbc[:, SSM_INNER:SSM_INNER + LANES].T.astype(BF16)
        hb = hb_scr[...]
        hbs_scr[cc] = hb.astype(BF16)
        hb_scr[...] = hb * _stack_decay(dec[0:1]) + _state_update(bm_t, xw)

    @pl.when(ph == 1)
    def _():
        cc = t

        @pl.when(t == 0)
        def _():
            win_scr[0:HALO, :] = jnp.zeros((HALO, SSM_CONV_DIM), F32)
            win_scr[HALO:2 * HALO, :] = mx_ref[...].astype(F32)
            win_scr[2 * HALO:3 * HALO, :] = xc_ref[0:HALO, :].astype(F32)
            xm = _conv_silu(win_scr, cw_ref, cb_ref, N_META)
            dtm, cumm = decay_terms(mdt_ref[...])
            (w_m,) = _expand_rows([(jnp.exp(cumm[N_META - 1:N_META, :] - cumm) * dtm, fwd0)], sel_ref)
            xwm = (xm[:, :SSM_INNER] * w_m).astype(BF16)
            bmm_t = xm[:, SSM_INNER:SSM_INNER + LANES].T.astype(BF16)
            hf_scr[...] = _state_update(bmm_t, xwm)

        xbc = xs_scr[cc]
        x_bf = xbc[:, :SSM_INNER]
        bm = xbc[:, SSM_INNER:SSM_INNER + LANES]
        cm = xbc[:, SSM_INNER + LANES:SSM_INNER + 2 * LANES]
        x = x_bf.astype(F32)

        dt = dts_scr[cc]
        cum = cum_scr[cc]
        eb = cum - dt * a_row
        dt_t, cum_t, eb_t = dt.T, cum.T, eb.T
        last = cum[cs - 1:cs, :]

        lane = lax.broadcasted_iota(jnp.int32, (cs, LANES), 1)
        zc = jnp.zeros_like(cm)
        nt_dims = (((1,), (1,)), ((), ()))
        c_grp = [jnp.where(lane // SSM_STATE == g, cm, zc) for g in range(SSM_GROUPS)]
        g_mats = [lax.dot_general(c_g, bm, nt_dims, preferred_element_type=F32) for c_g in c_grp]

        l_i = lax.broadcasted_iota(jnp.int32, (cs, cs), 0)
        s_i = lax.broadcasted_iota(jnp.int32, (cs, cs), 1)
        lower = s_i <= l_i
        diag = s_i == l_i
        hpg = SSM_HEADS // SSM_GROUPS
        zx = jnp.zeros((cs, LANES), BF16)
        pieces = []
        for hp in range(SSM_HEADS // 2):
            w_pair = []
            for h in (2 * hp, 2 * hp + 1):
                arg_f = cum[:, fwd0 + h:fwd0 + h + 1] - cum_t[fwd0 + h:fwd0 + h + 1, :]
                arg_b = eb_t[bwd0 + h:bwd0 + h + 1, :] - eb[:, bwd0 + h:bwd0 + h + 1]
                e = jnp.exp(jnp.minimum(jnp.where(lower, arg_f, arg_b), 0.0))
                dt_f_row = dt_t[fwd0 + h:fwd0 + h + 1, :]
                dt_b_row = dt_t[bwd0 + h:bwd0 + h + 1, :]
                m = e * jnp.where(lower, dt_f_row, dt_b_row) + jnp.where(diag, dt_b_row, 0.0)
                w_pair.append((g_mats[h // hpg] * m).astype(BF16))
            xp = x_bf[:, hp * LANES:(hp + 1) * LANES]
            rhs = jnp.concatenate([jnp.where(lane < SSM_HEADDIM, xp, zx),
                                   jnp.where(lane >= SSM_HEADDIM, xp, zx)], axis=0)
            pieces.append(jnp.dot(jnp.concatenate(w_pair, axis=1), rhs, preferred_element_type=F32))
        y = jnp.concatenate(pieces, axis=1)

        d_f, d_b, w_f, dec = _expand_rows(
            [(jnp.exp(cum), fwd0), (jnp.exp(last - eb), bwd0), (jnp.exp(last - cum) * dt, fwd0),
             (bcast8(jnp.exp(last)), fwd0)], sel_ref)
        hf = hf_scr[...]
        hf_bf = hf.astype(BF16)
        hb_bf = hbs_scr[cc]
        y = y + d_f * jnp.concatenate([jnp.dot(c_g, hf_bf, preferred_element_type=F32) for c_g in c_grp], axis=1)
        y = y + d_b * jnp.concatenate([jnp.dot(c_g, hb_bf, preferred_element_type=F32) for c_g in c_grp], axis=1)
        y = y + x * dsk_ref[...]

        xw = (x * w_f).astype(BF16)
        hf_scr[...] = hf * _stack_decay(dec[0:1]) + _state_update(bm.astype(F32).T.astype(BF16), xw)

        zf = z_ref[...].astype(F32)
        y = y * (zf * jax.nn.sigmoid(zf))
        o_ref[...] = _rmsnorm(y, nw_ref[...]).astype(o_ref.dtype)


def _ssd_call(z, xbc, dt, mxbc, mdt, cw, cb, dtb, alog, dskip, nw, sel, batch, seq, cs):
    n = z.shape[0]
    assert n == batch * seq and seq % cs == 0 and cs % HALO == 0
    nc = seq // cs
    hpc = cs // HALO
    n_halo = n // HALO

    def visit(ph, t):
        return (1 - ph) * (nc - 1 - t) + ph * t

    def ph0_chunk(ph, t):
        return (1 - ph) * (nc - 1 - t)

    const2 = lambda shape: pl.BlockSpec(shape, lambda b, ph, t: (0, 0))
    return pl.pallas_call(
        functools.partial(_ssd_kernel, cs=cs, nc=nc),
        grid=(batch, 2, nc),
        in_specs=[
            pl.BlockSpec((cs, SSM_INNER), lambda b, ph, t: (b * nc + ph * t, 0)),
            pl.BlockSpec((cs, SSM_CONV_DIM), lambda b, ph, t: (b * nc + ph0_chunk(ph, t), 0)),
            pl.BlockSpec((HALO, SSM_CONV_DIM),
                         lambda b, ph, t: (jnp.maximum((b * nc + ph0_chunk(ph, t)) * hpc - 1, 0), 0)),
            pl.BlockSpec((HALO, SSM_CONV_DIM),
                         lambda b, ph, t: (jnp.minimum((b * nc + ph0_chunk(ph, t) + 1) * hpc, n_halo - 1), 0)),
            pl.BlockSpec((cs, DT_PAD), lambda b, ph, t: (b * nc + ph0_chunk(ph, t), 0)),
            const2(mxbc.shape), const2(mdt.shape), const2(cw.shape), const2(cb.shape), const2(dtb.shape),
            const2(alog.shape), const2(dskip.shape), const2(nw.shape), const2(sel.shape),
        ],
        out_specs=pl.BlockSpec((cs, SSM_INNER), lambda b, ph, t: (b * nc + ph * t, 0)),
        out_shape=jax.ShapeDtypeStruct((n, SSM_INNER), BF16),
        scratch_shapes=[
            pltpu.VMEM((nc, cs, SSM_CONV_DIM), BF16),
            pltpu.VMEM((nc, cs, DT_PAD), F32),
            pltpu.VMEM((nc, cs, DT_PAD), F32),
            pltpu.VMEM((nc, LANES, GROUP_COLS), BF16),
            pltpu.VMEM((LANES, GROUP_COLS), F32),
            pltpu.VMEM((LANES, GROUP_COLS), F32),
            pltpu.VMEM((cs + 2 * HALO, SSM_CONV_DIM), F32),
        ],
        compiler_params=pltpu.CompilerParams(dimension_semantics=("arbitrary",) * 3, vmem_limit_bytes=VMEM_LIMIT),
        name="bi_ssd",
    )(z, xbc, xbc, xbc, dt, mxbc, mdt, cw, cb, dtb, alog, dskip, nw, sel)


def _head_selector():
    k = jnp.arange(LANES)[:, None]
    col = jnp.arange(SSM_INNER)[None, :]
    return ((k % SSM_HEADS == col // SSM_HEADDIM) & (k < 2 * SSM_HEADS)).astype(BF16)


def _prep_weights(ffn1_norm_w, ffn1_w_gate, ffn1_w_up, ffn1_w_down, mix_norm_w, w_in, lambda_q1, lambda_k1,
                  lambda_q2, lambda_k2, attn_subln_w, conv_w, conv_b, dt_bias_fwd, dt_bias_bwd, a_log_fwd,
                  a_log_bwd, ssm_d, ssm_norm_w, w_out, ffn2_norm_w, ffn2_w_gate, ffn2_w_up, ffn2_w_down,
                  final_norm_w):
    def ffn(norm_w, wg, wu, wd):
        chunk_cols = lambda w: w[0].astype(BF16).reshape(D_MODEL, N_FF, FF_TILE).transpose(1, 0, 2)
        return (norm_w[0][None, :], chunk_cols(wg), chunk_cols(wu),
                wd[0].astype(BF16).reshape(N_FF, FF_TILE, D_MODEL))

    pad_lanes = lambda v, width: jnp.pad(v, (0, width - v.shape[0]))[None, :]
    o_v, o_z = 2 * ATT_QK, 2 * ATT_QK + ATT_V
    wi = w_in[0].astype(BF16)
    win = jnp.pad(jnp.concatenate([wi[:, :o_v], wi[:, o_z:]], axis=1),
                  ((0, 0), (0, D_IN_PAD - (w_in.shape[2] - ATT_V))))
    return dict(
        ffn1=ffn(ffn1_norm_w, ffn1_w_gate, ffn1_w_up, ffn1_w_down),
        ffn2=ffn(ffn2_norm_w, ffn2_w_gate, ffn2_w_up, ffn2_w_down),
        mix_norm=mix_norm_w[0][None, :],
        win=win,
        wvt=wi[:, o_v:o_z].T,
        lamv=jnp.stack([lambda_q1[0], lambda_k1[0], lambda_q2[0], lambda_k2[0]]),
        subw_col=attn_subln_w[0][:, None],
        cw=jnp.pad(conv_w[0], ((0, 8 - SSM_CONV), (0, 0))),
        cb=conv_b[0][None, :],
        dtb=pad_lanes(jnp.concatenate([dt_bias_fwd[0], dt_bias_bwd[0]]), DT_PAD),
        alog=pad_lanes(jnp.concatenate([a_log_fwd[0], a_log_bwd[0]]), DT_PAD),
        dskip=jnp.repeat(ssm_d[0], SSM_HEADDIM)[None, :],
        ssm_norm=ssm_norm_w[0][None, :],
        wo=w_out[0].astype(BF16).reshape(2, ATT_V, D_MODEL),
        final=final_norm_w[None, :],
        sel=_head_selector(),
    )


def _encode(x, w, meta_proj, tab, mtab):
    batch, seq, _ = x.shape
    km, vmt, mxbc, mdt = meta_proj
    h0 = x.reshape(batch * seq, D_MODEL)
    h1 = _ffn_call(h0, *w["ffn1"])
    q, k, vt, z, xbc, dt = _inproj_call(h1, w["mix_norm"], w["win"], w["wvt"])
    att = _attn_call(w["lamv"], q, k, vt, km, vmt, tab, mtab, w["subw_col"], batch, seq, ATT_TQ, ATT_TK)
    ssm = _ssd_call(z, xbc, dt, mxbc, mdt, w["cw"], w["cb"], w["dtb"], w["alog"], w["dskip"], w["ssm_norm"],
                    w["sel"], batch, seq, SSD_CHUNK)
    y = _ffn_call(h1, *w["ffn2"], mix=(att, ssm, w["wo"]), final_w=w["final"])
    return y.reshape(batch, seq, D_MODEL)


def kernel(x_prompt, x_sample, meta_tokens, ffn1_norm_w, ffn1_w_gate, ffn1_w_up, ffn1_w_down, mix_norm_w, w_in, rel_bias, lambda_q1, lambda_k1, lambda_q2, lambda_k2, attn_subln_w, conv_w, conv_b, dt_bias_fwd, dt_bias_bwd, a_log_fwd, a_log_bwd, ssm_d, ssm_norm_w, w_out, ffn2_norm_w, ffn2_w_gate, ffn2_w_up, ffn2_w_down, final_norm_w):
    w = _prep_weights(ffn1_norm_w, ffn1_w_gate, ffn1_w_up, ffn1_w_down, mix_norm_w, w_in, lambda_q1, lambda_k1,
                      lambda_q2, lambda_k2, attn_subln_w, conv_w, conv_b, dt_bias_fwd, dt_bias_bwd, a_log_fwd,
                      a_log_bwd, ssm_d, ssm_norm_w, w_out, ffn2_norm_w, ffn2_w_gate, ffn2_w_up, ffn2_w_down,
                      final_norm_w)
    hm = _ffn_call(meta_tokens, *w["ffn1"])
    _, km, vmt, _, mxbc, mdt = _inproj_call(hm, w["mix_norm"], w["win"], w["wvt"])
    meta_proj = (km, vmt[0], mxbc, mdt)
    tab, mtab = _bias_call(rel_bias, ATT_TQ, ATT_TK)
    return (_encode(x_prompt, w, meta_proj, tab, mtab), _encode(x_sample, w, meta_proj, tab, mtab))
```

```python
import functools
import math

import jax
import jax.numpy as jnp
from jax import lax
from jax.experimental import pallas as pl
from jax.experimental.pallas import tpu as pltpu

F32 = jnp.float32
BF16 = jnp.bfloat16

D_MODEL = 1024
N_META = 16
N_ATT_HEADS = 8
ATT_DH = 64
ATT_DV = 128
ATT_QK = 1024
ATT_V = 1024
NUM_BUCKETS = 32
MAX_DISTANCE = 128
SSM_HEADS = 16
SSM_HEADDIM = 64
SSM_INNER = 1024
SSM_GROUPS = 2
SSM_STATE = 64
SSM_CONV = 7
SSM_CONV_DIM = 1280
D_FF = 2816
EPS = 1e-6
LAYER = 0
LAM_INIT = 0.8 - 0.6 * math.exp(-0.3 * LAYER)
LOG2E = math.log2(math.e)
Q_SCALE = ATT_DH ** -0.5 * LOG2E
NEG_BIG = -1e30

LANES = 128
BF16_ROWS = 16
VMEM_LIMIT = 56 * 1024 * 1024

FF_TILE = 256
N_FF = D_FF // FF_TILE
DT_PAD = LANES
D_IN_PAD = 2 * ATT_QK + SSM_INNER + SSM_CONV_DIM + DT_PAD
T5_BAND = 91

ROW_TILE = 512
ATT_TQ = 512
ATT_TK = 512
ATT_COLS = 256
SSD_CHUNK = 128
HALO = BF16_ROWS
VT_ROWS = ATT_DV + BF16_ROWS


def _rmsnorm(x, w):
    ms = jnp.mean(x * x, axis=-1, keepdims=True)
    return x * lax.rsqrt(ms + EPS) * w


def _resident(shape):
    nd = len(shape)
    return pl.BlockSpec(shape, lambda *_: (0,) * nd, pipeline_mode=pl.Buffered(1))


def _ffn_kernel(*refs, has_mix, has_final):
    it = iter(refs)
    h_ref = next(it)
    if has_mix:
        att_ref, ssm_ref, wo_ref = next(it), next(it), next(it)
    nw_ref, wg_ref, wu_ref, wd_ref = next(it), next(it), next(it), next(it)
    fw_ref = next(it) if has_final else None
    o_ref = next(it)

    h = h_ref[...]
    if has_mix:
        h = (h + jnp.dot(att_ref[...], wo_ref[0], preferred_element_type=F32)
             + jnp.dot(ssm_ref[...], wo_ref[1], preferred_element_type=F32))
    u = _rmsnorm(h, nw_ref[...]).astype(BF16)
    acc = jnp.zeros_like(h)
    for j in range(N_FF):
        g = jnp.dot(u, wg_ref[j], preferred_element_type=F32)
        up = jnp.dot(u, wu_ref[j], preferred_element_type=F32)
        a = (g * jax.nn.sigmoid(g) * up).astype(BF16)
        acc = acc + jnp.dot(a, wd_ref[j], preferred_element_type=F32)
    h = h + 0.5 * acc
    if has_final:
        h = _rmsnorm(h, fw_ref[...])
    o_ref[...] = h


def _ffn_call(h, norm_w, wg, wu, wd, mix=None, final_w=None):
    n = h.shape[0]
    tm = min(ROW_TILE, n)
    assert n % tm == 0
    row = lambda width: pl.BlockSpec((tm, width), lambda i: (i, 0))
    args, specs = [h], [row(D_MODEL)]
    if mix is not None:
        att, ssm, wo = mix
        args += [att, ssm, wo]
        specs += [row(ATT_V), row(SSM_INNER), _resident(wo.shape)]
    args += [norm_w, wg, wu, wd]
    specs += [_resident(norm_w.shape), _resident(wg.shape), _resident(wu.shape), _resident(wd.shape)]
    if final_w is not None:
        args.append(final_w)
        specs.append(_resident(final_w.shape))
    return pl.pallas_call(
        functools.partial(_ffn_kernel, has_mix=mix is not None, has_final=final_w is not None),
        grid=(n // tm,),
        in_specs=specs,
        out_specs=row(D_MODEL),
        out_shape=jax.ShapeDtypeStruct((n, D_MODEL), F32),
        compiler_params=pltpu.CompilerParams(dimension_semantics=("arbitrary",), vmem_limit_bytes=VMEM_LIMIT),
        name="ffn_mix" if mix is not None else "ffn",
    )(*args)


_IN_SEGS = (("q", 0, ATT_QK), ("k", ATT_QK, ATT_QK), ("z", 2 * ATT_QK, SSM_INNER),
            ("xbc", 2 * ATT_QK + SSM_INNER, SSM_CONV_DIM), ("dt", D_IN_PAD - DT_PAD, DT_PAD))


def _inproj_kernel(h_ref, nw_ref, win_ref, wvt_ref, q_ref, k_ref, vt_ref, z_ref, xbc_ref, dt_ref):
    u = _rmsnorm(h_ref[...], nw_ref[...]).astype(BF16)
    outs = dict(q=q_ref, k=k_ref, z=z_ref, xbc=xbc_ref, dt=dt_ref)
    for name, c0, width in _IN_SEGS:
        o_ref = outs[name]
        step = 512 if width % 512 == 0 else (256 if width % 256 == 0 else LANES)
        for s in range(0, width, step):
            r = jnp.dot(u, win_ref[:, c0 + s:c0 + s + step], preferred_element_type=F32)
            if name == "q":
                r = r * Q_SCALE
            o_ref[:, s:s + step] = r.astype(o_ref.dtype)
    nt_dims = (((1,), (1,)), ((), ()))
    ones = jnp.ones((VT_ROWS - ATT_DV, u.shape[0]), vt_ref.dtype)
    for s in range(0, ATT_V, 256):
        r = lax.dot_general(wvt_ref[s:s + 256, :], u, nt_dims, preferred_element_type=F32).astype(vt_ref.dtype)
        for hh in range(256 // ATT_DV):
            head = s // ATT_DV + hh
            vt_ref[0, head * VT_ROWS:head * VT_ROWS + ATT_DV, :] = r[hh * ATT_DV:(hh + 1) * ATT_DV]
            vt_ref[0, head * VT_ROWS + ATT_DV:(head + 1) * VT_ROWS, :] = ones


def _inproj_call(h, norm_w, win, wvt):
    n = h.shape[0]
    tm = min(ATT_TK, n)
    assert n % tm == 0
    row = lambda width: pl.BlockSpec((tm, width), lambda i: (i, 0))
    widths = (ATT_QK, ATT_QK, SSM_INNER, SSM_CONV_DIM, DT_PAD)
    dtypes = (BF16, BF16, BF16, BF16, F32)
    shapes = [jax.ShapeDtypeStruct((n, w), dt) for w, dt in zip(widths, dtypes)]
    specs = [row(w) for w in widths]
    shapes.insert(2, jax.ShapeDtypeStruct((n // tm, N_ATT_HEADS * VT_ROWS, tm), BF16))
    specs.insert(2, pl.BlockSpec((1, N_ATT_HEADS * VT_ROWS, tm), lambda i: (i, 0, 0)))
    return pl.pallas_call(
        _inproj_kernel,
        grid=(n // tm,),
        in_specs=[row(D_MODEL), _resident(norm_w.shape), _resident(win.shape), _resident(wvt.shape)],
        out_specs=specs,
        out_shape=shapes,
        compiler_params=pltpu.CompilerParams(dimension_semantics=("arbitrary",), vmem_limit_bytes=VMEM_LIMIT),
        name="inproj",
    )(h, norm_w, win, wvt)


def _t5_bias(rel, rb_ref, head):
    half = NUM_BUCKETS // 2
    max_exact = half // 2
    ret = jnp.where(rel > 0, half, 0)
    n = jnp.abs(rel)
    nf = jnp.maximum(n, 1).astype(F32)
    large = max_exact + (jnp.log(nf / max_exact) / math.log(MAX_DISTANCE / max_exact)
                         * (half - max_exact)).astype(jnp.int32)
    large = jnp.minimum(large, half - 1)
    bucket = ret + jnp.where(n < max_exact, n, large)
    val = jnp.zeros(rel.shape, F32)
    for jb in range(NUM_BUCKETS):
        val = jnp.where(bucket == jb, rb_ref[jb, head], val)
    return val * LOG2E


def _bias_kernel(rb_ref, tab_ref, mtab_ref, *, tq, tk):
    head = pl.program_id(0)
    r = tk // tq
    n_near = r + 2
    krow = lax.broadcasted_iota(jnp.int32, (tk, tq), 0)
    qcol = lax.broadcasted_iota(jnp.int32, (tk, tq), 1)
    for t in range(n_near):
        tab_ref[0, t] = _t5_bias(krow - qcol + (t - r) * tq, rb_ref, head)
    far_left = rb_ref[NUM_BUCKETS // 2 - 1, head] * LOG2E
    far_right = rb_ref[NUM_BUCKETS - 1, head] * LOG2E
    tab_ref[0, n_near] = jnp.full((tk, tq), far_left, F32)
    tab_ref[0, n_near + 1] = jnp.full((tk, tq), far_right, F32)
    mrow = lax.broadcasted_iota(jnp.int32, (N_META, tq), 0)
    mcol = lax.broadcasted_iota(jnp.int32, (N_META, tq), 1)
    mtab_ref[0, 0] = _t5_bias(mrow - N_META - mcol, rb_ref, head)
    mtab_ref[0, 1] = jnp.full((N_META, tq), far_left, F32)


def _bias_call(rel_bias, tq, tk):
    assert tk % tq == 0 and tq >= T5_BAND + 1
    nt = tk // tq + 4
    return pl.pallas_call(
        functools.partial(_bias_kernel, tq=tq, tk=tk),
        grid=(N_ATT_HEADS,),
        in_specs=[pl.BlockSpec(memory_space=pltpu.SMEM)],
        out_specs=[pl.BlockSpec((1, nt, tk, tq), lambda h: (h, 0, 0, 0)),
                   pl.BlockSpec((1, 2, N_META, tq), lambda h: (h, 0, 0, 0))],
        out_shape=[jax.ShapeDtypeStruct((N_ATT_HEADS, nt, tk, tq), F32),
                   jax.ShapeDtypeStruct((N_ATT_HEADS, 2, N_META, tq), F32)],
        compiler_params=pltpu.CompilerParams(dimension_semantics=("arbitrary",)),
        name="t5_bias",
    )(rel_bias)


def _attn_kernel(lam_ref, q_ref, k_ref, vt_ref, km_ref, vmt_ref, tab_ref, mtab_ref, sw_ref, o_ref,
                 sa_ref, sb_ref, mca_ref, mcb_ref, m_scr, acc_scr, *, tq, tk, nkc):
    qi = pl.program_id(2)
    r = tk // tq
    n_near = r + 2
    q = q_ref[...]
    lane = lax.broadcasted_iota(jnp.int32, (tq, LANES), 1)
    zero = jnp.zeros_like(q)
    q2 = jnp.concatenate([jnp.where(lane < ATT_DH, q, zero), jnp.where(lane >= ATT_DH, q, zero)], axis=0)
    nt_dims = (((1,), (1,)), ((), ()))

    def add_bias(s, b):
        return jnp.concatenate([s[:, :tq] + b, s[:, tq:] + b], axis=1)

    n_col = 2 * tq // ATT_COLS

    def produce(j, c, s_ref, mc_ref):
        cols = pl.ds(c * ATT_COLS, ATT_COLS)
        s = lax.dot_general(k_ref[pl.ds(j * tk, tk), :], q2[c * ATT_COLS:(c + 1) * ATT_COLS], nt_dims,
                            preferred_element_type=F32)
        du = j * r - qi
        idx = jnp.where(du <= -(r + 1), n_near, jnp.where(du >= 2, n_near + 1, du + r))
        s = s + tab_ref[0, idx, :, pl.ds((c * ATT_COLS) % tq, ATT_COLS)]
        s_ref[:, cols] = s
        mc_ref[:, cols] = jnp.max(s, axis=0, keepdims=True)

    def consume(s, m_cur, vt, cols, first=False):
        if first:
            m_new = m_cur
        else:
            m_prev = m_scr[:, cols]
            m_new = jnp.maximum(m_prev, m_cur)
            alpha = jnp.exp2(m_prev - m_new)
        p = jnp.exp2(s - m_new).astype(BF16)
        pv = jnp.dot(vt, p, preferred_element_type=F32)
        acc_scr[:, cols] = pv if first else alpha * acc_scr[:, cols] + pv
        m_scr[:, cols] = m_new

    sm = lax.dot_general(km_ref[...], q2, nt_dims, preferred_element_type=F32)
    sm = add_bias(sm, mtab_ref[0, jnp.minimum(qi, 1)])
    consume(sm, jnp.max(sm, axis=0, keepdims=True), vmt_ref[...], pl.ds(0, 2 * tq), first=True)

    bufs = ((sa_ref, mca_ref), (sb_ref, mcb_ref))
    for c in range(n_col):
        produce(0, c, *bufs[0])
    for j in range(nkc):
        s_ref, mc_ref = bufs[j % 2]
        for c in range(n_col):
            cols = pl.ds(c * ATT_COLS, ATT_COLS)
            if j + 1 < nkc:
                produce(j + 1, c, *bufs[(j + 1) % 2])
            consume(s_ref[:, cols], mc_ref[:, cols], vt_ref[j], cols)

    acc = acc_scr[...]
    o = acc[:ATT_DV] / acc[ATT_DV:ATT_DV + 1]
    lv = lam_ref[...]
    lam = (jnp.exp(jnp.sum(lv[0:1] * lv[1:2], axis=1, keepdims=True))
           - jnp.exp(jnp.sum(lv[2:3] * lv[3:4], axis=1, keepdims=True)) + LAM_INIT)
    out = o[:, :tq] - lam * o[:, tq:]
    ms = jnp.mean(out * out, axis=0, keepdims=True)
    out = out * lax.rsqrt(ms + EPS) * sw_ref[...] * (1.0 - LAM_INIT)
    o_ref[...] = out.T.astype(o_ref.dtype)


def _attn_call(lamv, q, k, vt, km, vmt, tab, mtab, subw_col, batch, seq, tq, tk):
    n = q.shape[0]
    assert n == batch * seq and seq % (2 * tk) == 0 and seq % tq == 0 and vt.shape[2] == tk
    nq = seq // tq
    nkc = seq // tk
    nt = tab.shape[1]
    return pl.pallas_call(
        functools.partial(_attn_kernel, tq=tq, tk=tk, nkc=nkc),
        grid=(N_ATT_HEADS, batch, nq),
        in_specs=[
            pl.BlockSpec(lamv.shape, lambda h, b, i: (0, 0)),
            pl.BlockSpec((tq, LANES), lambda h, b, i: (b * nq + i, h)),
            pl.BlockSpec((seq, LANES), lambda h, b, i: (b, h)),
            pl.BlockSpec((nkc, VT_ROWS, tk), lambda h, b, i: (b, h, 0)),
            pl.BlockSpec((N_META, LANES), lambda h, b, i: (0, h)),
            pl.BlockSpec((VT_ROWS, N_META), lambda h, b, i: (h, 0)),
            pl.BlockSpec((1, nt, tk, tq), lambda h, b, i: (h, 0, 0, 0)),
            pl.BlockSpec((1, 2, N_META, tq), lambda h, b, i: (h, 0, 0, 0)),
            pl.BlockSpec(subw_col.shape, lambda h, b, i: (0, 0)),
        ],
        out_specs=pl.BlockSpec((tq, LANES), lambda h, b, i: (b * nq + i, h)),
        out_shape=jax.ShapeDtypeStruct((n, ATT_V), BF16),
        scratch_shapes=[pltpu.VMEM((tk, 2 * tq), F32)] * 2 + [pltpu.VMEM((1, 2 * tq), F32)] * 3
        + [pltpu.VMEM((VT_ROWS, 2 * tq), F32)],
        compiler_params=pltpu.CompilerParams(dimension_semantics=("arbitrary",) * 3, vmem_limit_bytes=VMEM_LIMIT),
        name="diff_attn",
    )(lamv, q, k, vt, km, vmt, tab, mtab, subw_col)


def _split3(x):
    hi = x.astype(BF16)
    r1 = x - hi.astype(F32)
    mid = r1.astype(BF16)
    lo = (r1 - mid.astype(F32)).astype(BF16)
    return hi, mid, lo


def _cumsum_rows(a):
    rows = a.shape[0]
    r_i = lax.broadcasted_iota(jnp.int32, (rows, rows), 0)
    c_i = lax.broadcasted_iota(jnp.int32, (rows, rows), 1)
    tri = jnp.where(c_i <= r_i, 1.0, 0.0).astype(BF16)
    out = None
    for term in _split3(a):
        part = jnp.dot(tri, term, preferred_element_type=F32)
        out = part if out is None else out + part
    return out


def _expand_rows(parts, sel_ref):
    masked = []
    for w, first in parts:
        lane = lax.broadcasted_iota(jnp.int32, w.shape, 1)
        masked.append(jnp.where((lane >= first) & (lane < first + SSM_HEADS), w, 0.0))
    stacked = jnp.concatenate(masked, axis=0)
    hi = stacked.astype(BF16)
    lo = (stacked - hi.astype(F32)).astype(BF16)
    sel = sel_ref[...]
    full = jnp.dot(hi, sel, preferred_element_type=F32) + jnp.dot(lo, sel, preferred_element_type=F32)
    outs, r0 = [], 0
    for w, _ in parts:
        outs.append(full[r0:r0 + w.shape[0]])
        r0 += w.shape[0]
    return outs


def _softplus(x):
    return jnp.maximum(x, 0.0) + jnp.log(1.0 + jnp.exp(-jnp.abs(x)))


GROUP_COLS = SSM_INNER // SSM_GROUPS


def _state_update(b_t, xw):
    return jnp.concatenate(
        [jnp.dot(b_t[g * SSM_STATE:(g + 1) * SSM_STATE], xw[:, g * GROUP_COLS:(g + 1) * GROUP_COLS],
                 preferred_element_type=F32) for g in range(SSM_GROUPS)], axis=0)


def _stack_decay(dec_row):
    return jnp.concatenate(
        [jnp.broadcast_to(dec_row[:, g * GROUP_COLS:(g + 1) * GROUP_COLS], (SSM_STATE, GROUP_COLS))
         for g in range(SSM_GROUPS)], axis=0)


def _conv_silu(win_ref, cw_ref, cb_ref, rows):
    acc = jnp.broadcast_to(cb_ref[...], (rows, SSM_CONV_DIM))
    for j in range(SSM_CONV):
        acc = acc + cw_ref[j:j + 1, :] * win_ref[pl.ds(HALO - SSM_CONV // 2 + j, rows), :]
    return acc * jax.nn.sigmoid(acc)


def _ssd_kernel(z_ref, xc_ref, xl_ref, xr_ref, dt_ref, mx_ref, mdt_ref, cw_ref, cb_ref, dtb_ref, alog_ref,
                dsk_ref, nw_ref, sel_ref, o_ref, xs_scr, dts_scr, cum_scr, hbs_scr, hf_scr, hb_scr, win_scr,
                *, cs, nc):
    ph = pl.program_id(1)
    t = pl.program_id(2)
    fwd0, bwd0 = 0, SSM_HEADS
    a_row = -jnp.exp(alog_ref[...])

    def decay_terms(dt_raw):
        dt = _softplus(dt_raw + dtb_ref[...])
        return dt, _cumsum_rows(dt * a_row)

    def bcast8(row):
        return jnp.broadcast_to(row, (8, LANES))

    @pl.when(ph == 0)
    def _():
        cc = nc - 1 - t

        @pl.when(t == 0)
        def _():
            hb_scr[...] = jnp.zeros(hb_scr.shape, F32)

        left = jnp.where(cc == 0, mx_ref[...], xl_ref[...])
        right = jnp.where(cc == nc - 1, jnp.zeros_like(xr_ref[...]), xr_ref[...])
        win_scr[0:HALO, :] = left.astype(F32)
        win_scr[HALO:HALO + cs, :] = xc_ref[...].astype(F32)
        win_scr[HALO + cs:HALO + cs + HALO, :] = right.astype(F32)
        xbc = _conv_silu(win_scr, cw_ref, cb_ref, cs)
        xs_scr[cc] = xbc.astype(BF16)

        dt, cum = decay_terms(dt_ref[...])
        dts_scr[cc] = dt
        cum_scr[cc] = cum
        eb = cum - dt * a_row
        w_b, dec = _expand_rows([(jnp.exp(eb) * dt, bwd0), (bcast8(jnp.exp(cum[cs - 1:cs, :])), bwd0)], sel_ref)
        xw = (xbc[:, :SSM_INNER] * w_b).astype(BF16)
        bm_t = xbc[:, SSM_INNER:SSM_INNER + LANES].T.astype(BF16)
        hb = hb_scr[...]
        hbs_scr[cc] = hb.astype(BF16)
        hb_scr[...] = hb * _stack_decay(dec[0:1]) + _state_update(bm_t, xw)

    @pl.when(ph == 1)
    def _():
        cc = t

        @pl.when(t == 0)
        def _():
            win_scr[0:HALO, :] = jnp.zeros((HALO, SSM_CONV_DIM), F32)
            win_scr[HALO:2 * HALO, :] = mx_ref[...].astype(F32)
            win_scr[2 * HALO:3 * HALO, :] = xc_ref[0:HALO, :].astype(F32)
            xm = _conv_silu(win_scr, cw_ref, cb_ref, N_META)
            dtm, cumm = decay_terms(mdt_ref[...])
            (w_m,) = _expand_rows([(jnp.exp(cumm[N_META - 1:N_META, :] - cumm) * dtm, fwd0)], sel_ref)
            xwm = (xm[:, :SSM_INNER] * w_m).astype(BF16)
            bmm_t = xm[:, SSM_INNER:SSM_INNER + LANES].T.astype(BF16)
            hf_scr[...] = _state_update(bmm_t, xwm)

        xbc = xs_scr[cc]
        x_bf = xbc[:, :SSM_INNER]
        bm = xbc[:, SSM_INNER:SSM_INNER + LANES]
        cm = xbc[:, SSM_INNER + LANES:SSM_INNER + 2 * LANES]
        x = x_bf.astype(F32)

        dt = dts_scr[cc]
        cum = cum_scr[cc]
        eb = cum - dt * a_row
        dt_t, cum_t, eb_t = dt.T, cum.T, eb.T
        last = cum[cs - 1:cs, :]

        lane = lax.broadcasted_iota(jnp.int32, (cs, LANES), 1)
        zc = jnp.zeros_like(cm)
        nt_dims = (((1,), (1,)), ((), ()))
        c_grp = [jnp.where(lane // SSM_STATE == g, cm, zc) for g in range(SSM_GROUPS)]
        g_mats = [lax.dot_general(c_g, bm, nt_dims, preferred_element_type=F32) for c_g in c_grp]

        l_i = lax.broadcasted_iota(jnp.int32, (cs, cs), 0)
        s_i = lax.broadcasted_iota(jnp.int32, (cs, cs), 1)
        lower = s_i <= l_i
        diag = s_i == l_i
        hpg = SSM_HEADS // SSM_GROUPS
        zx = jnp.zeros((cs, LANES), BF16)
        pieces = []
        for hp in range(SSM_HEADS // 2):
            w_pair = []
            for h in (2 * hp, 2 * hp + 1):
                arg_f = cum[:, fwd0 + h:fwd0 + h + 1] - cum_t[fwd0 + h:fwd0 + h + 1, :]
                arg_b = eb_t[bwd0 + h:bwd0 + h + 1, :] - eb[:, bwd0 + h:bwd0 + h + 1]
                e = jnp.exp(jnp.minimum(jnp.where(lower, arg_f, arg_b), 0.0))
                dt_f_row = dt_t[fwd0 + h:fwd0 + h + 1, :]
                dt_b_row = dt_t[bwd0 + h:bwd0 + h + 1, :]
                m = e * jnp.where(lower, dt_f_row, dt_b_row) + jnp.where(diag, dt_b_row, 0.0)
                w_pair.append((g_mats[h // hpg] * m).astype(BF16))
            xp = x_bf[:, hp * LANES:(hp + 1) * LANES]
            rhs = jnp.concatenate([jnp.where(lane < SSM_HEADDIM, xp, zx),
                                   jnp.where(lane >= SSM_HEADDIM, xp, zx)], axis=0)
            pieces.append(jnp.dot(jnp.concatenate(w_pair, axis=1), rhs, preferred_element_type=F32))
        y = jnp.concatenate(pieces, axis=1)

        d_f, d_b, w_f, dec = _expand_rows(
            [(jnp.exp(cum), fwd0), (jnp.exp(last - eb), bwd0), (jnp.exp(last - cum) * dt, fwd0),
             (bcast8(jnp.exp(last)), fwd0)], sel_ref)
        hf = hf_scr[...]
        hf_bf = hf.astype(BF16)
        hb_bf = hbs_scr[cc]
        y = y + d_f * jnp.concatenate([jnp.dot(c_g, hf_bf, preferred_element_type=F32) for c_g in c_grp], axis=1)
        y = y + d_b * jnp.concatenate([jnp.dot(c_g, hb_bf, preferred_element_type=F32) for c_g in c_grp], axis=1)
        y = y + x * dsk_ref[...]

        xw = (x * w_f).astype(BF16)
        hf_scr[...] = hf * _stack_decay(dec[0:1]) + _state_update(bm.astype(F32).T.astype(BF16), xw)

        zf = z_ref[...].astype(F32)
        y = y * (zf * jax.nn.sigmoid(zf))
        o_ref[...] = _rmsnorm(y, nw_ref[...]).astype(o_ref.dtype)


def _ssd_call(z, xbc, dt, mxbc, mdt, cw, cb, dtb, alog, dskip, nw, sel, batch, seq, cs):
    n = z.shape[0]
    assert n == batch * seq and seq % cs == 0 and cs % HALO == 0
    nc = seq // cs
    hpc = cs // HALO
    n_halo = n // HALO

    def visit(ph, t):
        return (1 - ph) * (nc - 1 - t) + ph * t

    def ph0_chunk(ph, t):
        return (1 - ph) * (nc - 1 - t)

    const2 = lambda shape: pl.BlockSpec(shape, lambda b, ph, t: (0, 0))
    return pl.pallas_call(
        functools.partial(_ssd_kernel, cs=cs, nc=nc),
        grid=(batch, 2, nc),
        in_specs=[
            pl.BlockSpec((cs, SSM_INNER), lambda b, ph, t: (b * nc + ph * t, 0)),
            pl.BlockSpec((cs, SSM_CONV_DIM), lambda b, ph, t: (b * nc + ph0_chunk(ph, t), 0)),
            pl.BlockSpec((HALO, SSM_CONV_DIM),
                         lambda b, ph, t: (jnp.maximum((b * nc + ph0_chunk(ph, t)) * hpc - 1, 0), 0)),
            pl.BlockSpec((HALO, SSM_CONV_DIM),
                         lambda b, ph, t: (jnp.minimum((b * nc + ph0_chunk(ph, t) + 1) * hpc, n_halo - 1), 0)),
            pl.BlockSpec((cs, DT_PAD), lambda b, ph, t: (b * nc + ph0_chunk(ph, t), 0)),
            const2(mxbc.shape), const2(mdt.shape), const2(cw.shape), const2(cb.shape), const2(dtb.shape),
            const2(alog.shape), const2(dskip.shape), const2(nw.shape), const2(sel.shape),
        ],
        out_specs=pl.BlockSpec((cs, SSM_INNER), lambda b, ph, t: (b * nc + ph * t, 0)),
        out_shape=jax.ShapeDtypeStruct((n, SSM_INNER), BF16),
        scratch_shapes=[
            pltpu.VMEM((nc, cs, SSM_CONV_DIM), BF16),
            pltpu.VMEM((nc, cs, DT_PAD), F32),
            pltpu.VMEM((nc, cs, DT_PAD), F32),
            pltpu.VMEM((nc, LANES, GROUP_COLS), BF16),
            pltpu.VMEM((LANES, GROUP_COLS), F32),
            pltpu.VMEM((LANES, GROUP_COLS), F32),
            pltpu.VMEM((cs + 2 * HALO, SSM_CONV_DIM), F32),
        ],
        compiler_params=pltpu.CompilerParams(dimension_semantics=("arbitrary",) * 3, vmem_limit_bytes=VMEM_LIMIT),
        name="bi_ssd",
    )(z, xbc, xbc, xbc, dt, mxbc, mdt, cw, cb, dtb, alog, dskip, nw, sel)


def _head_selector():
    k = jnp.arange(LANES)[:, None]
    col = jnp.arange(SSM_INNER)[None, :]
    return ((k % SSM_HEADS == col // SSM_HEADDIM) & (k < 2 * SSM_HEADS)).astype(BF16)


def _prep_weights(ffn1_norm_w, ffn1_w_gate, ffn1_w_up, ffn1_w_down, mix_norm_w, w_in, lambda_q1, lambda_k1,
                  lambda_q2, lambda_k2, attn_subln_w, conv_w, conv_b, dt_bias_fwd, dt_bias_bwd, a_log_fwd,
                  a_log_bwd, ssm_d, ssm_norm_w, w_out, ffn2_norm_w, ffn2_w_gate, ffn2_w_up, ffn2_w_down,
                  final_norm_w):
    def ffn(norm_w, wg, wu, wd):
        chunk_cols = lambda w: w[0].astype(BF16).reshape(D_MODEL, N_FF, FF_TILE).transpose(1, 0, 2)
        return (norm_w[0][None, :], chunk_cols(wg), chunk_cols(wu),
                wd[0].astype(BF16).reshape(N_FF, FF_TILE, D_MODEL))

    pad_lanes = lambda v, width: jnp.pad(v, (0, width - v.shape[0]))[None, :]
    o_v, o_z = 2 * ATT_QK, 2 * ATT_QK + ATT_V
    wi = w_in[0].astype(BF16)
    win = jnp.pad(jnp.concatenate([wi[:, :o_v], wi[:, o_z:]], axis=1),
                  ((0, 0), (0, D_IN_PAD - (w_in.shape[2] - ATT_V))))
    return dict(
        ffn1=ffn(ffn1_norm_w, ffn1_w_gate, ffn1_w_up, ffn1_w_down),
        ffn2=ffn(ffn2_norm_w, ffn2_w_gate, ffn2_w_up, ffn2_w_down),
        mix_norm=mix_norm_w[0][None, :],
        win=win,
        wvt=wi[:, o_v:o_z].T,
        lamv=jnp.stack([lambda_q1[0], lambda_k1[0], lambda_q2[0], lambda_k2[0]]),
        subw_col=attn_subln_w[0][:, None],
        cw=jnp.pad(conv_w[0], ((0, 8 - SSM_CONV), (0, 0))),
        cb=conv_b[0][None, :],
        dtb=pad_lanes(jnp.concatenate([dt_bias_fwd[0], dt_bias_bwd[0]]), DT_PAD),
        alog=pad_lanes(jnp.concatenate([a_log_fwd[0], a_log_bwd[0]]), DT_PAD),
        dskip=jnp.repeat(ssm_d[0], SSM_HEADDIM)[None, :],
        ssm_norm=ssm_norm_w[0][None, :],
        wo=w_out[0].astype(BF16).reshape(2, ATT_V, D_MODEL),
        final=final_norm_w[None, :],
        sel=_head_selector(),
    )


def _encode(x, w, meta_proj, tab, mtab):
    batch, seq, _ = x.shape
    km, vmt, mxbc, mdt = meta_proj
    h0 = x.reshape(batch * seq, D_MODEL)
    h1 = _ffn_call(h0, *w["ffn1"])
    q, k, vt, z, xbc, dt = _inproj_call(h1, w["mix_norm"], w["win"], w["wvt"])
    att = _attn_call(w["lamv"], q, k, vt, km, vmt, tab, mtab, w["subw_col"], batch, seq, ATT_TQ, ATT_TK)
    ssm = _ssd_call(z, xbc, dt, mxbc, mdt, w["cw"], w["cb"], w["dtb"], w["alog"], w["dskip"], w["ssm_norm"],
                    w["sel"], batch, seq, SSD_CHUNK)
    y = _ffn_call(h1, *w["ffn2"], mix=(att, ssm, w["wo"]), final_w=w["final"])
    return y.reshape(batch, seq, D_MODEL)


def kernel(x_prompt, x_sample, meta_tokens, ffn1_norm_w, ffn1_w_gate, ffn1_w_up, ffn1_w_down, mix_norm_w, w_in, rel_bias, lambda_q1, lambda_k1, lambda_q2, lambda_k2, attn_subln_w, conv_w, conv_b, dt_bias_fwd, dt_bias_bwd, a_log_fwd, a_log_bwd, ssm_d, ssm_norm_w, w_out, ffn2_norm_w, ffn2_w_gate, ffn2_w_up, ffn2_w_down, final_norm_w):
    w = _prep_weights(ffn1_norm_w, ffn1_w_gate, ffn1_w_up, ffn1_w_down, mix_norm_w, w_in, lambda_q1, lambda_k1,
                      lambda_q2, lambda_k2, attn_subln_w, conv_w, conv_b, dt_bias_fwd, dt_bias_bwd, a_log_fwd,
                      a_log_bwd, ssm_d, ssm_norm_w, w_out, ffn2_norm_w, ffn2_w_gate, ffn2_w_up, ffn2_w_down,
                      final_norm_w)
    hm = _ffn_call(meta_tokens, *w["ffn1"])
    _, km, vmt, _, mxbc, mdt = _inproj_call(hm, w["mix_norm"], w["win"], w["wvt"])
    meta_proj = (km, vmt[0], mxbc, mdt)
    tab, mtab = _bias_call(rel_bias, ATT_TQ, ATT_TK)
    return (_encode(x_prompt, w, meta_proj, tab, mtab), _encode(x_sample, w, meta_proj, tab, mtab))
```

```python
import functools
import math

import jax
import jax.numpy as jnp
from jax import lax
from jax.experimental import pallas as pl
from jax.experimental.pallas import tpu as pltpu

F32 = jnp.float32
BF16 = jnp.bfloat16

D_MODEL = 1024
N_META = 16
N_ATT_HEADS = 8
ATT_DH = 64
ATT_DV = 128
ATT_QK = 1024
ATT_V = 1024
NUM_BUCKETS = 32
MAX_DISTANCE = 128
SSM_HEADS = 16
SSM_HEADDIM = 64
SSM_INNER = 1024
SSM_GROUPS = 2
SSM_STATE = 64
SSM_CONV = 7
SSM_CONV_DIM = 1280
D_FF = 2816
EPS = 1e-6
LAYER = 0
LAM_INIT = 0.8 - 0.6 * math.exp(-0.3 * LAYER)
LOG2E = math.log2(math.e)
Q_SCALE = ATT_DH ** -0.5 * LOG2E
NEG_BIG = -1e30

LANES = 128
BF16_ROWS = 16
VMEM_LIMIT = 56 * 1024 * 1024

FF_TILE = 256
N_FF = D_FF // FF_TILE
DT_PAD = LANES
D_IN_PAD = 2 * ATT_QK + SSM_INNER + SSM_CONV_DIM + DT_PAD
T5_BAND = 91

ROW_TILE = 512
ATT_TQ = 512
ATT_TK = 512
ATT_COLS = 256
SSD_CHUNK = 128
SSD_SUB = 4
HALO = BF16_ROWS
VT_ROWS = ATT_DV + BF16_ROWS


def _rmsnorm(x, w):
    ms = jnp.mean(x * x, axis=-1, keepdims=True)
    return x * lax.rsqrt(ms + EPS) * w


def _resident(shape):
    nd = len(shape)
    return pl.BlockSpec(shape, lambda *_: (0,) * nd, pipeline_mode=pl.Buffered(1))


def _ffn_kernel(*refs, has_mix, has_final):
    it = iter(refs)
    h_ref = next(it)
    if has_mix:
        att_ref, ssm_ref, wo_ref = next(it), next(it), next(it)
    nw_ref, wg_ref, wu_ref, wd_ref = next(it), next(it), next(it), next(it)
    fw_ref = next(it) if has_final else None
    o_ref = next(it)

    h = h_ref[...]
    if has_mix:
        h = (h + jnp.dot(att_ref[...], wo_ref[0], preferred_element_type=F32)
             + jnp.dot(ssm_ref[...], wo_ref[1], preferred_element_type=F32))
    u = _rmsnorm(h, nw_ref[...]).astype(BF16)
    acc = jnp.zeros_like(h)
    for j in range(N_FF):
        g = jnp.dot(u, wg_ref[j], preferred_element_type=F32)
        up = jnp.dot(u, wu_ref[j], preferred_element_type=F32)
        a = (g * jax.nn.sigmoid(g) * up).astype(BF16)
        acc = acc + jnp.dot(a, wd_ref[j], preferred_element_type=F32)
    h = h + 0.5 * acc
    if has_final:
        h = _rmsnorm(h, fw_ref[...])
    o_ref[...] = h


def _ffn_call(h, norm_w, wg, wu, wd, mix=None, final_w=None):
    n = h.shape[0]
    tm = min(ROW_TILE, n)
    assert n % tm == 0
    row = lambda width: pl.BlockSpec((tm, width), lambda i: (i, 0))
    args, specs = [h], [row(D_MODEL)]
    if mix is not None:
        att, ssm, wo = mix
        args += [att, ssm, wo]
        specs += [row(ATT_V), row(SSM_INNER), _resident(wo.shape)]
    args += [norm_w, wg, wu, wd]
    specs += [_resident(norm_w.shape), _resident(wg.shape), _resident(wu.shape), _resident(wd.shape)]
    if final_w is not None:
        args.append(final_w)
        specs.append(_resident(final_w.shape))
    return pl.pallas_call(
        functools.partial(_ffn_kernel, has_mix=mix is not None, has_final=final_w is not None),
        grid=(n // tm,),
        in_specs=specs,
        out_specs=row(D_MODEL),
        out_shape=jax.ShapeDtypeStruct((n, D_MODEL), F32),
        compiler_params=pltpu.CompilerParams(dimension_semantics=("arbitrary",), vmem_limit_bytes=VMEM_LIMIT),
        name="ffn_mix" if mix is not None else "ffn",
    )(*args)


_IN_SEGS = (("q", 0, ATT_QK), ("k", ATT_QK, ATT_QK), ("z", 2 * ATT_QK, SSM_INNER),
            ("xbc", 2 * ATT_QK + SSM_INNER, SSM_CONV_DIM), ("dt", D_IN_PAD - DT_PAD, DT_PAD))


def _inproj_kernel(h_ref, nw_ref, win_ref, wvt_ref, q_ref, k_ref, vt_ref, z_ref, xbc_ref, dt_ref):
    u = _rmsnorm(h_ref[...], nw_ref[...]).astype(BF16)
    outs = dict(q=q_ref, k=k_ref, z=z_ref, xbc=xbc_ref, dt=dt_ref)
    for name, c0, width in _IN_SEGS:
        o_ref = outs[name]
        step = 512 if width % 512 == 0 else (256 if width % 256 == 0 else LANES)
        for s in range(0, width, step):
            r = jnp.dot(u, win_ref[:, c0 + s:c0 + s + step], preferred_element_type=F32)
            if name == "q":
                r = r * Q_SCALE
            o_ref[:, s:s + step] = r.astype(o_ref.dtype)
    nt_dims = (((1,), (1,)), ((), ()))
    ones = jnp.ones((VT_ROWS - ATT_DV, u.shape[0]), vt_ref.dtype)
    for s in range(0, ATT_V, 256):
        r = lax.dot_general(wvt_ref[s:s + 256, :], u, nt_dims, preferred_element_type=F32).astype(vt_ref.dtype)
        for hh in range(256 // ATT_DV):
            head = s // ATT_DV + hh
            vt_ref[0, head * VT_ROWS:head * VT_ROWS + ATT_DV, :] = r[hh * ATT_DV:(hh + 1) * ATT_DV]
            vt_ref[0, head * VT_ROWS + ATT_DV:(head + 1) * VT_ROWS, :] = ones


def _inproj_call(h, norm_w, win, wvt):
    n = h.shape[0]
    tm = min(ATT_TK, n)
    assert n % tm == 0
    row = lambda width: pl.BlockSpec((tm, width), lambda i: (i, 0))
    widths = (ATT_QK, ATT_QK, SSM_INNER, SSM_CONV_DIM, DT_PAD)
    dtypes = (BF16, BF16, BF16, BF16, F32)
    shapes = [jax.ShapeDtypeStruct((n, w), dt) for w, dt in zip(widths, dtypes)]
    specs = [row(w) for w in widths]
    shapes.insert(2, jax.ShapeDtypeStruct((n // tm, N_ATT_HEADS * VT_ROWS, tm), BF16))
    specs.insert(2, pl.BlockSpec((1, N_ATT_HEADS * VT_ROWS, tm), lambda i: (i, 0, 0)))
    return pl.pallas_call(
        _inproj_kernel,
        grid=(n // tm,),
        in_specs=[row(D_MODEL), _resident(norm_w.shape), _resident(win.shape), _resident(wvt.shape)],
        out_specs=specs,
        out_shape=shapes,
        compiler_params=pltpu.CompilerParams(dimension_semantics=("arbitrary",), vmem_limit_bytes=VMEM_LIMIT),
        name="inproj",
    )(h, norm_w, win, wvt)


def _t5_bias(rel, rb_ref, head):
    half = NUM_BUCKETS // 2
    max_exact = half // 2
    ret = jnp.where(rel > 0, half, 0)
    n = jnp.abs(rel)
    nf = jnp.maximum(n, 1).astype(F32)
    large = max_exact + (jnp.log(nf / max_exact) / math.log(MAX_DISTANCE / max_exact)
                         * (half - max_exact)).astype(jnp.int32)
    large = jnp.minimum(large, half - 1)
    bucket = ret + jnp.where(n < max_exact, n, large)
    val = jnp.zeros(rel.shape, F32)
    for jb in range(NUM_BUCKETS):
        val = jnp.where(bucket == jb, rb_ref[jb, head], val)
    return val * LOG2E


def _bias_kernel(rb_ref, tab_ref, mtab_ref, *, tq, tk):
    head = pl.program_id(0)
    r = tk // tq
    n_near = r + 2
    krow = lax.broadcasted_iota(jnp.int32, (tk, tq), 0)
    qcol = lax.broadcasted_iota(jnp.int32, (tk, tq), 1)
    for t in range(n_near):
        tab_ref[0, t] = _t5_bias(krow - qcol + (t - r) * tq, rb_ref, head)
    far_left = rb_ref[NUM_BUCKETS // 2 - 1, head] * LOG2E
    far_right = rb_ref[NUM_BUCKETS - 1, head] * LOG2E
    tab_ref[0, n_near] = jnp.full((tk, tq), far_left, F32)
    tab_ref[0, n_near + 1] = jnp.full((tk, tq), far_right, F32)
    mrow = lax.broadcasted_iota(jnp.int32, (N_META, tq), 0)
    mcol = lax.broadcasted_iota(jnp.int32, (N_META, tq), 1)
    mtab_ref[0, 0] = _t5_bias(mrow - N_META - mcol, rb_ref, head)
    mtab_ref[0, 1] = jnp.full((N_META, tq), far_left, F32)


def _bias_call(rel_bias, tq, tk):
    assert tk % tq == 0 and tq >= T5_BAND + 1
    nt = tk // tq + 4
    return pl.pallas_call(
        functools.partial(_bias_kernel, tq=tq, tk=tk),
        grid=(N_ATT_HEADS,),
        in_specs=[pl.BlockSpec(memory_space=pltpu.SMEM)],
        out_specs=[pl.BlockSpec((1, nt, tk, tq), lambda h: (h, 0, 0, 0)),
                   pl.BlockSpec((1, 2, N_META, tq), lambda h: (h, 0, 0, 0))],
        out_shape=[jax.ShapeDtypeStruct((N_ATT_HEADS, nt, tk, tq), F32),
                   jax.ShapeDtypeStruct((N_ATT_HEADS, 2, N_META, tq), F32)],
        compiler_params=pltpu.CompilerParams(dimension_semantics=("arbitrary",)),
        name="t5_bias",
    )(rel_bias)


def _attn_kernel(lam_ref, q_ref, k_ref, vt_ref, km_ref, vmt_ref, tab_ref, mtab_ref, sw_ref, o_ref,
                 sa_ref, sb_ref, mca_ref, mcb_ref, m_scr, acc_scr, *, tq, tk, nkc):
    qi = pl.program_id(2)
    r = tk // tq
    n_near = r + 2
    q = q_ref[...]
    lane = lax.broadcasted_iota(jnp.int32, (tq, LANES), 1)
    zero = jnp.zeros_like(q)
    q2 = jnp.concatenate([jnp.where(lane < ATT_DH, q, zero), jnp.where(lane >= ATT_DH, q, zero)], axis=0)
    nt_dims = (((1,), (1,)), ((), ()))

    def add_bias(s, b):
        return jnp.concatenate([s[:, :tq] + b, s[:, tq:] + b], axis=1)

    n_col = 2 * tq // ATT_COLS

    def produce(j, c, s_ref, mc_ref):
        cols = pl.ds(c * ATT_COLS, ATT_COLS)
        s = lax.dot_general(k_ref[pl.ds(j * tk, tk), :], q2[c * ATT_COLS:(c + 1) * ATT_COLS], nt_dims,
                            preferred_element_type=F32)
        du = j * r - qi
        idx = jnp.where(du <= -(r + 1), n_near, jnp.where(du >= 2, n_near + 1, du + r))
        s = s + tab_ref[0, idx, :, pl.ds((c * ATT_COLS) % tq, ATT_COLS)]
        s_ref[:, cols] = s
        mc_ref[:, cols] = jnp.max(s, axis=0, keepdims=True)

    def consume(s, m_cur, vt, cols, first=False):
        if first:
            m_new = m_cur
        else:
            m_prev = m_scr[:, cols]
            m_new = jnp.maximum(m_prev, m_cur)
            alpha = jnp.exp2(m_prev - m_new)
        p = jnp.exp2(s - m_new).astype(BF16)
        pv = jnp.dot(vt, p, preferred_element_type=F32)
        acc_scr[:, cols] = pv if first else alpha * acc_scr[:, cols] + pv
        m_scr[:, cols] = m_new

    sm = lax.dot_general(km_ref[...], q2, nt_dims, preferred_element_type=F32)
    sm = add_bias(sm, mtab_ref[0, jnp.minimum(qi, 1)])
    consume(sm, jnp.max(sm, axis=0, keepdims=True), vmt_ref[...], pl.ds(0, 2 * tq), first=True)

    bufs = ((sa_ref, mca_ref), (sb_ref, mcb_ref))
    for c in range(n_col):
        produce(0, c, *bufs[0])
    for j in range(nkc):
        s_ref, mc_ref = bufs[j % 2]
        for c in range(n_col):
            cols = pl.ds(c * ATT_COLS, ATT_COLS)
            if j + 1 < nkc:
                produce(j + 1, c, *bufs[(j + 1) % 2])
            consume(s_ref[:, cols], mc_ref[:, cols], vt_ref[j], cols)

    acc = acc_scr[...]
    o = acc[:ATT_DV] / acc[ATT_DV:ATT_DV + 1]
    lv = lam_ref[...]
    lam = (jnp.exp(jnp.sum(lv[0:1] * lv[1:2], axis=1, keepdims=True))
           - jnp.exp(jnp.sum(lv[2:3] * lv[3:4], axis=1, keepdims=True)) + LAM_INIT)
    out = o[:, :tq] - lam * o[:, tq:]
    ms = jnp.mean(out * out, axis=0, keepdims=True)
    out = out * lax.rsqrt(ms + EPS) * sw_ref[...] * (1.0 - LAM_INIT)
    o_ref[...] = out.T.astype(o_ref.dtype)


def _attn_call(lamv, q, k, vt, km, vmt, tab, mtab, subw_col, batch, seq, tq, tk):
    n = q.shape[0]
    assert n == batch * seq and seq % (2 * tk) == 0 and seq % tq == 0 and vt.shape[2] == tk
    nq = seq // tq
    nkc = seq // tk
    nt = tab.shape[1]
    return pl.pallas_call(
        functools.partial(_attn_kernel, tq=tq, tk=tk, nkc=nkc),
        grid=(N_ATT_HEADS, batch, nq),
        in_specs=[
            pl.BlockSpec(lamv.shape, lambda h, b, i: (0, 0)),
            pl.BlockSpec((tq, LANES), lambda h, b, i: (b * nq + i, h)),
            pl.BlockSpec((seq, LANES), lambda h, b, i: (b, h)),
            pl.BlockSpec((nkc, VT_ROWS, tk), lambda h, b, i: (b, h, 0)),
            pl.BlockSpec((N_META, LANES), lambda h, b, i: (0, h)),
            pl.BlockSpec((VT_ROWS, N_META), lambda h, b, i: (h, 0)),
            pl.BlockSpec((1, nt, tk, tq), lambda h, b, i: (h, 0, 0, 0)),
            pl.BlockSpec((1, 2, N_META, tq), lambda h, b, i: (h, 0, 0, 0)),
            pl.BlockSpec(subw_col.shape, lambda h, b, i: (0, 0)),
        ],
        out_specs=pl.BlockSpec((tq, LANES), lambda h, b, i: (b * nq + i, h)),
        out_shape=jax.ShapeDtypeStruct((n, ATT_V), BF16),
        scratch_shapes=[pltpu.VMEM((tk, 2 * tq), F32)] * 2 + [pltpu.VMEM((1, 2 * tq), F32)] * 3
        + [pltpu.VMEM((VT_ROWS, 2 * tq), F32)],
        compiler_params=pltpu.CompilerParams(dimension_semantics=("arbitrary",) * 3, vmem_limit_bytes=VMEM_LIMIT),
        name="diff_attn",
    )(lamv, q, k, vt, km, vmt, tab, mtab, subw_col)


def _split3(x):
    hi = x.astype(BF16)
    r1 = x - hi.astype(F32)
    mid = r1.astype(BF16)
    lo = (r1 - mid.astype(F32)).astype(BF16)
    return hi, mid, lo


def _cumsum_rows(a):
    rows = a.shape[0]
    r_i = lax.broadcasted_iota(jnp.int32, (rows, rows), 0)
    c_i = lax.broadcasted_iota(jnp.int32, (rows, rows), 1)
    tri = jnp.where(c_i <= r_i, 1.0, 0.0).astype(BF16)
    out = None
    for term in _split3(a):
        part = jnp.dot(tri, term, preferred_element_type=F32)
        out = part if out is None else out + part
    return out


def _expand_rows(parts, sel_ref):
    masked = []
    for w, first in parts:
        lane = lax.broadcasted_iota(jnp.int32, w.shape, 1)
        masked.append(jnp.where((lane >= first) & (lane < first + SSM_HEADS), w, 0.0))
    stacked = jnp.concatenate(masked, axis=0)
    hi = stacked.astype(BF16)
    lo = (stacked - hi.astype(F32)).astype(BF16)
    sel = sel_ref[...]
    full = jnp.dot(hi, sel, preferred_element_type=F32) + jnp.dot(lo, sel, preferred_element_type=F32)
    outs, r0 = [], 0
    for w, _ in parts:
        outs.append(full[r0:r0 + w.shape[0]])
        r0 += w.shape[0]
    return outs


def _softplus(x):
    return jnp.maximum(x, 0.0) + jnp.log(1.0 + jnp.exp(-jnp.abs(x)))


GROUP_COLS = SSM_INNER // SSM_GROUPS


def _state_update(b_t, xw):
    return jnp.concatenate(
        [jnp.dot(b_t[g * SSM_STATE:(g + 1) * SSM_STATE], xw[:, g * GROUP_COLS:(g + 1) * GROUP_COLS],
                 preferred_element_type=F32) for g in range(SSM_GROUPS)], axis=0)


def _stack_decay(dec_row):
    return jnp.concatenate(
        [jnp.broadcast_to(dec_row[:, g * GROUP_COLS:(g + 1) * GROUP_COLS], (SSM_STATE, GROUP_COLS))
         for g in range(SSM_GROUPS)], axis=0)


def _conv_silu(win_ref, cw_ref, cb_ref, rows):
    win = win_ref[0:rows + 2 * HALO, :]
    total = rows + 2 * HALO
    acc = jnp.broadcast_to(cb_ref[...], (rows, SSM_CONV_DIM))
    for j in range(SSM_CONV):
        off = j - SSM_CONV // 2
        shifted = win if off == 0 else pltpu.roll(win, (total - off) % total, axis=0)
        acc = acc + cw_ref[j:j + 1, :] * shifted[HALO:HALO + rows]
    return acc * jax.nn.sigmoid(acc)


def _ssd_kernel(z_ref, xc_ref, xl_ref, xr_ref, dt_ref, mx_ref, mdt_ref, cw_ref, cb_ref, dtb_ref, alog_ref,
                dsk_ref, nw_ref, sel_ref, o_ref, xs_scr, dts_scr, cum_scr, hbs_scr, hf_scr, hb_scr, win_scr,
                *, cs, sub, nb):
    rows = cs * sub
    ph = pl.program_id(1)
    t = pl.program_id(2)
    fwd0, bwd0 = 0, SSM_HEADS
    a_row = -jnp.exp(alog_ref[...])

    def decay_terms(dt_raw):
        dt = _softplus(dt_raw + dtb_ref[...])
        return dt, _cumsum_rows(dt * a_row)

    def bcast8(row):
        return jnp.broadcast_to(row, (8, LANES))

    @pl.when(ph == 0)
    def _():
        blk = nb - 1 - t

        @pl.when(t == 0)
        def _():
            hb_scr[...] = jnp.zeros(hb_scr.shape, F32)

        left = jnp.where(blk == 0, mx_ref[...], xl_ref[...])
        right = jnp.where(blk == nb - 1, jnp.zeros_like(xr_ref[...]), xr_ref[...])
        win_scr[0:HALO, :] = left.astype(F32)
        win_scr[HALO:HALO + rows, :] = xc_ref[...].astype(F32)
        win_scr[HALO + rows:HALO + rows + HALO, :] = right.astype(F32)

        xbc_all = _conv_silu(win_scr, cw_ref, cb_ref, rows)
        hb = hb_scr[...]
        for si in reversed(range(sub)):
            cc = blk * sub + si
            xbc = xbc_all[si * cs:(si + 1) * cs]
            xs_scr[cc] = xbc.astype(BF16)
            dt, cum = decay_terms(dt_ref[si * cs:(si + 1) * cs, :])
            dts_scr[cc] = dt
            cum_scr[cc] = cum
            eb = cum - dt * a_row
            w_b, dec = _expand_rows([(jnp.exp(eb) * dt, bwd0), (bcast8(jnp.exp(cum[cs - 1:cs, :])), bwd0)],
                                    sel_ref)
            xw = (xbc[:, :SSM_INNER] * w_b).astype(BF16)
            bm_t = xbc[:, SSM_INNER:SSM_INNER + LANES].T.astype(BF16)
            hbs_scr[cc] = hb.astype(BF16)
            hb = hb * _stack_decay(dec[0:1]) + _state_update(bm_t, xw)
        hb_scr[...] = hb

    @pl.when(ph == 1)
    def _():
        @pl.when(t == 0)
        def _():
            win_scr[0:HALO, :] = jnp.zeros((HALO, SSM_CONV_DIM), F32)
            win_scr[HALO:2 * HALO, :] = mx_ref[...].astype(F32)
            win_scr[2 * HALO:3 * HALO, :] = xc_ref[0:HALO, :].astype(F32)
            xm = _conv_silu(win_scr, cw_ref, cb_ref, N_META)
            dtm, cumm = decay_terms(mdt_ref[...])
            (w_m,) = _expand_rows([(jnp.exp(cumm[N_META - 1:N_META, :] - cumm) * dtm, fwd0)], sel_ref)
            xwm = (xm[:, :SSM_INNER] * w_m).astype(BF16)
            bmm_t = xm[:, SSM_INNER:SSM_INNER + LANES].T.astype(BF16)
            hf_scr[...] = _state_update(bmm_t, xwm)

        lane = lax.broadcasted_iota(jnp.int32, (cs, LANES), 1)
        l_i = lax.broadcasted_iota(jnp.int32, (cs, cs), 0)
        s_i = lax.broadcasted_iota(jnp.int32, (cs, cs), 1)
        lower = s_i <= l_i
        diag = s_i == l_i
        hpg = SSM_HEADS // SSM_GROUPS
        zx = jnp.zeros((cs, LANES), BF16)
        nt_dims = (((1,), (1,)), ((), ()))

        hf = hf_scr[...]
        for si in range(sub):
            cc = t * sub + si
            xbc = xs_scr[cc]
            x_bf = xbc[:, :SSM_INNER]
            bm = xbc[:, SSM_INNER:SSM_INNER + LANES]
            cm = xbc[:, SSM_INNER + LANES:SSM_INNER + 2 * LANES]
            x = x_bf.astype(F32)

            dt = dts_scr[cc]
            cum = cum_scr[cc]
            eb = cum - dt * a_row
            dt_t, cum_t, eb_t = dt.T, cum.T, eb.T
            last = cum[cs - 1:cs, :]

            c_grp = [jnp.where(lane // SSM_STATE == g, cm, jnp.zeros_like(cm)) for g in range(SSM_GROUPS)]
            g_mats = [lax.dot_general(c_g, bm, nt_dims, preferred_element_type=F32) for c_g in c_grp]

            pieces = []
            for hp in range(SSM_HEADS // 2):
                w_pair = []
                for h in (2 * hp, 2 * hp + 1):
                    arg_f = cum[:, fwd0 + h:fwd0 + h + 1] - cum_t[fwd0 + h:fwd0 + h + 1, :]
                    arg_b = eb_t[bwd0 + h:bwd0 + h + 1, :] - eb[:, bwd0 + h:bwd0 + h + 1]
                    e = jnp.exp(jnp.minimum(jnp.where(lower, arg_f, arg_b), 0.0))
                    dt_f_row = dt_t[fwd0 + h:fwd0 + h + 1, :]
                    dt_b_row = dt_t[bwd0 + h:bwd0 + h + 1, :]
                    m = e * jnp.where(lower, dt_f_row, dt_b_row) + jnp.where(diag, dt_b_row, 0.0)
                    w_pair.append((g_mats[h // hpg] * m).astype(BF16))
                xp = x_bf[:, hp * LANES:(hp + 1) * LANES]
                rhs = jnp.concatenate([jnp.where(lane < SSM_HEADDIM, xp, zx),
                                       jnp.where(lane >= SSM_HEADDIM, xp, zx)], axis=0)
                pieces.append(jnp.dot(jnp.concatenate(w_pair, axis=1), rhs, preferred_element_type=F32))
            y = jnp.concatenate(pieces, axis=1)

            d_f, d_b, w_f, dec = _expand_rows(
                [(jnp.exp(cum), fwd0), (jnp.exp(last - eb), bwd0), (jnp.exp(last - cum) * dt, fwd0),
                 (bcast8(jnp.exp(last)), fwd0)], sel_ref)
            hf_bf = hf.astype(BF16)
            hb_bf = hbs_scr[cc]
            y = y + d_f * jnp.concatenate([jnp.dot(c_g, hf_bf, preferred_element_type=F32) for c_g in c_grp],
                                          axis=1)
            y = y + d_b * jnp.concatenate([jnp.dot(c_g, hb_bf, preferred_element_type=F32) for c_g in c_grp],
                                          axis=1)
            y = y + x * dsk_ref[...]

            xw = (x * w_f).astype(BF16)
            hf = hf * _stack_decay(dec[0:1]) + _state_update(bm.astype(F32).T.astype(BF16), xw)

            zf = z_ref[si * cs:(si + 1) * cs, :].astype(F32)
            y = y * (zf * jax.nn.sigmoid(zf))
            o_ref[si * cs:(si + 1) * cs, :] = _rmsnorm(y, nw_ref[...]).astype(o_ref.dtype)
        hf_scr[...] = hf


def _ssd_call(z, xbc, dt, mxbc, mdt, cw, cb, dtb, alog, dskip, nw, sel, batch, seq, cs, sub):
    n = z.shape[0]
    rows = cs * sub
    assert n == batch * seq and seq % rows == 0 and cs % HALO == 0
    nc = seq // cs
    nb = seq // rows
    hpb = rows // HALO
    n_halo = n // HALO

    def ph0_block(ph, t):
        return (1 - ph) * (nb - 1 - t)

    const2 = lambda shape: pl.BlockSpec(shape, lambda b, ph, t: (0, 0))
    return pl.pallas_call(
        functools.partial(_ssd_kernel, cs=cs, sub=sub, nb=nb),
        grid=(batch, 2, nb),
        in_specs=[
            pl.BlockSpec((rows, SSM_INNER), lambda b, ph, t: (b * nb + ph * t, 0)),
            pl.BlockSpec((rows, SSM_CONV_DIM), lambda b, ph, t: (b * nb + ph0_block(ph, t), 0)),
            pl.BlockSpec((HALO, SSM_CONV_DIM),
                         lambda b, ph, t: (jnp.maximum((b * nb + ph0_block(ph, t)) * hpb - 1, 0), 0)),
            pl.BlockSpec((HALO, SSM_CONV_DIM),
                         lambda b, ph, t: (jnp.minimum((b * nb + ph0_block(ph, t) + 1) * hpb, n_halo - 1), 0)),
            pl.BlockSpec((rows, DT_PAD), lambda b, ph, t: (b * nb + ph0_block(ph, t), 0)),
            const2(mxbc.shape), const2(mdt.shape), const2(cw.shape), const2(cb.shape), const2(dtb.shape),
            const2(alog.shape), const2(dskip.shape), const2(nw.shape), const2(sel.shape),
        ],
        out_specs=pl.BlockSpec((rows, SSM_INNER), lambda b, ph, t: (b * nb + ph * t, 0)),
        out_shape=jax.ShapeDtypeStruct((n, SSM_INNER), BF16),
        scratch_shapes=[
            pltpu.VMEM((nc, cs, SSM_CONV_DIM), BF16),
            pltpu.VMEM((nc, cs, DT_PAD), F32),
            pltpu.VMEM((nc, cs, DT_PAD), F32),
            pltpu.VMEM((nc, LANES, GROUP_COLS), BF16),
            pltpu.VMEM((LANES, GROUP_COLS), F32),
            pltpu.VMEM((LANES, GROUP_COLS), F32),
            pltpu.VMEM((rows + 2 * HALO, SSM_CONV_DIM), F32),
        ],
        compiler_params=pltpu.CompilerParams(dimension_semantics=("arbitrary",) * 3, vmem_limit_bytes=VMEM_LIMIT),
        name="bi_ssd",
    )(z, xbc, xbc, xbc, dt, mxbc, mdt, cw, cb, dtb, alog, dskip, nw, sel)


def _head_selector():
    k = jnp.arange(LANES)[:, None]
    col = jnp.arange(SSM_INNER)[None, :]
    return ((k % SSM_HEADS == col // SSM_HEADDIM) & (k < 2 * SSM_HEADS)).astype(BF16)


def _prep_weights(ffn1_norm_w, ffn1_w_gate, ffn1_w_up, ffn1_w_down, mix_norm_w, w_in, lambda_q1, lambda_k1,
                  lambda_q2, lambda_k2, attn_subln_w, conv_w, conv_b, dt_bias_fwd, dt_bias_bwd, a_log_fwd,
                  a_log_bwd, ssm_d, ssm_norm_w, w_out, ffn2_norm_w, ffn2_w_gate, ffn2_w_up, ffn2_w_down,
                  final_norm_w):
    def ffn(norm_w, wg, wu, wd):
        chunk_cols = lambda w: w[0].astype(BF16).reshape(D_MODEL, N_FF, FF_TILE).transpose(1, 0, 2)
        return (norm_w[0][None, :], chunk_cols(wg), chunk_cols(wu),
                wd[0].astype(BF16).reshape(N_FF, FF_TILE, D_MODEL))

    pad_lanes = lambda v, width: jnp.pad(v, (0, width - v.shape[0]))[None, :]
    o_v, o_z = 2 * ATT_QK, 2 * ATT_QK + ATT_V
    wi = w_in[0].astype(BF16)
    win = jnp.pad(jnp.concatenate([wi[:, :o_v], wi[:, o_z:]], axis=1),
                  ((0, 0), (0, D_IN_PAD - (w_in.shape[2] - ATT_V))))
    return dict(
        ffn1=ffn(ffn1_norm_w, ffn1_w_gate, ffn1_w_up, ffn1_w_down),
        ffn2=ffn(ffn2_norm_w, ffn2_w_gate, ffn2_w_up, ffn2_w_down),
        mix_norm=mix_norm_w[0][None, :],
        win=win,
        wvt=wi[:, o_v:o_z].T,
        lamv=jnp.stack([lambda_q1[0], lambda_k1[0], lambda_q2[0], lambda_k2[0]]),
        subw_col=attn_subln_w[0][:, None],
        cw=jnp.pad(conv_w[0], ((0, 8 - SSM_CONV), (0, 0))),
        cb=conv_b[0][None, :],
        dtb=pad_lanes(jnp.concatenate([dt_bias_fwd[0], dt_bias_bwd[0]]), DT_PAD),
        alog=pad_lanes(jnp.concatenate([a_log_fwd[0], a_log_bwd[0]]), DT_PAD),
        dskip=jnp.repeat(ssm_d[0], SSM_HEADDIM)[None, :],
        ssm_norm=ssm_norm_w[0][None, :],
        wo=w_out[0].astype(BF16).reshape(2, ATT_V, D_MODEL),
        final=final_norm_w[None, :],
        sel=_head_selector(),
    )


def _encode(x, w, meta_proj, tab, mtab):
    batch, seq, _ = x.shape
    km, vmt, mxbc, mdt = meta_proj
    h0 = x.reshape(batch * seq, D_MODEL)
    h1 = _ffn_call(h0, *w["ffn1"])
    q, k, vt, z, xbc, dt = _inproj_call(h1, w["mix_norm"], w["win"], w["wvt"])
    att = _attn_call(w["lamv"], q, k, vt, km, vmt, tab, mtab, w["subw_col"], batch, seq, ATT_TQ, ATT_TK)
    ssm = _ssd_call(z, xbc, dt, mxbc, mdt, w["cw"], w["cb"], w["dtb"], w["alog"], w["dskip"], w["ssm_norm"],
                    w["sel"], batch, seq, SSD_CHUNK, SSD_SUB)
    y = _ffn_call(h1, *w["ffn2"], mix=(att, ssm, w["wo"]), final_w=w["final"])
    return y.reshape(batch, seq, D_MODEL)


def kernel(x_prompt, x_sample, meta_tokens, ffn1_norm_w, ffn1_w_gate, ffn1_w_up, ffn1_w_down, mix_norm_w, w_in, rel_bias, lambda_q1, lambda_k1, lambda_q2, lambda_k2, attn_subln_w, conv_w, conv_b, dt_bias_fwd, dt_bias_bwd, a_log_fwd, a_log_bwd, ssm_d, ssm_norm_w, w_out, ffn2_norm_w, ffn2_w_gate, ffn2_w_up, ffn2_w_down, final_norm_w):
    w = _prep_weights(ffn1_norm_w, ffn1_w_gate, ffn1_w_up, ffn1_w_down, mix_norm_w, w_in, lambda_q1, lambda_k1,
                      lambda_q2, lambda_k2, attn_subln_w, conv_w, conv_b, dt_bias_fwd, dt_bias_bwd, a_log_fwd,
                      a_log_bwd, ssm_d, ssm_norm_w, w_out, ffn2_norm_w, ffn2_w_gate, ffn2_w_up, ffn2_w_down,
                      final_norm_w)
    hm = _ffn_call(meta_tokens, *w["ffn1"])
    _, km, vmt, _, mxbc, mdt = _inproj_call(hm, w["mix_norm"], w["win"], w["wvt"])
    meta_proj = (km, vmt[0], mxbc, mdt)
    tab, mtab = _bias_call(rel_bias, ATT_TQ, ATT_TK)
    return (_encode(x_prompt, w, meta_proj, tab, mtab), _encode(x_sample, w, meta_proj, tab, mtab))
```

```python
import functools
import math

import jax
import jax.numpy as jnp
from jax import lax
from jax.experimental import pallas as pl
from jax.experimental.pallas import tpu as pltpu

F32 = jnp.float32
BF16 = jnp.bfloat16

D_MODEL = 1024
N_META = 16
N_ATT_HEADS = 8
ATT_DH = 64
ATT_DV = 128
ATT_QK = 1024
ATT_V = 1024
NUM_BUCKETS = 32
MAX_DISTANCE = 128
SSM_HEADS = 16
SSM_HEADDIM = 64
SSM_INNER = 1024
SSM_GROUPS = 2
SSM_STATE = 64
SSM_CONV = 7
SSM_CONV_DIM = 1280
D_FF = 2816
EPS = 1e-6
LAYER = 0
LAM_INIT = 0.8 - 0.6 * math.exp(-0.3 * LAYER)
LOG2E = math.log2(math.e)
Q_SCALE = ATT_DH ** -0.5 * LOG2E
NEG_BIG = -1e30

LANES = 128
BF16_ROWS = 16
VMEM_LIMIT = 56 * 1024 * 1024

FF_TILE = 256
N_FF = D_FF // FF_TILE
DT_PAD = LANES
D_IN_PAD = 2 * ATT_QK + SSM_INNER + SSM_CONV_DIM + DT_PAD
T5_BAND = 91

ROW_TILE = 512
ATT_TQ = 512
ATT_TK = 512
ATT_COLS = 256
SSD_CHUNK = 128
SSD_SUB = 4
HALO = BF16_ROWS
VT_ROWS = ATT_DV + BF16_ROWS


def _rmsnorm(x, w):
    ms = jnp.mean(x * x, axis=-1, keepdims=True)
    return x * lax.rsqrt(ms + EPS) * w


def _resident(shape):
    nd = len(shape)
    return pl.BlockSpec(shape, lambda *_: (0,) * nd, pipeline_mode=pl.Buffered(1))


def _ffn_kernel(*refs, has_mix, has_final):
    it = iter(refs)
    h_ref = next(it)
    if has_mix:
        att_ref, ssm_ref, wo_ref = next(it), next(it), next(it)
    nw_ref, wg_ref, wu_ref, wd_ref = next(it), next(it), next(it), next(it)
    fw_ref = next(it) if has_final else None
    o_ref = next(it)

    h = h_ref[...]
    if has_mix:
        h = (h + jnp.dot(att_ref[...], wo_ref[0], preferred_element_type=F32)
             + jnp.dot(ssm_ref[...], wo_ref[1], preferred_element_type=F32))
    u = _rmsnorm(h, nw_ref[...]).astype(BF16)
    acc = jnp.zeros_like(h)
    for j in range(N_FF):
        ff = slice(j * FF_TILE, (j + 1) * FF_TILE)
        g = jnp.dot(u, wg_ref[:, ff], preferred_element_type=F32)
        up = jnp.dot(u, wu_ref[:, ff], preferred_element_type=F32)
        a = (g * jax.nn.sigmoid(g) * up).astype(BF16)
        acc = acc + jnp.dot(a, wd_ref[ff, :], preferred_element_type=F32)
    h = h + 0.5 * acc
    if has_final:
        h = _rmsnorm(h, fw_ref[...])
    o_ref[...] = h


def _ffn_call(h, norm_w, wg, wu, wd, mix=None, final_w=None):
    n = h.shape[0]
    tm = min(ROW_TILE, n)
    assert n % tm == 0
    row = lambda width: pl.BlockSpec((tm, width), lambda i: (i, 0))
    args, specs = [h], [row(D_MODEL)]
    if mix is not None:
        att, ssm, wo = mix
        args += [att, ssm, wo]
        specs += [row(ATT_V), row(SSM_INNER), _resident(wo.shape)]
    args += [norm_w, wg, wu, wd]
    specs += [_resident(norm_w.shape), _resident(wg.shape), _resident(wu.shape), _resident(wd.shape)]
    if final_w is not None:
        args.append(final_w)
        specs.append(_resident(final_w.shape))
    return pl.pallas_call(
        functools.partial(_ffn_kernel, has_mix=mix is not None, has_final=final_w is not None),
        grid=(n // tm,),
        in_specs=specs,
        out_specs=row(D_MODEL),
        out_shape=jax.ShapeDtypeStruct((n, D_MODEL), F32),
        compiler_params=pltpu.CompilerParams(dimension_semantics=("arbitrary",), vmem_limit_bytes=VMEM_LIMIT),
        name="ffn_mix" if mix is not None else "ffn",
    )(*args)


_IN_SEGS = (("q", 0, ATT_QK), ("k", ATT_QK, ATT_QK), ("z", 2 * ATT_QK, SSM_INNER),
            ("xbc", 2 * ATT_QK + SSM_INNER, SSM_CONV_DIM), ("dt", D_IN_PAD - DT_PAD, DT_PAD))


def _inproj_kernel(h_ref, nw_ref, win_ref, wvt_ref, q_ref, k_ref, vt_ref, z_ref, xbc_ref, dt_ref):
    u = _rmsnorm(h_ref[...], nw_ref[...]).astype(BF16)
    outs = dict(q=q_ref, k=k_ref, z=z_ref, xbc=xbc_ref, dt=dt_ref)
    for name, c0, width in _IN_SEGS:
        o_ref = outs[name]
        step = 512 if width % 512 == 0 else (256 if width % 256 == 0 else LANES)
        for s in range(0, width, step):
            r = jnp.dot(u, win_ref[:, c0 + s:c0 + s + step], preferred_element_type=F32)
            if name == "q":
                r = r * Q_SCALE
            o_ref[:, s:s + step] = r.astype(o_ref.dtype)
    nt_dims = (((1,), (1,)), ((), ()))
    ones = jnp.ones((VT_ROWS - ATT_DV, u.shape[0]), vt_ref.dtype)
    for s in range(0, ATT_V, 256):
        r = lax.dot_general(wvt_ref[s:s + 256, :], u, nt_dims, preferred_element_type=F32).astype(vt_ref.dtype)
        for hh in range(256 // ATT_DV):
            head = s // ATT_DV + hh
            vt_ref[0, head * VT_ROWS:head * VT_ROWS + ATT_DV, :] = r[hh * ATT_DV:(hh + 1) * ATT_DV]
            vt_ref[0, head * VT_ROWS + ATT_DV:(head + 1) * VT_ROWS, :] = ones


def _inproj_call(h, norm_w, win, wvt):
    n = h.shape[0]
    tm = min(ATT_TK, n)
    assert n % tm == 0
    row = lambda width: pl.BlockSpec((tm, width), lambda i: (i, 0))
    widths = (ATT_QK, ATT_QK, SSM_INNER, SSM_CONV_DIM, DT_PAD)
    dtypes = (BF16, BF16, BF16, BF16, F32)
    shapes = [jax.ShapeDtypeStruct((n, w), dt) for w, dt in zip(widths, dtypes)]
    specs = [row(w) for w in widths]
    shapes.insert(2, jax.ShapeDtypeStruct((n // tm, N_ATT_HEADS * VT_ROWS, tm), BF16))
    specs.insert(2, pl.BlockSpec((1, N_ATT_HEADS * VT_ROWS, tm), lambda i: (i, 0, 0)))
    return pl.pallas_call(
        _inproj_kernel,
        grid=(n // tm,),
        in_specs=[row(D_MODEL), _resident(norm_w.shape), _resident(win.shape), _resident(wvt.shape)],
        out_specs=specs,
        out_shape=shapes,
        compiler_params=pltpu.CompilerParams(dimension_semantics=("arbitrary",), vmem_limit_bytes=VMEM_LIMIT),
        name="inproj",
    )(h, norm_w, win, wvt)


def _t5_bias(rel, rb_ref, head):
    half = NUM_BUCKETS // 2
    max_exact = half // 2
    ret = jnp.where(rel > 0, half, 0)
    n = jnp.abs(rel)
    nf = jnp.maximum(n, 1).astype(F32)
    large = max_exact + (jnp.log(nf / max_exact) / math.log(MAX_DISTANCE / max_exact)
                         * (half - max_exact)).astype(jnp.int32)
    large = jnp.minimum(large, half - 1)
    bucket = ret + jnp.where(n < max_exact, n, large)
    val = jnp.zeros(rel.shape, F32)
    for jb in range(NUM_BUCKETS):
        val = jnp.where(bucket == jb, rb_ref[jb, head], val)
    return val * LOG2E


def _bias_kernel(rb_ref, tab_ref, mtab_ref, *, tq, tk):
    head = pl.program_id(0)
    r = tk // tq
    n_near = r + 2
    krow = lax.broadcasted_iota(jnp.int32, (tk, tq), 0)
    qcol = lax.broadcasted_iota(jnp.int32, (tk, tq), 1)
    for t in range(n_near):
        tab_ref[0, t] = _t5_bias(krow - qcol + (t - r) * tq, rb_ref, head)
    far_left = rb_ref[NUM_BUCKETS // 2 - 1, head] * LOG2E
    far_right = rb_ref[NUM_BUCKETS - 1, head] * LOG2E
    tab_ref[0, n_near] = jnp.full((tk, tq), far_left, F32)
    tab_ref[0, n_near + 1] = jnp.full((tk, tq), far_right, F32)
    mrow = lax.broadcasted_iota(jnp.int32, (N_META, tq), 0)
    mcol = lax.broadcasted_iota(jnp.int32, (N_META, tq), 1)
    mtab_ref[0, 0] = _t5_bias(mrow - N_META - mcol, rb_ref, head)
    mtab_ref[0, 1] = jnp.full((N_META, tq), far_left, F32)


def _bias_call(rel_bias, tq, tk):
    assert tk % tq == 0 and tq >= T5_BAND + 1
    nt = tk // tq + 4
    return pl.pallas_call(
        functools.partial(_bias_kernel, tq=tq, tk=tk),
        grid=(N_ATT_HEADS,),
        in_specs=[pl.BlockSpec(memory_space=pltpu.SMEM)],
        out_specs=[pl.BlockSpec((1, nt, tk, tq), lambda h: (h, 0, 0, 0)),
                   pl.BlockSpec((1, 2, N_META, tq), lambda h: (h, 0, 0, 0))],
        out_shape=[jax.ShapeDtypeStruct((N_ATT_HEADS, nt, tk, tq), F32),
                   jax.ShapeDtypeStruct((N_ATT_HEADS, 2, N_META, tq), F32)],
        compiler_params=pltpu.CompilerParams(dimension_semantics=("arbitrary",)),
        name="t5_bias",
    )(rel_bias)


def _attn_kernel(lam_ref, q_ref, k_ref, vt_ref, km_ref, vmt_ref, tab_ref, mtab_ref, sw_ref, o_ref,
                 sa_ref, sb_ref, mca_ref, mcb_ref, m_scr, acc_scr, *, tq, tk, nkc):
    qi = pl.program_id(2)
    r = tk // tq
    n_near = r + 2
    q = q_ref[...]
    lane = lax.broadcasted_iota(jnp.int32, (tq, LANES), 1)
    zero = jnp.zeros_like(q)
    q2 = jnp.concatenate([jnp.where(lane < ATT_DH, q, zero), jnp.where(lane >= ATT_DH, q, zero)], axis=0)
    nt_dims = (((1,), (1,)), ((), ()))

    def add_bias(s, b):
        return jnp.concatenate([s[:, :tq] + b, s[:, tq:] + b], axis=1)

    n_col = 2 * tq // ATT_COLS

    def produce(j, c, s_ref, mc_ref):
        cols = pl.ds(c * ATT_COLS, ATT_COLS)
        s = lax.dot_general(k_ref[pl.ds(j * tk, tk), :], q2[c * ATT_COLS:(c + 1) * ATT_COLS], nt_dims,
                            preferred_element_type=F32)
        du = j * r - qi
        idx = jnp.where(du <= -(r + 1), n_near, jnp.where(du >= 2, n_near + 1, du + r))
        s = (s + tab_ref[0, idx, :, pl.ds((c * ATT_COLS) % tq, ATT_COLS)]).astype(BF16)
        s_ref[:, cols] = s
        mc_ref[:, cols] = jnp.max(s, axis=0, keepdims=True).astype(F32)

    def consume(s, m_cur, vt, cols, first=False):
        if first:
            m_new = m_cur
        else:
            m_prev = m_scr[:, cols]
            m_new = jnp.maximum(m_prev, m_cur)
            alpha = jnp.exp2(m_prev - m_new)
        p = jnp.exp2(s.astype(BF16) - m_new.astype(BF16))
        pv = jnp.dot(vt, p, preferred_element_type=F32)
        acc_scr[:, cols] = pv if first else alpha * acc_scr[:, cols] + pv
        m_scr[:, cols] = m_new

    sm = lax.dot_general(km_ref[...], q2, nt_dims, preferred_element_type=F32)
    sm = add_bias(sm, mtab_ref[0, jnp.minimum(qi, 1)])
    consume(sm, jnp.max(sm, axis=0, keepdims=True), vmt_ref[...], pl.ds(0, 2 * tq), first=True)

    bufs = ((sa_ref, mca_ref), (sb_ref, mcb_ref))
    for c in range(n_col):
        produce(0, c, *bufs[0])
    for j in range(nkc):
        s_ref, mc_ref = bufs[j % 2]
        for c in range(n_col):
            cols = pl.ds(c * ATT_COLS, ATT_COLS)
            if j + 1 < nkc:
                produce(j + 1, c, *bufs[(j + 1) % 2])
            consume(s_ref[:, cols], mc_ref[:, cols], vt_ref[j], cols)

    acc = acc_scr[...]
    o = acc[:ATT_DV] / acc[ATT_DV:ATT_DV + 1]
    lv = lam_ref[...]
    lam = (jnp.exp(jnp.sum(lv[0:1] * lv[1:2], axis=1, keepdims=True))
           - jnp.exp(jnp.sum(lv[2:3] * lv[3:4], axis=1, keepdims=True)) + LAM_INIT)
    out = o[:, :tq] - lam * o[:, tq:]
    ms = jnp.mean(out * out, axis=0, keepdims=True)
    out = out * lax.rsqrt(ms + EPS) * sw_ref[...] * (1.0 - LAM_INIT)
    o_ref[...] = out.T.astype(o_ref.dtype)


def _attn_call(lamv, q, k, vt, km, vmt, tab, mtab, subw_col, batch, seq, tq, tk):
    n = q.shape[0]
    assert n == batch * seq and seq % (2 * tk) == 0 and seq % tq == 0 and vt.shape[2] == tk
    nq = seq // tq
    nkc = seq // tk
    nt = tab.shape[1]
    return pl.pallas_call(
        functools.partial(_attn_kernel, tq=tq, tk=tk, nkc=nkc),
        grid=(N_ATT_HEADS, batch, nq),
        in_specs=[
            pl.BlockSpec(lamv.shape, lambda h, b, i: (0, 0)),
            pl.BlockSpec((tq, LANES), lambda h, b, i: (b * nq + i, h)),
            pl.BlockSpec((seq, LANES), lambda h, b, i: (b, h)),
            pl.BlockSpec((nkc, VT_ROWS, tk), lambda h, b, i: (b, h, 0)),
            pl.BlockSpec((N_META, LANES), lambda h, b, i: (0, h)),
            pl.BlockSpec((VT_ROWS, N_META), lambda h, b, i: (h, 0)),
            pl.BlockSpec((1, nt, tk, tq), lambda h, b, i: (h, 0, 0, 0)),
            pl.BlockSpec((1, 2, N_META, tq), lambda h, b, i: (h, 0, 0, 0)),
            pl.BlockSpec(subw_col.shape, lambda h, b, i: (0, 0)),
        ],
        out_specs=pl.BlockSpec((tq, LANES), lambda h, b, i: (b * nq + i, h)),
        out_shape=jax.ShapeDtypeStruct((n, ATT_V), BF16),
        scratch_shapes=[pltpu.VMEM((tk, 2 * tq), BF16)] * 2 + [pltpu.VMEM((1, 2 * tq), F32)] * 3
        + [pltpu.VMEM((VT_ROWS, 2 * tq), F32)],
        compiler_params=pltpu.CompilerParams(dimension_semantics=("arbitrary",) * 3, vmem_limit_bytes=VMEM_LIMIT),
        name="diff_attn",
    )(lamv, q, k, vt, km, vmt, tab, mtab, subw_col)


def _split3(x):
    hi = x.astype(BF16)
    r1 = x - hi.astype(F32)
    mid = r1.astype(BF16)
    lo = (r1 - mid.astype(F32)).astype(BF16)
    return hi, mid, lo


def _cumsum_rows(a):
    rows = a.shape[0]
    r_i = lax.broadcasted_iota(jnp.int32, (rows, rows), 0)
    c_i = lax.broadcasted_iota(jnp.int32, (rows, rows), 1)
    tri = jnp.where(c_i <= r_i, 1.0, 0.0).astype(BF16)
    out = None
    for term in _split3(a):
        part = jnp.dot(tri, term, preferred_element_type=F32)
        out = part if out is None else out + part
    return out


def _expand_rows(parts, sel_ref):
    masked = []
    for w, first in parts:
        lane = lax.broadcasted_iota(jnp.int32, w.shape, 1)
        masked.append(jnp.where((lane >= first) & (lane < first + SSM_HEADS), w, 0.0))
    stacked = jnp.concatenate(masked, axis=0)
    hi = stacked.astype(BF16)
    lo = (stacked - hi.astype(F32)).astype(BF16)
    sel = sel_ref[...]
    full = jnp.dot(hi, sel, preferred_element_type=F32) + jnp.dot(lo, sel, preferred_element_type=F32)
    outs, r0 = [], 0
    for w, _ in parts:
        outs.append(full[r0:r0 + w.shape[0]])
        r0 += w.shape[0]
    return outs


def _softplus(x):
    return jnp.maximum(x, 0.0) + jnp.log(1.0 + jnp.exp(-jnp.abs(x)))


GROUP_COLS = SSM_INNER // SSM_GROUPS


def _state_update(b_t, xw):
    return jnp.concatenate(
        [jnp.dot(b_t[g * SSM_STATE:(g + 1) * SSM_STATE], xw[:, g * GROUP_COLS:(g + 1) * GROUP_COLS],
                 preferred_element_type=F32) for g in range(SSM_GROUPS)], axis=0)


def _stack_decay(dec_row):
    return jnp.concatenate(
        [jnp.broadcast_to(dec_row[:, g * GROUP_COLS:(g + 1) * GROUP_COLS], (SSM_STATE, GROUP_COLS))
         for g in range(SSM_GROUPS)], axis=0)


def _conv_silu(win_ref, cw_ref, cb_ref, rows):
    win = win_ref[0:rows + 2 * HALO, :]
    total = rows + 2 * HALO
    acc = jnp.broadcast_to(cb_ref[...], (rows, SSM_CONV_DIM))
    for j in range(SSM_CONV):
        off = j - SSM_CONV // 2
        shifted = win if off == 0 else pltpu.roll(win, (total - off) % total, axis=0)
        acc = acc + cw_ref[j:j + 1, :] * shifted[HALO:HALO + rows]
    return acc * jax.nn.sigmoid(acc)


def _ssd_kernel(z_ref, xc_ref, xl_ref, xr_ref, dt_ref, mx_ref, mdt_ref, cw_ref, cb_ref, dtb_ref, alog_ref,
                dsk_ref, nw_ref, sel_ref, o_ref, xs_scr, dts_scr, cum_scr, hbs_scr, hf_scr, hb_scr, win_scr,
                *, cs, sub, nb):
    rows = cs * sub
    ph = pl.program_id(1)
    t = pl.program_id(2)
    fwd0, bwd0 = 0, SSM_HEADS
    a_row = -jnp.exp(alog_ref[...])

    def decay_terms(dt_raw):
        dt = _softplus(dt_raw + dtb_ref[...])
        return dt, _cumsum_rows(dt * a_row)

    def bcast8(row):
        return jnp.broadcast_to(row, (8, LANES))

    @pl.when(ph == 0)
    def _():
        blk = nb - 1 - t

        @pl.when(t == 0)
        def _():
            hb_scr[...] = jnp.zeros(hb_scr.shape, F32)

        left = jnp.where(blk == 0, mx_ref[...], xl_ref[...])
        right = jnp.where(blk == nb - 1, jnp.zeros_like(xr_ref[...]), xr_ref[...])
        win_scr[0:HALO, :] = left.astype(F32)
        win_scr[HALO:HALO + rows, :] = xc_ref[...].astype(F32)
        win_scr[HALO + rows:HALO + rows + HALO, :] = right.astype(F32)

        xbc_all = _conv_silu(win_scr, cw_ref, cb_ref, rows)
        hb = hb_scr[...]
        for si in reversed(range(sub)):
            cc = blk * sub + si
            xbc = xbc_all[si * cs:(si + 1) * cs]
            xs_scr[cc] = xbc.astype(BF16)
            dt, cum = decay_terms(dt_ref[si * cs:(si + 1) * cs, :])
            dts_scr[cc] = dt
            cum_scr[cc] = cum
            eb = cum - dt * a_row
            w_b, dec = _expand_rows([(jnp.exp(eb) * dt, bwd0), (bcast8(jnp.exp(cum[cs - 1:cs, :])), bwd0)],
                                    sel_ref)
            xw = (xbc[:, :SSM_INNER] * w_b).astype(BF16)
            bm_t = xbc[:, SSM_INNER:SSM_INNER + LANES].T.astype(BF16)
            hbs_scr[cc] = hb.astype(BF16)
            hb = hb * _stack_decay(dec[0:1]) + _state_update(bm_t, xw)
        hb_scr[...] = hb

    @pl.when(ph == 1)
    def _():
        @pl.when(t == 0)
        def _():
            win_scr[0:HALO, :] = jnp.zeros((HALO, SSM_CONV_DIM), F32)
            win_scr[HALO:2 * HALO, :] = mx_ref[...].astype(F32)
            win_scr[2 * HALO:3 * HALO, :] = xc_ref[0:HALO, :].astype(F32)
            xm = _conv_silu(win_scr, cw_ref, cb_ref, N_META)
            dtm, cumm = decay_terms(mdt_ref[...])
            (w_m,) = _expand_rows([(jnp.exp(cumm[N_META - 1:N_META, :] - cumm) * dtm, fwd0)], sel_ref)
            xwm = (xm[:, :SSM_INNER] * w_m).astype(BF16)
            bmm_t = xm[:, SSM_INNER:SSM_INNER + LANES].T.astype(BF16)
            hf_scr[...] = _state_update(bmm_t, xwm)

        lane = lax.broadcasted_iota(jnp.int32, (cs, LANES), 1)
        l_i = lax.broadcasted_iota(jnp.int32, (cs, cs), 0)
        s_i = lax.broadcasted_iota(jnp.int32, (cs, cs), 1)
        lower = s_i <= l_i
        diag = s_i == l_i
        hpg = SSM_HEADS // SSM_GROUPS
        zx = jnp.zeros((cs, LANES), BF16)
        nt_dims = (((1,), (1,)), ((), ()))

        hf = hf_scr[...]
        for si in range(sub):
            cc = t * sub + si
            xbc = xs_scr[cc]
            x_bf = xbc[:, :SSM_INNER]
            bm = xbc[:, SSM_INNER:SSM_INNER + LANES]
            cm = xbc[:, SSM_INNER + LANES:SSM_INNER + 2 * LANES]
            x = x_bf.astype(F32)

            dt = dts_scr[cc]
            cum = cum_scr[cc]
            eb = cum - dt * a_row
            dt_t, cum_t, eb_t = dt.T, cum.T, eb.T
            last = cum[cs - 1:cs, :]

            c_grp = [jnp.where(lane // SSM_STATE == g, cm, jnp.zeros_like(cm)) for g in range(SSM_GROUPS)]
            g_mats = [lax.dot_general(c_g, bm, nt_dims, preferred_element_type=F32) for c_g in c_grp]

            pieces = []
            for hp in range(SSM_HEADS // 2):
                w_pair = []
                for h in (2 * hp, 2 * hp + 1):
                    arg_f = cum[:, fwd0 + h:fwd0 + h + 1] - cum_t[fwd0 + h:fwd0 + h + 1, :]
                    arg_b = eb_t[bwd0 + h:bwd0 + h + 1, :] - eb[:, bwd0 + h:bwd0 + h + 1]
                    e = jnp.exp(jnp.minimum(jnp.where(lower, arg_f, arg_b), 0.0))
                    dt_f_row = dt_t[fwd0 + h:fwd0 + h + 1, :]
                    dt_b_row = dt_t[bwd0 + h:bwd0 + h + 1, :]
                    m = e * jnp.where(lower, dt_f_row, dt_b_row) + jnp.where(diag, dt_b_row, 0.0)
                    w_pair.append((g_mats[h // hpg] * m).astype(BF16))
                xp = x_bf[:, hp * LANES:(hp + 1) * LANES]
                rhs = jnp.concatenate([jnp.where(lane < SSM_HEADDIM, xp, zx),
                                       jnp.where(lane >= SSM_HEADDIM, xp, zx)], axis=0)
                pieces.append(jnp.dot(jnp.concatenate(w_pair, axis=1), rhs, preferred_element_type=F32))
            y = jnp.concatenate(pieces, axis=1)

            d_f, d_b, w_f, dec = _expand_rows(
                [(jnp.exp(cum), fwd0), (jnp.exp(last - eb), bwd0), (jnp.exp(last - cum) * dt, fwd0),
                 (bcast8(jnp.exp(last)), fwd0)], sel_ref)
            hf_bf = hf.astype(BF16)
            hb_bf = hbs_scr[cc]
            y = y + d_f * jnp.concatenate([jnp.dot(c_g, hf_bf, preferred_element_type=F32) for c_g in c_grp],
                                          axis=1)
            y = y + d_b * jnp.concatenate([jnp.dot(c_g, hb_bf, preferred_element_type=F32) for c_g in c_grp],
                                          axis=1)
            y = y + x * dsk_ref[...]

            xw = (x * w_f).astype(BF16)
            hf = hf * _stack_decay(dec[0:1]) + _state_update(bm.astype(F32).T.astype(BF16), xw)

            zf = z_ref[si * cs:(si + 1) * cs, :].astype(F32)
            y = y * (zf * jax.nn.sigmoid(zf))
            o_ref[si * cs:(si + 1) * cs, :] = _rmsnorm(y, nw_ref[...]).astype(o_ref.dtype)
        hf_scr[...] = hf


def _ssd_call(z, xbc, dt, mxbc, mdt, cw, cb, dtb, alog, dskip, nw, sel, batch, seq, cs, sub):
    n = z.shape[0]
    rows = cs * sub
    assert n == batch * seq and seq % rows == 0 and cs % HALO == 0
    nc = seq // cs
    nb = seq // rows
    hpb = rows // HALO
    n_halo = n // HALO

    def ph0_block(ph, t):
        return (1 - ph) * (nb - 1 - t)

    const2 = lambda shape: pl.BlockSpec(shape, lambda b, ph, t: (0, 0))
    return pl.pallas_call(
        functools.partial(_ssd_kernel, cs=cs, sub=sub, nb=nb),
        grid=(batch, 2, nb),
        in_specs=[
            pl.BlockSpec((rows, SSM_INNER), lambda b, ph, t: (b * nb + ph * t, 0)),
            pl.BlockSpec((rows, SSM_CONV_DIM), lambda b, ph, t: (b * nb + ph0_block(ph, t), 0)),
            pl.BlockSpec((HALO, SSM_CONV_DIM),
                         lambda b, ph, t: (jnp.maximum((b * nb + ph0_block(ph, t)) * hpb - 1, 0), 0)),
            pl.BlockSpec((HALO, SSM_CONV_DIM),
                         lambda b, ph, t: (jnp.minimum((b * nb + ph0_block(ph, t) + 1) * hpb, n_halo - 1), 0)),
            pl.BlockSpec((rows, DT_PAD), lambda b, ph, t: (b * nb + ph0_block(ph, t), 0)),
            const2(mxbc.shape), const2(mdt.shape), const2(cw.shape), const2(cb.shape), const2(dtb.shape),
            const2(alog.shape), const2(dskip.shape), const2(nw.shape), const2(sel.shape),
        ],
        out_specs=pl.BlockSpec((rows, SSM_INNER), lambda b, ph, t: (b * nb + ph * t, 0)),
        out_shape=jax.ShapeDtypeStruct((n, SSM_INNER), BF16),
        scratch_shapes=[
            pltpu.VMEM((nc, cs, SSM_CONV_DIM), BF16),
            pltpu.VMEM((nc, cs, DT_PAD), F32),
            pltpu.VMEM((nc, cs, DT_PAD), F32),
            pltpu.VMEM((nc, LANES, GROUP_COLS), BF16),
            pltpu.VMEM((LANES, GROUP_COLS), F32),
            pltpu.VMEM((LANES, GROUP_COLS), F32),
            pltpu.VMEM((rows + 2 * HALO, SSM_CONV_DIM), F32),
        ],
        compiler_params=pltpu.CompilerParams(dimension_semantics=("arbitrary",) * 3, vmem_limit_bytes=VMEM_LIMIT),
        name="bi_ssd",
    )(z, xbc, xbc, xbc, dt, mxbc, mdt, cw, cb, dtb, alog, dskip, nw, sel)


def _head_selector():
    k = jnp.arange(LANES)[:, None]
    col = jnp.arange(SSM_INNER)[None, :]
    return ((k % SSM_HEADS == col // SSM_HEADDIM) & (k < 2 * SSM_HEADS)).astype(BF16)


def _prep_weights(ffn1_norm_w, ffn1_w_gate, ffn1_w_up, ffn1_w_down, mix_norm_w, w_in, lambda_q1, lambda_k1,
                  lambda_q2, lambda_k2, attn_subln_w, conv_w, conv_b, dt_bias_fwd, dt_bias_bwd, a_log_fwd,
                  a_log_bwd, ssm_d, ssm_norm_w, w_out, ffn2_norm_w, ffn2_w_gate, ffn2_w_up, ffn2_w_down,
                  final_norm_w):
    def ffn(norm_w, wg, wu, wd):
        return norm_w[0][None, :], wg[0].astype(BF16), wu[0].astype(BF16), wd[0].astype(BF16)

    pad_lanes = lambda v, width: jnp.pad(v, (0, width - v.shape[0]))[None, :]
    o_v, o_z = 2 * ATT_QK, 2 * ATT_QK + ATT_V
    wi = w_in[0].astype(BF16)
    win = jnp.pad(jnp.concatenate([wi[:, :o_v], wi[:, o_z:]], axis=1),
                  ((0, 0), (0, D_IN_PAD - (w_in.shape[2] - ATT_V))))
    return dict(
        ffn1=ffn(ffn1_norm_w, ffn1_w_gate, ffn1_w_up, ffn1_w_down),
        ffn2=ffn(ffn2_norm_w, ffn2_w_gate, ffn2_w_up, ffn2_w_down),
        mix_norm=mix_norm_w[0][None, :],
        win=win,
        wvt=wi[:, o_v:o_z].T,
        lamv=jnp.stack([lambda_q1[0], lambda_k1[0], lambda_q2[0], lambda_k2[0]]),
        subw_col=attn_subln_w[0][:, None],
        cw=jnp.pad(conv_w[0], ((0, 8 - SSM_CONV), (0, 0))),
        cb=conv_b[0][None, :],
        dtb=pad_lanes(jnp.concatenate([dt_bias_fwd[0], dt_bias_bwd[0]]), DT_PAD),
        alog=pad_lanes(jnp.concatenate([a_log_fwd[0], a_log_bwd[0]]), DT_PAD),
        dskip=jnp.repeat(ssm_d[0], SSM_HEADDIM)[None, :],
        ssm_norm=ssm_norm_w[0][None, :],
        wo=w_out[0].astype(BF16).reshape(2, ATT_V, D_MODEL),
        final=final_norm_w[None, :],
        sel=_head_selector(),
    )


def _encode(x, w, meta_proj, tab, mtab):
    batch, seq, _ = x.shape
    km, vmt, mxbc, mdt = meta_proj
    h0 = x.reshape(batch * seq, D_MODEL)
    h1 = _ffn_call(h0, *w["ffn1"])
    q, k, vt, z, xbc, dt = _inproj_call(h1, w["mix_norm"], w["win"], w["wvt"])
    att = _attn_call(w["lamv"], q, k, vt, km, vmt, tab, mtab, w["subw_col"], batch, seq, ATT_TQ, ATT_TK)
    ssm = _ssd_call(z, xbc, dt, mxbc, mdt, w["cw"], w["cb"], w["dtb"], w["alog"], w["dskip"], w["ssm_norm"],
                    w["sel"], batch, seq, SSD_CHUNK, SSD_SUB)
    y = _ffn_call(h1, *w["ffn2"], mix=(att, ssm, w["wo"]), final_w=w["final"])
    return y.reshape(batch, seq, D_MODEL)


def kernel(x_prompt, x_sample, meta_tokens, ffn1_norm_w, ffn1_w_gate, ffn1_w_up, ffn1_w_down, mix_norm_w, w_in, rel_bias, lambda_q1, lambda_k1, lambda_q2, lambda_k2, attn_subln_w, conv_w, conv_b, dt_bias_fwd, dt_bias_bwd, a_log_fwd, a_log_bwd, ssm_d, ssm_norm_w, w_out, ffn2_norm_w, ffn2_w_gate, ffn2_w_up, ffn2_w_down, final_norm_w):
    w = _prep_weights(ffn1_norm_w, ffn1_w_gate, ffn1_w_up, ffn1_w_down, mix_norm_w, w_in, lambda_q1, lambda_k1,
                      lambda_q2, lambda_k2, attn_subln_w, conv_w, conv_b, dt_bias_fwd, dt_bias_bwd, a_log_fwd,
                      a_log_bwd, ssm_d, ssm_norm_w, w_out, ffn2_norm_w, ffn2_w_gate, ffn2_w_up, ffn2_w_down,
                      final_norm_w)
    hm = _ffn_call(meta_tokens, *w["ffn1"])
    _, km, vmt, _, mxbc, mdt = _inproj_call(hm, w["mix_norm"], w["win"], w["wvt"])
    meta_proj = (km, vmt[0], mxbc, mdt)
    tab, mtab = _bias_call(rel_bias, ATT_TQ, ATT_TK)
    return (_encode(x_prompt, w, meta_proj, tab, mtab), _encode(x_sample, w, meta_proj, tab, mtab))
```

```python
import functools
import math

import jax
import jax.numpy as jnp
from jax import lax
from jax.experimental import pallas as pl
from jax.experimental.pallas import tpu as pltpu

F32 = jnp.float32
BF16 = jnp.bfloat16

D_MODEL = 1024
N_META = 16
N_ATT_HEADS = 8
ATT_DH = 64
ATT_DV = 128
ATT_QK = 1024
ATT_V = 1024
NUM_BUCKETS = 32
MAX_DISTANCE = 128
SSM_HEADS = 16
SSM_HEADDIM = 64
SSM_INNER = 1024
SSM_GROUPS = 2
SSM_STATE = 64
SSM_CONV = 7
SSM_CONV_DIM = 1280
D_FF = 2816
EPS = 1e-6
LAYER = 0
LAM_INIT = 0.8 - 0.6 * math.exp(-0.3 * LAYER)
LOG2E = math.log2(math.e)
Q_SCALE = ATT_DH ** -0.5 * LOG2E
NEG_BIG = -1e30

LANES = 128
BF16_ROWS = 16
VMEM_LIMIT = 56 * 1024 * 1024

FF_TILE = 256
N_FF = D_FF // FF_TILE
DT_PAD = LANES
D_IN_PAD = 2 * ATT_QK + SSM_INNER + SSM_CONV_DIM + DT_PAD
T5_BAND = 91

ROW_TILE = 512
ATT_TQ = 512
ATT_TK = 512
ATT_COLS = 256
SSD_CHUNK = 128
SSD_SUB = 4
HALO = BF16_ROWS
VT_ROWS = ATT_DV + BF16_ROWS


def _rmsnorm(x, w):
    ms = jnp.mean(x * x, axis=-1, keepdims=True)
    return x * lax.rsqrt(ms + EPS) * w


def _resident(shape):
    nd = len(shape)
    return pl.BlockSpec(shape, lambda *_: (0,) * nd, pipeline_mode=pl.Buffered(1))


def _ffn_kernel(*refs, has_mix, has_final):
    it = iter(refs)
    h_ref = next(it)
    if has_mix:
        att_ref, ssm_ref, wo_ref = next(it), next(it), next(it)
    nw_ref, wg_ref, wu_ref, wd_ref = next(it), next(it), next(it), next(it)
    fw_ref = next(it) if has_final else None
    o_ref = next(it)

    h = h_ref[...]
    if has_mix:
        h = (h + jnp.dot(att_ref[...], wo_ref[0], preferred_element_type=F32)
             + jnp.dot(ssm_ref[...], wo_ref[1], preferred_element_type=F32))
    u = _rmsnorm(h, nw_ref[...]).astype(BF16)
    acc = jnp.zeros_like(h)
    for j in range(N_FF):
        ff = slice(j * FF_TILE, (j + 1) * FF_TILE)
        g = jnp.dot(u, wg_ref[:, ff], preferred_element_type=F32)
        up = jnp.dot(u, wu_ref[:, ff], preferred_element_type=F32)
        a = (g * jax.nn.sigmoid(g) * up).astype(BF16)
        acc = acc + jnp.dot(a, wd_ref[ff, :], preferred_element_type=F32)
    h = h + 0.5 * acc
    if has_final:
        h = _rmsnorm(h, fw_ref[...])
    o_ref[...] = h


def _ffn_call(h, norm_w, wg, wu, wd, mix=None, final_w=None):
    n = h.shape[0]
    tm = min(ROW_TILE, n)
    assert n % tm == 0
    row = lambda width: pl.BlockSpec((tm, width), lambda i: (i, 0))
    args, specs = [h], [row(D_MODEL)]
    if mix is not None:
        att, ssm, wo = mix
        args += [att, ssm, wo]
        specs += [row(ATT_V), row(SSM_INNER), _resident(wo.shape)]
    args += [norm_w, wg, wu, wd]
    specs += [_resident(norm_w.shape), _resident(wg.shape), _resident(wu.shape), _resident(wd.shape)]
    if final_w is not None:
        args.append(final_w)
        specs.append(_resident(final_w.shape))
    return pl.pallas_call(
        functools.partial(_ffn_kernel, has_mix=mix is not None, has_final=final_w is not None),
        grid=(n // tm,),
        in_specs=specs,
        out_specs=row(D_MODEL),
        out_shape=jax.ShapeDtypeStruct((n, D_MODEL), F32),
        compiler_params=pltpu.CompilerParams(dimension_semantics=("arbitrary",), vmem_limit_bytes=VMEM_LIMIT),
        name="ffn_mix" if mix is not None else "ffn",
    )(*args)


_IN_SEGS = (("q", 0, ATT_QK), ("k", ATT_QK, ATT_QK), ("z", 2 * ATT_QK, SSM_INNER),
            ("xbc", 2 * ATT_QK + SSM_INNER, SSM_CONV_DIM), ("dt", D_IN_PAD - DT_PAD, DT_PAD))


def _inproj_kernel(h_ref, nw_ref, win_ref, wvt_ref, q_ref, k_ref, vt_ref, z_ref, xbc_ref, dt_ref):
    u = _rmsnorm(h_ref[...], nw_ref[...]).astype(BF16)
    outs = dict(q=q_ref, k=k_ref, z=z_ref, xbc=xbc_ref, dt=dt_ref)
    for name, c0, width in _IN_SEGS:
        o_ref = outs[name]
        step = 512 if width % 512 == 0 else (256 if width % 256 == 0 else LANES)
        for s in range(0, width, step):
            r = jnp.dot(u, win_ref[:, c0 + s:c0 + s + step], preferred_element_type=F32)
            if name == "q":
                r = r * Q_SCALE
            o_ref[:, s:s + step] = r.astype(o_ref.dtype)
    nt_dims = (((1,), (1,)), ((), ()))
    ones = jnp.ones((VT_ROWS - ATT_DV, u.shape[0]), vt_ref.dtype)
    for s in range(0, ATT_V, 256):
        r = lax.dot_general(wvt_ref[s:s + 256, :], u, nt_dims, preferred_element_type=F32).astype(vt_ref.dtype)
        for hh in range(256 // ATT_DV):
            head = s // ATT_DV + hh
            vt_ref[0, head * VT_ROWS:head * VT_ROWS + ATT_DV, :] = r[hh * ATT_DV:(hh + 1) * ATT_DV]
            vt_ref[0, head * VT_ROWS + ATT_DV:(head + 1) * VT_ROWS, :] = ones


def _inproj_call(h, norm_w, win, wvt):
    n = h.shape[0]
    tm = min(ATT_TK, n)
    assert n % tm == 0
    row = lambda width: pl.BlockSpec((tm, width), lambda i: (i, 0))
    widths = (ATT_QK, ATT_QK, SSM_INNER, SSM_CONV_DIM, DT_PAD)
    dtypes = (BF16, BF16, BF16, BF16, F32)
    shapes = [jax.ShapeDtypeStruct((n, w), dt) for w, dt in zip(widths, dtypes)]
    specs = [row(w) for w in widths]
    shapes.insert(2, jax.ShapeDtypeStruct((n // tm, N_ATT_HEADS * VT_ROWS, tm), BF16))
    specs.insert(2, pl.BlockSpec((1, N_ATT_HEADS * VT_ROWS, tm), lambda i: (i, 0, 0)))
    return pl.pallas_call(
        _inproj_kernel,
        grid=(n // tm,),
        in_specs=[row(D_MODEL), _resident(norm_w.shape), _resident(win.shape), _resident(wvt.shape)],
        out_specs=specs,
        out_shape=shapes,
        compiler_params=pltpu.CompilerParams(dimension_semantics=("arbitrary",), vmem_limit_bytes=VMEM_LIMIT),
        name="inproj",
    )(h, norm_w, win, wvt)


def _t5_bias(rel, rb_ref, head):
    half = NUM_BUCKETS // 2
    max_exact = half // 2
    ret = jnp.where(rel > 0, half, 0)
    n = jnp.abs(rel)
    nf = jnp.maximum(n, 1).astype(F32)
    large = max_exact + (jnp.log(nf / max_exact) / math.log(MAX_DISTANCE / max_exact)
                         * (half - max_exact)).astype(jnp.int32)
    large = jnp.minimum(large, half - 1)
    bucket = ret + jnp.where(n < max_exact, n, large)
    val = jnp.zeros(rel.shape, F32)
    for jb in range(NUM_BUCKETS):
        val = jnp.where(bucket == jb, rb_ref[jb, head], val)
    return val * LOG2E


def _bias_kernel(rb_ref, tab_ref, mtab_ref, *, tq, tk):
    head = pl.program_id(0)
    r = tk // tq
    n_near = r + 2
    krow = lax.broadcasted_iota(jnp.int32, (tk, tq), 0)
    qcol = lax.broadcasted_iota(jnp.int32, (tk, tq), 1)
    for t in range(n_near):
        tab_ref[0, t] = _t5_bias(krow - qcol + (t - r) * tq, rb_ref, head)
    far_left = rb_ref[NUM_BUCKETS // 2 - 1, head] * LOG2E
    far_right = rb_ref[NUM_BUCKETS - 1, head] * LOG2E
    tab_ref[0, n_near] = jnp.full((tk, tq), far_left, F32)
    tab_ref[0, n_near + 1] = jnp.full((tk, tq), far_right, F32)
    mrow = lax.broadcasted_iota(jnp.int32, (N_META, tq), 0)
    mcol = lax.broadcasted_iota(jnp.int32, (N_META, tq), 1)
    mtab_ref[0, 0] = _t5_bias(mrow - N_META - mcol, rb_ref, head)
    mtab_ref[0, 1] = jnp.full((N_META, tq), far_left, F32)


def _bias_call(rel_bias, tq, tk):
    assert tk % tq == 0 and tq >= T5_BAND + 1
    nt = tk // tq + 4
    return pl.pallas_call(
        functools.partial(_bias_kernel, tq=tq, tk=tk),
        grid=(N_ATT_HEADS,),
        in_specs=[pl.BlockSpec(memory_space=pltpu.SMEM)],
        out_specs=[pl.BlockSpec((1, nt, tk, tq), lambda h: (h, 0, 0, 0)),
                   pl.BlockSpec((1, 2, N_META, tq), lambda h: (h, 0, 0, 0))],
        out_shape=[jax.ShapeDtypeStruct((N_ATT_HEADS, nt, tk, tq), F32),
                   jax.ShapeDtypeStruct((N_ATT_HEADS, 2, N_META, tq), F32)],
        compiler_params=pltpu.CompilerParams(dimension_semantics=("arbitrary",)),
        name="t5_bias",
    )(rel_bias)


def _attn_kernel(lam_ref, q_ref, k_ref, vt_ref, km_ref, vmt_ref, tab_ref, mtab_ref, sw_ref, o_ref,
                 sa_ref, sb_ref, mca_ref, mcb_ref, m_scr, acc_scr, accp_scr, *, tq, tk, nkc, nq, n_steps):
    g = pl.program_id(0)
    qi = jnp.minimum(g, n_steps - 1) % nq
    r = tk // tq
    n_near = r + 2
    nt_dims = (((1,), (1,)), ((), ()))
    n_col = 2 * tq // ATT_COLS

    def finalize_previous():
        acc = accp_scr[...]
        o = acc[:ATT_DV] / acc[ATT_DV:ATT_DV + 1]
        lv = lam_ref[...]
        lam = (jnp.exp(jnp.sum(lv[0:1] * lv[1:2], axis=1, keepdims=True))
               - jnp.exp(jnp.sum(lv[2:3] * lv[3:4], axis=1, keepdims=True)) + LAM_INIT)
        out = o[:, :tq] - lam * o[:, tq:]
        ms = jnp.mean(out * out, axis=0, keepdims=True)
        out = out * lax.rsqrt(ms + EPS) * sw_ref[...] * (1.0 - LAM_INIT)
        o_ref[...] = out.T.astype(o_ref.dtype)

    @pl.when(g == 0)
    def _():
        accp_scr[...] = jnp.ones(accp_scr.shape, F32)

    @pl.when(g < n_steps)
    def _():
        finalize_previous()

        q = q_ref[...]
        lane = lax.broadcasted_iota(jnp.int32, (tq, LANES), 1)
        zero = jnp.zeros_like(q)
        q2 = jnp.concatenate([jnp.where(lane < ATT_DH, q, zero), jnp.where(lane >= ATT_DH, q, zero)], axis=0)

        def add_bias(s, b):
            return jnp.concatenate([s[:, :tq] + b, s[:, tq:] + b], axis=1)

        def produce(j, c, s_ref, mc_ref):
            cols = pl.ds(c * ATT_COLS, ATT_COLS)
            s = lax.dot_general(k_ref[pl.ds(j * tk, tk), :], q2[c * ATT_COLS:(c + 1) * ATT_COLS], nt_dims,
                                preferred_element_type=F32)
            du = j * r - qi
            idx = jnp.where(du <= -(r + 1), n_near, jnp.where(du >= 2, n_near + 1, du + r))
            s = s + tab_ref[0, idx, :, pl.ds((c * ATT_COLS) % tq, ATT_COLS)]
            s_ref[:, cols] = s
            mc_ref[:, cols] = jnp.max(s, axis=0, keepdims=True)

        def consume(s, m_cur, vt, cols, first=False):
            if first:
                m_new = m_cur
            else:
                m_prev = m_scr[:, cols]
                m_new = jnp.maximum(m_prev, m_cur)
                alpha = jnp.exp2(m_prev - m_new)
            p = jnp.exp2(s - m_new).astype(BF16)
            pv = jnp.dot(vt, p, preferred_element_type=F32)
            acc_scr[:, cols] = pv if first else alpha * acc_scr[:, cols] + pv
            m_scr[:, cols] = m_new

        sm = lax.dot_general(km_ref[...], q2, nt_dims, preferred_element_type=F32)
        sm = add_bias(sm, mtab_ref[0, jnp.minimum(qi, 1)])
        consume(sm, jnp.max(sm, axis=0, keepdims=True), vmt_ref[...], pl.ds(0, 2 * tq), first=True)

        bufs = ((sa_ref, mca_ref), (sb_ref, mcb_ref))
        for c in range(n_col):
            produce(0, c, *bufs[0])
        for j in range(nkc):
            s_ref, mc_ref = bufs[j % 2]
            for c in range(n_col):
                cols = pl.ds(c * ATT_COLS, ATT_COLS)
                if j + 1 < nkc:
                    produce(j + 1, c, *bufs[(j + 1) % 2])
                consume(s_ref[:, cols], mc_ref[:, cols], vt_ref[j], cols)
        accp_scr[...] = acc_scr[...]

    @pl.when(g == n_steps)
    def _():
        finalize_previous()


def _attn_call(lamv, q, k, vt, km, vmt, tab, mtab, subw_col, batch, seq, tq, tk):
    n = q.shape[0]
    assert n == batch * seq and seq % (2 * tk) == 0 and seq % tq == 0 and vt.shape[2] == tk
    nq = seq // tq
    nkc = seq // tk
    nt = tab.shape[1]
    n_steps = N_ATT_HEADS * batch * nq

    def tile(g):
        g = jnp.minimum(g, n_steps - 1)
        return g // (batch * nq), (g // nq) % batch, g % nq

    def cur(f):
        return lambda g: f(*tile(g))

    def prev(f):
        return lambda g: f(*tile(jnp.maximum(g - 1, 0)))

    return pl.pallas_call(
        functools.partial(_attn_kernel, tq=tq, tk=tk, nkc=nkc, nq=nq, n_steps=n_steps),
        grid=(n_steps + 1,),
        in_specs=[
            pl.BlockSpec(lamv.shape, lambda g: (0, 0)),
            pl.BlockSpec((tq, LANES), cur(lambda h, b, i: (b * nq + i, h))),
            pl.BlockSpec((seq, LANES), cur(lambda h, b, i: (b, h))),
            pl.BlockSpec((nkc, VT_ROWS, tk), cur(lambda h, b, i: (b, h, 0))),
            pl.BlockSpec((N_META, LANES), cur(lambda h, b, i: (0, h))),
            pl.BlockSpec((VT_ROWS, N_META), cur(lambda h, b, i: (h, 0))),
            pl.BlockSpec((1, nt, tk, tq), cur(lambda h, b, i: (h, 0, 0, 0))),
            pl.BlockSpec((1, 2, N_META, tq), cur(lambda h, b, i: (h, 0, 0, 0))),
            pl.BlockSpec(subw_col.shape, lambda g: (0, 0)),
        ],
        out_specs=pl.BlockSpec((tq, LANES), prev(lambda h, b, i: (b * nq + i, h))),
        out_shape=jax.ShapeDtypeStruct((n, ATT_V), BF16),
        scratch_shapes=[pltpu.VMEM((tk, 2 * tq), F32)] * 2 + [pltpu.VMEM((1, 2 * tq), F32)] * 3
        + [pltpu.VMEM((VT_ROWS, 2 * tq), F32)] * 2,
        compiler_params=pltpu.CompilerParams(dimension_semantics=("arbitrary",), vmem_limit_bytes=VMEM_LIMIT),
        name="diff_attn",
    )(lamv, q, k, vt, km, vmt, tab, mtab, subw_col)


def _split3(x):
    hi = x.astype(BF16)
    r1 = x - hi.astype(F32)
    mid = r1.astype(BF16)
    lo = (r1 - mid.astype(F32)).astype(BF16)
    return hi, mid, lo


def _cumsum_rows(a):
    rows = a.shape[0]
    r_i = lax.broadcasted_iota(jnp.int32, (rows, rows), 0)
    c_i = lax.broadcasted_iota(jnp.int32, (rows, rows), 1)
    tri = jnp.where(c_i <= r_i, 1.0, 0.0).astype(BF16)
    out = None
    for term in _split3(a):
        part = jnp.dot(tri, term, preferred_element_type=F32)
        out = part if out is None else out + part
    return out


def _expand_rows(parts, sel_ref):
    masked = []
    for w, first in parts:
        lane = lax.broadcasted_iota(jnp.int32, w.shape, 1)
        masked.append(jnp.where((lane >= first) & (lane < first + SSM_HEADS), w, 0.0))
    stacked = jnp.concatenate(masked, axis=0)
    hi = stacked.astype(BF16)
    lo = (stacked - hi.astype(F32)).astype(BF16)
    sel = sel_ref[...]
    full = jnp.dot(hi, sel, preferred_element_type=F32) + jnp.dot(lo, sel, preferred_element_type=F32)
    outs, r0 = [], 0
    for w, _ in parts:
        outs.append(full[r0:r0 + w.shape[0]])
        r0 += w.shape[0]
    return outs


def _softplus(x):
    return jnp.maximum(x, 0.0) + jnp.log(1.0 + jnp.exp(-jnp.abs(x)))


GROUP_COLS = SSM_INNER // SSM_GROUPS


def _state_update(b_t, xw):
    return jnp.concatenate(
        [jnp.dot(b_t[g * SSM_STATE:(g + 1) * SSM_STATE], xw[:, g * GROUP_COLS:(g + 1) * GROUP_COLS],
                 preferred_element_type=F32) for g in range(SSM_GROUPS)], axis=0)


def _stack_decay(dec_row):
    return jnp.concatenate(
        [jnp.broadcast_to(dec_row[:, g * GROUP_COLS:(g + 1) * GROUP_COLS], (SSM_STATE, GROUP_COLS))
         for g in range(SSM_GROUPS)], axis=0)


def _conv_silu(win_ref, cw_ref, cb_ref, rows):
    win = win_ref[0:rows + 2 * HALO, :]
    total = rows + 2 * HALO
    acc = jnp.broadcast_to(cb_ref[...], (rows, SSM_CONV_DIM))
    for j in range(SSM_CONV):
        off = j - SSM_CONV // 2
        shifted = win if off == 0 else pltpu.roll(win, (total - off) % total, axis=0)
        acc = acc + cw_ref[j:j + 1, :] * shifted[HALO:HALO + rows]
    return acc * jax.nn.sigmoid(acc)


def _ssd_kernel(z_ref, xc_ref, xl_ref, xr_ref, dt_ref, mx_ref, mdt_ref, cw_ref, cb_ref, dtb_ref, alog_ref,
                dsk_ref, nw_ref, sel_ref, o_ref, xs_scr, dts_scr, cum_scr, hbs_scr, hf_scr, hb_scr, win_scr,
                *, cs, sub, nb):
    rows = cs * sub
    ph = pl.program_id(1)
    t = pl.program_id(2)
    fwd0, bwd0 = 0, SSM_HEADS
    a_row = -jnp.exp(alog_ref[...])

    def decay_terms(dt_raw):
        dt = _softplus(dt_raw + dtb_ref[...])
        return dt, _cumsum_rows(dt * a_row)

    def bcast8(row):
        return jnp.broadcast_to(row, (8, LANES))

    @pl.when(ph == 0)
    def _():
        blk = nb - 1 - t

        @pl.when(t == 0)
        def _():
            hb_scr[...] = jnp.zeros(hb_scr.shape, F32)

        left = jnp.where(blk == 0, mx_ref[...], xl_ref[...])
        right = jnp.where(blk == nb - 1, jnp.zeros_like(xr_ref[...]), xr_ref[...])
        win_scr[0:HALO, :] = left.astype(F32)
        win_scr[HALO:HALO + rows, :] = xc_ref[...].astype(F32)
        win_scr[HALO + rows:HALO + rows + HALO, :] = right.astype(F32)

        xbc_all = _conv_silu(win_scr, cw_ref, cb_ref, rows)
        hb = hb_scr[...]
        for si in reversed(range(sub)):
            cc = blk * sub + si
            xbc = xbc_all[si * cs:(si + 1) * cs]
            xs_scr[cc] = xbc.astype(BF16)
            dt, cum = decay_terms(dt_ref[si * cs:(si + 1) * cs, :])
            dts_scr[cc] = dt
            cum_scr[cc] = cum
            eb = cum - dt * a_row
            w_b, dec = _expand_rows([(jnp.exp(eb) * dt, bwd0), (bcast8(jnp.exp(cum[cs - 1:cs, :])), bwd0)],
                                    sel_ref)
            xw = (xbc[:, :SSM_INNER] * w_b).astype(BF16)
            bm_t = xbc[:, SSM_INNER:SSM_INNER + LANES].T.astype(BF16)
            hbs_scr[cc] = hb.astype(BF16)
            hb = hb * _stack_decay(dec[0:1]) + _state_update(bm_t, xw)
        hb_scr[...] = hb

    @pl.when(ph == 1)
    def _():
        @pl.when(t == 0)
        def _():
            win_scr[0:HALO, :] = jnp.zeros((HALO, SSM_CONV_DIM), F32)
            win_scr[HALO:2 * HALO, :] = mx_ref[...].astype(F32)
            win_scr[2 * HALO:3 * HALO, :] = xc_ref[0:HALO, :].astype(F32)
            xm = _conv_silu(win_scr, cw_ref, cb_ref, N_META)
            dtm, cumm = decay_terms(mdt_ref[...])
            (w_m,) = _expand_rows([(jnp.exp(cumm[N_META - 1:N_META, :] - cumm) * dtm, fwd0)], sel_ref)
            xwm = (xm[:, :SSM_INNER] * w_m).astype(BF16)
            bmm_t = xm[:, SSM_INNER:SSM_INNER + LANES].T.astype(BF16)
            hf_scr[...] = _state_update(bmm_t, xwm)

        lane = lax.broadcasted_iota(jnp.int32, (cs, LANES), 1)
        l_i = lax.broadcasted_iota(jnp.int32, (cs, cs), 0)
        s_i = lax.broadcasted_iota(jnp.int32, (cs, cs), 1)
        lower = s_i <= l_i
        diag = s_i == l_i
        hpg = SSM_HEADS // SSM_GROUPS
        zx = jnp.zeros((cs, LANES), BF16)
        nt_dims = (((1,), (1,)), ((), ()))

        hf = hf_scr[...]
        for si in range(sub):
            cc = t * sub + si
            xbc = xs_scr[cc]
            x_bf = xbc[:, :SSM_INNER]
            bm = xbc[:, SSM_INNER:SSM_INNER + LANES]
            cm = xbc[:, SSM_INNER + LANES:SSM_INNER + 2 * LANES]
            x = x_bf.astype(F32)

            dt = dts_scr[cc]
            cum = cum_scr[cc]
            eb = cum - dt * a_row
            dt_t, cum_t, eb_t = dt.T, cum.T, eb.T
            last = cum[cs - 1:cs, :]

            c_grp = [jnp.where(lane // SSM_STATE == g, cm, jnp.zeros_like(cm)) for g in range(SSM_GROUPS)]
            g_mats = [lax.dot_general(c_g, bm, nt_dims, preferred_element_type=F32) for c_g in c_grp]

            pieces = []
            for hp in range(SSM_HEADS // 2):
                w_pair = []
                for h in (2 * hp, 2 * hp + 1):
                    arg_f = cum[:, fwd0 + h:fwd0 + h + 1] - cum_t[fwd0 + h:fwd0 + h + 1, :]
                    arg_b = eb_t[bwd0 + h:bwd0 + h + 1, :] - eb[:, bwd0 + h:bwd0 + h + 1]
                    e = jnp.exp(jnp.minimum(jnp.where(lower, arg_f, arg_b), 0.0))
                    dt_f_row = dt_t[fwd0 + h:fwd0 + h + 1, :]
                    dt_b_row = dt_t[bwd0 + h:bwd0 + h + 1, :]
                    m = e * jnp.where(lower, dt_f_row, dt_b_row) + jnp.where(diag, dt_b_row, 0.0)
                    w_pair.append((g_mats[h // hpg] * m).astype(BF16))
                xp = x_bf[:, hp * LANES:(hp + 1) * LANES]
                rhs = jnp.concatenate([jnp.where(lane < SSM_HEADDIM, xp, zx),
                                       jnp.where(lane >= SSM_HEADDIM, xp, zx)], axis=0)
                pieces.append(jnp.dot(jnp.concatenate(w_pair, axis=1), rhs, preferred_element_type=F32))
            y = jnp.concatenate(pieces, axis=1)

            d_f, d_b, w_f, dec = _expand_rows(
                [(jnp.exp(cum), fwd0), (jnp.exp(last - eb), bwd0), (jnp.exp(last - cum) * dt, fwd0),
                 (bcast8(jnp.exp(last)), fwd0)], sel_ref)
            hf_bf = hf.astype(BF16)
            hb_bf = hbs_scr[cc]
            y = y + d_f * jnp.concatenate([jnp.dot(c_g, hf_bf, preferred_element_type=F32) for c_g in c_grp],
                                          axis=1)
            y = y + d_b * jnp.concatenate([jnp.dot(c_g, hb_bf, preferred_element_type=F32) for c_g in c_grp],
                                          axis=1)
            y = y + x * dsk_ref[...]

            xw = (x * w_f).astype(BF16)
            hf = hf * _stack_decay(dec[0:1]) + _state_update(bm.astype(F32).T.astype(BF16), xw)

            zf = z_ref[si * cs:(si + 1) * cs, :].astype(F32)
            y = y * (zf * jax.nn.sigmoid(zf))
            o_ref[si * cs:(si + 1) * cs, :] = _rmsnorm(y, nw_ref[...]).astype(o_ref.dtype)
        hf_scr[...] = hf


def _ssd_call(z, xbc, dt, mxbc, mdt, cw, cb, dtb, alog, dskip, nw, sel, batch, seq, cs, sub):
    n = z.shape[0]
    rows = cs * sub
    assert n == batch * seq and seq % rows == 0 and cs % HALO == 0
    nc = seq // cs
    nb = seq // rows
    hpb = rows // HALO
    n_halo = n // HALO

    def ph0_block(ph, t):
        return (1 - ph) * (nb - 1 - t)

    const2 = lambda shape: pl.BlockSpec(shape, lambda b, ph, t: (0, 0))
    return pl.pallas_call(
        functools.partial(_ssd_kernel, cs=cs, sub=sub, nb=nb),
        grid=(batch, 2, nb),
        in_specs=[
            pl.BlockSpec((rows, SSM_INNER), lambda b, ph, t: (b * nb + ph * t, 0)),
            pl.BlockSpec((rows, SSM_CONV_DIM), lambda b, ph, t: (b * nb + ph0_block(ph, t), 0)),
            pl.BlockSpec((HALO, SSM_CONV_DIM),
                         lambda b, ph, t: (jnp.maximum((b * nb + ph0_block(ph, t)) * hpb - 1, 0), 0)),
            pl.BlockSpec((HALO, SSM_CONV_DIM),
                         lambda b, ph, t: (jnp.minimum((b * nb + ph0_block(ph, t) + 1) * hpb, n_halo - 1), 0)),
            pl.BlockSpec((rows, DT_PAD), lambda b, ph, t: (b * nb + ph0_block(ph, t), 0)),
            const2(mxbc.shape), const2(mdt.shape), const2(cw.shape), const2(cb.shape), const2(dtb.shape),
            const2(alog.shape), const2(dskip.shape), const2(nw.shape), const2(sel.shape),
        ],
        out_specs=pl.BlockSpec((rows, SSM_INNER), lambda b, ph, t: (b * nb + ph * t, 0)),
        out_shape=jax.ShapeDtypeStruct((n, SSM_INNER), BF16),
        scratch_shapes=[
            pltpu.VMEM((nc, cs, SSM_CONV_DIM), BF16),
            pltpu.VMEM((nc, cs, DT_PAD), F32),
            pltpu.VMEM((nc, cs, DT_PAD), F32),
            pltpu.VMEM((nc, LANES, GROUP_COLS), BF16),
            pltpu.VMEM((LANES, GROUP_COLS), F32),
            pltpu.VMEM((LANES, GROUP_COLS), F32),
            pltpu.VMEM((rows + 2 * HALO, SSM_CONV_DIM), F32),
        ],
        compiler_params=pltpu.CompilerParams(dimension_semantics=("arbitrary",) * 3, vmem_limit_bytes=VMEM_LIMIT),
        name="bi_ssd",
    )(z, xbc, xbc, xbc, dt, mxbc, mdt, cw, cb, dtb, alog, dskip, nw, sel)


def _head_selector():
    k = jnp.arange(LANES)[:, None]
    col = jnp.arange(SSM_INNER)[None, :]
    return ((k % SSM_HEADS == col // SSM_HEADDIM) & (k < 2 * SSM_HEADS)).astype(BF16)


def _prep_weights(ffn1_norm_w, ffn1_w_gate, ffn1_w_up, ffn1_w_down, mix_norm_w, w_in, lambda_q1, lambda_k1,
                  lambda_q2, lambda_k2, attn_subln_w, conv_w, conv_b, dt_bias_fwd, dt_bias_bwd, a_log_fwd,
                  a_log_bwd, ssm_d, ssm_norm_w, w_out, ffn2_norm_w, ffn2_w_gate, ffn2_w_up, ffn2_w_down,
                  final_norm_w):
    def ffn(norm_w, wg, wu, wd):
        return norm_w[0][None, :], wg[0].astype(BF16), wu[0].astype(BF16), wd[0].astype(BF16)

    pad_lanes = lambda v, width: jnp.pad(v, (0, width - v.shape[0]))[None, :]
    o_v, o_z = 2 * ATT_QK, 2 * ATT_QK + ATT_V
    wi = w_in[0].astype(BF16)
    win = jnp.pad(jnp.concatenate([wi[:, :o_v], wi[:, o_z:]], axis=1),
                  ((0, 0), (0, D_IN_PAD - (w_in.shape[2] - ATT_V))))
    return dict(
        ffn1=ffn(ffn1_norm_w, ffn1_w_gate, ffn1_w_up, ffn1_w_down),
        ffn2=ffn(ffn2_norm_w, ffn2_w_gate, ffn2_w_up, ffn2_w_down),
        mix_norm=mix_norm_w[0][None, :],
        win=win,
        wvt=wi[:, o_v:o_z].T,
        lamv=jnp.stack([lambda_q1[0], lambda_k1[0], lambda_q2[0], lambda_k2[0]]),
        subw_col=attn_subln_w[0][:, None],
        cw=jnp.pad(conv_w[0], ((0, 8 - SSM_CONV), (0, 0))),
        cb=conv_b[0][None, :],
        dtb=pad_lanes(jnp.concatenate([dt_bias_fwd[0], dt_bias_bwd[0]]), DT_PAD),
        alog=pad_lanes(jnp.concatenate([a_log_fwd[0], a_log_bwd[0]]), DT_PAD),
        dskip=jnp.repeat(ssm_d[0], SSM_HEADDIM)[None, :],
        ssm_norm=ssm_norm_w[0][None, :],
        wo=w_out[0].astype(BF16).reshape(2, ATT_V, D_MODEL),
        final=final_norm_w[None, :],
        sel=_head_selector(),
    )


def _encode(x, w, meta_proj, tab, mtab):
    batch, seq, _ = x.shape
    km, vmt, mxbc, mdt = meta_proj
    h0 = x.reshape(batch * seq, D_MODEL)
    h1 = _ffn_call(h0, *w["ffn1"])
    q, k, vt, z, xbc, dt = _inproj_call(h1, w["mix_norm"], w["win"], w["wvt"])
    att = _attn_call(w["lamv"], q, k, vt, km, vmt, tab, mtab, w["subw_col"], batch, seq, ATT_TQ, ATT_TK)
    ssm = _ssd_call(z, xbc, dt, mxbc, mdt, w["cw"], w["cb"], w["dtb"], w["alog"], w["dskip"], w["ssm_norm"],
                    w["sel"], batch, seq, SSD_CHUNK, SSD_SUB)
    y = _ffn_call(h1, *w["ffn2"], mix=(att, ssm, w["wo"]), final_w=w["final"])
    return y.reshape(batch, seq, D_MODEL)


def kernel(x_prompt, x_sample, meta_tokens, ffn1_norm_w, ffn1_w_gate, ffn1_w_up, ffn1_w_down, mix_norm_w, w_in, rel_bias, lambda_q1, lambda_k1, lambda_q2, lambda_k2, attn_subln_w, conv_w, conv_b, dt_bias_fwd, dt_bias_bwd, a_log_fwd, a_log_bwd, ssm_d, ssm_norm_w, w_out, ffn2_norm_w, ffn2_w_gate, ffn2_w_up, ffn2_w_down, final_norm_w):
    w = _prep_weights(ffn1_norm_w, ffn1_w_gate, ffn1_w_up, ffn1_w_down, mix_norm_w, w_in, lambda_q1, lambda_k1,
                      lambda_q2, lambda_k2, attn_subln_w, conv_w, conv_b, dt_bias_fwd, dt_bias_bwd, a_log_fwd,
                      a_log_bwd, ssm_d, ssm_norm_w, w_out, ffn2_norm_w, ffn2_w_gate, ffn2_w_up, ffn2_w_down,
                      final_norm_w)
    hm = _ffn_call(meta_tokens, *w["ffn1"])
    _, km, vmt, _, mxbc, mdt = _inproj_call(hm, w["mix_norm"], w["win"], w["wvt"])
    meta_proj = (km, vmt[0], mxbc, mdt)
    tab, mtab = _bias_call(rel_bias, ATT_TQ, ATT_TK)
    return (_encode(x_prompt, w, meta_proj, tab, mtab), _encode(x_sample, w, meta_proj, tab, mtab))
```

```python
import functools
import math

import jax
import jax.numpy as jnp
from jax import lax
from jax.experimental import pallas as pl
from jax.experimental.pallas import tpu as pltpu

F32 = jnp.float32
BF16 = jnp.bfloat16

D_MODEL = 1024
N_META = 16
N_ATT_HEADS = 8
ATT_DH = 64
ATT_DV = 128
ATT_QK = 1024
ATT_V = 1024
NUM_BUCKETS = 32
MAX_DISTANCE = 128
SSM_HEADS = 16
SSM_HEADDIM = 64
SSM_INNER = 1024
SSM_GROUPS = 2
SSM_STATE = 64
SSM_CONV = 7
SSM_CONV_DIM = 1280
D_FF = 2816
EPS = 1e-6
LAYER = 0
LAM_INIT = 0.8 - 0.6 * math.exp(-0.3 * LAYER)
LOG2E = math.log2(math.e)
Q_SCALE = ATT_DH ** -0.5 * LOG2E
NEG_BIG = -1e30
NORM_SLACK = 1.02
MAX_SHIFT_GAP = 100.0

LANES = 128
BF16_ROWS = 16
VMEM_LIMIT = 56 * 1024 * 1024

FF_TILE = 256
N_FF = D_FF // FF_TILE
DT_PAD = LANES
D_IN_PAD = 2 * ATT_QK + SSM_INNER + SSM_CONV_DIM + DT_PAD
T5_BAND = 91

ROW_TILE = 512
ATT_TQ = 512
ATT_TK = 512
ATT_COLS = 256
SSD_CHUNK = 128
SSD_SUB = 4
HALO = BF16_ROWS
VT_ROWS = ATT_DV + BF16_ROWS


def _rmsnorm(x, w):
    ms = jnp.mean(x * x, axis=-1, keepdims=True)
    return x * lax.rsqrt(ms + EPS) * w


def _resident(shape):
    nd = len(shape)
    return pl.BlockSpec(shape, lambda *_: (0,) * nd, pipeline_mode=pl.Buffered(1))


def _ffn_kernel(*refs, has_mix, has_final):
    it = iter(refs)
    h_ref = next(it)
    if has_mix:
        att_ref, ssm_ref, wo_ref = next(it), next(it), next(it)
    nw_ref, wg_ref, wu_ref, wd_ref = next(it), next(it), next(it), next(it)
    fw_ref = next(it) if has_final else None
    o_ref = next(it)

    h = h_ref[...]
    if has_mix:
        h = (h + jnp.dot(att_ref[...], wo_ref[0], preferred_element_type=F32)
             + jnp.dot(ssm_ref[...], wo_ref[1], preferred_element_type=F32))
    u = _rmsnorm(h, nw_ref[...]).astype(BF16)
    acc = jnp.zeros_like(h)
    for j in range(N_FF):
        ff = slice(j * FF_TILE, (j + 1) * FF_TILE)
        g = jnp.dot(u, wg_ref[:, ff], preferred_element_type=F32)
        up = jnp.dot(u, wu_ref[:, ff], preferred_element_type=F32)
        a = (g * jax.nn.sigmoid(g) * up).astype(BF16)
        acc = acc + jnp.dot(a, wd_ref[ff, :], preferred_element_type=F32)
    h = h + 0.5 * acc
    if has_final:
        h = _rmsnorm(h, fw_ref[...])
    o_ref[...] = h


def _ffn_call(h, norm_w, wg, wu, wd, mix=None, final_w=None):
    n = h.shape[0]
    tm = min(ROW_TILE, n)
    assert n % tm == 0
    row = lambda width: pl.BlockSpec((tm, width), lambda i: (i, 0))
    args, specs = [h], [row(D_MODEL)]
    if mix is not None:
        att, ssm, wo = mix
        args += [att, ssm, wo]
        specs += [row(ATT_V), row(SSM_INNER), _resident(wo.shape)]
    args += [norm_w, wg, wu, wd]
    specs += [_resident(norm_w.shape), _resident(wg.shape), _resident(wu.shape), _resident(wd.shape)]
    if final_w is not None:
        args.append(final_w)
        specs.append(_resident(final_w.shape))
    return pl.pallas_call(
        functools.partial(_ffn_kernel, has_mix=mix is not None, has_final=final_w is not None),
        grid=(n // tm,),
        in_specs=specs,
        out_specs=row(D_MODEL),
        out_shape=jax.ShapeDtypeStruct((n, D_MODEL), F32),
        compiler_params=pltpu.CompilerParams(dimension_semantics=("arbitrary",), vmem_limit_bytes=VMEM_LIMIT),
        name="ffn_mix" if mix is not None else "ffn",
    )(*args)


_IN_SEGS = (("q", 0, ATT_QK), ("k", ATT_QK, ATT_QK), ("z", 2 * ATT_QK, SSM_INNER),
            ("xbc", 2 * ATT_QK + SSM_INNER, SSM_CONV_DIM), ("dt", D_IN_PAD - DT_PAD, DT_PAD))


def _inproj_kernel(h_ref, nw_ref, win_ref, wvt_ref, q_ref, k_ref, vt_ref, z_ref, xbc_ref, dt_ref):
    u = _rmsnorm(h_ref[...], nw_ref[...]).astype(BF16)
    outs = dict(q=q_ref, k=k_ref, z=z_ref, xbc=xbc_ref, dt=dt_ref)
    for name, c0, width in _IN_SEGS:
        o_ref = outs[name]
        step = 512 if width % 512 == 0 else (256 if width % 256 == 0 else LANES)
        for s in range(0, width, step):
            r = jnp.dot(u, win_ref[:, c0 + s:c0 + s + step], preferred_element_type=F32)
            if name == "q":
                r = r * Q_SCALE
            o_ref[:, s:s + step] = r.astype(o_ref.dtype)
    nt_dims = (((1,), (1,)), ((), ()))
    ones = jnp.ones((VT_ROWS - ATT_DV, u.shape[0]), vt_ref.dtype)
    for s in range(0, ATT_V, 256):
        r = lax.dot_general(wvt_ref[s:s + 256, :], u, nt_dims, preferred_element_type=F32).astype(vt_ref.dtype)
        for hh in range(256 // ATT_DV):
            head = s // ATT_DV + hh
            vt_ref[0, head * VT_ROWS:head * VT_ROWS + ATT_DV, :] = r[hh * ATT_DV:(hh + 1) * ATT_DV]
            vt_ref[0, head * VT_ROWS + ATT_DV:(head + 1) * VT_ROWS, :] = ones


def _inproj_call(h, norm_w, win, wvt):
    n = h.shape[0]
    tm = min(ATT_TK, n)
    assert n % tm == 0
    row = lambda width: pl.BlockSpec((tm, width), lambda i: (i, 0))
    widths = (ATT_QK, ATT_QK, SSM_INNER, SSM_CONV_DIM, DT_PAD)
    dtypes = (BF16, BF16, BF16, BF16, F32)
    shapes = [jax.ShapeDtypeStruct((n, w), dt) for w, dt in zip(widths, dtypes)]
    specs = [row(w) for w in widths]
    shapes.insert(2, jax.ShapeDtypeStruct((n // tm, N_ATT_HEADS * VT_ROWS, tm), BF16))
    specs.insert(2, pl.BlockSpec((1, N_ATT_HEADS * VT_ROWS, tm), lambda i: (i, 0, 0)))
    return pl.pallas_call(
        _inproj_kernel,
        grid=(n // tm,),
        in_specs=[row(D_MODEL), _resident(norm_w.shape), _resident(win.shape), _resident(wvt.shape)],
        out_specs=specs,
        out_shape=shapes,
        compiler_params=pltpu.CompilerParams(dimension_semantics=("arbitrary",), vmem_limit_bytes=VMEM_LIMIT),
        name="inproj",
    )(h, norm_w, win, wvt)


def _t5_bias(rel, rb_ref, head):
    half = NUM_BUCKETS // 2
    max_exact = half // 2
    ret = jnp.where(rel > 0, half, 0)
    n = jnp.abs(rel)
    nf = jnp.maximum(n, 1).astype(F32)
    large = max_exact + (jnp.log(nf / max_exact) / math.log(MAX_DISTANCE / max_exact)
                         * (half - max_exact)).astype(jnp.int32)
    large = jnp.minimum(large, half - 1)
    bucket = ret + jnp.where(n < max_exact, n, large)
    val = jnp.zeros(rel.shape, F32)
    for jb in range(NUM_BUCKETS):
        val = jnp.where(bucket == jb, rb_ref[jb, head], val)
    return val * LOG2E


def _bias_kernel(rb_ref, tab_ref, mtab_ref, *, tq, tk):
    head = pl.program_id(0)
    r = tk // tq
    n_near = r + 2
    krow = lax.broadcasted_iota(jnp.int32, (tk, tq), 0)
    qcol = lax.broadcasted_iota(jnp.int32, (tk, tq), 1)
    for t in range(n_near):
        tab_ref[0, t] = _t5_bias(krow - qcol + (t - r) * tq, rb_ref, head)
    far_left = rb_ref[NUM_BUCKETS // 2 - 1, head] * LOG2E
    far_right = rb_ref[NUM_BUCKETS - 1, head] * LOG2E
    tab_ref[0, n_near] = jnp.full((tk, tq), far_left, F32)
    tab_ref[0, n_near + 1] = jnp.full((tk, tq), far_right, F32)
    mrow = lax.broadcasted_iota(jnp.int32, (N_META, tq), 0)
    mcol = lax.broadcasted_iota(jnp.int32, (N_META, tq), 1)
    mtab_ref[0, 0] = _t5_bias(mrow - N_META - mcol, rb_ref, head)
    mtab_ref[0, 1] = jnp.full((N_META, tq), far_left, F32)


def _bias_call(rel_bias, tq, tk):
    assert tk % tq == 0 and tq >= T5_BAND + 1
    nt = tk // tq + 4
    return pl.pallas_call(
        functools.partial(_bias_kernel, tq=tq, tk=tk),
        grid=(N_ATT_HEADS,),
        in_specs=[pl.BlockSpec(memory_space=pltpu.SMEM)],
        out_specs=[pl.BlockSpec((1, nt, tk, tq), lambda h: (h, 0, 0, 0)),
                   pl.BlockSpec((1, 2, N_META, tq), lambda h: (h, 0, 0, 0))],
        out_shape=[jax.ShapeDtypeStruct((N_ATT_HEADS, nt, tk, tq), F32),
                   jax.ShapeDtypeStruct((N_ATT_HEADS, 2, N_META, tq), F32)],
        compiler_params=pltpu.CompilerParams(dimension_semantics=("arbitrary",)),
        name="t5_bias",
    )(rel_bias)


def _attn_kernel(lam_ref, q_ref, qall_ref, k_ref, vt_ref, km_ref, vmt_ref, tab_ref, mtab_ref, sw_ref, o_ref,
                 sa_ref, sb_ref, mca_ref, mcb_ref, m_scr, acc_scr, accp_scr, shift_scr, flag_scr,
                 *, tq, tk, nkc, nq, n_steps):
    g = pl.program_id(0)
    qi = jnp.minimum(g, n_steps - 1) % nq
    r = tk // tq
    n_near = r + 2
    nt_dims = (((1,), (1,)), ((), ()))
    n_col = 2 * tq // ATT_COLS

    def finalize_previous():
        acc = accp_scr[...]
        o = acc[:ATT_DV] / acc[ATT_DV:ATT_DV + 1]
        lv = lam_ref[...]
        lam = (jnp.exp(jnp.sum(lv[0:1] * lv[1:2], axis=1, keepdims=True))
               - jnp.exp(jnp.sum(lv[2:3] * lv[3:4], axis=1, keepdims=True)) + LAM_INIT)
        out = o[:, :tq] - lam * o[:, tq:]
        ms = jnp.mean(out * out, axis=0, keepdims=True)
        out = out * lax.rsqrt(ms + EPS) * sw_ref[...] * (1.0 - LAM_INIT)
        o_ref[...] = out.T.astype(o_ref.dtype)

    @pl.when(g == 0)
    def _():
        accp_scr[...] = jnp.ones(accp_scr.shape, F32)

    @pl.when(g < n_steps)
    def _():
        q = q_ref[...]
        lane = lax.broadcasted_iota(jnp.int32, (tq, LANES), 1)
        zero = jnp.zeros_like(q)
        q2 = jnp.concatenate([jnp.where(lane < ATT_DH, q, zero), jnp.where(lane >= ATT_DH, q, zero)], axis=0)

        def bias_index(j):
            du = j * r - qi
            return jnp.where(du <= -(r + 1), n_near, jnp.where(du >= 2, n_near + 1, du + r))

        def add_bias(s, b):
            return jnp.concatenate([s[:, :tq] + b, s[:, tq:] + b], axis=1)

        @pl.when(qi == 0)
        def _():
            half = jnp.where(lax.broadcasted_iota(jnp.int32, (LANES, LANES), 0) // ATT_DH
                             == lax.broadcasted_iota(jnp.int32, (LANES, LANES), 1), 1.0, 0.0).astype(BF16)

            def max_sq_norm(x_ref):
                x32 = x_ref[...].astype(F32)
                sq = jnp.dot((x32 * x32).astype(BF16), half, preferred_element_type=F32)
                return jnp.max(sq, axis=0, keepdims=True)

            bound2 = max_sq_norm(qall_ref) * jnp.maximum(max_sq_norm(k_ref), max_sq_norm(km_ref))
            lane_row = lax.broadcasted_iota(jnp.int32, (1, LANES), 1)
            qk_bound = [NORM_SLACK * jnp.sqrt(jnp.max(jnp.where(lane_row == mp, bound2, 0.0), axis=1, keepdims=True))
                        for mp in range(2)]
            diag_tile = tab_ref[0, r]
            hi_b = jnp.max(jnp.max(diag_tile, axis=0, keepdims=True), axis=1, keepdims=True)
            lo_b = jnp.min(jnp.min(diag_tile, axis=0, keepdims=True), axis=1, keepdims=True)
            col = lax.broadcasted_iota(jnp.int32, (1, 2 * tq), 1)
            shift_scr[...] = jnp.where(col < tq, qk_bound[0], qk_bound[1]) + hi_b
            worst_gap = 2.0 * jnp.maximum(qk_bound[0], qk_bound[1]) + (hi_b - lo_b)
            flag_scr[0] = (worst_gap[0, 0] <= MAX_SHIFT_GAP).astype(jnp.int32)

        shift = shift_scr[...]
        bounded = flag_scr[0] == 1

        @pl.when(bounded)
        def _():
            finalize_previous()
            sm = lax.dot_general(km_ref[...], q2, nt_dims, preferred_element_type=F32)
            sm = add_bias(sm, mtab_ref[0, jnp.minimum(qi, 1)])
            acc_scr[...] = jnp.dot(vmt_ref[...], jnp.exp2(sm - shift).astype(BF16), preferred_element_type=F32)

            def stage_logits(u, j, c):
                buf = (sa_ref, sb_ref)[u % 2]
                buf[:, pl.ds(c * ATT_COLS, ATT_COLS)] = lax.dot_general(
                    k_ref[pl.ds(j * tk, tk), :], q2[c * ATT_COLS:(c + 1) * ATT_COLS], nt_dims,
                    preferred_element_type=F32)

            units = [(j, c) for j in range(nkc) for c in range(n_col)]
            stage_logits(0, *units[0])
            for u, (j, c) in enumerate(units):
                if u + 1 < len(units):
                    stage_logits(u + 1, *units[u + 1])
                cols = pl.ds(c * ATT_COLS, ATT_COLS)
                s = (sa_ref, sb_ref)[u % 2][:, cols] + tab_ref[0, bias_index(j), :, pl.ds((c * ATT_COLS) % tq, ATT_COLS)]
                p = jnp.exp2(s - shift[:, c * ATT_COLS:(c + 1) * ATT_COLS]).astype(BF16)
                acc_scr[:, cols] += jnp.dot(vt_ref[j], p, preferred_element_type=F32)
            accp_scr[...] = acc_scr[...]

        @pl.when(jnp.logical_not(bounded))
        def _():
            finalize_previous()

            def produce(j, c, s_ref, mc_ref):
                cols = pl.ds(c * ATT_COLS, ATT_COLS)
                s = lax.dot_general(k_ref[pl.ds(j * tk, tk), :], q2[c * ATT_COLS:(c + 1) * ATT_COLS], nt_dims,
                                    preferred_element_type=F32)
                s = s + tab_ref[0, bias_index(j), :, pl.ds((c * ATT_COLS) % tq, ATT_COLS)]
                s_ref[:, cols] = s
                mc_ref[:, cols] = jnp.max(s, axis=0, keepdims=True)

            def consume(s, m_cur, vt, cols, first=False):
                if first:
                    m_new = m_cur
                else:
                    m_prev = m_scr[:, cols]
                    m_new = jnp.maximum(m_prev, m_cur)
                    alpha = jnp.exp2(m_prev - m_new)
                p = jnp.exp2(s - m_new).astype(BF16)
                pv = jnp.dot(vt, p, preferred_element_type=F32)
                acc_scr[:, cols] = pv if first else alpha * acc_scr[:, cols] + pv
                m_scr[:, cols] = m_new

            sm = lax.dot_general(km_ref[...], q2, nt_dims, preferred_element_type=F32)
            sm = add_bias(sm, mtab_ref[0, jnp.minimum(qi, 1)])
            consume(sm, jnp.max(sm, axis=0, keepdims=True), vmt_ref[...], pl.ds(0, 2 * tq), first=True)

            bufs = ((sa_ref, mca_ref), (sb_ref, mcb_ref))
            for c in range(n_col):
                produce(0, c, *bufs[0])
            for j in range(nkc):
                s_ref, mc_ref = bufs[j % 2]
                for c in range(n_col):
                    cols = pl.ds(c * ATT_COLS, ATT_COLS)
                    if j + 1 < nkc:
                        produce(j + 1, c, *bufs[(j + 1) % 2])
                    consume(s_ref[:, cols], mc_ref[:, cols], vt_ref[j], cols)
            accp_scr[...] = acc_scr[...]

    @pl.when(g == n_steps)
    def _():
        finalize_previous()


def _attn_call(lamv, q, k, vt, km, vmt, tab, mtab, subw_col, batch, seq, tq, tk):
    n = q.shape[0]
    assert n == batch * seq and seq % (2 * tk) == 0 and seq % tq == 0 and vt.shape[2] == tk
    nq = seq // tq
    nkc = seq // tk
    nt = tab.shape[1]
    n_steps = N_ATT_HEADS * batch * nq

    def tile(g):
        g = jnp.minimum(g, n_steps - 1)
        return g // (batch * nq), (g // nq) % batch, g % nq

    def cur(f):
        return lambda g: f(*tile(g))

    def prev(f):
        return lambda g: f(*tile(jnp.maximum(g - 1, 0)))

    return pl.pallas_call(
        functools.partial(_attn_kernel, tq=tq, tk=tk, nkc=nkc, nq=nq, n_steps=n_steps),
        grid=(n_steps + 1,),
        in_specs=[
            pl.BlockSpec(lamv.shape, lambda g: (0, 0)),
            pl.BlockSpec((tq, LANES), cur(lambda h, b, i: (b * nq + i, h))),
            pl.BlockSpec((seq, LANES), cur(lambda h, b, i: (b, h))),
            pl.BlockSpec((seq, LANES), cur(lambda h, b, i: (b, h))),
            pl.BlockSpec((nkc, VT_ROWS, tk), cur(lambda h, b, i: (b, h, 0))),
            pl.BlockSpec((N_META, LANES), cur(lambda h, b, i: (0, h))),
            pl.BlockSpec((VT_ROWS, N_META), cur(lambda h, b, i: (h, 0))),
            pl.BlockSpec((1, nt, tk, tq), cur(lambda h, b, i: (h, 0, 0, 0))),
            pl.BlockSpec((1, 2, N_META, tq), cur(lambda h, b, i: (h, 0, 0, 0))),
            pl.BlockSpec(subw_col.shape, lambda g: (0, 0)),
        ],
        out_specs=pl.BlockSpec((tq, LANES), prev(lambda h, b, i: (b * nq + i, h))),
        out_shape=jax.ShapeDtypeStruct((n, ATT_V), BF16),
        scratch_shapes=[pltpu.VMEM((tk, 2 * tq), F32)] * 2 + [pltpu.VMEM((1, 2 * tq), F32)] * 3
        + [pltpu.VMEM((VT_ROWS, 2 * tq), F32)] * 2 + [pltpu.VMEM((1, 2 * tq), F32), pltpu.SMEM((1,), jnp.int32)],
        compiler_params=pltpu.CompilerParams(dimension_semantics=("arbitrary",), vmem_limit_bytes=VMEM_LIMIT),
        name="diff_attn",
    )(lamv, q, q, k, vt, km, vmt, tab, mtab, subw_col)


def _split3(x):
    hi = x.astype(BF16)
    r1 = x - hi.astype(F32)
    mid = r1.astype(BF16)
    lo = (r1 - mid.astype(F32)).astype(BF16)
    return hi, mid, lo


def _cumsum_rows(a):
    rows = a.shape[0]
    r_i = lax.broadcasted_iota(jnp.int32, (rows, rows), 0)
    c_i = lax.broadcasted_iota(jnp.int32, (rows, rows), 1)
    tri = jnp.where(c_i <= r_i, 1.0, 0.0).astype(BF16)
    out = None
    for term in _split3(a):
        part = jnp.dot(tri, term, preferred_element_type=F32)
        out = part if out is None else out + part
    return out


def _expand_rows(parts, sel_ref):
    masked = []
    for w, first in parts:
        lane = lax.broadcasted_iota(jnp.int32, w.shape, 1)
        masked.append(jnp.where((lane >= first) & (lane < first + SSM_HEADS), w, 0.0))
    stacked = jnp.concatenate(masked, axis=0)
    hi = stacked.astype(BF16)
    lo = (stacked - hi.astype(F32)).astype(BF16)
    sel = sel_ref[...]
    full = jnp.dot(hi, sel, preferred_element_type=F32) + jnp.dot(lo, sel, preferred_element_type=F32)
    outs, r0 = [], 0
    for w, _ in parts:
        outs.append(full[r0:r0 + w.shape[0]])
        r0 += w.shape[0]
    return outs


def _softplus(x):
    return jnp.maximum(x, 0.0) + jnp.log(1.0 + jnp.exp(-jnp.abs(x)))


GROUP_COLS = SSM_INNER // SSM_GROUPS


def _state_update(b_t, xw):
    return jnp.concatenate(
        [jnp.dot(b_t[g * SSM_STATE:(g + 1) * SSM_STATE], xw[:, g * GROUP_COLS:(g + 1) * GROUP_COLS],
                 preferred_element_type=F32) for g in range(SSM_GROUPS)], axis=0)


def _stack_decay(dec_row):
    return jnp.concatenate(
        [jnp.broadcast_to(dec_row[:, g * GROUP_COLS:(g + 1) * GROUP_COLS], (SSM_STATE, GROUP_COLS))
         for g in range(SSM_GROUPS)], axis=0)


def _conv_silu(win_ref, cw_ref, cb_ref, rows):
    win = win_ref[0:rows + 2 * HALO, :]
    total = rows + 2 * HALO
    acc = jnp.broadcast_to(cb_ref[...], (rows, SSM_CONV_DIM))
    for j in range(SSM_CONV):
        off = j - SSM_CONV // 2
        shifted = win if off == 0 else pltpu.roll(win, (total - off) % total, axis=0)
        acc = acc + cw_ref[j:j + 1, :] * shifted[HALO:HALO + rows]
    return acc * jax.nn.sigmoid(acc)


def _ssd_kernel(z_ref, xc_ref, xl_ref, xr_ref, dt_ref, mx_ref, mdt_ref, cw_ref, cb_ref, dtb_ref, alog_ref,
                dsk_ref, nw_ref, sel_ref, o_ref, xs_scr, dts_scr, cum_scr, hbs_scr, hf_scr, hb_scr, win_scr,
                *, cs, sub, nb):
    rows = cs * sub
    ph = pl.program_id(1)
    t = pl.program_id(2)
    fwd0, bwd0 = 0, SSM_HEADS
    a_row = -jnp.exp(alog_ref[...])

    def decay_terms(dt_raw):
        dt = _softplus(dt_raw + dtb_ref[...])
        return dt, _cumsum_rows(dt * a_row)

    def bcast8(row):
        return jnp.broadcast_to(row, (8, LANES))

    @pl.when(ph == 0)
    def _():
        blk = nb - 1 - t

        @pl.when(t == 0)
        def _():
            hb_scr[...] = jnp.zeros(hb_scr.shape, F32)

        left = jnp.where(blk == 0, mx_ref[...], xl_ref[...])
        right = jnp.where(blk == nb - 1, jnp.zeros_like(xr_ref[...]), xr_ref[...])
        win_scr[0:HALO, :] = left.astype(F32)
        win_scr[HALO:HALO + rows, :] = xc_ref[...].astype(F32)
        win_scr[HALO + rows:HALO + rows + HALO, :] = right.astype(F32)

        xbc_all = _conv_silu(win_scr, cw_ref, cb_ref, rows)
        hb = hb_scr[...]
        for si in reversed(range(sub)):
            cc = blk * sub + si
            xbc = xbc_all[si * cs:(si + 1) * cs]
            xs_scr[cc] = xbc.astype(BF16)
            dt, cum = decay_terms(dt_ref[si * cs:(si + 1) * cs, :])
            dts_scr[cc] = dt
            cum_scr[cc] = cum
            eb = cum - dt * a_row
            w_b, dec = _expand_rows([(jnp.exp(eb) * dt, bwd0), (bcast8(jnp.exp(cum[cs - 1:cs, :])), bwd0)],
                                    sel_ref)
            xw = (xbc[:, :SSM_INNER] * w_b).astype(BF16)
            bm_t = xbc[:, SSM_INNER:SSM_INNER + LANES].T.astype(BF16)
            hbs_scr[cc] = hb.astype(BF16)
            hb = hb * _stack_decay(dec[0:1]) + _state_update(bm_t, xw)
        hb_scr[...] = hb

    @pl.when(ph == 1)
    def _():
        @pl.when(t == 0)
        def _():
            win_scr[0:HALO, :] = jnp.zeros((HALO, SSM_CONV_DIM), F32)
            win_scr[HALO:2 * HALO, :] = mx_ref[...].astype(F32)
            win_scr[2 * HALO:3 * HALO, :] = xc_ref[0:HALO, :].astype(F32)
            xm = _conv_silu(win_scr, cw_ref, cb_ref, N_META)
            dtm, cumm = decay_terms(mdt_ref[...])
            (w_m,) = _expand_rows([(jnp.exp(cumm[N_META - 1:N_META, :] - cumm) * dtm, fwd0)], sel_ref)
            xwm = (xm[:, :SSM_INNER] * w_m).astype(BF16)
            bmm_t = xm[:, SSM_INNER:SSM_INNER + LANES].T.astype(BF16)
            hf_scr[...] = _state_update(bmm_t, xwm)

        lane = lax.broadcasted_iota(jnp.int32, (cs, LANES), 1)
        l_i = lax.broadcasted_iota(jnp.int32, (cs, cs), 0)
        s_i = lax.broadcasted_iota(jnp.int32, (cs, cs), 1)
        lower = s_i <= l_i
        diag = s_i == l_i
        hpg = SSM_HEADS // SSM_GROUPS
        zx = jnp.zeros((cs, LANES), BF16)
        nt_dims = (((1,), (1,)), ((), ()))

        hf = hf_scr[...]
        for si in range(sub):
            cc = t * sub + si
            xbc = xs_scr[cc]
            x_bf = xbc[:, :SSM_INNER]
            bm = xbc[:, SSM_INNER:SSM_INNER + LANES]
            cm = xbc[:, SSM_INNER + LANES:SSM_INNER + 2 * LANES]
            x = x_bf.astype(F32)

            dt = dts_scr[cc]
            cum = cum_scr[cc]
            eb = cum - dt * a_row
            dt_t, cum_t, eb_t = dt.T, cum.T, eb.T
            last = cum[cs - 1:cs, :]

            c_grp = [jnp.where(lane // SSM_STATE == g, cm, jnp.zeros_like(cm)) for g in range(SSM_GROUPS)]
            g_mats = [lax.dot_general(c_g, bm, nt_dims, preferred_element_type=F32) for c_g in c_grp]

            pieces = []
            for hp in range(SSM_HEADS // 2):
                w_pair = []
                for h in (2 * hp, 2 * hp + 1):
                    arg_f = cum[:, fwd0 + h:fwd0 + h + 1] - cum_t[fwd0 + h:fwd0 + h + 1, :]
                    arg_b = eb_t[bwd0 + h:bwd0 + h + 1, :] - eb[:, bwd0 + h:bwd0 + h + 1]
                    e = jnp.exp(jnp.minimum(jnp.where(lower, arg_f, arg_b), 0.0))
                    dt_f_row = dt_t[fwd0 + h:fwd0 + h + 1, :]
                    dt_b_row = dt_t[bwd0 + h:bwd0 + h + 1, :]
                    m = e * jnp.where(lower, dt_f_row, dt_b_row) + jnp.where(diag, dt_b_row, 0.0)
                    w_pair.append((g_mats[h // hpg] * m).astype(BF16))
                xp = x_bf[:, hp * LANES:(hp + 1) * LANES]
                rhs = jnp.concatenate([jnp.where(lane < SSM_HEADDIM, xp, zx),
                                       jnp.where(lane >= SSM_HEADDIM, xp, zx)], axis=0)
                pieces.append(jnp.dot(jnp.concatenate(w_pair, axis=1), rhs, preferred_element_type=F32))
            y = jnp.concatenate(pieces, axis=1)

            d_f, d_b, w_f, dec = _expand_rows(
                [(jnp.exp(cum), fwd0), (jnp.exp(last - eb), bwd0), (jnp.exp(last - cum) * dt, fwd0),
                 (bcast8(jnp.exp(last)), fwd0)], sel_ref)
            hf_bf = hf.astype(BF16)
            hb_bf = hbs_scr[cc]
            y = y + d_f * jnp.concatenate([jnp.dot(c_g, hf_bf, preferred_element_type=F32) for c_g in c_grp],
                                          axis=1)
            y = y + d_b * jnp.concatenate([jnp.dot(c_g, hb_bf, preferred_element_type=F32) for c_g in c_grp],
                                          axis=1)
            y = y + x * dsk_ref[...]

            xw = (x * w_f).astype(BF16)
            hf = hf * _stack_decay(dec[0:1]) + _state_update(bm.astype(F32).T.astype(BF16), xw)

            zf = z_ref[si * cs:(si + 1) * cs, :].astype(F32)
            y = y * (zf * jax.nn.sigmoid(zf))
            o_ref[si * cs:(si + 1) * cs, :] = _rmsnorm(y, nw_ref[...]).astype(o_ref.dtype)
        hf_scr[...] = hf


def _ssd_call(z, xbc, dt, mxbc, mdt, cw, cb, dtb, alog, dskip, nw, sel, batch, seq, cs, sub):
    n = z.shape[0]
    rows = cs * sub
    assert n == batch * seq and seq % rows == 0 and cs % HALO == 0
    nc = seq // cs
    nb = seq // rows
    hpb = rows // HALO
    n_halo = n // HALO

    def ph0_block(ph, t):
        return (1 - ph) * (nb - 1 - t)

    const2 = lambda shape: pl.BlockSpec(shape, lambda b, ph, t: (0, 0))
    return pl.pallas_call(
        functools.partial(_ssd_kernel, cs=cs, sub=sub, nb=nb),
        grid=(batch, 2, nb),
        in_specs=[
            pl.BlockSpec((rows, SSM_INNER), lambda b, ph, t: (b * nb + ph * t, 0)),
            pl.BlockSpec((rows, SSM_CONV_DIM), lambda b, ph, t: (b * nb + ph0_block(ph, t), 0)),
            pl.BlockSpec((HALO, SSM_CONV_DIM),
                         lambda b, ph, t: (jnp.maximum((b * nb + ph0_block(ph, t)) * hpb - 1, 0), 0)),
            pl.BlockSpec((HALO, SSM_CONV_DIM),
                         lambda b, ph, t: (jnp.minimum((b * nb + ph0_block(ph, t) + 1) * hpb, n_halo - 1), 0)),
            pl.BlockSpec((rows, DT_PAD), lambda b, ph, t: (b * nb + ph0_block(ph, t), 0)),
            const2(mxbc.shape), const2(mdt.shape), const2(cw.shape), const2(cb.shape), const2(dtb.shape),
            const2(alog.shape), const2(dskip.shape), const2(nw.shape), const2(sel.shape),
        ],
        out_specs=pl.BlockSpec((rows, SSM_INNER), lambda b, ph, t: (b * nb + ph * t, 0)),
        out_shape=jax.ShapeDtypeStruct((n, SSM_INNER), BF16),
        scratch_shapes=[
            pltpu.VMEM((nc, cs, SSM_CONV_DIM), BF16),
            pltpu.VMEM((nc, cs, DT_PAD), F32),
            pltpu.VMEM((nc, cs, DT_PAD), F32),
            pltpu.VMEM((nc, LANES, GROUP_COLS), BF16),
            pltpu.VMEM((LANES, GROUP_COLS), F32),
            pltpu.VMEM((LANES, GROUP_COLS), F32),
            pltpu.VMEM((rows + 2 * HALO, SSM_CONV_DIM), F32),
        ],
        compiler_params=pltpu.CompilerParams(dimension_semantics=("arbitrary",) * 3, vmem_limit_bytes=VMEM_LIMIT),
        name="bi_ssd",
    )(z, xbc, xbc, xbc, dt, mxbc, mdt, cw, cb, dtb, alog, dskip, nw, sel)


def _head_selector():
    k = jnp.arange(LANES)[:, None]
    col = jnp.arange(SSM_INNER)[None, :]
    return ((k % SSM_HEADS == col // SSM_HEADDIM) & (k < 2 * SSM_HEADS)).astype(BF16)


def _prep_weights(ffn1_norm_w, ffn1_w_gate, ffn1_w_up, ffn1_w_down, mix_norm_w, w_in, lambda_q1, lambda_k1,
                  lambda_q2, lambda_k2, attn_subln_w, conv_w, conv_b, dt_bias_fwd, dt_bias_bwd, a_log_fwd,
                  a_log_bwd, ssm_d, ssm_norm_w, w_out, ffn2_norm_w, ffn2_w_gate, ffn2_w_up, ffn2_w_down,
                  final_norm_w):
    def ffn(norm_w, wg, wu, wd):
        return norm_w[0][None, :], wg[0].astype(BF16), wu[0].astype(BF16), wd[0].astype(BF16)

    pad_lanes = lambda v, width: jnp.pad(v, (0, width - v.shape[0]))[None, :]
    o_v, o_z = 2 * ATT_QK, 2 * ATT_QK + ATT_V
    wi = w_in[0].astype(BF16)
    win = jnp.pad(jnp.concatenate([wi[:, :o_v], wi[:, o_z:]], axis=1),
                  ((0, 0), (0, D_IN_PAD - (w_in.shape[2] - ATT_V))))
    return dict(
        ffn1=ffn(ffn1_norm_w, ffn1_w_gate, ffn1_w_up, ffn1_w_down),
        ffn2=ffn(ffn2_norm_w, ffn2_w_gate, ffn2_w_up, ffn2_w_down),
        mix_norm=mix_norm_w[0][None, :],
        win=win,
        wvt=wi[:, o_v:o_z].T,
        lamv=jnp.stack([lambda_q1[0], lambda_k1[0], lambda_q2[0], lambda_k2[0]]),
        subw_col=attn_subln_w[0][:, None],
        cw=jnp.pad(conv_w[0], ((0, 8 - SSM_CONV), (0, 0))),
        cb=conv_b[0][None, :],
        dtb=pad_lanes(jnp.concatenate([dt_bias_fwd[0], dt_bias_bwd[0]]), DT_PAD),
        alog=pad_lanes(jnp.concatenate([a_log_fwd[0], a_log_bwd[0]]), DT_PAD),
        dskip=jnp.repeat(ssm_d[0], SSM_HEADDIM)[None, :],
        ssm_norm=ssm_norm_w[0][None, :],
        wo=w_out[0].astype(BF16).reshape(2, ATT_V, D_MODEL),
        final=final_norm_w[None, :],
        sel=_head_selector(),
    )


def _encode(x, w, meta_proj, tab, mtab):
    batch, seq, _ = x.shape
    km, vmt, mxbc, mdt = meta_proj
    h0 = x.reshape(batch * seq, D_MODEL)
    h1 = _ffn_call(h0, *w["ffn1"])
    q, k, vt, z, xbc, dt = _inproj_call(h1, w["mix_norm"], w["win"], w["wvt"])
    att = _attn_call(w["lamv"], q, k, vt, km, vmt, tab, mtab, w["subw_col"], batch, seq, ATT_TQ, ATT_TK)
    ssm = _ssd_call(z, xbc, dt, mxbc, mdt, w["cw"], w["cb"], w["dtb"], w["alog"], w["dskip"], w["ssm_norm"],
                    w["sel"], batch, seq, SSD_CHUNK, SSD_SUB)
    y = _ffn_call(h1, *w["ffn2"], mix=(att, ssm, w["wo"]), final_w=w["final"])
    return y.reshape(batch, seq, D_MODEL)


def kernel(x_prompt, x_sample, meta_tokens, ffn1_norm_w, ffn1_w_gate, ffn1_w_up, ffn1_w_down, mix_norm_w, w_in, rel_bias, lambda_q1, lambda_k1, lambda_q2, lambda_k2, attn_subln_w, conv_w, conv_b, dt_bias_fwd, dt_bias_bwd, a_log_fwd, a_log_bwd, ssm_d, ssm_norm_w, w_out, ffn2_norm_w, ffn2_w_gate, ffn2_w_up, ffn2_w_down, final_norm_w):
    w = _prep_weights(ffn1_norm_w, ffn1_w_gate, ffn1_w_up, ffn1_w_down, mix_norm_w, w_in, lambda_q1, lambda_k1,
                      lambda_q2, lambda_k2, attn_subln_w, conv_w, conv_b, dt_bias_fwd, dt_bias_bwd, a_log_fwd,
                      a_log_bwd, ssm_d, ssm_norm_w, w_out, ffn2_norm_w, ffn2_w_gate, ffn2_w_up, ffn2_w_down,
                      final_norm_w)
    hm = _ffn_call(meta_tokens, *w["ffn1"])
    _, km, vmt, _, mxbc, mdt = _inproj_call(hm, w["mix_norm"], w["win"], w["wvt"])
    meta_proj = (km, vmt[0], mxbc, mdt)
    tab, mtab = _bias_call(rel_bias, ATT_TQ, ATT_TK)
    return (_encode(x_prompt, w, meta_proj, tab, mtab), _encode(x_sample, w, meta_proj, tab, mtab))
```

```python
import functools
import math

import jax
import jax.numpy as jnp
from jax import lax
from jax.experimental import pallas as pl
from jax.experimental.pallas import tpu as pltpu

F32 = jnp.float32
BF16 = jnp.bfloat16

D_MODEL = 1024
N_META = 16
N_ATT_HEADS = 8
ATT_DH = 64
ATT_DV = 128
ATT_QK = 1024
ATT_V = 1024
NUM_BUCKETS = 32
MAX_DISTANCE = 128
SSM_HEADS = 16
SSM_HEADDIM = 64
SSM_INNER = 1024
SSM_GROUPS = 2
SSM_STATE = 64
SSM_CONV = 7
SSM_CONV_DIM = 1280
D_FF = 2816
EPS = 1e-6
LAYER = 0
LAM_INIT = 0.8 - 0.6 * math.exp(-0.3 * LAYER)
LOG2E = math.log2(math.e)
Q_SCALE = ATT_DH ** -0.5 * LOG2E
NEG_BIG = -1e30
NORM_SLACK = 1.02
MAX_SHIFT_GAP = 100.0

LANES = 128
BF16_ROWS = 16
VMEM_LIMIT = 56 * 1024 * 1024

FF_TILE = 256
N_FF = D_FF // FF_TILE
DT_PAD = LANES
D_IN_PAD = 2 * ATT_QK + SSM_INNER + SSM_CONV_DIM + DT_PAD
T5_BAND = 91

ROW_TILE = 512
ATT_TQ = 512
ATT_TK = 512
ATT_COLS = 256
SSD_CHUNK = 128
SSD_SUB = 4
HALO = BF16_ROWS
VT_ROWS = ATT_DV + BF16_ROWS


def _rmsnorm(x, w):
    ms = jnp.mean(x * x, axis=-1, keepdims=True)
    return x * lax.rsqrt(ms + EPS) * w


def _resident(shape):
    nd = len(shape)
    return pl.BlockSpec(shape, lambda *_: (0,) * nd, pipeline_mode=pl.Buffered(1))


def _ffn_kernel(*refs, has_mix, has_final):
    it = iter(refs)
    h_ref = next(it)
    if has_mix:
        att_ref, ssm_ref, wo_ref = next(it), next(it), next(it)
    nw_ref, wg_ref, wu_ref, wd_ref = next(it), next(it), next(it), next(it)
    fw_ref = next(it) if has_final else None
    o_ref = next(it)

    h = h_ref[...]
    if has_mix:
        h = (h + jnp.dot(att_ref[...], wo_ref[0], preferred_element_type=F32)
             + jnp.dot(ssm_ref[...], wo_ref[1], preferred_element_type=F32))
    u = _rmsnorm(h, nw_ref[...]).astype(BF16)
    acc = jnp.zeros_like(h)
    for j in range(N_FF):
        ff = slice(j * FF_TILE, (j + 1) * FF_TILE)
        g = jnp.dot(u, wg_ref[:, ff], preferred_element_type=F32)
        up = jnp.dot(u, wu_ref[:, ff], preferred_element_type=F32)
        a = (g * jax.nn.sigmoid(g) * up).astype(BF16)
        acc = acc + jnp.dot(a, wd_ref[ff, :], preferred_element_type=F32)
    h = h + 0.5 * acc
    if has_final:
        h = _rmsnorm(h, fw_ref[...])
    o_ref[...] = h


def _ffn_call(h, norm_w, wg, wu, wd, mix=None, final_w=None):
    n = h.shape[0]
    tm = min(ROW_TILE, n)
    assert n % tm == 0
    row = lambda width: pl.BlockSpec((tm, width), lambda i: (i, 0))
    args, specs = [h], [row(D_MODEL)]
    if mix is not None:
        att, ssm, wo = mix
        args += [att, ssm, wo]
        specs += [row(ATT_V), row(SSM_INNER), _resident(wo.shape)]
    args += [norm_w, wg, wu, wd]
    specs += [_resident(norm_w.shape), _resident(wg.shape), _resident(wu.shape), _resident(wd.shape)]
    if final_w is not None:
        args.append(final_w)
        specs.append(_resident(final_w.shape))
    return pl.pallas_call(
        functools.partial(_ffn_kernel, has_mix=mix is not None, has_final=final_w is not None),
        grid=(n // tm,),
        in_specs=specs,
        out_specs=row(D_MODEL),
        out_shape=jax.ShapeDtypeStruct((n, D_MODEL), F32),
        compiler_params=pltpu.CompilerParams(dimension_semantics=("arbitrary",), vmem_limit_bytes=VMEM_LIMIT),
        name="ffn_mix" if mix is not None else "ffn",
    )(*args)


_IN_SEGS = (("q", 0, ATT_QK), ("k", ATT_QK, ATT_QK), ("z", 2 * ATT_QK, SSM_INNER),
            ("xbc", 2 * ATT_QK + SSM_INNER, SSM_CONV_DIM), ("dt", D_IN_PAD - DT_PAD, DT_PAD))


def _inproj_kernel(h_ref, nw_ref, win_ref, wvt_ref, q_ref, k_ref, vt_ref, z_ref, xbc_ref, dt_ref):
    u = _rmsnorm(h_ref[...], nw_ref[...]).astype(BF16)
    outs = dict(q=q_ref, k=k_ref, z=z_ref, xbc=xbc_ref, dt=dt_ref)
    for name, c0, width in _IN_SEGS:
        o_ref = outs[name]
        step = 512 if width % 512 == 0 else (256 if width % 256 == 0 else LANES)
        for s in range(0, width, step):
            r = jnp.dot(u, win_ref[:, c0 + s:c0 + s + step], preferred_element_type=F32)
            if name == "q":
                r = r * Q_SCALE
            o_ref[:, s:s + step] = r.astype(o_ref.dtype)
    nt_dims = (((1,), (1,)), ((), ()))
    ones = jnp.ones((VT_ROWS - ATT_DV, u.shape[0]), vt_ref.dtype)
    for s in range(0, ATT_V, 256):
        r = lax.dot_general(wvt_ref[s:s + 256, :], u, nt_dims, preferred_element_type=F32).astype(vt_ref.dtype)
        for hh in range(256 // ATT_DV):
            head = s // ATT_DV + hh
            vt_ref[0, head * VT_ROWS:head * VT_ROWS + ATT_DV, :] = r[hh * ATT_DV:(hh + 1) * ATT_DV]
            vt_ref[0, head * VT_ROWS + ATT_DV:(head + 1) * VT_ROWS, :] = ones


def _inproj_call(h, norm_w, win, wvt):
    n = h.shape[0]
    tm = min(ATT_TK, n)
    assert n % tm == 0
    row = lambda width: pl.BlockSpec((tm, width), lambda i: (i, 0))
    widths = (ATT_QK, ATT_QK, SSM_INNER, SSM_CONV_DIM, DT_PAD)
    dtypes = (BF16, BF16, BF16, BF16, F32)
    shapes = [jax.ShapeDtypeStruct((n, w), dt) for w, dt in zip(widths, dtypes)]
    specs = [row(w) for w in widths]
    shapes.insert(2, jax.ShapeDtypeStruct((n // tm, N_ATT_HEADS * VT_ROWS, tm), BF16))
    specs.insert(2, pl.BlockSpec((1, N_ATT_HEADS * VT_ROWS, tm), lambda i: (i, 0, 0)))
    return pl.pallas_call(
        _inproj_kernel,
        grid=(n // tm,),
        in_specs=[row(D_MODEL), _resident(norm_w.shape), _resident(win.shape), _resident(wvt.shape)],
        out_specs=specs,
        out_shape=shapes,
        compiler_params=pltpu.CompilerParams(dimension_semantics=("arbitrary",), vmem_limit_bytes=VMEM_LIMIT),
        name="inproj",
    )(h, norm_w, win, wvt)


def _t5_bias(rel, rb_ref, head):
    half = NUM_BUCKETS // 2
    max_exact = half // 2
    ret = jnp.where(rel > 0, half, 0)
    n = jnp.abs(rel)
    nf = jnp.maximum(n, 1).astype(F32)
    large = max_exact + (jnp.log(nf / max_exact) / math.log(MAX_DISTANCE / max_exact)
                         * (half - max_exact)).astype(jnp.int32)
    large = jnp.minimum(large, half - 1)
    bucket = ret + jnp.where(n < max_exact, n, large)
    val = jnp.zeros(rel.shape, F32)
    for jb in range(NUM_BUCKETS):
        val = jnp.where(bucket == jb, rb_ref[jb, head], val)
    return val * LOG2E


def _bias_kernel(rb_ref, tab_ref, mtab_ref, *, tq, tk):
    head = pl.program_id(0)
    r = tk // tq
    n_near = r + 2
    far_left = rb_ref[NUM_BUCKETS // 2 - 1, head] * LOG2E
    far_right = rb_ref[NUM_BUCKETS - 1, head] * LOG2E
    krow = lax.broadcasted_iota(jnp.int32, (LANES, LANES), 0)
    qcol = lax.broadcasted_iota(jnp.int32, (LANES, LANES), 1)
    for t in range(n_near):
        for a in range(tk // LANES):
            for b in range(tq // LANES):
                base = (a - b) * LANES + (t - r) * tq
                blk = (slice(a * LANES, (a + 1) * LANES), slice(b * LANES, (b + 1) * LANES))
                if base + LANES - 1 <= -T5_BAND:
                    tab_ref[(0, t) + blk] = jnp.full((LANES, LANES), far_left, F32)
                elif base - LANES + 1 >= T5_BAND:
                    tab_ref[(0, t) + blk] = jnp.full((LANES, LANES), far_right, F32)
                else:
                    tab_ref[(0, t) + blk] = _t5_bias(krow - qcol + base, rb_ref, head)
    tab_ref[0, n_near] = jnp.full((tk, tq), far_left, F32)
    tab_ref[0, n_near + 1] = jnp.full((tk, tq), far_right, F32)
    mrow = lax.broadcasted_iota(jnp.int32, (N_META, tq), 0)
    mcol = lax.broadcasted_iota(jnp.int32, (N_META, tq), 1)
    mtab_ref[0, 0] = _t5_bias(mrow - N_META - mcol, rb_ref, head)
    mtab_ref[0, 1] = jnp.full((N_META, tq), far_left, F32)


def _bias_call(rel_bias, tq, tk):
    assert tk % tq == 0 and tq >= T5_BAND + 1
    nt = tk // tq + 4
    return pl.pallas_call(
        functools.partial(_bias_kernel, tq=tq, tk=tk),
        grid=(N_ATT_HEADS,),
        in_specs=[pl.BlockSpec(memory_space=pltpu.SMEM)],
        out_specs=[pl.BlockSpec((1, nt, tk, tq), lambda h: (h, 0, 0, 0)),
                   pl.BlockSpec((1, 2, N_META, tq), lambda h: (h, 0, 0, 0))],
        out_shape=[jax.ShapeDtypeStruct((N_ATT_HEADS, nt, tk, tq), F32),
                   jax.ShapeDtypeStruct((N_ATT_HEADS, 2, N_META, tq), F32)],
        compiler_params=pltpu.CompilerParams(dimension_semantics=("arbitrary",)),
        name="t5_bias",
    )(rel_bias)


def _attn_kernel(lam_ref, q_ref, qall_ref, k_ref, vt_ref, km_ref, vmt_ref, tab_ref, mtab_ref, sw_ref, o_ref,
                 sa_ref, sb_ref, mca_ref, mcb_ref, m_scr, acc_scr, accp_scr, shift_scr, flag_scr,
                 *, tq, tk, nkc, nq, n_steps):
    g = pl.program_id(0)
    qi = jnp.minimum(g, n_steps - 1) % nq
    r = tk // tq
    n_near = r + 2
    nt_dims = (((1,), (1,)), ((), ()))
    n_col = 2 * tq // ATT_COLS

    def finalize_previous():
        acc = accp_scr[...]
        o = acc[:ATT_DV] / acc[ATT_DV:ATT_DV + 1]
        lv = lam_ref[...]
        lam = (jnp.exp(jnp.sum(lv[0:1] * lv[1:2], axis=1, keepdims=True))
               - jnp.exp(jnp.sum(lv[2:3] * lv[3:4], axis=1, keepdims=True)) + LAM_INIT)
        out = o[:, :tq] - lam * o[:, tq:]
        ms = jnp.mean(out * out, axis=0, keepdims=True)
        out = out * lax.rsqrt(ms + EPS) * sw_ref[...] * (1.0 - LAM_INIT)
        o_ref[...] = out.T.astype(o_ref.dtype)

    @pl.when(g == 0)
    def _():
        accp_scr[...] = jnp.ones(accp_scr.shape, F32)

    @pl.when(g < n_steps)
    def _():
        q = q_ref[...]
        lane = lax.broadcasted_iota(jnp.int32, (tq, LANES), 1)
        zero = jnp.zeros_like(q)
        q2 = jnp.concatenate([jnp.where(lane < ATT_DH, q, zero), jnp.where(lane >= ATT_DH, q, zero)], axis=0)

        def bias_index(j):
            du = j * r - qi
            return jnp.where(du <= -(r + 1), n_near, jnp.where(du >= 2, n_near + 1, du + r))

        def add_bias(s, b):
            return jnp.concatenate([s[:, :tq] + b, s[:, tq:] + b], axis=1)

        @pl.when(qi == 0)
        def _():
            half = jnp.where(lax.broadcasted_iota(jnp.int32, (LANES, LANES), 0) // ATT_DH
                             == lax.broadcasted_iota(jnp.int32, (LANES, LANES), 1), 1.0, 0.0).astype(BF16)

            def max_sq_norm(x_ref):
                x32 = x_ref[...].astype(F32)
                sq = jnp.dot((x32 * x32).astype(BF16), half, preferred_element_type=F32)
                return jnp.max(sq, axis=0, keepdims=True)

            bound2 = max_sq_norm(qall_ref) * jnp.maximum(max_sq_norm(k_ref), max_sq_norm(km_ref))
            lane_row = lax.broadcasted_iota(jnp.int32, (1, LANES), 1)
            qk_bound = [NORM_SLACK * jnp.sqrt(jnp.max(jnp.where(lane_row == mp, bound2, 0.0), axis=1, keepdims=True))
                        for mp in range(2)]
            diag_tile = tab_ref[0, r]
            hi_b = jnp.max(jnp.max(diag_tile, axis=0, keepdims=True), axis=1, keepdims=True)
            lo_b = jnp.min(jnp.min(diag_tile, axis=0, keepdims=True), axis=1, keepdims=True)
            col = lax.broadcasted_iota(jnp.int32, (1, 2 * tq), 1)
            shift_scr[...] = jnp.where(col < tq, qk_bound[0], qk_bound[1]) + hi_b
            worst_gap = 2.0 * jnp.maximum(qk_bound[0], qk_bound[1]) + (hi_b - lo_b)
            flag_scr[0] = (worst_gap[0, 0] <= MAX_SHIFT_GAP).astype(jnp.int32)

        shift = shift_scr[...]
        bounded = flag_scr[0] == 1

        @pl.when(bounded)
        def _():
            finalize_previous()
            sm = lax.dot_general(km_ref[...], q2, nt_dims, preferred_element_type=F32)
            sm = add_bias(sm, mtab_ref[0, jnp.minimum(qi, 1)])
            acc_scr[...] = jnp.dot(vmt_ref[...], jnp.exp2(sm - shift).astype(BF16), preferred_element_type=F32)

            def stage_logits(u, j, c):
                buf = (sa_ref, sb_ref)[u % 2]
                buf[:, pl.ds(c * ATT_COLS, ATT_COLS)] = lax.dot_general(
                    k_ref[pl.ds(j * tk, tk), :], q2[c * ATT_COLS:(c + 1) * ATT_COLS], nt_dims,
                    preferred_element_type=F32)

            units = [(j, c) for j in range(nkc) for c in range(n_col)]
            stage_logits(0, *units[0])
            for u, (j, c) in enumerate(units):
                if u + 1 < len(units):
                    stage_logits(u + 1, *units[u + 1])
                cols = pl.ds(c * ATT_COLS, ATT_COLS)
                s = (sa_ref, sb_ref)[u % 2][:, cols] + tab_ref[0, bias_index(j), :, pl.ds((c * ATT_COLS) % tq, ATT_COLS)]
                p = jnp.exp2(s - shift[:, c * ATT_COLS:(c + 1) * ATT_COLS]).astype(BF16)
                acc_scr[:, cols] += jnp.dot(vt_ref[j], p, preferred_element_type=F32)
            accp_scr[...] = acc_scr[...]

        @pl.when(jnp.logical_not(bounded))
        def _():
            finalize_previous()

            def produce(j, c, s_ref, mc_ref):
                cols = pl.ds(c * ATT_COLS, ATT_COLS)
                s = lax.dot_general(k_ref[pl.ds(j * tk, tk), :], q2[c * ATT_COLS:(c + 1) * ATT_COLS], nt_dims,
                                    preferred_element_type=F32)
                s = s + tab_ref[0, bias_index(j), :, pl.ds((c * ATT_COLS) % tq, ATT_COLS)]
                s_ref[:, cols] = s
                mc_ref[:, cols] = jnp.max(s, axis=0, keepdims=True)

            def consume(s, m_cur, vt, cols, first=False):
                if first:
                    m_new = m_cur
                else:
                    m_prev = m_scr[:, cols]
                    m_new = jnp.maximum(m_prev, m_cur)
                    alpha = jnp.exp2(m_prev - m_new)
                p = jnp.exp2(s - m_new).astype(BF16)
                pv = jnp.dot(vt, p, preferred_element_type=F32)
                acc_scr[:, cols] = pv if first else alpha * acc_scr[:, cols] + pv
                m_scr[:, cols] = m_new

            sm = lax.dot_general(km_ref[...], q2, nt_dims, preferred_element_type=F32)
            sm = add_bias(sm, mtab_ref[0, jnp.minimum(qi, 1)])
            consume(sm, jnp.max(sm, axis=0, keepdims=True), vmt_ref[...], pl.ds(0, 2 * tq), first=True)

            bufs = ((sa_ref, mca_ref), (sb_ref, mcb_ref))
            for c in range(n_col):
                produce(0, c, *bufs[0])
            for j in range(nkc):
                s_ref, mc_ref = bufs[j % 2]
                for c in range(n_col):
                    cols = pl.ds(c * ATT_COLS, ATT_COLS)
                    if j + 1 < nkc:
                        produce(j + 1, c, *bufs[(j + 1) % 2])
                    consume(s_ref[:, cols], mc_ref[:, cols], vt_ref[j], cols)
            accp_scr[...] = acc_scr[...]

    @pl.when(g == n_steps)
    def _():
        finalize_previous()


def _attn_call(lamv, q, k, vt, km, vmt, tab, mtab, subw_col, batch, seq, tq, tk):
    n = q.shape[0]
    assert n == batch * seq and seq % (2 * tk) == 0 and seq % tq == 0 and vt.shape[2] == tk
    nq = seq // tq
    nkc = seq // tk
    nt = tab.shape[1]
    n_steps = N_ATT_HEADS * batch * nq

    def tile(g):
        g = jnp.minimum(g, n_steps - 1)
        return g // (batch * nq), (g // nq) % batch, g % nq

    def cur(f):
        return lambda g: f(*tile(g))

    def prev(f):
        return lambda g: f(*tile(jnp.maximum(g - 1, 0)))

    return pl.pallas_call(
        functools.partial(_attn_kernel, tq=tq, tk=tk, nkc=nkc, nq=nq, n_steps=n_steps),
        grid=(n_steps + 1,),
        in_specs=[
            pl.BlockSpec(lamv.shape, lambda g: (0, 0)),
            pl.BlockSpec((tq, LANES), cur(lambda h, b, i: (b * nq + i, h))),
            pl.BlockSpec((seq, LANES), cur(lambda h, b, i: (b, h))),
            pl.BlockSpec((seq, LANES), cur(lambda h, b, i: (b, h))),
            pl.BlockSpec((nkc, VT_ROWS, tk), cur(lambda h, b, i: (b, h, 0))),
            pl.BlockSpec((N_META, LANES), cur(lambda h, b, i: (0, h))),
            pl.BlockSpec((VT_ROWS, N_META), cur(lambda h, b, i: (h, 0))),
            pl.BlockSpec((1, nt, tk, tq), cur(lambda h, b, i: (h, 0, 0, 0))),
            pl.BlockSpec((1, 2, N_META, tq), cur(lambda h, b, i: (h, 0, 0, 0))),
            pl.BlockSpec(subw_col.shape, lambda g: (0, 0)),
        ],
        out_specs=pl.BlockSpec((tq, LANES), prev(lambda h, b, i: (b * nq + i, h))),
        out_shape=jax.ShapeDtypeStruct((n, ATT_V), BF16),
        scratch_shapes=[pltpu.VMEM((tk, 2 * tq), F32)] * 2 + [pltpu.VMEM((1, 2 * tq), F32)] * 3
        + [pltpu.VMEM((VT_ROWS, 2 * tq), F32)] * 2 + [pltpu.VMEM((1, 2 * tq), F32), pltpu.SMEM((1,), jnp.int32)],
        compiler_params=pltpu.CompilerParams(dimension_semantics=("arbitrary",), vmem_limit_bytes=VMEM_LIMIT),
        name="diff_attn",
    )(lamv, q, q, k, vt, km, vmt, tab, mtab, subw_col)


def _split3(x):
    hi = x.astype(BF16)
    r1 = x - hi.astype(F32)
    mid = r1.astype(BF16)
    lo = (r1 - mid.astype(F32)).astype(BF16)
    return hi, mid, lo


def _cumsum_rows(a):
    rows = a.shape[0]
    r_i = lax.broadcasted_iota(jnp.int32, (rows, rows), 0)
    c_i = lax.broadcasted_iota(jnp.int32, (rows, rows), 1)
    tri = jnp.where(c_i <= r_i, 1.0, 0.0).astype(BF16)
    out = None
    for term in _split3(a):
        part = jnp.dot(tri, term, preferred_element_type=F32)
        out = part if out is None else out + part
    return out


def _expand_rows(parts, sel_ref):
    masked = []
    for w, first in parts:
        lane = lax.broadcasted_iota(jnp.int32, w.shape, 1)
        masked.append(jnp.where((lane >= first) & (lane < first + SSM_HEADS), w, 0.0))
    stacked = jnp.concatenate(masked, axis=0)
    hi = stacked.astype(BF16)
    lo = (stacked - hi.astype(F32)).astype(BF16)
    sel = sel_ref[...]
    full = jnp.dot(hi, sel, preferred_element_type=F32) + jnp.dot(lo, sel, preferred_element_type=F32)
    outs, r0 = [], 0
    for w, _ in parts:
        outs.append(full[r0:r0 + w.shape[0]])
        r0 += w.shape[0]
    return outs


def _softplus(x):
    return jnp.maximum(x, 0.0) + jnp.log(1.0 + jnp.exp(-jnp.abs(x)))


GROUP_COLS = SSM_INNER // SSM_GROUPS


def _state_update(b_t, xw):
    return jnp.concatenate(
        [jnp.dot(b_t[g * SSM_STATE:(g + 1) * SSM_STATE], xw[:, g * GROUP_COLS:(g + 1) * GROUP_COLS],
                 preferred_element_type=F32) for g in range(SSM_GROUPS)], axis=0)


def _stack_decay(dec_row):
    return jnp.concatenate(
        [jnp.broadcast_to(dec_row[:, g * GROUP_COLS:(g + 1) * GROUP_COLS], (SSM_STATE, GROUP_COLS))
         for g in range(SSM_GROUPS)], axis=0)


def _conv_silu(win, shift_ref, cw_ref, cb_ref, rows):
    assert win.shape[0] == rows + 2 * HALO and shift_ref.shape == ((SSM_CONV - 1) * rows, rows + 2 * HALO)
    shifted = jnp.dot(shift_ref[...], win, preferred_element_type=F32)
    acc = cb_ref[...] + cw_ref[SSM_CONV // 2:SSM_CONV // 2 + 1, :] * win[HALO:HALO + rows].astype(F32)
    taps = [j for j in range(SSM_CONV) if j != SSM_CONV // 2]
    for n, j in enumerate(taps):
        acc = acc + cw_ref[j:j + 1, :] * shifted[n * rows:(n + 1) * rows]
    return acc * jax.nn.sigmoid(acc)


def _shift_matrix(rows):
    offs = jnp.array([j - SSM_CONV // 2 for j in range(SSM_CONV) if j != SSM_CONV // 2])
    t = jnp.arange(rows)
    src = HALO + t[None, :] + offs[:, None]
    return (src.reshape(-1)[:, None] == jnp.arange(rows + 2 * HALO)[None, :]).astype(BF16)


def _ssd_kernel(z_ref, xc_ref, xl_ref, xr_ref, dt_ref, mx_ref, mdt_ref, cw_ref, cb_ref, dtb_ref, alog_ref,
                dsk_ref, nw_ref, sel_ref, shc_ref, shm_ref, o_ref, xs_scr, dts_scr, cum_scr, hbs_scr, hf_scr, hb_scr,
                win_scr,
                *, cs, sub, nb):
    rows = cs * sub
    ph = pl.program_id(1)
    t = pl.program_id(2)
    fwd0, bwd0 = 0, SSM_HEADS
    a_row = -jnp.exp(alog_ref[...])

    def decay_terms(dt_raw):
        dt = _softplus(dt_raw + dtb_ref[...])
        return dt, _cumsum_rows(dt * a_row)

    def bcast8(row):
        return jnp.broadcast_to(row, (8, LANES))

    @pl.when(ph == 0)
    def _():
        blk = nb - 1 - t

        @pl.when(t == 0)
        def _():
            hb_scr[...] = jnp.zeros(hb_scr.shape, F32)

        left = jnp.where(blk == 0, mx_ref[...], xl_ref[...])
        right = jnp.where(blk == nb - 1, jnp.zeros_like(xr_ref[...]), xr_ref[...])
        win_scr[0:HALO, :] = left
        win_scr[HALO:HALO + rows, :] = xc_ref[...]
        win_scr[HALO + rows:HALO + rows + HALO, :] = right

        hb = hb_scr[...]
        for si in reversed(range(sub)):
            cc = blk * sub + si
            xbc = _conv_silu(win_scr[si * cs:si * cs + cs + 2 * HALO, :], shc_ref, cw_ref, cb_ref, cs)
            xs_scr[cc] = xbc.astype(BF16)
            dt, cum = decay_terms(dt_ref[si * cs:(si + 1) * cs, :])
            dts_scr[cc] = dt
            cum_scr[cc] = cum
            eb = cum - dt * a_row
            w_b, dec = _expand_rows([(jnp.exp(eb) * dt, bwd0), (bcast8(jnp.exp(cum[cs - 1:cs, :])), bwd0)],
                                    sel_ref)
            xw = (xbc[:, :SSM_INNER] * w_b).astype(BF16)
            bm_t = xbc[:, SSM_INNER:SSM_INNER + LANES].T.astype(BF16)
            hbs_scr[cc] = hb.astype(BF16)
            hb = hb * _stack_decay(dec[0:1]) + _state_update(bm_t, xw)
        hb_scr[...] = hb

    @pl.when(ph == 1)
    def _():
        @pl.when(t == 0)
        def _():
            wm = jnp.concatenate([jnp.zeros((HALO, SSM_CONV_DIM), BF16), mx_ref[...], xc_ref[0:HALO, :]], axis=0)
            xm = _conv_silu(wm, shm_ref, cw_ref, cb_ref, N_META)
            dtm, cumm = decay_terms(mdt_ref[...])
            (w_m,) = _expand_rows([(jnp.exp(cumm[N_META - 1:N_META, :] - cumm) * dtm, fwd0)], sel_ref)
            xwm = (xm[:, :SSM_INNER] * w_m).astype(BF16)
            bmm_t = xm[:, SSM_INNER:SSM_INNER + LANES].T.astype(BF16)
            hf_scr[...] = _state_update(bmm_t, xwm)

        lane = lax.broadcasted_iota(jnp.int32, (cs, LANES), 1)
        l_i = lax.broadcasted_iota(jnp.int32, (cs, cs), 0)
        s_i = lax.broadcasted_iota(jnp.int32, (cs, cs), 1)
        lower = s_i <= l_i
        diag = s_i == l_i
        hpg = SSM_HEADS // SSM_GROUPS
        zx = jnp.zeros((cs, LANES), BF16)
        nt_dims = (((1,), (1,)), ((), ()))

        hf = hf_scr[...]
        for si in range(sub):
            cc = t * sub + si
            xbc = xs_scr[cc]
            x_bf = xbc[:, :SSM_INNER]
            bm = xbc[:, SSM_INNER:SSM_INNER + LANES]
            cm = xbc[:, SSM_INNER + LANES:SSM_INNER + 2 * LANES]
            x = x_bf.astype(F32)

            dt = dts_scr[cc]
            cum = cum_scr[cc]
            eb = cum - dt * a_row
            dt_t, cum_t, eb_t = dt.T, cum.T, eb.T
            last = cum[cs - 1:cs, :]

            c_grp = [jnp.where(lane // SSM_STATE == g, cm, jnp.zeros_like(cm)) for g in range(SSM_GROUPS)]
            g_mats = [lax.dot_general(c_g, bm, nt_dims, preferred_element_type=F32) for c_g in c_grp]

            pieces = []
            for hp in range(SSM_HEADS // 2):
                w_pair = []
                for h in (2 * hp, 2 * hp + 1):
                    arg_f = cum[:, fwd0 + h:fwd0 + h + 1] - cum_t[fwd0 + h:fwd0 + h + 1, :]
                    arg_b = eb_t[bwd0 + h:bwd0 + h + 1, :] - eb[:, bwd0 + h:bwd0 + h + 1]
                    e = jnp.exp(jnp.minimum(jnp.where(lower, arg_f, arg_b), 0.0))
                    dt_f_row = dt_t[fwd0 + h:fwd0 + h + 1, :]
                    dt_b_row = dt_t[bwd0 + h:bwd0 + h + 1, :]
                    m = e * jnp.where(lower, dt_f_row, dt_b_row) + jnp.where(diag, dt_b_row, 0.0)
                    w_pair.append((g_mats[h // hpg] * m).astype(BF16))
                xp = x_bf[:, hp * LANES:(hp + 1) * LANES]
                rhs = jnp.concatenate([jnp.where(lane < SSM_HEADDIM, xp, zx),
                                       jnp.where(lane >= SSM_HEADDIM, xp, zx)], axis=0)
                pieces.append(jnp.dot(jnp.concatenate(w_pair, axis=1), rhs, preferred_element_type=F32))
            y = jnp.concatenate(pieces, axis=1)

            d_f, d_b, w_f, dec = _expand_rows(
                [(jnp.exp(cum), fwd0), (jnp.exp(last - eb), bwd0), (jnp.exp(last - cum) * dt, fwd0),
                 (bcast8(jnp.exp(last)), fwd0)], sel_ref)
            hf_bf = hf.astype(BF16)
            hb_bf = hbs_scr[cc]
            y = y + d_f * jnp.concatenate([jnp.dot(c_g, hf_bf, preferred_element_type=F32) for c_g in c_grp],
                                          axis=1)
            y = y + d_b * jnp.concatenate([jnp.dot(c_g, hb_bf, preferred_element_type=F32) for c_g in c_grp],
                                          axis=1)
            y = y + x * dsk_ref[...]

            xw = (x * w_f).astype(BF16)
            hf = hf * _stack_decay(dec[0:1]) + _state_update(bm.astype(F32).T.astype(BF16), xw)

            zf = z_ref[si * cs:(si + 1) * cs, :].astype(F32)
            y = y * (zf * jax.nn.sigmoid(zf))
            o_ref[si * cs:(si + 1) * cs, :] = _rmsnorm(y, nw_ref[...]).astype(o_ref.dtype)
        hf_scr[...] = hf


def _ssd_call(z, xbc, dt, mxbc, mdt, cw, cb, dtb, alog, dskip, nw, sel, shc, shm, batch, seq, cs, sub):
    n = z.shape[0]
    rows = cs * sub
    assert n == batch * seq and seq % rows == 0 and cs % HALO == 0
    nc = seq // cs
    nb = seq // rows
    hpb = rows // HALO
    n_halo = n // HALO

    def ph0_block(ph, t):
        return (1 - ph) * (nb - 1 - t)

    const2 = lambda shape: pl.BlockSpec(shape, lambda b, ph, t: (0, 0))
    return pl.pallas_call(
        functools.partial(_ssd_kernel, cs=cs, sub=sub, nb=nb),
        grid=(batch, 2, nb),
        in_specs=[
            pl.BlockSpec((rows, SSM_INNER), lambda b, ph, t: (b * nb + ph * t, 0)),
            pl.BlockSpec((rows, SSM_CONV_DIM), lambda b, ph, t: (b * nb + ph0_block(ph, t), 0)),
            pl.BlockSpec((HALO, SSM_CONV_DIM),
                         lambda b, ph, t: (jnp.maximum((b * nb + ph0_block(ph, t)) * hpb - 1, 0), 0)),
            pl.BlockSpec((HALO, SSM_CONV_DIM),
                         lambda b, ph, t: (jnp.minimum((b * nb + ph0_block(ph, t) + 1) * hpb, n_halo - 1), 0)),
            pl.BlockSpec((rows, DT_PAD), lambda b, ph, t: (b * nb + ph0_block(ph, t), 0)),
            const2(mxbc.shape), const2(mdt.shape), const2(cw.shape), const2(cb.shape), const2(dtb.shape),
            const2(alog.shape), const2(dskip.shape), const2(nw.shape), const2(sel.shape), const2(shc.shape),
            const2(shm.shape),
        ],
        out_specs=pl.BlockSpec((rows, SSM_INNER), lambda b, ph, t: (b * nb + ph * t, 0)),
        out_shape=jax.ShapeDtypeStruct((n, SSM_INNER), BF16),
        scratch_shapes=[
            pltpu.VMEM((nc, cs, SSM_CONV_DIM), BF16),
            pltpu.VMEM((nc, cs, DT_PAD), F32),
            pltpu.VMEM((nc, cs, DT_PAD), F32),
            pltpu.VMEM((nc, LANES, GROUP_COLS), BF16),
            pltpu.VMEM((LANES, GROUP_COLS), F32),
            pltpu.VMEM((LANES, GROUP_COLS), F32),
            pltpu.VMEM((rows + 2 * HALO, SSM_CONV_DIM), BF16),
        ],
        compiler_params=pltpu.CompilerParams(dimension_semantics=("arbitrary",) * 3, vmem_limit_bytes=VMEM_LIMIT),
        name="bi_ssd",
    )(z, xbc, xbc, xbc, dt, mxbc, mdt, cw, cb, dtb, alog, dskip, nw, sel, shc, shm)


def _head_selector():
    k = jnp.arange(LANES)[:, None]
    col = jnp.arange(SSM_INNER)[None, :]
    return ((k % SSM_HEADS == col // SSM_HEADDIM) & (k < 2 * SSM_HEADS)).astype(BF16)


def _prep_weights(ffn1_norm_w, ffn1_w_gate, ffn1_w_up, ffn1_w_down, mix_norm_w, w_in, lambda_q1, lambda_k1,
                  lambda_q2, lambda_k2, attn_subln_w, conv_w, conv_b, dt_bias_fwd, dt_bias_bwd, a_log_fwd,
                  a_log_bwd, ssm_d, ssm_norm_w, w_out, ffn2_norm_w, ffn2_w_gate, ffn2_w_up, ffn2_w_down,
                  final_norm_w):
    def ffn(norm_w, wg, wu, wd):
        return norm_w[0][None, :], wg[0].astype(BF16), wu[0].astype(BF16), wd[0].astype(BF16)

    pad_lanes = lambda v, width: jnp.pad(v, (0, width - v.shape[0]))[None, :]
    o_v, o_z = 2 * ATT_QK, 2 * ATT_QK + ATT_V
    wi = w_in[0].astype(BF16)
    win = jnp.pad(jnp.concatenate([wi[:, :o_v], wi[:, o_z:]], axis=1),
                  ((0, 0), (0, D_IN_PAD - (w_in.shape[2] - ATT_V))))
    return dict(
        ffn1=ffn(ffn1_norm_w, ffn1_w_gate, ffn1_w_up, ffn1_w_down),
        ffn2=ffn(ffn2_norm_w, ffn2_w_gate, ffn2_w_up, ffn2_w_down),
        mix_norm=mix_norm_w[0][None, :],
        win=win,
        wvt=wi[:, o_v:o_z].T,
        lamv=jnp.stack([lambda_q1[0], lambda_k1[0], lambda_q2[0], lambda_k2[0]]),
        subw_col=attn_subln_w[0][:, None],
        cw=jnp.pad(conv_w[0], ((0, 8 - SSM_CONV), (0, 0))),
        cb=conv_b[0][None, :],
        dtb=pad_lanes(jnp.concatenate([dt_bias_fwd[0], dt_bias_bwd[0]]), DT_PAD),
        alog=pad_lanes(jnp.concatenate([a_log_fwd[0], a_log_bwd[0]]), DT_PAD),
        dskip=jnp.repeat(ssm_d[0], SSM_HEADDIM)[None, :],
        ssm_norm=ssm_norm_w[0][None, :],
        wo=w_out[0].astype(BF16).reshape(2, ATT_V, D_MODEL),
        final=final_norm_w[None, :],
        sel=_head_selector(),
        shc=_shift_matrix(SSD_CHUNK),
        shm=_shift_matrix(N_META),
    )


def _encode(x, w, meta_proj, tab, mtab):
    batch, seq, _ = x.shape
    km, vmt, mxbc, mdt = meta_proj
    h0 = x.reshape(batch * seq, D_MODEL)
    h1 = _ffn_call(h0, *w["ffn1"])
    q, k, vt, z, xbc, dt = _inproj_call(h1, w["mix_norm"], w["win"], w["wvt"])
    att = _attn_call(w["lamv"], q, k, vt, km, vmt, tab, mtab, w["subw_col"], batch, seq, ATT_TQ, ATT_TK)
    ssm = _ssd_call(z, xbc, dt, mxbc, mdt, w["cw"], w["cb"], w["dtb"], w["alog"], w["dskip"], w["ssm_norm"],
                    w["sel"], w["shc"], w["shm"], batch, seq, SSD_CHUNK, SSD_SUB)
    y = _ffn_call(h1, *w["ffn2"], mix=(att, ssm, w["wo"]), final_w=w["final"])
    return y.reshape(batch, seq, D_MODEL)


def kernel(x_prompt, x_sample, meta_tokens, ffn1_norm_w, ffn1_w_gate, ffn1_w_up, ffn1_w_down, mix_norm_w, w_in, rel_bias, lambda_q1, lambda_k1, lambda_q2, lambda_k2, attn_subln_w, conv_w, conv_b, dt_bias_fwd, dt_bias_bwd, a_log_fwd, a_log_bwd, ssm_d, ssm_norm_w, w_out, ffn2_norm_w, ffn2_w_gate, ffn2_w_up, ffn2_w_down, final_norm_w):
    w = _prep_weights(ffn1_norm_w, ffn1_w_gate, ffn1_w_up, ffn1_w_down, mix_norm_w, w_in, lambda_q1, lambda_k1,
                      lambda_q2, lambda_k2, attn_subln_w, conv_w, conv_b, dt_bias_fwd, dt_bias_bwd, a_log_fwd,
                      a_log_bwd, ssm_d, ssm_norm_w, w_out, ffn2_norm_w, ffn2_w_gate, ffn2_w_up, ffn2_w_down,
                      final_norm_w)
    hm = _ffn_call(meta_tokens, *w["ffn1"])
    _, km, vmt, _, mxbc, mdt = _inproj_call(hm, w["mix_norm"], w["win"], w["wvt"])
    meta_proj = (km, vmt[0], mxbc, mdt)
    tab, mtab = _bias_call(rel_bias, ATT_TQ, ATT_TK)
    return (_encode(x_prompt, w, meta_proj, tab, mtab), _encode(x_sample, w, meta_proj, tab, mtab))
```

```python
import functools
import math

import jax
import jax.numpy as jnp
from jax import lax
from jax.experimental import pallas as pl
from jax.experimental.pallas import tpu as pltpu

F32 = jnp.float32
BF16 = jnp.bfloat16

D_MODEL = 1024
N_META = 16
N_ATT_HEADS = 8
ATT_DH = 64
ATT_DV = 128
ATT_QK = 1024
ATT_V = 1024
NUM_BUCKETS = 32
MAX_DISTANCE = 128
SSM_HEADS = 16
SSM_HEADDIM = 64
SSM_INNER = 1024
SSM_GROUPS = 2
SSM_STATE = 64
SSM_CONV = 7
SSM_CONV_DIM = 1280
D_FF = 2816
EPS = 1e-6
LAYER = 0
LAM_INIT = 0.8 - 0.6 * math.exp(-0.3 * LAYER)
LOG2E = math.log2(math.e)
Q_SCALE = ATT_DH ** -0.5 * LOG2E
NEG_BIG = -1e30
NORM_SLACK = 1.02
MAX_SHIFT_GAP = 100.0

LANES = 128
BF16_ROWS = 16
VMEM_LIMIT = 56 * 1024 * 1024

FF_TILE = 256
N_FF = D_FF // FF_TILE
DT_PAD = LANES
D_IN_PAD = 2 * ATT_QK + SSM_INNER + SSM_CONV_DIM + DT_PAD
T5_BAND = 91

ROW_TILE = 512
ATT_TQ = 512
ATT_TK = 512
ATT_COLS = 256
SSD_CHUNK = 128
SSD_SUB = 4
HALO = BF16_ROWS
VT_ROWS = ATT_DV + BF16_ROWS


def _rmsnorm(x, w):
    ms = jnp.mean(x * x, axis=-1, keepdims=True)
    return x * lax.rsqrt(ms + EPS) * w


def _resident(shape):
    nd = len(shape)
    return pl.BlockSpec(shape, lambda *_: (0,) * nd, pipeline_mode=pl.Buffered(1))


def _ffn_kernel(*refs, has_mix, has_final):
    it = iter(refs)
    h_ref = next(it)
    if has_mix:
        att_ref, ssm_ref, wo_ref = next(it), next(it), next(it)
    nw_ref, wg_ref, wu_ref, wd_ref = next(it), next(it), next(it), next(it)
    fw_ref = next(it) if has_final else None
    o_ref = next(it)

    h = h_ref[...]
    if has_mix:
        h = (h + jnp.dot(att_ref[...], wo_ref[0], preferred_element_type=F32)
             + jnp.dot(ssm_ref[...], wo_ref[1], preferred_element_type=F32))
    u = _rmsnorm(h, nw_ref[...]).astype(BF16)
    acc = jnp.zeros_like(h)
    for j in range(N_FF):
        ff = slice(j * FF_TILE, (j + 1) * FF_TILE)
        g = jnp.dot(u, wg_ref[:, ff], preferred_element_type=F32)
        up = jnp.dot(u, wu_ref[:, ff], preferred_element_type=F32)
        a = (g * jax.nn.sigmoid(g) * up).astype(BF16)
        acc = acc + jnp.dot(a, wd_ref[ff, :], preferred_element_type=F32)
    h = h + 0.5 * acc
    if has_final:
        h = _rmsnorm(h, fw_ref[...])
    o_ref[...] = h


def _ffn_call(h, norm_w, wg, wu, wd, mix=None, final_w=None):
    n = h.shape[0]
    tm = min(ROW_TILE, n)
    assert n % tm == 0
    row = lambda width: pl.BlockSpec((tm, width), lambda i: (i, 0))
    args, specs = [h], [row(D_MODEL)]
    if mix is not None:
        att, ssm, wo = mix
        args += [att, ssm, wo]
        specs += [row(ATT_V), row(SSM_INNER), _resident(wo.shape)]
    args += [norm_w, wg, wu, wd]
    specs += [_resident(norm_w.shape), _resident(wg.shape), _resident(wu.shape), _resident(wd.shape)]
    if final_w is not None:
        args.append(final_w)
        specs.append(_resident(final_w.shape))
    return pl.pallas_call(
        functools.partial(_ffn_kernel, has_mix=mix is not None, has_final=final_w is not None),
        grid=(n // tm,),
        in_specs=specs,
        out_specs=row(D_MODEL),
        out_shape=jax.ShapeDtypeStruct((n, D_MODEL), F32),
        compiler_params=pltpu.CompilerParams(dimension_semantics=("arbitrary",), vmem_limit_bytes=VMEM_LIMIT),
        name="ffn_mix" if mix is not None else "ffn",
    )(*args)


_IN_SEGS = (("q", 0, ATT_QK), ("k", ATT_QK, ATT_QK), ("z", 2 * ATT_QK, SSM_INNER),
            ("xbc", 2 * ATT_QK + SSM_INNER, SSM_CONV_DIM), ("dt", D_IN_PAD - DT_PAD, DT_PAD))


def _inproj_kernel(h_ref, nw_ref, win_ref, wvt_ref, q_ref, k_ref, vt_ref, z_ref, xbc_ref, dt_ref):
    u = _rmsnorm(h_ref[...], nw_ref[...]).astype(BF16)
    outs = dict(q=q_ref, k=k_ref, z=z_ref, xbc=xbc_ref, dt=dt_ref)
    for name, c0, width in _IN_SEGS:
        o_ref = outs[name]
        step = 512 if width % 512 == 0 else (256 if width % 256 == 0 else LANES)
        for s in range(0, width, step):
            r = jnp.dot(u, win_ref[:, c0 + s:c0 + s + step], preferred_element_type=F32)
            if name == "q":
                r = r * Q_SCALE
            o_ref[:, s:s + step] = r.astype(o_ref.dtype)
    nt_dims = (((1,), (1,)), ((), ()))
    ones = jnp.ones((VT_ROWS - ATT_DV, u.shape[0]), vt_ref.dtype)
    for s in range(0, ATT_V, 256):
        r = lax.dot_general(wvt_ref[s:s + 256, :], u, nt_dims, preferred_element_type=F32).astype(vt_ref.dtype)
        for hh in range(256 // ATT_DV):
            head = s // ATT_DV + hh
            vt_ref[0, head * VT_ROWS:head * VT_ROWS + ATT_DV, :] = r[hh * ATT_DV:(hh + 1) * ATT_DV]
            vt_ref[0, head * VT_ROWS + ATT_DV:(head + 1) * VT_ROWS, :] = ones


def _inproj_call(h, norm_w, win, wvt):
    n = h.shape[0]
    tm = min(ATT_TK, n)
    assert n % tm == 0
    row = lambda width: pl.BlockSpec((tm, width), lambda i: (i, 0))
    widths = (ATT_QK, ATT_QK, SSM_INNER, SSM_CONV_DIM, DT_PAD)
    dtypes = (BF16, BF16, BF16, BF16, F32)
    shapes = [jax.ShapeDtypeStruct((n, w), dt) for w, dt in zip(widths, dtypes)]
    specs = [row(w) for w in widths]
    shapes.insert(2, jax.ShapeDtypeStruct((n // tm, N_ATT_HEADS * VT_ROWS, tm), BF16))
    specs.insert(2, pl.BlockSpec((1, N_ATT_HEADS * VT_ROWS, tm), lambda i: (i, 0, 0)))
    return pl.pallas_call(
        _inproj_kernel,
        grid=(n // tm,),
        in_specs=[row(D_MODEL), _resident(norm_w.shape), _resident(win.shape), _resident(wvt.shape)],
        out_specs=specs,
        out_shape=shapes,
        compiler_params=pltpu.CompilerParams(dimension_semantics=("arbitrary",), vmem_limit_bytes=VMEM_LIMIT),
        name="inproj",
    )(h, norm_w, win, wvt)


def _t5_bias(rel, rb_ref, head):
    half = NUM_BUCKETS // 2
    max_exact = half // 2
    ret = jnp.where(rel > 0, half, 0)
    n = jnp.abs(rel)
    nf = jnp.maximum(n, 1).astype(F32)
    large = max_exact + (jnp.log(nf / max_exact) / math.log(MAX_DISTANCE / max_exact)
                         * (half - max_exact)).astype(jnp.int32)
    large = jnp.minimum(large, half - 1)
    bucket = ret + jnp.where(n < max_exact, n, large)
    val = jnp.zeros(rel.shape, F32)
    for jb in range(NUM_BUCKETS):
        val = jnp.where(bucket == jb, rb_ref[jb, head], val)
    return val * LOG2E


def _bias_kernel(rb_ref, tab_ref, mtab_ref, *, tq, tk):
    head = pl.program_id(0)
    r = tk // tq
    n_near = r + 2
    far_left = rb_ref[NUM_BUCKETS // 2 - 1, head] * LOG2E
    far_right = rb_ref[NUM_BUCKETS - 1, head] * LOG2E
    krow = lax.broadcasted_iota(jnp.int32, (LANES, LANES), 0)
    qcol = lax.broadcasted_iota(jnp.int32, (LANES, LANES), 1)
    for t in range(n_near):
        for a in range(tk // LANES):
            for b in range(tq // LANES):
                base = (a - b) * LANES + (t - r) * tq
                blk = (slice(a * LANES, (a + 1) * LANES), slice(b * LANES, (b + 1) * LANES))
                if base + LANES - 1 <= -T5_BAND:
                    tab_ref[(0, t) + blk] = jnp.full((LANES, LANES), far_left, F32)
                elif base - LANES + 1 >= T5_BAND:
                    tab_ref[(0, t) + blk] = jnp.full((LANES, LANES), far_right, F32)
                else:
                    tab_ref[(0, t) + blk] = _t5_bias(krow - qcol + base, rb_ref, head)
    tab_ref[0, n_near] = jnp.full((tk, tq), far_left, F32)
    tab_ref[0, n_near + 1] = jnp.full((tk, tq), far_right, F32)
    mrow = lax.broadcasted_iota(jnp.int32, (N_META, tq), 0)
    mcol = lax.broadcasted_iota(jnp.int32, (N_META, tq), 1)
    mtab_ref[0, 0] = _t5_bias(mrow - N_META - mcol, rb_ref, head)
    mtab_ref[0, 1] = jnp.full((N_META, tq), far_left, F32)


def _bias_call(rel_bias, tq, tk):
    assert tk % tq == 0 and tq >= T5_BAND + 1
    nt = tk // tq + 4
    return pl.pallas_call(
        functools.partial(_bias_kernel, tq=tq, tk=tk),
        grid=(N_ATT_HEADS,),
        in_specs=[pl.BlockSpec(memory_space=pltpu.SMEM)],
        out_specs=[pl.BlockSpec((1, nt, tk, tq), lambda h: (h, 0, 0, 0)),
                   pl.BlockSpec((1, 2, N_META, tq), lambda h: (h, 0, 0, 0))],
        out_shape=[jax.ShapeDtypeStruct((N_ATT_HEADS, nt, tk, tq), F32),
                   jax.ShapeDtypeStruct((N_ATT_HEADS, 2, N_META, tq), F32)],
        compiler_params=pltpu.CompilerParams(dimension_semantics=("arbitrary",)),
        name="t5_bias",
    )(rel_bias)


def _attn_kernel(lam_ref, qall_ref, k_ref, vt_ref, km_ref, vmt_ref, tab_ref, mtab_ref, sw_ref, o_ref,
                 sa_ref, sb_ref, mca_ref, mcb_ref, m_scr, acc_scr, accp_scr, shift_scr, flag_scr,
                 *, tq, tk, nkc, nq, n_steps):
    g = pl.program_id(0)
    qi = jnp.minimum(g, n_steps - 1) % nq
    r = tk // tq
    n_near = r + 2
    nt_dims = (((1,), (1,)), ((), ()))
    n_col = 2 * tq // ATT_COLS

    def finalize_previous():
        acc = accp_scr[...]
        o = acc[:ATT_DV] / acc[ATT_DV:ATT_DV + 1]
        lv = lam_ref[...]
        lam = (jnp.exp(jnp.sum(lv[0:1] * lv[1:2], axis=1, keepdims=True))
               - jnp.exp(jnp.sum(lv[2:3] * lv[3:4], axis=1, keepdims=True)) + LAM_INIT)
        out = o[:, :tq] - lam * o[:, tq:]
        ms = jnp.mean(out * out, axis=0, keepdims=True)
        out = out * lax.rsqrt(ms + EPS) * sw_ref[...] * (1.0 - LAM_INIT)
        o_ref[...] = out.T.astype(o_ref.dtype)

    @pl.when(g == 0)
    def _():
        accp_scr[...] = jnp.ones(accp_scr.shape, F32)

    @pl.when(g < n_steps)
    def _():
        q = qall_ref[pl.ds(pl.multiple_of(qi * tq, tq), tq), :]
        lane = lax.broadcasted_iota(jnp.int32, (tq, LANES), 1)
        zero = jnp.zeros_like(q)
        q2 = jnp.concatenate([jnp.where(lane < ATT_DH, q, zero), jnp.where(lane >= ATT_DH, q, zero)], axis=0)

        def bias_index(j):
            du = j * r - qi
            return jnp.where(du <= -(r + 1), n_near, jnp.where(du >= 2, n_near + 1, du + r))

        def add_bias(s, b):
            return jnp.concatenate([s[:, :tq] + b, s[:, tq:] + b], axis=1)

        @pl.when(qi == 0)
        def _():
            half = jnp.where(lax.broadcasted_iota(jnp.int32, (LANES, LANES), 0) // ATT_DH
                             == lax.broadcasted_iota(jnp.int32, (LANES, LANES), 1), 1.0, 0.0).astype(BF16)

            def max_sq_norm(x_ref):
                x = x_ref[...]
                sq = jnp.dot(x * x, half, preferred_element_type=F32)
                return jnp.max(sq, axis=0, keepdims=True)

            bound2 = max_sq_norm(qall_ref) * jnp.maximum(max_sq_norm(k_ref), max_sq_norm(km_ref))
            lane_row = lax.broadcasted_iota(jnp.int32, (1, LANES), 1)
            qk_bound = [NORM_SLACK * jnp.sqrt(jnp.max(jnp.where(lane_row == mp, bound2, 0.0), axis=1, keepdims=True))
                        for mp in range(2)]
            diag_tile = tab_ref[0, r]
            hi_b = jnp.max(jnp.max(diag_tile, axis=0, keepdims=True), axis=1, keepdims=True)
            lo_b = jnp.min(jnp.min(diag_tile, axis=0, keepdims=True), axis=1, keepdims=True)
            col = lax.broadcasted_iota(jnp.int32, (1, 2 * tq), 1)
            shift_scr[...] = jnp.where(col < tq, qk_bound[0], qk_bound[1]) + hi_b
            worst_gap = 2.0 * jnp.maximum(qk_bound[0], qk_bound[1]) + (hi_b - lo_b)
            flag_scr[0] = (worst_gap[0, 0] <= MAX_SHIFT_GAP).astype(jnp.int32)

        shift = shift_scr[...]
        bounded = flag_scr[0] == 1

        @pl.when(bounded)
        def _():
            finalize_previous()
            sm = lax.dot_general(km_ref[...], q2, nt_dims, preferred_element_type=F32)
            sm = add_bias(sm, mtab_ref[0, jnp.minimum(qi, 1)])
            acc_scr[...] = jnp.dot(vmt_ref[...], jnp.exp2(sm - shift).astype(BF16), preferred_element_type=F32)

            def stage_logits(u, j, c):
                buf = (sa_ref, sb_ref)[u % 2]
                buf[:, pl.ds(c * ATT_COLS, ATT_COLS)] = lax.dot_general(
                    k_ref[pl.ds(j * tk, tk), :], q2[c * ATT_COLS:(c + 1) * ATT_COLS], nt_dims,
                    preferred_element_type=F32)

            units = [(j, c) for j in range(nkc) for c in range(n_col)]
            stage_logits(0, *units[0])
            for u, (j, c) in enumerate(units):
                if u + 1 < len(units):
                    stage_logits(u + 1, *units[u + 1])
                cols = pl.ds(c * ATT_COLS, ATT_COLS)
                s = (sa_ref, sb_ref)[u % 2][:, cols] + tab_ref[0, bias_index(j), :, pl.ds((c * ATT_COLS) % tq, ATT_COLS)]
                p = jnp.exp2(s - shift[:, c * ATT_COLS:(c + 1) * ATT_COLS]).astype(BF16)
                acc_scr[:, cols] += jnp.dot(vt_ref[j], p, preferred_element_type=F32)
            accp_scr[...] = acc_scr[...]

        @pl.when(jnp.logical_not(bounded))
        def _():
            finalize_previous()

            def produce(j, c, s_ref, mc_ref):
                cols = pl.ds(c * ATT_COLS, ATT_COLS)
                s = lax.dot_general(k_ref[pl.ds(j * tk, tk), :], q2[c * ATT_COLS:(c + 1) * ATT_COLS], nt_dims,
                                    preferred_element_type=F32)
                s = s + tab_ref[0, bias_index(j), :, pl.ds((c * ATT_COLS) % tq, ATT_COLS)]
                s_ref[:, cols] = s
                mc_ref[:, cols] = jnp.max(s, axis=0, keepdims=True)

            def consume(s, m_cur, vt, cols, first=False):
                if first:
                    m_new = m_cur
                else:
                    m_prev = m_scr[:, cols]
                    m_new = jnp.maximum(m_prev, m_cur)
                    alpha = jnp.exp2(m_prev - m_new)
                p = jnp.exp2(s - m_new).astype(BF16)
                pv = jnp.dot(vt, p, preferred_element_type=F32)
                acc_scr[:, cols] = pv if first else alpha * acc_scr[:, cols] + pv
                m_scr[:, cols] = m_new

            sm = lax.dot_general(km_ref[...], q2, nt_dims, preferred_element_type=F32)
            sm = add_bias(sm, mtab_ref[0, jnp.minimum(qi, 1)])
            consume(sm, jnp.max(sm, axis=0, keepdims=True), vmt_ref[...], pl.ds(0, 2 * tq), first=True)

            bufs = ((sa_ref, mca_ref), (sb_ref, mcb_ref))
            for c in range(n_col):
                produce(0, c, *bufs[0])
            for j in range(nkc):
                s_ref, mc_ref = bufs[j % 2]
                for c in range(n_col):
                    cols = pl.ds(c * ATT_COLS, ATT_COLS)
                    if j + 1 < nkc:
                        produce(j + 1, c, *bufs[(j + 1) % 2])
                    consume(s_ref[:, cols], mc_ref[:, cols], vt_ref[j], cols)
            accp_scr[...] = acc_scr[...]

    @pl.when(g == n_steps)
    def _():
        finalize_previous()


def _attn_call(lamv, q, k, vt, km, vmt, tab, mtab, subw_col, batch, seq, tq, tk):
    n = q.shape[0]
    assert n == batch * seq and seq % (2 * tk) == 0 and seq % tq == 0 and vt.shape[2] == tk
    nq = seq // tq
    nkc = seq // tk
    nt = tab.shape[1]
    n_steps = N_ATT_HEADS * batch * nq

    def tile(g):
        g = jnp.minimum(g, n_steps - 1)
        return g // (batch * nq), (g // nq) % batch, g % nq

    def cur(f):
        return lambda g: f(*tile(g))

    def prev(f):
        return lambda g: f(*tile(jnp.maximum(g - 1, 0)))

    return pl.pallas_call(
        functools.partial(_attn_kernel, tq=tq, tk=tk, nkc=nkc, nq=nq, n_steps=n_steps),
        grid=(n_steps + 1,),
        in_specs=[
            pl.BlockSpec(lamv.shape, lambda g: (0, 0)),
            pl.BlockSpec((seq, LANES), cur(lambda h, b, i: (b, h))),
            pl.BlockSpec((seq, LANES), cur(lambda h, b, i: (b, h))),
            pl.BlockSpec((nkc, VT_ROWS, tk), cur(lambda h, b, i: (b, h, 0))),
            pl.BlockSpec((N_META, LANES), cur(lambda h, b, i: (0, h))),
            pl.BlockSpec((VT_ROWS, N_META), cur(lambda h, b, i: (h, 0))),
            pl.BlockSpec((1, nt, tk, tq), cur(lambda h, b, i: (h, 0, 0, 0))),
            pl.BlockSpec((1, 2, N_META, tq), cur(lambda h, b, i: (h, 0, 0, 0))),
            pl.BlockSpec(subw_col.shape, lambda g: (0, 0)),
        ],
        out_specs=pl.BlockSpec((tq, LANES), prev(lambda h, b, i: (b * nq + i, h))),
        out_shape=jax.ShapeDtypeStruct((n, ATT_V), BF16),
        scratch_shapes=[pltpu.VMEM((tk, 2 * tq), F32)] * 2 + [pltpu.VMEM((1, 2 * tq), F32)] * 3
        + [pltpu.VMEM((VT_ROWS, 2 * tq), F32)] * 2 + [pltpu.VMEM((1, 2 * tq), F32), pltpu.SMEM((1,), jnp.int32)],
        compiler_params=pltpu.CompilerParams(dimension_semantics=("arbitrary",), vmem_limit_bytes=VMEM_LIMIT),
        name="diff_attn",
    )(lamv, q, k, vt, km, vmt, tab, mtab, subw_col)


def _split3(x):
    hi = x.astype(BF16)
    r1 = x - hi.astype(F32)
    mid = r1.astype(BF16)
    lo = (r1 - mid.astype(F32)).astype(BF16)
    return hi, mid, lo


def _cumsum_rows(a):
    rows = a.shape[0]
    r_i = lax.broadcasted_iota(jnp.int32, (rows, rows), 0)
    c_i = lax.broadcasted_iota(jnp.int32, (rows, rows), 1)
    tri = jnp.where(c_i <= r_i, 1.0, 0.0).astype(BF16)
    out = None
    for term in _split3(a):
        part = jnp.dot(tri, term, preferred_element_type=F32)
        out = part if out is None else out + part
    return out


def _expand_rows(parts, sel_ref):
    masked = []
    for w, first in parts:
        lane = lax.broadcasted_iota(jnp.int32, w.shape, 1)
        masked.append(jnp.where((lane >= first) & (lane < first + SSM_HEADS), w, 0.0))
    stacked = jnp.concatenate(masked, axis=0)
    hi = stacked.astype(BF16)
    lo = (stacked - hi.astype(F32)).astype(BF16)
    sel = sel_ref[...]
    full = jnp.dot(hi, sel, preferred_element_type=F32) + jnp.dot(lo, sel, preferred_element_type=F32)
    outs, r0 = [], 0
    for w, _ in parts:
        outs.append(full[r0:r0 + w.shape[0]])
        r0 += w.shape[0]
    return outs


def _softplus(x):
    return jnp.maximum(x, 0.0) + jnp.log(1.0 + jnp.exp(-jnp.abs(x)))


GROUP_COLS = SSM_INNER // SSM_GROUPS


def _state_update(b_t, xw):
    return jnp.concatenate(
        [jnp.dot(b_t[g * SSM_STATE:(g + 1) * SSM_STATE], xw[:, g * GROUP_COLS:(g + 1) * GROUP_COLS],
                 preferred_element_type=F32) for g in range(SSM_GROUPS)], axis=0)


def _stack_decay(dec_row):
    return jnp.concatenate(
        [jnp.broadcast_to(dec_row[:, g * GROUP_COLS:(g + 1) * GROUP_COLS], (SSM_STATE, GROUP_COLS))
         for g in range(SSM_GROUPS)], axis=0)


def _conv_silu(win, shift_ref, cw_ref, cb_ref, rows):
    assert win.shape[0] == rows + 2 * HALO and shift_ref.shape == ((SSM_CONV - 1) * rows, rows + 2 * HALO)
    shifted = jnp.dot(shift_ref[...], win, preferred_element_type=F32)
    acc = cb_ref[...] + cw_ref[SSM_CONV // 2:SSM_CONV // 2 + 1, :] * win[HALO:HALO + rows].astype(F32)
    taps = [j for j in range(SSM_CONV) if j != SSM_CONV // 2]
    for n, j in enumerate(taps):
        acc = acc + cw_ref[j:j + 1, :] * shifted[n * rows:(n + 1) * rows]
    return acc * jax.nn.sigmoid(acc)


def _shift_matrix(rows):
    offs = jnp.array([j - SSM_CONV // 2 for j in range(SSM_CONV) if j != SSM_CONV // 2])
    t = jnp.arange(rows)
    src = HALO + t[None, :] + offs[:, None]
    return (src.reshape(-1)[:, None] == jnp.arange(rows + 2 * HALO)[None, :]).astype(BF16)


def _ssd_kernel(z_ref, xc_ref, xl_ref, xr_ref, dt_ref, mx_ref, mdt_ref, cw_ref, cb_ref, dtb_ref, alog_ref,
                dsk_ref, nw_ref, sel_ref, shc_ref, shm_ref, o_ref, xs_scr, dts_scr, cum_scr, hbs_scr, hf_scr, hb_scr,
                win_scr,
                *, cs, sub, nb):
    rows = cs * sub
    ph = pl.program_id(1)
    t = pl.program_id(2)
    fwd0, bwd0 = 0, SSM_HEADS
    a_row = -jnp.exp(alog_ref[...])

    def decay_terms(dt_raw):
        dt = _softplus(dt_raw + dtb_ref[...])
        return dt, _cumsum_rows(dt * a_row)

    def bcast8(row):
        return jnp.broadcast_to(row, (8, LANES))

    @pl.when(ph == 0)
    def _():
        blk = nb - 1 - t

        @pl.when(t == 0)
        def _():
            hb_scr[...] = jnp.zeros(hb_scr.shape, F32)

        left = jnp.where(blk == 0, mx_ref[...], xl_ref[...])
        right = jnp.where(blk == nb - 1, jnp.zeros_like(xr_ref[...]), xr_ref[...])
        win_scr[0:HALO, :] = left
        win_scr[HALO:HALO + rows, :] = xc_ref[...]
        win_scr[HALO + rows:HALO + rows + HALO, :] = right

        hb = hb_scr[...]
        for si in reversed(range(sub)):
            cc = blk * sub + si
            xbc = _conv_silu(win_scr[si * cs:si * cs + cs + 2 * HALO, :], shc_ref, cw_ref, cb_ref, cs)
            xs_scr[cc] = xbc.astype(BF16)
            dt, cum = decay_terms(dt_ref[si * cs:(si + 1) * cs, :])
            dts_scr[cc] = dt
            cum_scr[cc] = cum
            eb = cum - dt * a_row
            w_b, dec = _expand_rows([(jnp.exp(eb) * dt, bwd0), (bcast8(jnp.exp(cum[cs - 1:cs, :])), bwd0)],
                                    sel_ref)
            xw = (xbc[:, :SSM_INNER] * w_b).astype(BF16)
            bm_t = xbc[:, SSM_INNER:SSM_INNER + LANES].T.astype(BF16)
            hbs_scr[cc] = hb.astype(BF16)
            hb = hb * _stack_decay(dec[0:1]) + _state_update(bm_t, xw)
        hb_scr[...] = hb

    @pl.when(ph == 1)
    def _():
        @pl.when(t == 0)
        def _():
            wm = jnp.concatenate([jnp.zeros((HALO, SSM_CONV_DIM), BF16), mx_ref[...], xc_ref[0:HALO, :]], axis=0)
            xm = _conv_silu(wm, shm_ref, cw_ref, cb_ref, N_META)
            dtm, cumm = decay_terms(mdt_ref[...])
            (w_m,) = _expand_rows([(jnp.exp(cumm[N_META - 1:N_META, :] - cumm) * dtm, fwd0)], sel_ref)
            xwm = (xm[:, :SSM_INNER] * w_m).astype(BF16)
            bmm_t = xm[:, SSM_INNER:SSM_INNER + LANES].T.astype(BF16)
            hf_scr[...] = _state_update(bmm_t, xwm)

        lane = lax.broadcasted_iota(jnp.int32, (cs, LANES), 1)
        l_i = lax.broadcasted_iota(jnp.int32, (cs, cs), 0)
        s_i = lax.broadcasted_iota(jnp.int32, (cs, cs), 1)
        lower = s_i <= l_i
        diag = s_i == l_i
        hpg = SSM_HEADS // SSM_GROUPS
        zx = jnp.zeros((cs, LANES), BF16)
        nt_dims = (((1,), (1,)), ((), ()))

        hf = hf_scr[...]
        for si in range(sub):
            cc = t * sub + si
            xbc = xs_scr[cc]
            x_bf = xbc[:, :SSM_INNER]
            bm = xbc[:, SSM_INNER:SSM_INNER + LANES]
            cm = xbc[:, SSM_INNER + LANES:SSM_INNER + 2 * LANES]
            x = x_bf.astype(F32)

            dt = dts_scr[cc]
            cum = cum_scr[cc]
            eb = cum - dt * a_row
            dt_t, cum_t, eb_t = dt.T, cum.T, eb.T
            last = cum[cs - 1:cs, :]

            c_grp = [jnp.where(lane // SSM_STATE == g, cm, jnp.zeros_like(cm)) for g in range(SSM_GROUPS)]
            g_mats = [lax.dot_general(c_g, bm, nt_dims, preferred_element_type=F32) for c_g in c_grp]

            pieces = []
            for hp in range(SSM_HEADS // 2):
                w_pair = []
                for h in (2 * hp, 2 * hp + 1):
                    arg_f = cum[:, fwd0 + h:fwd0 + h + 1] - cum_t[fwd0 + h:fwd0 + h + 1, :]
                    arg_b = eb_t[bwd0 + h:bwd0 + h + 1, :] - eb[:, bwd0 + h:bwd0 + h + 1]
                    e = jnp.exp(jnp.minimum(jnp.where(lower, arg_f, arg_b), 0.0))
                    dt_f_row = dt_t[fwd0 + h:fwd0 + h + 1, :]
                    dt_b_row = dt_t[bwd0 + h:bwd0 + h + 1, :]
                    m = e * jnp.where(lower, dt_f_row, dt_b_row) + jnp.where(diag, dt_b_row, 0.0)
                    w_pair.append((g_mats[h // hpg] * m).astype(BF16))
                xp = x_bf[:, hp * LANES:(hp + 1) * LANES]
                rhs = jnp.concatenate([jnp.where(lane < SSM_HEADDIM, xp, zx),
                                       jnp.where(lane >= SSM_HEADDIM, xp, zx)], axis=0)
                pieces.append(jnp.dot(jnp.concatenate(w_pair, axis=1), rhs, preferred_element_type=F32))
            y = jnp.concatenate(pieces, axis=1)

            d_f, d_b, w_f, dec = _expand_rows(
                [(jnp.exp(cum), fwd0), (jnp.exp(last - eb), bwd0), (jnp.exp(last - cum) * dt, fwd0),
                 (bcast8(jnp.exp(last)), fwd0)], sel_ref)
            hf_bf = hf.astype(BF16)
            hb_bf = hbs_scr[cc]
            y = y + d_f * jnp.concatenate([jnp.dot(c_g, hf_bf, preferred_element_type=F32) for c_g in c_grp],
                                          axis=1)
            y = y + d_b * jnp.concatenate([jnp.dot(c_g, hb_bf, preferred_element_type=F32) for c_g in c_grp],
                                          axis=1)
            y = y + x * dsk_ref[...]

            xw = (x * w_f).astype(BF16)
            hf = hf * _stack_decay(dec[0:1]) + _state_update(bm.astype(F32).T.astype(BF16), xw)

            zf = z_ref[si * cs:(si + 1) * cs, :].astype(F32)
            y = y * (zf * jax.nn.sigmoid(zf))
            o_ref[si * cs:(si + 1) * cs, :] = _rmsnorm(y, nw_ref[...]).astype(o_ref.dtype)
        hf_scr[...] = hf


def _ssd_call(z, xbc, dt, mxbc, mdt, cw, cb, dtb, alog, dskip, nw, sel, shc, shm, batch, seq, cs, sub):
    n = z.shape[0]
    rows = cs * sub
    assert n == batch * seq and seq % rows == 0 and cs % HALO == 0
    nc = seq // cs
    nb = seq // rows
    hpb = rows // HALO
    n_halo = n // HALO

    def ph0_block(ph, t):
        return (1 - ph) * (nb - 1 - t)

    const2 = lambda shape: pl.BlockSpec(shape, lambda b, ph, t: (0, 0))
    return pl.pallas_call(
        functools.partial(_ssd_kernel, cs=cs, sub=sub, nb=nb),
        grid=(batch, 2, nb),
        in_specs=[
            pl.BlockSpec((rows, SSM_INNER), lambda b, ph, t: (b * nb + ph * t, 0)),
            pl.BlockSpec((rows, SSM_CONV_DIM), lambda b, ph, t: (b * nb + ph0_block(ph, t), 0)),
            pl.BlockSpec((HALO, SSM_CONV_DIM),
                         lambda b, ph, t: (jnp.maximum((b * nb + ph0_block(ph, t)) * hpb - 1, 0), 0)),
            pl.BlockSpec((HALO, SSM_CONV_DIM),
                         lambda b, ph, t: (jnp.minimum((b * nb + ph0_block(ph, t) + 1) * hpb, n_halo - 1), 0)),
            pl.BlockSpec((rows, DT_PAD), lambda b, ph, t: (b * nb + ph0_block(ph, t), 0)),
            const2(mxbc.shape), const2(mdt.shape), const2(cw.shape), const2(cb.shape), const2(dtb.shape),
            const2(alog.shape), const2(dskip.shape), const2(nw.shape), const2(sel.shape), const2(shc.shape),
            const2(shm.shape),
        ],
        out_specs=pl.BlockSpec((rows, SSM_INNER), lambda b, ph, t: (b * nb + ph * t, 0)),
        out_shape=jax.ShapeDtypeStruct((n, SSM_INNER), BF16),
        scratch_shapes=[
            pltpu.VMEM((nc, cs, SSM_CONV_DIM), BF16),
            pltpu.VMEM((nc, cs, DT_PAD), F32),
            pltpu.VMEM((nc, cs, DT_PAD), F32),
            pltpu.VMEM((nc, LANES, GROUP_COLS), BF16),
            pltpu.VMEM((LANES, GROUP_COLS), F32),
            pltpu.VMEM((LANES, GROUP_COLS), F32),
            pltpu.VMEM((rows + 2 * HALO, SSM_CONV_DIM), BF16),
        ],
        compiler_params=pltpu.CompilerParams(dimension_semantics=("arbitrary",) * 3, vmem_limit_bytes=VMEM_LIMIT),
        name="bi_ssd",
    )(z, xbc, xbc, xbc, dt, mxbc, mdt, cw, cb, dtb, alog, dskip, nw, sel, shc, shm)


def _head_selector():
    k = jnp.arange(LANES)[:, None]
    col = jnp.arange(SSM_INNER)[None, :]
    return ((k % SSM_HEADS == col // SSM_HEADDIM) & (k < 2 * SSM_HEADS)).astype(BF16)


def _prep_weights(ffn1_norm_w, ffn1_w_gate, ffn1_w_up, ffn1_w_down, mix_norm_w, w_in, lambda_q1, lambda_k1,
                  lambda_q2, lambda_k2, attn_subln_w, conv_w, conv_b, dt_bias_fwd, dt_bias_bwd, a_log_fwd,
                  a_log_bwd, ssm_d, ssm_norm_w, w_out, ffn2_norm_w, ffn2_w_gate, ffn2_w_up, ffn2_w_down,
                  final_norm_w):
    def ffn(norm_w, wg, wu, wd):
        return norm_w[0][None, :], wg[0].astype(BF16), wu[0].astype(BF16), wd[0].astype(BF16)

    pad_lanes = lambda v, width: jnp.pad(v, (0, width - v.shape[0]))[None, :]
    o_v, o_z = 2 * ATT_QK, 2 * ATT_QK + ATT_V
    wi = w_in[0].astype(BF16)
    win = jnp.pad(jnp.concatenate([wi[:, :o_v], wi[:, o_z:]], axis=1),
                  ((0, 0), (0, D_IN_PAD - (w_in.shape[2] - ATT_V))))
    return dict(
        ffn1=ffn(ffn1_norm_w, ffn1_w_gate, ffn1_w_up, ffn1_w_down),
        ffn2=ffn(ffn2_norm_w, ffn2_w_gate, ffn2_w_up, ffn2_w_down),
        mix_norm=mix_norm_w[0][None, :],
        win=win,
        wvt=wi[:, o_v:o_z].T,
        lamv=jnp.stack([lambda_q1[0], lambda_k1[0], lambda_q2[0], lambda_k2[0]]),
        subw_col=attn_subln_w[0][:, None],
        cw=jnp.pad(conv_w[0], ((0, 8 - SSM_CONV), (0, 0))),
        cb=conv_b[0][None, :],
        dtb=pad_lanes(jnp.concatenate([dt_bias_fwd[0], dt_bias_bwd[0]]), DT_PAD),
        alog=pad_lanes(jnp.concatenate([a_log_fwd[0], a_log_bwd[0]]), DT_PAD),
        dskip=jnp.repeat(ssm_d[0], SSM_HEADDIM)[None, :],
        ssm_norm=ssm_norm_w[0][None, :],
        wo=w_out[0].astype(BF16).reshape(2, ATT_V, D_MODEL),
        final=final_norm_w[None, :],
        sel=_head_selector(),
        shc=_shift_matrix(SSD_CHUNK),
        shm=_shift_matrix(N_META),
    )


def _encode(x, w, meta_proj, tab, mtab):
    batch, seq, _ = x.shape
    km, vmt, mxbc, mdt = meta_proj
    h0 = x.reshape(batch * seq, D_MODEL)
    h1 = _ffn_call(h0, *w["ffn1"])
    q, k, vt, z, xbc, dt = _inproj_call(h1, w["mix_norm"], w["win"], w["wvt"])
    att = _attn_call(w["lamv"], q, k, vt, km, vmt, tab, mtab, w["subw_col"], batch, seq, ATT_TQ, ATT_TK)
    ssm = _ssd_call(z, xbc, dt, mxbc, mdt, w["cw"], w["cb"], w["dtb"], w["alog"], w["dskip"], w["ssm_norm"],
                    w["sel"], w["shc"], w["shm"], batch, seq, SSD_CHUNK, SSD_SUB)
    y = _ffn_call(h1, *w["ffn2"], mix=(att, ssm, w["wo"]), final_w=w["final"])
    return y.reshape(batch, seq, D_MODEL)


def kernel(x_prompt, x_sample, meta_tokens, ffn1_norm_w, ffn1_w_gate, ffn1_w_up, ffn1_w_down, mix_norm_w, w_in, rel_bias, lambda_q1, lambda_k1, lambda_q2, lambda_k2, attn_subln_w, conv_w, conv_b, dt_bias_fwd, dt_bias_bwd, a_log_fwd, a_log_bwd, ssm_d, ssm_norm_w, w_out, ffn2_norm_w, ffn2_w_gate, ffn2_w_up, ffn2_w_down, final_norm_w):
    w = _prep_weights(ffn1_norm_w, ffn1_w_gate, ffn1_w_up, ffn1_w_down, mix_norm_w, w_in, lambda_q1, lambda_k1,
                      lambda_q2, lambda_k2, attn_subln_w, conv_w, conv_b, dt_bias_fwd, dt_bias_bwd, a_log_fwd,
                      a_log_bwd, ssm_d, ssm_norm_w, w_out, ffn2_norm_w, ffn2_w_gate, ffn2_w_up, ffn2_w_down,
                      final_norm_w)
    hm = _ffn_call(meta_tokens, *w["ffn1"])
    _, km, vmt, _, mxbc, mdt = _inproj_call(hm, w["mix_norm"], w["win"], w["wvt"])
    meta_proj = (km, vmt[0], mxbc, mdt)
    tab, mtab = _bias_call(rel_bias, ATT_TQ, ATT_TK)
    return (_encode(x_prompt, w, meta_proj, tab, mtab), _encode(x_sample, w, meta_proj, tab, mtab))
```

```python
import functools
import math

import jax
import jax.numpy as jnp
from jax import lax
from jax.experimental import pallas as pl
from jax.experimental.pallas import tpu as pltpu

F32 = jnp.float32
BF16 = jnp.bfloat16

D_MODEL = 1024
N_META = 16
N_ATT_HEADS = 8
ATT_DH = 64
ATT_DV = 128
ATT_QK = 1024
ATT_V = 1024
NUM_BUCKETS = 32
MAX_DISTANCE = 128
SSM_HEADS = 16
SSM_HEADDIM = 64
SSM_INNER = 1024
SSM_GROUPS = 2
SSM_STATE = 64
SSM_CONV = 7
SSM_CONV_DIM = 1280
D_FF = 2816
EPS = 1e-6
LAYER = 0
LAM_INIT = 0.8 - 0.6 * math.exp(-0.3 * LAYER)
LOG2E = math.log2(math.e)
Q_SCALE = ATT_DH ** -0.5 * LOG2E
NEG_BIG = -1e30
NORM_SLACK = 1.02
MAX_SHIFT_GAP = 100.0

LANES = 128
BF16_ROWS = 16
VMEM_LIMIT = 56 * 1024 * 1024

FF_TILE = 256
N_FF = D_FF // FF_TILE
DT_PAD = LANES
D_IN_PAD = 2 * ATT_QK + SSM_INNER + SSM_CONV_DIM + DT_PAD
T5_BAND = 91

ROW_TILE = 512
ATT_TQ = 1024
ATT_TK = 512
ATT_COLS = 256
SSD_CHUNK = 128
SSD_SUB = 4
HALO = BF16_ROWS
VT_ROWS = ATT_DV + BF16_ROWS


def _rmsnorm(x, w):
    ms = jnp.mean(x * x, axis=-1, keepdims=True)
    return x * lax.rsqrt(ms + EPS) * w


def _resident(shape):
    nd = len(shape)
    return pl.BlockSpec(shape, lambda *_: (0,) * nd, pipeline_mode=pl.Buffered(1))


def _ffn_kernel(*refs, has_mix, has_final):
    it = iter(refs)
    h_ref = next(it)
    if has_mix:
        att_ref, ssm_ref, wo_ref = next(it), next(it), next(it)
    nw_ref, wg_ref, wu_ref, wd_ref = next(it), next(it), next(it), next(it)
    fw_ref = next(it) if has_final else None
    o_ref = next(it)

    h = h_ref[...]
    if has_mix:
        h = (h + jnp.dot(att_ref[...], wo_ref[0], preferred_element_type=F32)
             + jnp.dot(ssm_ref[...], wo_ref[1], preferred_element_type=F32))
    u = _rmsnorm(h, nw_ref[...]).astype(BF16)
    acc = jnp.zeros_like(h)
    for j in range(N_FF):
        ff = slice(j * FF_TILE, (j + 1) * FF_TILE)
        g = jnp.dot(u, wg_ref[:, ff], preferred_element_type=F32)
        up = jnp.dot(u, wu_ref[:, ff], preferred_element_type=F32)
        a = (g * jax.nn.sigmoid(g) * up).astype(BF16)
        acc = acc + jnp.dot(a, wd_ref[ff, :], preferred_element_type=F32)
    h = h + 0.5 * acc
    if has_final:
        h = _rmsnorm(h, fw_ref[...])
    o_ref[...] = h


def _ffn_call(h, norm_w, wg, wu, wd, mix=None, final_w=None):
    n = h.shape[0]
    tm = min(ROW_TILE, n)
    assert n % tm == 0
    row = lambda width: pl.BlockSpec((tm, width), lambda i: (i, 0))
    args, specs = [h], [row(D_MODEL)]
    if mix is not None:
        att, ssm, wo = mix
        args += [att, ssm, wo]
        specs += [row(ATT_V), row(SSM_INNER), _resident(wo.shape)]
    args += [norm_w, wg, wu, wd]
    specs += [_resident(norm_w.shape), _resident(wg.shape), _resident(wu.shape), _resident(wd.shape)]
    if final_w is not None:
        args.append(final_w)
        specs.append(_resident(final_w.shape))
    return pl.pallas_call(
        functools.partial(_ffn_kernel, has_mix=mix is not None, has_final=final_w is not None),
        grid=(n // tm,),
        in_specs=specs,
        out_specs=row(D_MODEL),
        out_shape=jax.ShapeDtypeStruct((n, D_MODEL), F32),
        compiler_params=pltpu.CompilerParams(dimension_semantics=("arbitrary",), vmem_limit_bytes=VMEM_LIMIT),
        name="ffn_mix" if mix is not None else "ffn",
    )(*args)


_IN_SEGS = (("q", 0, ATT_QK), ("k", ATT_QK, ATT_QK), ("z", 2 * ATT_QK, SSM_INNER),
            ("xbc", 2 * ATT_QK + SSM_INNER, SSM_CONV_DIM), ("dt", D_IN_PAD - DT_PAD, DT_PAD))


def _inproj_kernel(h_ref, nw_ref, win_ref, wvt_ref, q_ref, k_ref, vt_ref, z_ref, xbc_ref, dt_ref):
    u = _rmsnorm(h_ref[...], nw_ref[...]).astype(BF16)
    outs = dict(q=q_ref, k=k_ref, z=z_ref, xbc=xbc_ref, dt=dt_ref)
    for name, c0, width in _IN_SEGS:
        o_ref = outs[name]
        step = 512 if width % 512 == 0 else (256 if width % 256 == 0 else LANES)
        for s in range(0, width, step):
            r = jnp.dot(u, win_ref[:, c0 + s:c0 + s + step], preferred_element_type=F32)
            if name == "q":
                r = r * Q_SCALE
            o_ref[:, s:s + step] = r.astype(o_ref.dtype)
    nt_dims = (((1,), (1,)), ((), ()))
    ones = jnp.ones((VT_ROWS - ATT_DV, u.shape[0]), vt_ref.dtype)
    for s in range(0, ATT_V, 256):
        r = lax.dot_general(wvt_ref[s:s + 256, :], u, nt_dims, preferred_element_type=F32).astype(vt_ref.dtype)
        for hh in range(256 // ATT_DV):
            head = s // ATT_DV + hh
            vt_ref[0, head * VT_ROWS:head * VT_ROWS + ATT_DV, :] = r[hh * ATT_DV:(hh + 1) * ATT_DV]
            vt_ref[0, head * VT_ROWS + ATT_DV:(head + 1) * VT_ROWS, :] = ones


def _inproj_call(h, norm_w, win, wvt):
    n = h.shape[0]
    tm = min(ATT_TK, n)
    assert n % tm == 0
    row = lambda width: pl.BlockSpec((tm, width), lambda i: (i, 0))
    widths = (ATT_QK, ATT_QK, SSM_INNER, SSM_CONV_DIM, DT_PAD)
    dtypes = (BF16, BF16, BF16, BF16, F32)
    shapes = [jax.ShapeDtypeStruct((n, w), dt) for w, dt in zip(widths, dtypes)]
    specs = [row(w) for w in widths]
    shapes.insert(2, jax.ShapeDtypeStruct((n // tm, N_ATT_HEADS * VT_ROWS, tm), BF16))
    specs.insert(2, pl.BlockSpec((1, N_ATT_HEADS * VT_ROWS, tm), lambda i: (i, 0, 0)))
    return pl.pallas_call(
        _inproj_kernel,
        grid=(n // tm,),
        in_specs=[row(D_MODEL), _resident(norm_w.shape), _resident(win.shape), _resident(wvt.shape)],
        out_specs=specs,
        out_shape=shapes,
        compiler_params=pltpu.CompilerParams(dimension_semantics=("arbitrary",), vmem_limit_bytes=VMEM_LIMIT),
        name="inproj",
    )(h, norm_w, win, wvt)


def _t5_bias(rel, rb_ref, head):
    half = NUM_BUCKETS // 2
    max_exact = half // 2
    ret = jnp.where(rel > 0, half, 0)
    n = jnp.abs(rel)
    nf = jnp.maximum(n, 1).astype(F32)
    large = max_exact + (jnp.log(nf / max_exact) / math.log(MAX_DISTANCE / max_exact)
                         * (half - max_exact)).astype(jnp.int32)
    large = jnp.minimum(large, half - 1)
    bucket = ret + jnp.where(n < max_exact, n, large)
    val = jnp.zeros(rel.shape, F32)
    for jb in range(NUM_BUCKETS):
        val = jnp.where(bucket == jb, rb_ref[jb, head], val)
    return val * LOG2E


def _bias_geometry(tq, tk):
    unit = min(tq, tk)
    assert tq % unit == 0 and tk % unit == 0 and unit >= T5_BAND + 1
    return unit, tk // unit, tq // unit


def _bias_kernel(rb_ref, tab_ref, mtab_ref, *, tq, tk):
    head = pl.program_id(0)
    unit, lo, hi = _bias_geometry(tq, tk)
    n_near = lo + hi + 1
    far_left = rb_ref[NUM_BUCKETS // 2 - 1, head] * LOG2E
    far_right = rb_ref[NUM_BUCKETS - 1, head] * LOG2E
    krow = lax.broadcasted_iota(jnp.int32, (LANES, LANES), 0)
    qcol = lax.broadcasted_iota(jnp.int32, (LANES, LANES), 1)
    for t in range(n_near):
        for a in range(tk // LANES):
            for b in range(tq // LANES):
                base = (a - b) * LANES + (t - lo) * unit
                blk = (slice(a * LANES, (a + 1) * LANES), slice(b * LANES, (b + 1) * LANES))
                if base + LANES - 1 <= -T5_BAND:
                    tab_ref[(0, t) + blk] = jnp.full((LANES, LANES), far_left, F32)
                elif base - LANES + 1 >= T5_BAND:
                    tab_ref[(0, t) + blk] = jnp.full((LANES, LANES), far_right, F32)
                else:
                    tab_ref[(0, t) + blk] = _t5_bias(krow - qcol + base, rb_ref, head)
    tab_ref[0, n_near] = jnp.full((tk, tq), far_left, F32)
    tab_ref[0, n_near + 1] = jnp.full((tk, tq), far_right, F32)
    mrow = lax.broadcasted_iota(jnp.int32, (N_META, tq), 0)
    mcol = lax.broadcasted_iota(jnp.int32, (N_META, tq), 1)
    mtab_ref[0, 0] = _t5_bias(mrow - N_META - mcol, rb_ref, head)
    mtab_ref[0, 1] = jnp.full((N_META, tq), far_left, F32)


def _bias_call(rel_bias, tq, tk):
    _, lo, hi = _bias_geometry(tq, tk)
    nt = lo + hi + 3
    return pl.pallas_call(
        functools.partial(_bias_kernel, tq=tq, tk=tk),
        grid=(N_ATT_HEADS,),
        in_specs=[pl.BlockSpec(memory_space=pltpu.SMEM)],
        out_specs=[pl.BlockSpec((1, nt, tk, tq), lambda h: (h, 0, 0, 0)),
                   pl.BlockSpec((1, 2, N_META, tq), lambda h: (h, 0, 0, 0))],
        out_shape=[jax.ShapeDtypeStruct((N_ATT_HEADS, nt, tk, tq), F32),
                   jax.ShapeDtypeStruct((N_ATT_HEADS, 2, N_META, tq), F32)],
        compiler_params=pltpu.CompilerParams(dimension_semantics=("arbitrary",)),
        name="t5_bias",
    )(rel_bias)


def _attn_kernel(lam_ref, qall_ref, k_ref, vt_ref, km_ref, vmt_ref, tab_ref, mtab_ref, sw_ref, o_ref,
                 sa_ref, sb_ref, mca_ref, mcb_ref, m_scr, acc_scr, accp_scr, shift_scr, flag_scr,
                 *, tq, tk, nkc, nq, n_steps):
    g = pl.program_id(0)
    qi = jnp.minimum(g, n_steps - 1) % nq
    unit, lo, hi = _bias_geometry(tq, tk)
    n_near = lo + hi + 1
    nt_dims = (((1,), (1,)), ((), ()))
    n_col = 2 * tq // ATT_COLS

    def finalize_previous():
        acc = accp_scr[...]
        o = acc[:ATT_DV] / acc[ATT_DV:ATT_DV + 1]
        lv = lam_ref[...]
        lam = (jnp.exp(jnp.sum(lv[0:1] * lv[1:2], axis=1, keepdims=True))
               - jnp.exp(jnp.sum(lv[2:3] * lv[3:4], axis=1, keepdims=True)) + LAM_INIT)
        out = o[:, :tq] - lam * o[:, tq:]
        ms = jnp.mean(out * out, axis=0, keepdims=True)
        out = out * lax.rsqrt(ms + EPS) * sw_ref[...] * (1.0 - LAM_INIT)
        o_ref[...] = out.T.astype(o_ref.dtype)

    @pl.when(g == 0)
    def _():
        accp_scr[...] = jnp.ones(accp_scr.shape, F32)

    @pl.when(g < n_steps)
    def _():
        q = qall_ref[pl.ds(pl.multiple_of(qi * tq, tq), tq), :]
        lane = lax.broadcasted_iota(jnp.int32, (tq, LANES), 1)
        zero = jnp.zeros_like(q)
        q2 = jnp.concatenate([jnp.where(lane < ATT_DH, q, zero), jnp.where(lane >= ATT_DH, q, zero)], axis=0)

        def bias_index(j):
            du = j * (tk // unit) - qi * (tq // unit)
            return jnp.where(du < -lo, n_near, jnp.where(du > hi, n_near + 1, du + lo))

        def add_bias(s, b):
            return jnp.concatenate([s[:, :tq] + b, s[:, tq:] + b], axis=1)

        @pl.when(qi == 0)
        def _():
            half = jnp.where(lax.broadcasted_iota(jnp.int32, (LANES, LANES), 0) // ATT_DH
                             == lax.broadcasted_iota(jnp.int32, (LANES, LANES), 1), 1.0, 0.0).astype(BF16)

            def max_sq_norm(x_ref):
                x = x_ref[...]
                sq = jnp.dot(x * x, half, preferred_element_type=F32)
                return jnp.max(sq, axis=0, keepdims=True)

            bound2 = max_sq_norm(qall_ref) * jnp.maximum(max_sq_norm(k_ref), max_sq_norm(km_ref))
            lane_row = lax.broadcasted_iota(jnp.int32, (1, LANES), 1)
            qk_bound = [NORM_SLACK * jnp.sqrt(jnp.max(jnp.where(lane_row == mp, bound2, 0.0), axis=1, keepdims=True))
                        for mp in range(2)]
            diag_tile = tab_ref[0, lo]
            hi_b = jnp.max(jnp.max(diag_tile, axis=0, keepdims=True), axis=1, keepdims=True)
            lo_b = jnp.min(jnp.min(diag_tile, axis=0, keepdims=True), axis=1, keepdims=True)
            col = lax.broadcasted_iota(jnp.int32, (1, 2 * tq), 1)
            shift_scr[...] = jnp.where(col < tq, qk_bound[0], qk_bound[1]) + hi_b
            worst_gap = 2.0 * jnp.maximum(qk_bound[0], qk_bound[1]) + (hi_b - lo_b)
            flag_scr[0] = (worst_gap[0, 0] <= MAX_SHIFT_GAP).astype(jnp.int32)

        shift = shift_scr[...]
        bounded = flag_scr[0] == 1

        @pl.when(bounded)
        def _():
            finalize_previous()
            sm = lax.dot_general(km_ref[...], q2, nt_dims, preferred_element_type=F32)
            sm = add_bias(sm, mtab_ref[0, jnp.minimum(qi, 1)])
            acc_scr[...] = jnp.dot(vmt_ref[...], jnp.exp2(sm - shift).astype(BF16), preferred_element_type=F32)

            def stage_logits(u, j, c):
                buf = (sa_ref, sb_ref)[u % 2]
                buf[:, pl.ds(c * ATT_COLS, ATT_COLS)] = lax.dot_general(
                    k_ref[pl.ds(j * tk, tk), :], q2[c * ATT_COLS:(c + 1) * ATT_COLS], nt_dims,
                    preferred_element_type=F32)

            units = [(j, c) for j in range(nkc) for c in range(n_col)]
            stage_logits(0, *units[0])
            for u, (j, c) in enumerate(units):
                if u + 1 < len(units):
                    stage_logits(u + 1, *units[u + 1])
                cols = pl.ds(c * ATT_COLS, ATT_COLS)
                s = (sa_ref, sb_ref)[u % 2][:, cols] + tab_ref[0, bias_index(j), :, pl.ds((c * ATT_COLS) % tq, ATT_COLS)]
                p = jnp.exp2(s - shift[:, c * ATT_COLS:(c + 1) * ATT_COLS]).astype(BF16)
                acc_scr[:, cols] += jnp.dot(vt_ref[j], p, preferred_element_type=F32)
            accp_scr[...] = acc_scr[...]

        @pl.when(jnp.logical_not(bounded))
        def _():
            finalize_previous()

            def produce(j, c, s_ref, mc_ref):
                cols = pl.ds(c * ATT_COLS, ATT_COLS)
                s = lax.dot_general(k_ref[pl.ds(j * tk, tk), :], q2[c * ATT_COLS:(c + 1) * ATT_COLS], nt_dims,
                                    preferred_element_type=F32)
                s = s + tab_ref[0, bias_index(j), :, pl.ds((c * ATT_COLS) % tq, ATT_COLS)]
                s_ref[:, cols] = s
                mc_ref[:, cols] = jnp.max(s, axis=0, keepdims=True)

            def consume(s, m_cur, vt, cols, first=False):
                if first:
                    m_new = m_cur
                else:
                    m_prev = m_scr[:, cols]
                    m_new = jnp.maximum(m_prev, m_cur)
                    alpha = jnp.exp2(m_prev - m_new)
                p = jnp.exp2(s - m_new).astype(BF16)
                pv = jnp.dot(vt, p, preferred_element_type=F32)
                acc_scr[:, cols] = pv if first else alpha * acc_scr[:, cols] + pv
                m_scr[:, cols] = m_new

            sm = lax.dot_general(km_ref[...], q2, nt_dims, preferred_element_type=F32)
            sm = add_bias(sm, mtab_ref[0, jnp.minimum(qi, 1)])
            consume(sm, jnp.max(sm, axis=0, keepdims=True), vmt_ref[...], pl.ds(0, 2 * tq), first=True)

            bufs = ((sa_ref, mca_ref), (sb_ref, mcb_ref))
            for c in range(n_col):
                produce(0, c, *bufs[0])
            for j in range(nkc):
                s_ref, mc_ref = bufs[j % 2]
                for c in range(n_col):
                    cols = pl.ds(c * ATT_COLS, ATT_COLS)
                    if j + 1 < nkc:
                        produce(j + 1, c, *bufs[(j + 1) % 2])
                    consume(s_ref[:, cols], mc_ref[:, cols], vt_ref[j], cols)
            accp_scr[...] = acc_scr[...]

    @pl.when(g == n_steps)
    def _():
        finalize_previous()


def _attn_call(lamv, q, k, vt, km, vmt, tab, mtab, subw_col, batch, seq, tq, tk):
    n = q.shape[0]
    assert n == batch * seq and seq % (2 * tk) == 0 and seq % tq == 0 and vt.shape[2] == tk
    nq = seq // tq
    nkc = seq // tk
    nt = tab.shape[1]
    n_steps = N_ATT_HEADS * batch * nq

    def tile(g):
        g = jnp.minimum(g, n_steps - 1)
        return g // (batch * nq), (g // nq) % batch, g % nq

    def cur(f):
        return lambda g: f(*tile(g))

    def prev(f):
        return lambda g: f(*tile(jnp.maximum(g - 1, 0)))

    return pl.pallas_call(
        functools.partial(_attn_kernel, tq=tq, tk=tk, nkc=nkc, nq=nq, n_steps=n_steps),
        grid=(n_steps + 1,),
        in_specs=[
            pl.BlockSpec(lamv.shape, lambda g: (0, 0)),
            pl.BlockSpec((seq, LANES), cur(lambda h, b, i: (b, h))),
            pl.BlockSpec((seq, LANES), cur(lambda h, b, i: (b, h))),
            pl.BlockSpec((nkc, VT_ROWS, tk), cur(lambda h, b, i: (b, h, 0))),
            pl.BlockSpec((N_META, LANES), cur(lambda h, b, i: (0, h))),
            pl.BlockSpec((VT_ROWS, N_META), cur(lambda h, b, i: (h, 0))),
            pl.BlockSpec((1, nt, tk, tq), cur(lambda h, b, i: (h, 0, 0, 0))),
            pl.BlockSpec((1, 2, N_META, tq), cur(lambda h, b, i: (h, 0, 0, 0))),
            pl.BlockSpec(subw_col.shape, lambda g: (0, 0)),
        ],
        out_specs=pl.BlockSpec((tq, LANES), prev(lambda h, b, i: (b * nq + i, h))),
        out_shape=jax.ShapeDtypeStruct((n, ATT_V), BF16),
        scratch_shapes=[pltpu.VMEM((tk, 2 * tq), F32)] * 2 + [pltpu.VMEM((1, 2 * tq), F32)] * 3
        + [pltpu.VMEM((VT_ROWS, 2 * tq), F32)] * 2 + [pltpu.VMEM((1, 2 * tq), F32), pltpu.SMEM((1,), jnp.int32)],
        compiler_params=pltpu.CompilerParams(dimension_semantics=("arbitrary",), vmem_limit_bytes=VMEM_LIMIT),
        name="diff_attn",
    )(lamv, q, k, vt, km, vmt, tab, mtab, subw_col)


def _split3(x):
    hi = x.astype(BF16)
    r1 = x - hi.astype(F32)
    mid = r1.astype(BF16)
    lo = (r1 - mid.astype(F32)).astype(BF16)
    return hi, mid, lo


def _cumsum_rows(a):
    rows = a.shape[0]
    r_i = lax.broadcasted_iota(jnp.int32, (rows, rows), 0)
    c_i = lax.broadcasted_iota(jnp.int32, (rows, rows), 1)
    tri = jnp.where(c_i <= r_i, 1.0, 0.0).astype(BF16)
    out = None
    for term in _split3(a):
        part = jnp.dot(tri, term, preferred_element_type=F32)
        out = part if out is None else out + part
    return out


def _expand_rows(parts, sel_ref):
    masked = []
    for w, first in parts:
        lane = lax.broadcasted_iota(jnp.int32, w.shape, 1)
        masked.append(jnp.where((lane >= first) & (lane < first + SSM_HEADS), w, 0.0))
    stacked = jnp.concatenate(masked, axis=0)
    hi = stacked.astype(BF16)
    lo = (stacked - hi.astype(F32)).astype(BF16)
    sel = sel_ref[...]
    full = jnp.dot(hi, sel, preferred_element_type=F32) + jnp.dot(lo, sel, preferred_element_type=F32)
    outs, r0 = [], 0
    for w, _ in parts:
        outs.append(full[r0:r0 + w.shape[0]])
        r0 += w.shape[0]
    return outs


def _softplus(x):
    return jnp.maximum(x, 0.0) + jnp.log(1.0 + jnp.exp(-jnp.abs(x)))


GROUP_COLS = SSM_INNER // SSM_GROUPS


def _state_update(b_t, xw):
    return jnp.concatenate(
        [jnp.dot(b_t[g * SSM_STATE:(g + 1) * SSM_STATE], xw[:, g * GROUP_COLS:(g + 1) * GROUP_COLS],
                 preferred_element_type=F32) for g in range(SSM_GROUPS)], axis=0)


def _stack_decay(dec_row):
    return jnp.concatenate(
        [jnp.broadcast_to(dec_row[:, g * GROUP_COLS:(g + 1) * GROUP_COLS], (SSM_STATE, GROUP_COLS))
         for g in range(SSM_GROUPS)], axis=0)


def _conv_silu(win, shift_ref, cw_ref, cb_ref, rows):
    assert win.shape[0] == rows + 2 * HALO and shift_ref.shape == ((SSM_CONV - 1) * rows, rows + 2 * HALO)
    shifted = jnp.dot(shift_ref[...], win, preferred_element_type=F32)
    acc = cb_ref[...] + cw_ref[SSM_CONV // 2:SSM_CONV // 2 + 1, :] * win[HALO:HALO + rows].astype(F32)
    taps = [j for j in range(SSM_CONV) if j != SSM_CONV // 2]
    for n, j in enumerate(taps):
        acc = acc + cw_ref[j:j + 1, :] * shifted[n * rows:(n + 1) * rows]
    return acc * jax.nn.sigmoid(acc)


def _shift_matrix(rows):
    offs = jnp.array([j - SSM_CONV // 2 for j in range(SSM_CONV) if j != SSM_CONV // 2])
    t = jnp.arange(rows)
    src = HALO + t[None, :] + offs[:, None]
    return (src.reshape(-1)[:, None] == jnp.arange(rows + 2 * HALO)[None, :]).astype(BF16)


def _ssd_kernel(z_ref, xc_ref, xl_ref, xr_ref, dt_ref, mx_ref, mdt_ref, cw_ref, cb_ref, dtb_ref, alog_ref,
                dsk_ref, nw_ref, sel_ref, shc_ref, shm_ref, o_ref, xs_scr, dts_scr, cum_scr, hbs_scr, hf_scr, hb_scr,
                win_scr,
                *, cs, sub, nb):
    rows = cs * sub
    ph = pl.program_id(1)
    t = pl.program_id(2)
    fwd0, bwd0 = 0, SSM_HEADS
    a_row = -jnp.exp(alog_ref[...])

    def decay_terms(dt_raw):
        dt = _softplus(dt_raw + dtb_ref[...])
        return dt, _cumsum_rows(dt * a_row)

    def bcast8(row):
        return jnp.broadcast_to(row, (8, LANES))

    @pl.when(ph == 0)
    def _():
        blk = nb - 1 - t

        @pl.when(t == 0)
        def _():
            hb_scr[...] = jnp.zeros(hb_scr.shape, F32)

        left = jnp.where(blk == 0, mx_ref[...], xl_ref[...])
        right = jnp.where(blk == nb - 1, jnp.zeros_like(xr_ref[...]), xr_ref[...])
        win_scr[0:HALO, :] = left
        win_scr[HALO:HALO + rows, :] = xc_ref[...]
        win_scr[HALO + rows:HALO + rows + HALO, :] = right

        hb = hb_scr[...]
        for si in reversed(range(sub)):
            cc = blk * sub + si
            xbc = _conv_silu(win_scr[si * cs:si * cs + cs + 2 * HALO, :], shc_ref, cw_ref, cb_ref, cs)
            xs_scr[cc] = xbc.astype(BF16)
            dt, cum = decay_terms(dt_ref[si * cs:(si + 1) * cs, :])
            dts_scr[cc] = dt
            cum_scr[cc] = cum
            eb = cum - dt * a_row
            w_b, dec = _expand_rows([(jnp.exp(eb) * dt, bwd0), (bcast8(jnp.exp(cum[cs - 1:cs, :])), bwd0)],
                                    sel_ref)
            xw = (xbc[:, :SSM_INNER] * w_b).astype(BF16)
            bm_t = xbc[:, SSM_INNER:SSM_INNER + LANES].T.astype(BF16)
            hbs_scr[cc] = hb.astype(BF16)
            hb = hb * _stack_decay(dec[0:1]) + _state_update(bm_t, xw)
        hb_scr[...] = hb

    @pl.when(ph == 1)
    def _():
        @pl.when(t == 0)
        def _():
            wm = jnp.concatenate([jnp.zeros((HALO, SSM_CONV_DIM), BF16), mx_ref[...], xc_ref[0:HALO, :]], axis=0)
            xm = _conv_silu(wm, shm_ref, cw_ref, cb_ref, N_META)
            dtm, cumm = decay_terms(mdt_ref[...])
            (w_m,) = _expand_rows([(jnp.exp(cumm[N_META - 1:N_META, :] - cumm) * dtm, fwd0)], sel_ref)
            xwm = (xm[:, :SSM_INNER] * w_m).astype(BF16)
            bmm_t = xm[:, SSM_INNER:SSM_INNER + LANES].T.astype(BF16)
            hf_scr[...] = _state_update(bmm_t, xwm)

        lane = lax.broadcasted_iota(jnp.int32, (cs, LANES), 1)
        l_i = lax.broadcasted_iota(jnp.int32, (cs, cs), 0)
        s_i = lax.broadcasted_iota(jnp.int32, (cs, cs), 1)
        lower = s_i <= l_i
        diag = s_i == l_i
        hpg = SSM_HEADS // SSM_GROUPS
        zx = jnp.zeros((cs, LANES), BF16)
        nt_dims = (((1,), (1,)), ((), ()))

        hf = hf_scr[...]
        for si in range(sub):
            cc = t * sub + si
            xbc = xs_scr[cc]
            x_bf = xbc[:, :SSM_INNER]
            bm = xbc[:, SSM_INNER:SSM_INNER + LANES]
            cm = xbc[:, SSM_INNER + LANES:SSM_INNER + 2 * LANES]
            x = x_bf.astype(F32)

            dt = dts_scr[cc]
            cum = cum_scr[cc]
            eb = cum - dt * a_row
            dt_t, cum_t, eb_t = dt.T, cum.T, eb.T
            last = cum[cs - 1:cs, :]

            c_grp = [jnp.where(lane // SSM_STATE == g, cm, jnp.zeros_like(cm)) for g in range(SSM_GROUPS)]
            g_mats = [lax.dot_general(c_g, bm, nt_dims, preferred_element_type=F32) for c_g in c_grp]

            pieces = []
            for hp in range(SSM_HEADS // 2):
                w_pair = []
                for h in (2 * hp, 2 * hp + 1):
                    arg_f = cum[:, fwd0 + h:fwd0 + h + 1] - cum_t[fwd0 + h:fwd0 + h + 1, :]
                    arg_b = eb_t[bwd0 + h:bwd0 + h + 1, :] - eb[:, bwd0 + h:bwd0 + h + 1]
                    e = jnp.exp(jnp.minimum(jnp.where(lower, arg_f, arg_b), 0.0))
                    dt_f_row = dt_t[fwd0 + h:fwd0 + h + 1, :]
                    dt_b_row = dt_t[bwd0 + h:bwd0 + h + 1, :]
                    m = e * jnp.where(lower, dt_f_row, dt_b_row) + jnp.where(diag, dt_b_row, 0.0)
                    w_pair.append((g_mats[h // hpg] * m).astype(BF16))
                xp = x_bf[:, hp * LANES:(hp + 1) * LANES]
                rhs = jnp.concatenate([jnp.where(lane < SSM_HEADDIM, xp, zx),
                                       jnp.where(lane >= SSM_HEADDIM, xp, zx)], axis=0)
                pieces.append(jnp.dot(jnp.concatenate(w_pair, axis=1), rhs, preferred_element_type=F32))
            y = jnp.concatenate(pieces, axis=1)

            d_f, d_b, w_f, dec = _expand_rows(
                [(jnp.exp(cum), fwd0), (jnp.exp(last - eb), bwd0), (jnp.exp(last - cum) * dt, fwd0),
                 (bcast8(jnp.exp(last)), fwd0)], sel_ref)
            hf_bf = hf.astype(BF16)
            hb_bf = hbs_scr[cc]
            y = y + d_f * jnp.concatenate([jnp.dot(c_g, hf_bf, preferred_element_type=F32) for c_g in c_grp],
                                          axis=1)
            y = y + d_b * jnp.concatenate([jnp.dot(c_g, hb_bf, preferred_element_type=F32) for c_g in c_grp],
                                          axis=1)
            y = y + x * dsk_ref[...]

            xw = (x * w_f).astype(BF16)
            hf = hf * _stack_decay(dec[0:1]) + _state_update(bm.astype(F32).T.astype(BF16), xw)

            zf = z_ref[si * cs:(si + 1) * cs, :].astype(F32)
            y = y * (zf * jax.nn.sigmoid(zf))
            o_ref[si * cs:(si + 1) * cs, :] = _rmsnorm(y, nw_ref[...]).astype(o_ref.dtype)
        hf_scr[...] = hf


def _ssd_call(z, xbc, dt, mxbc, mdt, cw, cb, dtb, alog, dskip, nw, sel, shc, shm, batch, seq, cs, sub):
    n = z.shape[0]
    rows = cs * sub
    assert n == batch * seq and seq % rows == 0 and cs % HALO == 0
    nc = seq // cs
    nb = seq // rows
    hpb = rows // HALO
    n_halo = n // HALO

    def ph0_block(ph, t):
        return (1 - ph) * (nb - 1 - t)

    const2 = lambda shape: pl.BlockSpec(shape, lambda b, ph, t: (0, 0))
    return pl.pallas_call(
        functools.partial(_ssd_kernel, cs=cs, sub=sub, nb=nb),
        grid=(batch, 2, nb),
        in_specs=[
            pl.BlockSpec((rows, SSM_INNER), lambda b, ph, t: (b * nb + ph * t, 0)),
            pl.BlockSpec((rows, SSM_CONV_DIM), lambda b, ph, t: (b * nb + ph0_block(ph, t), 0)),
            pl.BlockSpec((HALO, SSM_CONV_DIM),
                         lambda b, ph, t: (jnp.maximum((b * nb + ph0_block(ph, t)) * hpb - 1, 0), 0)),
            pl.BlockSpec((HALO, SSM_CONV_DIM),
                         lambda b, ph, t: (jnp.minimum((b * nb + ph0_block(ph, t) + 1) * hpb, n_halo - 1), 0)),
            pl.BlockSpec((rows, DT_PAD), lambda b, ph, t: (b * nb + ph0_block(ph, t), 0)),
            const2(mxbc.shape), const2(mdt.shape), const2(cw.shape), const2(cb.shape), const2(dtb.shape),
            const2(alog.shape), const2(dskip.shape), const2(nw.shape), const2(sel.shape), const2(shc.shape),
            const2(shm.shape),
        ],
        out_specs=pl.BlockSpec((rows, SSM_INNER), lambda b, ph, t: (b * nb + ph * t, 0)),
        out_shape=jax.ShapeDtypeStruct((n, SSM_INNER), BF16),
        scratch_shapes=[
            pltpu.VMEM((nc, cs, SSM_CONV_DIM), BF16),
            pltpu.VMEM((nc, cs, DT_PAD), F32),
            pltpu.VMEM((nc, cs, DT_PAD), F32),
            pltpu.VMEM((nc, LANES, GROUP_COLS), BF16),
            pltpu.VMEM((LANES, GROUP_COLS), F32),
            pltpu.VMEM((LANES, GROUP_COLS), F32),
            pltpu.VMEM((rows + 2 * HALO, SSM_CONV_DIM), BF16),
        ],
        compiler_params=pltpu.CompilerParams(dimension_semantics=("arbitrary",) * 3, vmem_limit_bytes=VMEM_LIMIT),
        name="bi_ssd",
    )(z, xbc, xbc, xbc, dt, mxbc, mdt, cw, cb, dtb, alog, dskip, nw, sel, shc, shm)


def _head_selector():
    k = jnp.arange(LANES)[:, None]
    col = jnp.arange(SSM_INNER)[None, :]
    return ((k % SSM_HEADS == col // SSM_HEADDIM) & (k < 2 * SSM_HEADS)).astype(BF16)


def _prep_weights(ffn1_norm_w, ffn1_w_gate, ffn1_w_up, ffn1_w_down, mix_norm_w, w_in, lambda_q1, lambda_k1,
                  lambda_q2, lambda_k2, attn_subln_w, conv_w, conv_b, dt_bias_fwd, dt_bias_bwd, a_log_fwd,
                  a_log_bwd, ssm_d, ssm_norm_w, w_out, ffn2_norm_w, ffn2_w_gate, ffn2_w_up, ffn2_w_down,
                  final_norm_w):
    def ffn(norm_w, wg, wu, wd):
        return norm_w[0][None, :], wg[0].astype(BF16), wu[0].astype(BF16), wd[0].astype(BF16)

    pad_lanes = lambda v, width: jnp.pad(v, (0, width - v.shape[0]))[None, :]
    o_v, o_z = 2 * ATT_QK, 2 * ATT_QK + ATT_V
    wi = w_in[0].astype(BF16)
    win = jnp.pad(jnp.concatenate([wi[:, :o_v], wi[:, o_z:]], axis=1),
                  ((0, 0), (0, D_IN_PAD - (w_in.shape[2] - ATT_V))))
    return dict(
        ffn1=ffn(ffn1_norm_w, ffn1_w_gate, ffn1_w_up, ffn1_w_down),
        ffn2=ffn(ffn2_norm_w, ffn2_w_gate, ffn2_w_up, ffn2_w_down),
        mix_norm=mix_norm_w[0][None, :],
        win=win,
        wvt=wi[:, o_v:o_z].T,
        lamv=jnp.stack([lambda_q1[0], lambda_k1[0], lambda_q2[0], lambda_k2[0]]),
        subw_col=attn_subln_w[0][:, None],
        cw=jnp.pad(conv_w[0], ((0, 8 - SSM_CONV), (0, 0))),
        cb=conv_b[0][None, :],
        dtb=pad_lanes(jnp.concatenate([dt_bias_fwd[0], dt_bias_bwd[0]]), DT_PAD),
        alog=pad_lanes(jnp.concatenate([a_log_fwd[0], a_log_bwd[0]]), DT_PAD),
        dskip=jnp.repeat(ssm_d[0], SSM_HEADDIM)[None, :],
        ssm_norm=ssm_norm_w[0][None, :],
        wo=w_out[0].astype(BF16).reshape(2, ATT_V, D_MODEL),
        final=final_norm_w[None, :],
        sel=_head_selector(),
        shc=_shift_matrix(SSD_CHUNK),
        shm=_shift_matrix(N_META),
    )


def _encode(x, w, meta_proj, tab, mtab):
    batch, seq, _ = x.shape
    km, vmt, mxbc, mdt = meta_proj
    h0 = x.reshape(batch * seq, D_MODEL)
    h1 = _ffn_call(h0, *w["ffn1"])
    q, k, vt, z, xbc, dt = _inproj_call(h1, w["mix_norm"], w["win"], w["wvt"])
    att = _attn_call(w["lamv"], q, k, vt, km, vmt, tab, mtab, w["subw_col"], batch, seq, ATT_TQ, ATT_TK)
    ssm = _ssd_call(z, xbc, dt, mxbc, mdt, w["cw"], w["cb"], w["dtb"], w["alog"], w["dskip"], w["ssm_norm"],
                    w["sel"], w["shc"], w["shm"], batch, seq, SSD_CHUNK, SSD_SUB)
    y = _ffn_call(h1, *w["ffn2"], mix=(att, ssm, w["wo"]), final_w=w["final"])
    return y.reshape(batch, seq, D_MODEL)


def kernel(x_prompt, x_sample, meta_tokens, ffn1_norm_w, ffn1_w_gate, ffn1_w_up, ffn1_w_down, mix_norm_w, w_in, rel_bias, lambda_q1, lambda_k1, lambda_q2, lambda_k2, attn_subln_w, conv_w, conv_b, dt_bias_fwd, dt_bias_bwd, a_log_fwd, a_log_bwd, ssm_d, ssm_norm_w, w_out, ffn2_norm_w, ffn2_w_gate, ffn2_w_up, ffn2_w_down, final_norm_w):
    w = _prep_weights(ffn1_norm_w, ffn1_w_gate, ffn1_w_up, ffn1_w_down, mix_norm_w, w_in, lambda_q1, lambda_k1,
                      lambda_q2, lambda_k2, attn_subln_w, conv_w, conv_b, dt_bias_fwd, dt_bias_bwd, a_log_fwd,
                      a_log_bwd, ssm_d, ssm_norm_w, w_out, ffn2_norm_w, ffn2_w_gate, ffn2_w_up, ffn2_w_down,
                      final_norm_w)
    hm = _ffn_call(meta_tokens, *w["ffn1"])
    _, km, vmt, _, mxbc, mdt = _inproj_call(hm, w["mix_norm"], w["win"], w["wvt"])
    meta_proj = (km, vmt[0], mxbc, mdt)
    tab, mtab = _bias_call(rel_bias, ATT_TQ, ATT_TK)
    return (_encode(x_prompt, w, meta_proj, tab, mtab), _encode(x_sample, w, meta_proj, tab, mtab))
```

```python
import functools
import math

import jax
import jax.numpy as jnp
from jax import lax
from jax.experimental import pallas as pl
from jax.experimental.pallas import tpu as pltpu

F32 = jnp.float32
BF16 = jnp.bfloat16

D_MODEL = 1024
N_META = 16
N_ATT_HEADS = 8
ATT_DH = 64
ATT_DV = 128
ATT_QK = 1024
ATT_V = 1024
NUM_BUCKETS = 32
MAX_DISTANCE = 128
SSM_HEADS = 16
SSM_HEADDIM = 64
SSM_INNER = 1024
SSM_GROUPS = 2
SSM_STATE = 64
SSM_CONV = 7
SSM_CONV_DIM = 1280
D_FF = 2816
EPS = 1e-6
LAYER = 0
LAM_INIT = 0.8 - 0.6 * math.exp(-0.3 * LAYER)
LOG2E = math.log2(math.e)
Q_SCALE = ATT_DH ** -0.5 * LOG2E
NEG_BIG = -1e30
NORM_SLACK = 1.02
MAX_SHIFT_GAP = 100.0

LANES = 128
BF16_ROWS = 16
VMEM_LIMIT = 56 * 1024 * 1024

FF_TILE = 256
N_FF = D_FF // FF_TILE
DT_PAD = LANES
D_IN_PAD = 2 * ATT_QK + SSM_INNER + SSM_CONV_DIM + DT_PAD
T5_BAND = 91

ROW_TILE = 512
ATT_TQ = 1024
ATT_TK = 512
ATT_COLS = 256
SSD_CHUNK = 128
MXU_TAPS = (0, 1, 5)
SSD_SUB = 4
HALO = BF16_ROWS
VT_ROWS = ATT_DV + BF16_ROWS


def _rmsnorm(x, w):
    ms = jnp.mean(x * x, axis=-1, keepdims=True)
    return x * lax.rsqrt(ms + EPS) * w


def _resident(shape):
    nd = len(shape)
    return pl.BlockSpec(shape, lambda *_: (0,) * nd, pipeline_mode=pl.Buffered(1))


def _ffn_kernel(*refs, has_mix, has_final):
    it = iter(refs)
    h_ref = next(it)
    if has_mix:
        att_ref, ssm_ref, wo_ref = next(it), next(it), next(it)
    nw_ref, wg_ref, wu_ref, wd_ref = next(it), next(it), next(it), next(it)
    fw_ref = next(it) if has_final else None
    o_ref = next(it)

    h = h_ref[...]
    if has_mix:
        h = (h + jnp.dot(att_ref[...], wo_ref[0], preferred_element_type=F32)
             + jnp.dot(ssm_ref[...], wo_ref[1], preferred_element_type=F32))
    u = _rmsnorm(h, nw_ref[...]).astype(BF16)
    acc = jnp.zeros_like(h)
    for j in range(N_FF):
        ff = slice(j * FF_TILE, (j + 1) * FF_TILE)
        g = jnp.dot(u, wg_ref[:, ff], preferred_element_type=F32)
        up = jnp.dot(u, wu_ref[:, ff], preferred_element_type=F32)
        a = (g * jax.nn.sigmoid(g) * up).astype(BF16)
        acc = acc + jnp.dot(a, wd_ref[ff, :], preferred_element_type=F32)
    h = h + 0.5 * acc
    if has_final:
        h = _rmsnorm(h, fw_ref[...])
    o_ref[...] = h


def _ffn_call(h, norm_w, wg, wu, wd, mix=None, final_w=None):
    n = h.shape[0]
    tm = min(ROW_TILE, n)
    assert n % tm == 0
    row = lambda width: pl.BlockSpec((tm, width), lambda i: (i, 0))
    args, specs = [h], [row(D_MODEL)]
    if mix is not None:
        att, ssm, wo = mix
        args += [att, ssm, wo]
        specs += [row(ATT_V), row(SSM_INNER), _resident(wo.shape)]
    args += [norm_w, wg, wu, wd]
    specs += [_resident(norm_w.shape), _resident(wg.shape), _resident(wu.shape), _resident(wd.shape)]
    if final_w is not None:
        args.append(final_w)
        specs.append(_resident(final_w.shape))
    return pl.pallas_call(
        functools.partial(_ffn_kernel, has_mix=mix is not None, has_final=final_w is not None),
        grid=(n // tm,),
        in_specs=specs,
        out_specs=row(D_MODEL),
        out_shape=jax.ShapeDtypeStruct((n, D_MODEL), F32),
        compiler_params=pltpu.CompilerParams(dimension_semantics=("arbitrary",), vmem_limit_bytes=VMEM_LIMIT),
        name="ffn_mix" if mix is not None else "ffn",
    )(*args)


_IN_SEGS = (("q", 0, ATT_QK), ("k", ATT_QK, ATT_QK), ("z", 2 * ATT_QK, SSM_INNER),
            ("xbc", 2 * ATT_QK + SSM_INNER, SSM_CONV_DIM), ("dt", D_IN_PAD - DT_PAD, DT_PAD))


def _inproj_kernel(h_ref, nw_ref, win_ref, wvt_ref, q_ref, k_ref, vt_ref, z_ref, xbc_ref, dt_ref):
    u = _rmsnorm(h_ref[...], nw_ref[...]).astype(BF16)
    outs = dict(q=q_ref, k=k_ref, z=z_ref, xbc=xbc_ref, dt=dt_ref)
    for name, c0, width in _IN_SEGS:
        o_ref = outs[name]
        step = 512 if width % 512 == 0 else (256 if width % 256 == 0 else LANES)
        for s in range(0, width, step):
            r = jnp.dot(u, win_ref[:, c0 + s:c0 + s + step], preferred_element_type=F32)
            if name == "q":
                r = r * Q_SCALE
            o_ref[:, s:s + step] = r.astype(o_ref.dtype)
    nt_dims = (((1,), (1,)), ((), ()))
    ones = jnp.ones((VT_ROWS - ATT_DV, u.shape[0]), vt_ref.dtype)
    for s in range(0, ATT_V, 256):
        r = lax.dot_general(wvt_ref[s:s + 256, :], u, nt_dims, preferred_element_type=F32).astype(vt_ref.dtype)
        for hh in range(256 // ATT_DV):
            head = s // ATT_DV + hh
            vt_ref[0, head * VT_ROWS:head * VT_ROWS + ATT_DV, :] = r[hh * ATT_DV:(hh + 1) * ATT_DV]
            vt_ref[0, head * VT_ROWS + ATT_DV:(head + 1) * VT_ROWS, :] = ones


def _inproj_call(h, norm_w, win, wvt):
    n = h.shape[0]
    tm = min(ATT_TK, n)
    assert n % tm == 0
    row = lambda width: pl.BlockSpec((tm, width), lambda i: (i, 0))
    widths = (ATT_QK, ATT_QK, SSM_INNER, SSM_CONV_DIM, DT_PAD)
    dtypes = (BF16, BF16, BF16, BF16, F32)
    shapes = [jax.ShapeDtypeStruct((n, w), dt) for w, dt in zip(widths, dtypes)]
    specs = [row(w) for w in widths]
    shapes.insert(2, jax.ShapeDtypeStruct((n // tm, N_ATT_HEADS * VT_ROWS, tm), BF16))
    specs.insert(2, pl.BlockSpec((1, N_ATT_HEADS * VT_ROWS, tm), lambda i: (i, 0, 0)))
    return pl.pallas_call(
        _inproj_kernel,
        grid=(n // tm,),
        in_specs=[row(D_MODEL), _resident(norm_w.shape), _resident(win.shape), _resident(wvt.shape)],
        out_specs=specs,
        out_shape=shapes,
        compiler_params=pltpu.CompilerParams(dimension_semantics=("arbitrary",), vmem_limit_bytes=VMEM_LIMIT),
        name="inproj",
    )(h, norm_w, win, wvt)


def _t5_bias(rel, rb_ref, head):
    half = NUM_BUCKETS // 2
    max_exact = half // 2
    ret = jnp.where(rel > 0, half, 0)
    n = jnp.abs(rel)
    nf = jnp.maximum(n, 1).astype(F32)
    large = max_exact + (jnp.log(nf / max_exact) / math.log(MAX_DISTANCE / max_exact)
                         * (half - max_exact)).astype(jnp.int32)
    large = jnp.minimum(large, half - 1)
    bucket = ret + jnp.where(n < max_exact, n, large)
    val = jnp.zeros(rel.shape, F32)
    for jb in range(NUM_BUCKETS):
        val = jnp.where(bucket == jb, rb_ref[jb, head], val)
    return val * LOG2E


def _bias_geometry(tq, tk):
    unit = min(tq, tk)
    assert tq % unit == 0 and tk % unit == 0 and unit >= T5_BAND + 1
    return unit, tk // unit, tq // unit


def _bias_kernel(rb_ref, tab_ref, mtab_ref, *, tq, tk):
    head = pl.program_id(0)
    unit, lo, hi = _bias_geometry(tq, tk)
    n_near = lo + hi + 1
    far_left = rb_ref[NUM_BUCKETS // 2 - 1, head] * LOG2E
    far_right = rb_ref[NUM_BUCKETS - 1, head] * LOG2E
    krow = lax.broadcasted_iota(jnp.int32, (LANES, LANES), 0)
    qcol = lax.broadcasted_iota(jnp.int32, (LANES, LANES), 1)
    for t in range(n_near):
        for a in range(tk // LANES):
            for b in range(tq // LANES):
                base = (a - b) * LANES + (t - lo) * unit
                blk = (slice(a * LANES, (a + 1) * LANES), slice(b * LANES, (b + 1) * LANES))
                if base + LANES - 1 <= -T5_BAND:
                    tab_ref[(0, t) + blk] = jnp.full((LANES, LANES), far_left, F32)
                elif base - LANES + 1 >= T5_BAND:
                    tab_ref[(0, t) + blk] = jnp.full((LANES, LANES), far_right, F32)
                else:
                    tab_ref[(0, t) + blk] = _t5_bias(krow - qcol + base, rb_ref, head)
    tab_ref[0, n_near] = jnp.full((tk, tq), far_left, F32)
    tab_ref[0, n_near + 1] = jnp.full((tk, tq), far_right, F32)
    mrow = lax.broadcasted_iota(jnp.int32, (N_META, tq), 0)
    mcol = lax.broadcasted_iota(jnp.int32, (N_META, tq), 1)
    mtab_ref[0, 0] = _t5_bias(mrow - N_META - mcol, rb_ref, head)
    mtab_ref[0, 1] = jnp.full((N_META, tq), far_left, F32)


def _bias_call(rel_bias, tq, tk):
    _, lo, hi = _bias_geometry(tq, tk)
    nt = lo + hi + 3
    return pl.pallas_call(
        functools.partial(_bias_kernel, tq=tq, tk=tk),
        grid=(N_ATT_HEADS,),
        in_specs=[pl.BlockSpec(memory_space=pltpu.SMEM)],
        out_specs=[pl.BlockSpec((1, nt, tk, tq), lambda h: (h, 0, 0, 0)),
                   pl.BlockSpec((1, 2, N_META, tq), lambda h: (h, 0, 0, 0))],
        out_shape=[jax.ShapeDtypeStruct((N_ATT_HEADS, nt, tk, tq), F32),
                   jax.ShapeDtypeStruct((N_ATT_HEADS, 2, N_META, tq), F32)],
        compiler_params=pltpu.CompilerParams(dimension_semantics=("arbitrary",)),
        name="t5_bias",
    )(rel_bias)


def _attn_kernel(lam_ref, qall_ref, k_ref, vt_ref, km_ref, vmt_ref, tab_ref, mtab_ref, sw_ref, o_ref,
                 sa_ref, sb_ref, mca_ref, mcb_ref, m_scr, acc_scr, accp_scr, shift_scr, flag_scr,
                 *, tq, tk, nkc, nq, n_steps):
    g = pl.program_id(0)
    qi = jnp.minimum(g, n_steps - 1) % nq
    unit, lo, hi = _bias_geometry(tq, tk)
    n_near = lo + hi + 1
    nt_dims = (((1,), (1,)), ((), ()))
    n_col = 2 * tq // ATT_COLS

    def finalize_previous():
        acc = accp_scr[...]
        o = acc[:ATT_DV] / acc[ATT_DV:ATT_DV + 1]
        lv = lam_ref[...]
        lam = (jnp.exp(jnp.sum(lv[0:1] * lv[1:2], axis=1, keepdims=True))
               - jnp.exp(jnp.sum(lv[2:3] * lv[3:4], axis=1, keepdims=True)) + LAM_INIT)
        out = o[:, :tq] - lam * o[:, tq:]
        ms = jnp.mean(out * out, axis=0, keepdims=True)
        out = out * lax.rsqrt(ms + EPS) * sw_ref[...] * (1.0 - LAM_INIT)
        o_ref[...] = out.T.astype(o_ref.dtype)

    @pl.when(g == 0)
    def _():
        accp_scr[...] = jnp.ones(accp_scr.shape, F32)

    @pl.when(g < n_steps)
    def _():
        q = qall_ref[pl.ds(pl.multiple_of(qi * tq, tq), tq), :]
        lane = lax.broadcasted_iota(jnp.int32, (tq, LANES), 1)
        zero = jnp.zeros_like(q)
        q2 = jnp.concatenate([jnp.where(lane < ATT_DH, q, zero), jnp.where(lane >= ATT_DH, q, zero)], axis=0)

        def bias_index(j):
            du = j * (tk // unit) - qi * (tq // unit)
            return jnp.where(du < -lo, n_near, jnp.where(du > hi, n_near + 1, du + lo))

        def add_bias(s, b):
            return jnp.concatenate([s[:, :tq] + b, s[:, tq:] + b], axis=1)

        @pl.when(qi == 0)
        def _():
            half = jnp.where(lax.broadcasted_iota(jnp.int32, (LANES, LANES), 0) // ATT_DH
                             == lax.broadcasted_iota(jnp.int32, (LANES, LANES), 1), 1.0, 0.0).astype(BF16)

            def max_sq_norm(x_ref):
                x = x_ref[...]
                sq = jnp.dot(x * x, half, preferred_element_type=F32)
                return jnp.max(sq, axis=0, keepdims=True)

            bound2 = max_sq_norm(qall_ref) * jnp.maximum(max_sq_norm(k_ref), max_sq_norm(km_ref))
            lane_row = lax.broadcasted_iota(jnp.int32, (1, LANES), 1)
            qk_bound = [NORM_SLACK * jnp.sqrt(jnp.max(jnp.where(lane_row == mp, bound2, 0.0), axis=1, keepdims=True))
                        for mp in range(2)]
            diag_tile = tab_ref[0, lo]
            hi_b = jnp.max(jnp.max(diag_tile, axis=0, keepdims=True), axis=1, keepdims=True)
            lo_b = jnp.min(jnp.min(diag_tile, axis=0, keepdims=True), axis=1, keepdims=True)
            col = lax.broadcasted_iota(jnp.int32, (1, 2 * tq), 1)
            shift_scr[...] = jnp.where(col < tq, qk_bound[0], qk_bound[1]) + hi_b
            worst_gap = 2.0 * jnp.maximum(qk_bound[0], qk_bound[1]) + (hi_b - lo_b)
            flag_scr[0] = (worst_gap[0, 0] <= MAX_SHIFT_GAP).astype(jnp.int32)

        shift = shift_scr[...]
        bounded = flag_scr[0] == 1

        @pl.when(bounded)
        def _():
            finalize_previous()
            sm = lax.dot_general(km_ref[...], q2, nt_dims, preferred_element_type=F32)
            sm = add_bias(sm, mtab_ref[0, jnp.minimum(qi, 1)])
            acc_scr[...] = jnp.dot(vmt_ref[...], jnp.exp2(sm - shift).astype(BF16), preferred_element_type=F32)

            def stage_logits(u, j, c):
                buf = (sa_ref, sb_ref)[u % 2]
                buf[:, pl.ds(c * ATT_COLS, ATT_COLS)] = lax.dot_general(
                    k_ref[pl.ds(j * tk, tk), :], q2[c * ATT_COLS:(c + 1) * ATT_COLS], nt_dims,
                    preferred_element_type=F32)

            units = [(j, c) for j in range(nkc) for c in range(n_col)]
            stage_logits(0, *units[0])
            for u, (j, c) in enumerate(units):
                if u + 1 < len(units):
                    stage_logits(u + 1, *units[u + 1])
                cols = pl.ds(c * ATT_COLS, ATT_COLS)
                s = (sa_ref, sb_ref)[u % 2][:, cols] + tab_ref[0, bias_index(j), :, pl.ds((c * ATT_COLS) % tq, ATT_COLS)]
                p = jnp.exp2(s - shift[:, c * ATT_COLS:(c + 1) * ATT_COLS]).astype(BF16)
                acc_scr[:, cols] += jnp.dot(vt_ref[j], p, preferred_element_type=F32)
            accp_scr[...] = acc_scr[...]

        @pl.when(jnp.logical_not(bounded))
        def _():
            finalize_previous()

            def produce(j, c, s_ref, mc_ref):
                cols = pl.ds(c * ATT_COLS, ATT_COLS)
                s = lax.dot_general(k_ref[pl.ds(j * tk, tk), :], q2[c * ATT_COLS:(c + 1) * ATT_COLS], nt_dims,
                                    preferred_element_type=F32)
                s = s + tab_ref[0, bias_index(j), :, pl.ds((c * ATT_COLS) % tq, ATT_COLS)]
                s_ref[:, cols] = s
                mc_ref[:, cols] = jnp.max(s, axis=0, keepdims=True)

            def consume(s, m_cur, vt, cols, first=False):
                if first:
                    m_new = m_cur
                else:
                    m_prev = m_scr[:, cols]
                    m_new = jnp.maximum(m_prev, m_cur)
                    alpha = jnp.exp2(m_prev - m_new)
                p = jnp.exp2(s - m_new).astype(BF16)
                pv = jnp.dot(vt, p, preferred_element_type=F32)
                acc_scr[:, cols] = pv if first else alpha * acc_scr[:, cols] + pv
                m_scr[:, cols] = m_new

            sm = lax.dot_general(km_ref[...], q2, nt_dims, preferred_element_type=F32)
            sm = add_bias(sm, mtab_ref[0, jnp.minimum(qi, 1)])
            consume(sm, jnp.max(sm, axis=0, keepdims=True), vmt_ref[...], pl.ds(0, 2 * tq), first=True)

            bufs = ((sa_ref, mca_ref), (sb_ref, mcb_ref))
            for c in range(n_col):
                produce(0, c, *bufs[0])
            for j in range(nkc):
                s_ref, mc_ref = bufs[j % 2]
                for c in range(n_col):
                    cols = pl.ds(c * ATT_COLS, ATT_COLS)
                    if j + 1 < nkc:
                        produce(j + 1, c, *bufs[(j + 1) % 2])
                    consume(s_ref[:, cols], mc_ref[:, cols], vt_ref[j], cols)
            accp_scr[...] = acc_scr[...]

    @pl.when(g == n_steps)
    def _():
        finalize_previous()


def _attn_call(lamv, q, k, vt, km, vmt, tab, mtab, subw_col, batch, seq, tq, tk):
    n = q.shape[0]
    assert n == batch * seq and seq % (2 * tk) == 0 and seq % tq == 0 and vt.shape[2] == tk
    nq = seq // tq
    nkc = seq // tk
    nt = tab.shape[1]
    n_steps = N_ATT_HEADS * batch * nq

    def tile(g):
        g = jnp.minimum(g, n_steps - 1)
        return g // (batch * nq), (g // nq) % batch, g % nq

    def cur(f):
        return lambda g: f(*tile(g))

    def prev(f):
        return lambda g: f(*tile(jnp.maximum(g - 1, 0)))

    return pl.pallas_call(
        functools.partial(_attn_kernel, tq=tq, tk=tk, nkc=nkc, nq=nq, n_steps=n_steps),
        grid=(n_steps + 1,),
        in_specs=[
            pl.BlockSpec(lamv.shape, lambda g: (0, 0)),
            pl.BlockSpec((seq, LANES), cur(lambda h, b, i: (b, h))),
            pl.BlockSpec((seq, LANES), cur(lambda h, b, i: (b, h))),
            pl.BlockSpec((nkc, VT_ROWS, tk), cur(lambda h, b, i: (b, h, 0))),
            pl.BlockSpec((N_META, LANES), cur(lambda h, b, i: (0, h))),
            pl.BlockSpec((VT_ROWS, N_META), cur(lambda h, b, i: (h, 0))),
            pl.BlockSpec((1, nt, tk, tq), cur(lambda h, b, i: (h, 0, 0, 0))),
            pl.BlockSpec((1, 2, N_META, tq), cur(lambda h, b, i: (h, 0, 0, 0))),
            pl.BlockSpec(subw_col.shape, lambda g: (0, 0)),
        ],
        out_specs=pl.BlockSpec((tq, LANES), prev(lambda h, b, i: (b * nq + i, h))),
        out_shape=jax.ShapeDtypeStruct((n, ATT_V), BF16),
        scratch_shapes=[pltpu.VMEM((tk, 2 * tq), F32)] * 2 + [pltpu.VMEM((1, 2 * tq), F32)] * 3
        + [pltpu.VMEM((VT_ROWS, 2 * tq), F32)] * 2 + [pltpu.VMEM((1, 2 * tq), F32), pltpu.SMEM((1,), jnp.int32)],
        compiler_params=pltpu.CompilerParams(dimension_semantics=("arbitrary",), vmem_limit_bytes=VMEM_LIMIT),
        name="diff_attn",
    )(lamv, q, k, vt, km, vmt, tab, mtab, subw_col)


def _split3(x):
    hi = x.astype(BF16)
    r1 = x - hi.astype(F32)
    mid = r1.astype(BF16)
    lo = (r1 - mid.astype(F32)).astype(BF16)
    return hi, mid, lo


def _cumsum_rows(a):
    rows = a.shape[0]
    r_i = lax.broadcasted_iota(jnp.int32, (rows, rows), 0)
    c_i = lax.broadcasted_iota(jnp.int32, (rows, rows), 1)
    tri = jnp.where(c_i <= r_i, 1.0, 0.0).astype(BF16)
    out = None
    for term in _split3(a):
        part = jnp.dot(tri, term, preferred_element_type=F32)
        out = part if out is None else out + part
    return out


def _expand_rows(parts, sel_ref):
    masked = []
    for w, first in parts:
        lane = lax.broadcasted_iota(jnp.int32, w.shape, 1)
        masked.append(jnp.where((lane >= first) & (lane < first + SSM_HEADS), w, 0.0))
    stacked = jnp.concatenate(masked, axis=0)
    hi = stacked.astype(BF16)
    lo = (stacked - hi.astype(F32)).astype(BF16)
    sel = sel_ref[...]
    full = jnp.dot(hi, sel, preferred_element_type=F32) + jnp.dot(lo, sel, preferred_element_type=F32)
    outs, r0 = [], 0
    for w, _ in parts:
        outs.append(full[r0:r0 + w.shape[0]])
        r0 += w.shape[0]
    return outs


def _softplus(x):
    return jnp.maximum(x, 0.0) + jnp.log(1.0 + jnp.exp(-jnp.abs(x)))


GROUP_COLS = SSM_INNER // SSM_GROUPS


def _state_update(b_t, xw):
    return jnp.concatenate(
        [jnp.dot(b_t[g * SSM_STATE:(g + 1) * SSM_STATE], xw[:, g * GROUP_COLS:(g + 1) * GROUP_COLS],
                 preferred_element_type=F32) for g in range(SSM_GROUPS)], axis=0)


def _stack_decay(dec_row):
    return jnp.concatenate(
        [jnp.broadcast_to(dec_row[:, g * GROUP_COLS:(g + 1) * GROUP_COLS], (SSM_STATE, GROUP_COLS))
         for g in range(SSM_GROUPS)], axis=0)


def _conv_silu(win, shift_ref, cw_ref, cb_ref, rows):
    total = rows + 2 * HALO
    assert win.shape[0] == total and shift_ref.shape == (len(MXU_TAPS) * rows, total)
    shifted = jnp.dot(shift_ref[...], win, preferred_element_type=F32)
    win32 = win.astype(F32)
    acc = jnp.broadcast_to(cb_ref[...], (rows, SSM_CONV_DIM))
    for j in range(SSM_CONV):
        off = j - SSM_CONV // 2
        if j in MXU_TAPS:
            tap = shifted[MXU_TAPS.index(j) * rows:(MXU_TAPS.index(j) + 1) * rows]
        elif off == 0:
            tap = win32[HALO:HALO + rows]
        else:
            tap = pltpu.roll(win32, (total - off) % total, axis=0)[HALO:HALO + rows]
        acc = acc + cw_ref[j:j + 1, :] * tap
    return acc * jax.nn.sigmoid(acc)


def _shift_matrix(rows):
    offs = jnp.array([j - SSM_CONV // 2 for j in MXU_TAPS])
    t = jnp.arange(rows)
    src = HALO + t[None, :] + offs[:, None]
    return (src.reshape(-1)[:, None] == jnp.arange(rows + 2 * HALO)[None, :]).astype(BF16)


def _ssd_kernel(z_ref, xc_ref, xl_ref, xr_ref, dt_ref, mx_ref, mdt_ref, cw_ref, cb_ref, dtb_ref, alog_ref,
                dsk_ref, nw_ref, sel_ref, shc_ref, shm_ref, o_ref, xs_scr, dts_scr, cum_scr, hbs_scr, hf_scr, hb_scr,
                win_scr,
                *, cs, sub, nb):
    rows = cs * sub
    ph = pl.program_id(1)
    t = pl.program_id(2)
    fwd0, bwd0 = 0, SSM_HEADS
    a_row = -jnp.exp(alog_ref[...])

    def decay_terms(dt_raw):
        dt = _softplus(dt_raw + dtb_ref[...])
        return dt, _cumsum_rows(dt * a_row)

    def bcast8(row):
        return jnp.broadcast_to(row, (8, LANES))

    @pl.when(ph == 0)
    def _():
        blk = nb - 1 - t

        @pl.when(t == 0)
        def _():
            hb_scr[...] = jnp.zeros(hb_scr.shape, F32)

        left = jnp.where(blk == 0, mx_ref[...], xl_ref[...])
        right = jnp.where(blk == nb - 1, jnp.zeros_like(xr_ref[...]), xr_ref[...])
        win_scr[0:HALO, :] = left
        win_scr[HALO:HALO + rows, :] = xc_ref[...]
        win_scr[HALO + rows:HALO + rows + HALO, :] = right

        hb = hb_scr[...]
        for si in reversed(range(sub)):
            cc = blk * sub + si
            xbc = _conv_silu(win_scr[si * cs:si * cs + cs + 2 * HALO, :], shc_ref, cw_ref, cb_ref, cs)
            xs_scr[cc] = xbc.astype(BF16)
            dt, cum = decay_terms(dt_ref[si * cs:(si + 1) * cs, :])
            dts_scr[cc] = dt
            cum_scr[cc] = cum
            eb = cum - dt * a_row
            w_b, dec = _expand_rows([(jnp.exp(eb) * dt, bwd0), (bcast8(jnp.exp(cum[cs - 1:cs, :])), bwd0)],
                                    sel_ref)
            xw = (xbc[:, :SSM_INNER] * w_b).astype(BF16)
            bm_t = xbc[:, SSM_INNER:SSM_INNER + LANES].T.astype(BF16)
            hbs_scr[cc] = hb.astype(BF16)
            hb = hb * _stack_decay(dec[0:1]) + _state_update(bm_t, xw)
        hb_scr[...] = hb

    @pl.when(ph == 1)
    def _():
        @pl.when(t == 0)
        def _():
            wm = jnp.concatenate([jnp.zeros((HALO, SSM_CONV_DIM), BF16), mx_ref[...], xc_ref[0:HALO, :]], axis=0)
            xm = _conv_silu(wm, shm_ref, cw_ref, cb_ref, N_META)
            dtm, cumm = decay_terms(mdt_ref[...])
            (w_m,) = _expand_rows([(jnp.exp(cumm[N_META - 1:N_META, :] - cumm) * dtm, fwd0)], sel_ref)
            xwm = (xm[:, :SSM_INNER] * w_m).astype(BF16)
            bmm_t = xm[:, SSM_INNER:SSM_INNER + LANES].T.astype(BF16)
            hf_scr[...] = _state_update(bmm_t, xwm)

        lane = lax.broadcasted_iota(jnp.int32, (cs, LANES), 1)
        l_i = lax.broadcasted_iota(jnp.int32, (cs, cs), 0)
        s_i = lax.broadcasted_iota(jnp.int32, (cs, cs), 1)
        lower = s_i <= l_i
        diag = s_i == l_i
        hpg = SSM_HEADS // SSM_GROUPS
        zx = jnp.zeros((cs, LANES), BF16)
        nt_dims = (((1,), (1,)), ((), ()))

        hf = hf_scr[...]
        for si in range(sub):
            cc = t * sub + si
            xbc = xs_scr[cc]
            x_bf = xbc[:, :SSM_INNER]
            bm = xbc[:, SSM_INNER:SSM_INNER + LANES]
            cm = xbc[:, SSM_INNER + LANES:SSM_INNER + 2 * LANES]
            x = x_bf.astype(F32)

            dt = dts_scr[cc]
            cum = cum_scr[cc]
            eb = cum - dt * a_row
            dt_t, cum_t, eb_t = dt.T, cum.T, eb.T
            last = cum[cs - 1:cs, :]

            c_grp = [jnp.where(lane // SSM_STATE == g, cm, jnp.zeros_like(cm)) for g in range(SSM_GROUPS)]
            g_mats = [lax.dot_general(c_g, bm, nt_dims, preferred_element_type=F32) for c_g in c_grp]

            pieces = []
            for hp in range(SSM_HEADS // 2):
                w_pair = []
                for h in (2 * hp, 2 * hp + 1):
                    arg_f = cum[:, fwd0 + h:fwd0 + h + 1] - cum_t[fwd0 + h:fwd0 + h + 1, :]
                    arg_b = eb_t[bwd0 + h:bwd0 + h + 1, :] - eb[:, bwd0 + h:bwd0 + h + 1]
                    e = jnp.exp(jnp.minimum(jnp.where(lower, arg_f, arg_b), 0.0))
                    dt_f_row = dt_t[fwd0 + h:fwd0 + h + 1, :]
                    dt_b_row = dt_t[bwd0 + h:bwd0 + h + 1, :]
                    m = e * jnp.where(lower, dt_f_row, dt_b_row) + jnp.where(diag, dt_b_row, 0.0)
                    w_pair.append((g_mats[h // hpg] * m).astype(BF16))
                xp = x_bf[:, hp * LANES:(hp + 1) * LANES]
                rhs = jnp.concatenate([jnp.where(lane < SSM_HEADDIM, xp, zx),
                                       jnp.where(lane >= SSM_HEADDIM, xp, zx)], axis=0)
                pieces.append(jnp.dot(jnp.concatenate(w_pair, axis=1), rhs, preferred_element_type=F32))
            y = jnp.concatenate(pieces, axis=1)

            d_f, d_b, w_f, dec = _expand_rows(
                [(jnp.exp(cum), fwd0), (jnp.exp(last - eb), bwd0), (jnp.exp(last - cum) * dt, fwd0),
                 (bcast8(jnp.exp(last)), fwd0)], sel_ref)
            hf_bf = hf.astype(BF16)
            hb_bf = hbs_scr[cc]
            y = y + d_f * jnp.concatenate([jnp.dot(c_g, hf_bf, preferred_element_type=F32) for c_g in c_grp],
                                          axis=1)
            y = y + d_b * jnp.concatenate([jnp.dot(c_g, hb_bf, preferred_element_type=F32) for c_g in c_grp],
                                          axis=1)
            y = y + x * dsk_ref[...]

            xw = (x * w_f).astype(BF16)
            hf = hf * _stack_decay(dec[0:1]) + _state_update(bm.astype(F32).T.astype(BF16), xw)

            zf = z_ref[si * cs:(si + 1) * cs, :].astype(F32)
            y = y * (zf * jax.nn.sigmoid(zf))
            o_ref[si * cs:(si + 1) * cs, :] = _rmsnorm(y, nw_ref[...]).astype(o_ref.dtype)
        hf_scr[...] = hf


def _ssd_call(z, xbc, dt, mxbc, mdt, cw, cb, dtb, alog, dskip, nw, sel, shc, shm, batch, seq, cs, sub):
    n = z.shape[0]
    rows = cs * sub
    assert n == batch * seq and seq % rows == 0 and cs % HALO == 0
    nc = seq // cs
    nb = seq // rows
    hpb = rows // HALO
    n_halo = n // HALO

    def ph0_block(ph, t):
        return (1 - ph) * (nb - 1 - t)

    const2 = lambda shape: pl.BlockSpec(shape, lambda b, ph, t: (0, 0))
    return pl.pallas_call(
        functools.partial(_ssd_kernel, cs=cs, sub=sub, nb=nb),
        grid=(batch, 2, nb),
        in_specs=[
            pl.BlockSpec((rows, SSM_INNER), lambda b, ph, t: (b * nb + ph * t, 0)),
            pl.BlockSpec((rows, SSM_CONV_DIM), lambda b, ph, t: (b * nb + ph0_block(ph, t), 0)),
            pl.BlockSpec((HALO, SSM_CONV_DIM),
                         lambda b, ph, t: (jnp.maximum((b * nb + ph0_block(ph, t)) * hpb - 1, 0), 0)),
            pl.BlockSpec((HALO, SSM_CONV_DIM),
                         lambda b, ph, t: (jnp.minimum((b * nb + ph0_block(ph, t) + 1) * hpb, n_halo - 1), 0)),
            pl.BlockSpec((rows, DT_PAD), lambda b, ph, t: (b * nb + ph0_block(ph, t), 0)),
            const2(mxbc.shape), const2(mdt.shape), const2(cw.shape), const2(cb.shape), const2(dtb.shape),
            const2(alog.shape), const2(dskip.shape), const2(nw.shape), const2(sel.shape), const2(shc.shape),
            const2(shm.shape),
        ],
        out_specs=pl.BlockSpec((rows, SSM_INNER), lambda b, ph, t: (b * nb + ph * t, 0)),
        out_shape=jax.ShapeDtypeStruct((n, SSM_INNER), BF16),
        scratch_shapes=[
            pltpu.VMEM((nc, cs, SSM_CONV_DIM), BF16),
            pltpu.VMEM((nc, cs, DT_PAD), F32),
            pltpu.VMEM((nc, cs, DT_PAD), F32),
            pltpu.VMEM((nc, LANES, GROUP_COLS), BF16),
            pltpu.VMEM((LANES, GROUP_COLS), F32),
            pltpu.VMEM((LANES, GROUP_COLS), F32),
            pltpu.VMEM((rows + 2 * HALO, SSM_CONV_DIM), BF16),
        ],
        compiler_params=pltpu.CompilerParams(dimension_semantics=("arbitrary",) * 3, vmem_limit_bytes=VMEM_LIMIT),
        name="bi_ssd",
    )(z, xbc, xbc, xbc, dt, mxbc, mdt, cw, cb, dtb, alog, dskip, nw, sel, shc, shm)


def _head_selector():
    k = jnp.arange(LANES)[:, None]
    col = jnp.arange(SSM_INNER)[None, :]
    return ((k % SSM_HEADS == col // SSM_HEADDIM) & (k < 2 * SSM_HEADS)).astype(BF16)


def _prep_weights(ffn1_norm_w, ffn1_w_gate, ffn1_w_up, ffn1_w_down, mix_norm_w, w_in, lambda_q1, lambda_k1,
                  lambda_q2, lambda_k2, attn_subln_w, conv_w, conv_b, dt_bias_fwd, dt_bias_bwd, a_log_fwd,
                  a_log_bwd, ssm_d, ssm_norm_w, w_out, ffn2_norm_w, ffn2_w_gate, ffn2_w_up, ffn2_w_down,
                  final_norm_w):
    def ffn(norm_w, wg, wu, wd):
        return norm_w[0][None, :], wg[0].astype(BF16), wu[0].astype(BF16), wd[0].astype(BF16)

    pad_lanes = lambda v, width: jnp.pad(v, (0, width - v.shape[0]))[None, :]
    o_v, o_z = 2 * ATT_QK, 2 * ATT_QK + ATT_V
    wi = w_in[0].astype(BF16)
    win = jnp.pad(jnp.concatenate([wi[:, :o_v], wi[:, o_z:]], axis=1),
                  ((0, 0), (0, D_IN_PAD - (w_in.shape[2] - ATT_V))))
    return dict(
        ffn1=ffn(ffn1_norm_w, ffn1_w_gate, ffn1_w_up, ffn1_w_down),
        ffn2=ffn(ffn2_norm_w, ffn2_w_gate, ffn2_w_up, ffn2_w_down),
        mix_norm=mix_norm_w[0][None, :],
        win=win,
        wvt=wi[:, o_v:o_z].T,
        lamv=jnp.stack([lambda_q1[0], lambda_k1[0], lambda_q2[0], lambda_k2[0]]),
        subw_col=attn_subln_w[0][:, None],
        cw=jnp.pad(conv_w[0], ((0, 8 - SSM_CONV), (0, 0))),
        cb=conv_b[0][None, :],
        dtb=pad_lanes(jnp.concatenate([dt_bias_fwd[0], dt_bias_bwd[0]]), DT_PAD),
        alog=pad_lanes(jnp.concatenate([a_log_fwd[0], a_log_bwd[0]]), DT_PAD),
        dskip=jnp.repeat(ssm_d[0], SSM_HEADDIM)[None, :],
        ssm_norm=ssm_norm_w[0][None, :],
        wo=w_out[0].astype(BF16).reshape(2, ATT_V, D_MODEL),
        final=final_norm_w[None, :],
        sel=_head_selector(),
        shc=_shift_matrix(SSD_CHUNK),
        shm=_shift_matrix(N_META),
    )


def _encode(x, w, meta_proj, tab, mtab):
    batch, seq, _ = x.shape
    km, vmt, mxbc, mdt = meta_proj
    h0 = x.reshape(batch * seq, D_MODEL)
    h1 = _ffn_call(h0, *w["ffn1"])
    q, k, vt, z, xbc, dt = _inproj_call(h1, w["mix_norm"], w["win"], w["wvt"])
    att = _attn_call(w["lamv"], q, k, vt, km, vmt, tab, mtab, w["subw_col"], batch, seq, ATT_TQ, ATT_TK)
    ssm = _ssd_call(z, xbc, dt, mxbc, mdt, w["cw"], w["cb"], w["dtb"], w["alog"], w["dskip"], w["ssm_norm"],
                    w["sel"], w["shc"], w["shm"], batch, seq, SSD_CHUNK, SSD_SUB)
    y = _ffn_call(h1, *w["ffn2"], mix=(att, ssm, w["wo"]), final_w=w["final"])
    return y.reshape(batch, seq, D_MODEL)


def kernel(x_prompt, x_sample, meta_tokens, ffn1_norm_w, ffn1_w_gate, ffn1_w_up, ffn1_w_down, mix_norm_w, w_in, rel_bias, lambda_q1, lambda_k1, lambda_q2, lambda_k2, attn_subln_w, conv_w, conv_b, dt_bias_fwd, dt_bias_bwd, a_log_fwd, a_log_bwd, ssm_d, ssm_norm_w, w_out, ffn2_norm_w, ffn2_w_gate, ffn2_w_up, ffn2_w_down, final_norm_w):
    w = _prep_weights(ffn1_norm_w, ffn1_w_gate, ffn1_w_up, ffn1_w_down, mix_norm_w, w_in, lambda_q1, lambda_k1,
                      lambda_q2, lambda_k2, attn_subln_w, conv_w, conv_b, dt_bias_fwd, dt_bias_bwd, a_log_fwd,
                      a_log_bwd, ssm_d, ssm_norm_w, w_out, ffn2_norm_w, ffn2_w_gate, ffn2_w_up, ffn2_w_down,
                      final_norm_w)
    hm = _ffn_call(meta_tokens, *w["ffn1"])
    _, km, vmt, _, mxbc, mdt = _inproj_call(hm, w["mix_norm"], w["win"], w["wvt"])
    meta_proj = (km, vmt[0], mxbc, mdt)
    tab, mtab = _bias_call(rel_bias, ATT_TQ, ATT_TK)
    return (_encode(x_prompt, w, meta_proj, tab, mtab), _encode(x_sample, w, meta_proj, tab, mtab))
```

```python
import functools
import math

import jax
import jax.numpy as jnp
from jax import lax
from jax.experimental import pallas as pl
from jax.experimental.pallas import tpu as pltpu

F32 = jnp.float32
BF16 = jnp.bfloat16

D_MODEL = 1024
N_META = 16
N_ATT_HEADS = 8
ATT_DH = 64
ATT_DV = 128
ATT_QK = 1024
ATT_V = 1024
NUM_BUCKETS = 32
MAX_DISTANCE = 128
SSM_HEADS = 16
SSM_HEADDIM = 64
SSM_INNER = 1024
SSM_GROUPS = 2
SSM_STATE = 64
SSM_CONV = 7
SSM_CONV_DIM = 1280
D_FF = 2816
EPS = 1e-6
LAYER = 0
LAM_INIT = 0.8 - 0.6 * math.exp(-0.3 * LAYER)
LOG2E = math.log2(math.e)
Q_SCALE = ATT_DH ** -0.5 * LOG2E
NEG_BIG = -1e30
NORM_SLACK = 1.02
MAX_SHIFT_GAP = 100.0

LANES = 128
BF16_ROWS = 16
VMEM_LIMIT = 56 * 1024 * 1024

FF_TILE = 256
N_FF = D_FF // FF_TILE
DT_PAD = LANES
D_IN_PAD = 2 * ATT_QK + SSM_INNER + SSM_CONV_DIM + DT_PAD
T5_BAND = 91

ROW_TILE = 512
ATT_TQ = 1024
ATT_TK = 512
ATT_COLS = 256
SSD_CHUNK = 128
MXU_TAPS = (0, 1, 5)
SSD_SUB = 4
HALO = BF16_ROWS
VT_ROWS = ATT_DV + BF16_ROWS


def _rmsnorm(x, w):
    ms = jnp.mean(x * x, axis=-1, keepdims=True)
    return x * lax.rsqrt(ms + EPS) * w


def _resident(shape):
    nd = len(shape)
    return pl.BlockSpec(shape, lambda *_: (0,) * nd, pipeline_mode=pl.Buffered(1))


def _ffn_kernel(*refs, has_mix, has_final):
    it = iter(refs)
    h_ref = next(it)
    if has_mix:
        att_ref, ssm_ref, wo_ref = next(it), next(it), next(it)
    nw_ref, wg_ref, wu_ref, wd_ref = next(it), next(it), next(it), next(it)
    fw_ref = next(it) if has_final else None
    o_ref = next(it)

    h = h_ref[...]
    if has_mix:
        h = (h + jnp.dot(att_ref[...], wo_ref[0], preferred_element_type=F32)
             + jnp.dot(ssm_ref[...], wo_ref[1], preferred_element_type=F32))
    u = _rmsnorm(h, nw_ref[...]).astype(BF16)
    acc = jnp.zeros_like(h)
    for j in range(N_FF):
        ff = slice(j * FF_TILE, (j + 1) * FF_TILE)
        g = jnp.dot(u, wg_ref[:, ff], preferred_element_type=F32)
        up = jnp.dot(u, wu_ref[:, ff], preferred_element_type=F32)
        a = (g * jax.nn.sigmoid(g) * up).astype(BF16)
        acc = acc + jnp.dot(a, wd_ref[ff, :], preferred_element_type=F32)
    h = h + 0.5 * acc
    if has_final:
        h = _rmsnorm(h, fw_ref[...])
    o_ref[...] = h


def _ffn_call(h, norm_w, wg, wu, wd, mix=None, final_w=None):
    n = h.shape[0]
    tm = min(ROW_TILE, n)
    assert n % tm == 0
    row = lambda width: pl.BlockSpec((tm, width), lambda i: (i, 0))
    args, specs = [h], [row(D_MODEL)]
    if mix is not None:
        att, ssm, wo = mix
        args += [att, ssm, wo]
        specs += [row(ATT_V), row(SSM_INNER), _resident(wo.shape)]
    args += [norm_w, wg, wu, wd]
    specs += [_resident(norm_w.shape), _resident(wg.shape), _resident(wu.shape), _resident(wd.shape)]
    if final_w is not None:
        args.append(final_w)
        specs.append(_resident(final_w.shape))
    return pl.pallas_call(
        functools.partial(_ffn_kernel, has_mix=mix is not None, has_final=final_w is not None),
        grid=(n // tm,),
        in_specs=specs,
        out_specs=row(D_MODEL),
        out_shape=jax.ShapeDtypeStruct((n, D_MODEL), F32),
        compiler_params=pltpu.CompilerParams(dimension_semantics=("arbitrary",), vmem_limit_bytes=VMEM_LIMIT),
        name="ffn_mix" if mix is not None else "ffn",
    )(*args)


_IN_SEGS = (("q", 0, ATT_QK), ("k", ATT_QK, ATT_QK), ("z", 2 * ATT_QK, SSM_INNER),
            ("xbc", 2 * ATT_QK + SSM_INNER, SSM_CONV_DIM), ("dt", D_IN_PAD - DT_PAD, DT_PAD))


def _inproj_kernel(h_ref, nw_ref, win_ref, wvt_ref, nsel_ref, q_ref, k_ref, vt_ref, z_ref, xbc_ref, dt_ref,
                   qn_ref, kn_ref):
    u = _rmsnorm(h_ref[...], nw_ref[...]).astype(BF16)
    outs = dict(q=q_ref, k=k_ref, z=z_ref, xbc=xbc_ref, dt=dt_ref)
    sq_norm = dict(q=None, k=None)
    for name, c0, width in _IN_SEGS:
        o_ref = outs[name]
        step = 512 if width % 512 == 0 else (256 if width % 256 == 0 else LANES)
        for s in range(0, width, step):
            r = jnp.dot(u, win_ref[:, c0 + s:c0 + s + step], preferred_element_type=F32)
            if name == "q":
                r = r * Q_SCALE
            o_ref[:, s:s + step] = r.astype(o_ref.dtype)
            if name in sq_norm:
                part = jnp.dot((r * r).astype(BF16), nsel_ref[s:s + step, :], preferred_element_type=F32)
                sq_norm[name] = part if sq_norm[name] is None else sq_norm[name] + part
    qn_ref[0] = jnp.broadcast_to(jnp.max(sq_norm["q"], axis=0, keepdims=True), (8, LANES))
    kn_ref[0] = jnp.broadcast_to(jnp.max(sq_norm["k"], axis=0, keepdims=True), (8, LANES))
    nt_dims = (((1,), (1,)), ((), ()))
    ones = jnp.ones((VT_ROWS - ATT_DV, u.shape[0]), vt_ref.dtype)
    for s in range(0, ATT_V, 256):
        r = lax.dot_general(wvt_ref[s:s + 256, :], u, nt_dims, preferred_element_type=F32).astype(vt_ref.dtype)
        for hh in range(256 // ATT_DV):
            head = s // ATT_DV + hh
            vt_ref[0, head * VT_ROWS:head * VT_ROWS + ATT_DV, :] = r[hh * ATT_DV:(hh + 1) * ATT_DV]
            vt_ref[0, head * VT_ROWS + ATT_DV:(head + 1) * VT_ROWS, :] = ones


def _inproj_call(h, norm_w, win, wvt, nsel):
    n = h.shape[0]
    tm = min(ATT_TK, n)
    assert n % tm == 0
    row = lambda width: pl.BlockSpec((tm, width), lambda i: (i, 0))
    widths = (ATT_QK, ATT_QK, SSM_INNER, SSM_CONV_DIM, DT_PAD)
    dtypes = (BF16, BF16, BF16, BF16, F32)
    shapes = [jax.ShapeDtypeStruct((n, w), dt) for w, dt in zip(widths, dtypes)]
    specs = [row(w) for w in widths]
    shapes.insert(2, jax.ShapeDtypeStruct((n // tm, N_ATT_HEADS * VT_ROWS, tm), BF16))
    specs.insert(2, pl.BlockSpec((1, N_ATT_HEADS * VT_ROWS, tm), lambda i: (i, 0, 0)))
    shapes += [jax.ShapeDtypeStruct((n // tm, 8, LANES), F32)] * 2
    specs += [pl.BlockSpec((1, 8, LANES), lambda i: (i, 0, 0))] * 2
    return pl.pallas_call(
        _inproj_kernel,
        grid=(n // tm,),
        in_specs=[row(D_MODEL), _resident(norm_w.shape), _resident(win.shape), _resident(wvt.shape),
                  _resident(nsel.shape)],
        out_specs=specs,
        out_shape=shapes,
        compiler_params=pltpu.CompilerParams(dimension_semantics=("arbitrary",), vmem_limit_bytes=VMEM_LIMIT),
        name="inproj",
    )(h, norm_w, win, wvt, nsel)


def _t5_bias(rel, rb_ref, head):
    half = NUM_BUCKETS // 2
    max_exact = half // 2
    ret = jnp.where(rel > 0, half, 0)
    n = jnp.abs(rel)
    nf = jnp.maximum(n, 1).astype(F32)
    large = max_exact + (jnp.log(nf / max_exact) / math.log(MAX_DISTANCE / max_exact)
                         * (half - max_exact)).astype(jnp.int32)
    large = jnp.minimum(large, half - 1)
    bucket = ret + jnp.where(n < max_exact, n, large)
    val = jnp.zeros(rel.shape, F32)
    for jb in range(NUM_BUCKETS):
        val = jnp.where(bucket == jb, rb_ref[jb, head], val)
    return val * LOG2E


def _bias_geometry(tq, tk):
    unit = min(tq, tk)
    assert tq % unit == 0 and tk % unit == 0 and unit >= T5_BAND + 1
    return unit, tk // unit, tq // unit


def _bias_kernel(rb_ref, tab_ref, mtab_ref, *, tq, tk):
    head = pl.program_id(0)
    unit, lo, hi = _bias_geometry(tq, tk)
    n_near = lo + hi + 1
    far_left = rb_ref[NUM_BUCKETS // 2 - 1, head] * LOG2E
    far_right = rb_ref[NUM_BUCKETS - 1, head] * LOG2E
    krow = lax.broadcasted_iota(jnp.int32, (LANES, LANES), 0)
    qcol = lax.broadcasted_iota(jnp.int32, (LANES, LANES), 1)
    for t in range(n_near):
        for a in range(tk // LANES):
            for b in range(tq // LANES):
                base = (a - b) * LANES + (t - lo) * unit
                blk = (slice(a * LANES, (a + 1) * LANES), slice(b * LANES, (b + 1) * LANES))
                if base + LANES - 1 <= -T5_BAND:
                    tab_ref[(0, t) + blk] = jnp.full((LANES, LANES), far_left, F32)
                elif base - LANES + 1 >= T5_BAND:
                    tab_ref[(0, t) + blk] = jnp.full((LANES, LANES), far_right, F32)
                else:
                    tab_ref[(0, t) + blk] = _t5_bias(krow - qcol + base, rb_ref, head)
    tab_ref[0, n_near] = jnp.full((tk, tq), far_left, F32)
    tab_ref[0, n_near + 1] = jnp.full((tk, tq), far_right, F32)
    mrow = lax.broadcasted_iota(jnp.int32, (N_META, tq), 0)
    mcol = lax.broadcasted_iota(jnp.int32, (N_META, tq), 1)
    mtab_ref[0, 0] = _t5_bias(mrow - N_META - mcol, rb_ref, head)
    mtab_ref[0, 1] = jnp.full((N_META, tq), far_left, F32)


def _bias_call(rel_bias, tq, tk):
    _, lo, hi = _bias_geometry(tq, tk)
    nt = lo + hi + 3
    return pl.pallas_call(
        functools.partial(_bias_kernel, tq=tq, tk=tk),
        grid=(N_ATT_HEADS,),
        in_specs=[pl.BlockSpec(memory_space=pltpu.SMEM)],
        out_specs=[pl.BlockSpec((1, nt, tk, tq), lambda h: (h, 0, 0, 0)),
                   pl.BlockSpec((1, 2, N_META, tq), lambda h: (h, 0, 0, 0))],
        out_shape=[jax.ShapeDtypeStruct((N_ATT_HEADS, nt, tk, tq), F32),
                   jax.ShapeDtypeStruct((N_ATT_HEADS, 2, N_META, tq), F32)],
        compiler_params=pltpu.CompilerParams(dimension_semantics=("arbitrary",)),
        name="t5_bias",
    )(rel_bias)


def _attn_kernel(lam_ref, qall_ref, k_ref, vt_ref, km_ref, vmt_ref, tab_ref, mtab_ref, sw_ref, qn_ref, kn_ref,
                 knm_ref, o_ref,
                 sa_ref, sb_ref, mca_ref, mcb_ref, m_scr, acc_scr, accp_scr, shift_scr, flag_scr,
                 *, tq, tk, nkc, nq, n_steps):
    g = pl.program_id(0)
    qi = jnp.minimum(g, n_steps - 1) % nq
    head = jnp.minimum(g, n_steps - 1) // (n_steps // N_ATT_HEADS)
    unit, lo, hi = _bias_geometry(tq, tk)
    n_near = lo + hi + 1
    nt_dims = (((1,), (1,)), ((), ()))
    n_col = 2 * tq // ATT_COLS

    def finalize_previous():
        acc = accp_scr[...]
        o = acc[:ATT_DV] / acc[ATT_DV:ATT_DV + 1]
        lv = lam_ref[...]
        lam = (jnp.exp(jnp.sum(lv[0:1] * lv[1:2], axis=1, keepdims=True))
               - jnp.exp(jnp.sum(lv[2:3] * lv[3:4], axis=1, keepdims=True)) + LAM_INIT)
        out = o[:, :tq] - lam * o[:, tq:]
        ms = jnp.mean(out * out, axis=0, keepdims=True)
        out = out * lax.rsqrt(ms + EPS) * sw_ref[...] * (1.0 - LAM_INIT)
        o_ref[...] = out.T.astype(o_ref.dtype)

    @pl.when(g == 0)
    def _():
        accp_scr[...] = jnp.ones(accp_scr.shape, F32)

    @pl.when(g < n_steps)
    def _():
        q = qall_ref[pl.ds(pl.multiple_of(qi * tq, tq), tq), :]
        lane = lax.broadcasted_iota(jnp.int32, (tq, LANES), 1)
        zero = jnp.zeros_like(q)
        q2 = jnp.concatenate([jnp.where(lane < ATT_DH, q, zero), jnp.where(lane >= ATT_DH, q, zero)], axis=0)

        def bias_index(j):
            du = j * (tk // unit) - qi * (tq // unit)
            return jnp.where(du < -lo, n_near, jnp.where(du > hi, n_near + 1, du + lo))

        def add_bias(s, b):
            return jnp.concatenate([s[:, :tq] + b, s[:, tq:] + b], axis=1)

        @pl.when(qi == 0)
        def _():
            def tiles_max(n_ref):
                return jnp.max(jnp.max(n_ref[...], axis=0), axis=0, keepdims=True)

            bound2 = tiles_max(qn_ref) * jnp.maximum(tiles_max(kn_ref), tiles_max(knm_ref))
            lane_row = lax.broadcasted_iota(jnp.int32, (1, LANES), 1)
            qk_bound = [NORM_SLACK * jnp.sqrt(jnp.max(jnp.where(lane_row == 2 * head + mp, bound2, 0.0),
                                                      axis=1, keepdims=True))
                        for mp in range(2)]
            diag_tile = tab_ref[0, lo]
            hi_b = jnp.max(jnp.max(diag_tile, axis=0, keepdims=True), axis=1, keepdims=True)
            lo_b = jnp.min(jnp.min(diag_tile, axis=0, keepdims=True), axis=1, keepdims=True)
            col = lax.broadcasted_iota(jnp.int32, (1, 2 * tq), 1)
            shift_scr[...] = jnp.where(col < tq, qk_bound[0], qk_bound[1]) + hi_b
            worst_gap = 2.0 * jnp.maximum(qk_bound[0], qk_bound[1]) + (hi_b - lo_b)
            flag_scr[0] = (worst_gap[0, 0] <= MAX_SHIFT_GAP).astype(jnp.int32)

        shift = shift_scr[...]
        bounded = flag_scr[0] == 1

        @pl.when(bounded)
        def _():
            finalize_previous()
            sm = lax.dot_general(km_ref[...], q2, nt_dims, preferred_element_type=F32)
            sm = add_bias(sm, mtab_ref[0, jnp.minimum(qi, 1)])
            acc_scr[...] = jnp.dot(vmt_ref[...], jnp.exp2(sm - shift).astype(BF16), preferred_element_type=F32)

            def stage_logits(u, j, c):
                buf = (sa_ref, sb_ref)[u % 2]
                buf[:, pl.ds(c * ATT_COLS, ATT_COLS)] = lax.dot_general(
                    k_ref[pl.ds(j * tk, tk), :], q2[c * ATT_COLS:(c + 1) * ATT_COLS], nt_dims,
                    preferred_element_type=F32)

            units = [(j, c) for j in range(nkc) for c in range(n_col)]
            stage_logits(0, *units[0])
            for u, (j, c) in enumerate(units):
                if u + 1 < len(units):
                    stage_logits(u + 1, *units[u + 1])
                cols = pl.ds(c * ATT_COLS, ATT_COLS)
                s = (sa_ref, sb_ref)[u % 2][:, cols] + tab_ref[0, bias_index(j), :, pl.ds((c * ATT_COLS) % tq, ATT_COLS)]
                p = jnp.exp2(s - shift[:, c * ATT_COLS:(c + 1) * ATT_COLS]).astype(BF16)
                acc_scr[:, cols] += jnp.dot(vt_ref[j], p, preferred_element_type=F32)
            accp_scr[...] = acc_scr[...]

        @pl.when(jnp.logical_not(bounded))
        def _():
            finalize_previous()

            def produce(j, c, s_ref, mc_ref):
                cols = pl.ds(c * ATT_COLS, ATT_COLS)
                s = lax.dot_general(k_ref[pl.ds(j * tk, tk), :], q2[c * ATT_COLS:(c + 1) * ATT_COLS], nt_dims,
                                    preferred_element_type=F32)
                s = s + tab_ref[0, bias_index(j), :, pl.ds((c * ATT_COLS) % tq, ATT_COLS)]
                s_ref[:, cols] = s
                mc_ref[:, cols] = jnp.max(s, axis=0, keepdims=True)

            def consume(s, m_cur, vt, cols, first=False):
                if first:
                    m_new = m_cur
                else:
                    m_prev = m_scr[:, cols]
                    m_new = jnp.maximum(m_prev, m_cur)
                    alpha = jnp.exp2(m_prev - m_new)
                p = jnp.exp2(s - m_new).astype(BF16)
                pv = jnp.dot(vt, p, preferred_element_type=F32)
                acc_scr[:, cols] = pv if first else alpha * acc_scr[:, cols] + pv
                m_scr[:, cols] = m_new

            sm = lax.dot_general(km_ref[...], q2, nt_dims, preferred_element_type=F32)
            sm = add_bias(sm, mtab_ref[0, jnp.minimum(qi, 1)])
            consume(sm, jnp.max(sm, axis=0, keepdims=True), vmt_ref[...], pl.ds(0, 2 * tq), first=True)

            bufs = ((sa_ref, mca_ref), (sb_ref, mcb_ref))
            for c in range(n_col):
                produce(0, c, *bufs[0])
            for j in range(nkc):
                s_ref, mc_ref = bufs[j % 2]
                for c in range(n_col):
                    cols = pl.ds(c * ATT_COLS, ATT_COLS)
                    if j + 1 < nkc:
                        produce(j + 1, c, *bufs[(j + 1) % 2])
                    consume(s_ref[:, cols], mc_ref[:, cols], vt_ref[j], cols)
            accp_scr[...] = acc_scr[...]

    @pl.when(g == n_steps)
    def _():
        finalize_previous()


def _attn_call(lamv, q, k, vt, km, vmt, tab, mtab, subw_col, qn, kn, knm, batch, seq, tq, tk):
    n = q.shape[0]
    assert n == batch * seq and seq % (2 * tk) == 0 and seq % tq == 0 and vt.shape[2] == tk
    nq = seq // tq
    nkc = seq // tk
    nt = tab.shape[1]
    n_steps = N_ATT_HEADS * batch * nq

    def tile(g):
        g = jnp.minimum(g, n_steps - 1)
        return g // (batch * nq), (g // nq) % batch, g % nq

    def cur(f):
        return lambda g: f(*tile(g))

    def prev(f):
        return lambda g: f(*tile(jnp.maximum(g - 1, 0)))

    return pl.pallas_call(
        functools.partial(_attn_kernel, tq=tq, tk=tk, nkc=nkc, nq=nq, n_steps=n_steps),
        grid=(n_steps + 1,),
        in_specs=[
            pl.BlockSpec(lamv.shape, lambda g: (0, 0)),
            pl.BlockSpec((seq, LANES), cur(lambda h, b, i: (b, h))),
            pl.BlockSpec((seq, LANES), cur(lambda h, b, i: (b, h))),
            pl.BlockSpec((nkc, VT_ROWS, tk), cur(lambda h, b, i: (b, h, 0))),
            pl.BlockSpec((N_META, LANES), cur(lambda h, b, i: (0, h))),
            pl.BlockSpec((VT_ROWS, N_META), cur(lambda h, b, i: (h, 0))),
            pl.BlockSpec((1, nt, tk, tq), cur(lambda h, b, i: (h, 0, 0, 0))),
            pl.BlockSpec((1, 2, N_META, tq), cur(lambda h, b, i: (h, 0, 0, 0))),
            pl.BlockSpec(subw_col.shape, lambda g: (0, 0)),
            pl.BlockSpec((seq // tk, 8, LANES), cur(lambda h, b, i: (b, 0, 0))),
            pl.BlockSpec((seq // tk, 8, LANES), cur(lambda h, b, i: (b, 0, 0))),
            pl.BlockSpec(knm.shape, lambda g: (0, 0, 0)),
        ],
        out_specs=pl.BlockSpec((tq, LANES), prev(lambda h, b, i: (b * nq + i, h))),
        out_shape=jax.ShapeDtypeStruct((n, ATT_V), BF16),
        scratch_shapes=[pltpu.VMEM((tk, 2 * tq), F32)] * 2 + [pltpu.VMEM((1, 2 * tq), F32)] * 3
        + [pltpu.VMEM((VT_ROWS, 2 * tq), F32)] * 2 + [pltpu.VMEM((1, 2 * tq), F32), pltpu.SMEM((1,), jnp.int32)],
        compiler_params=pltpu.CompilerParams(dimension_semantics=("arbitrary",), vmem_limit_bytes=VMEM_LIMIT),
        name="diff_attn",
    )(lamv, q, k, vt, km, vmt, tab, mtab, subw_col, qn, kn, knm)


def _split3(x):
    hi = x.astype(BF16)
    r1 = x - hi.astype(F32)
    mid = r1.astype(BF16)
    lo = (r1 - mid.astype(F32)).astype(BF16)
    return hi, mid, lo


def _cumsum_rows(a):
    rows = a.shape[0]
    r_i = lax.broadcasted_iota(jnp.int32, (rows, rows), 0)
    c_i = lax.broadcasted_iota(jnp.int32, (rows, rows), 1)
    tri = jnp.where(c_i <= r_i, 1.0, 0.0).astype(BF16)
    out = None
    for term in _split3(a):
        part = jnp.dot(tri, term, preferred_element_type=F32)
        out = part if out is None else out + part
    return out


def _expand_rows(parts, sel_ref):
    masked = []
    for w, first in parts:
        lane = lax.broadcasted_iota(jnp.int32, w.shape, 1)
        masked.append(jnp.where((lane >= first) & (lane < first + SSM_HEADS), w, 0.0))
    stacked = jnp.concatenate(masked, axis=0)
    hi = stacked.astype(BF16)
    lo = (stacked - hi.astype(F32)).astype(BF16)
    sel = sel_ref[...]
    full = jnp.dot(hi, sel, preferred_element_type=F32) + jnp.dot(lo, sel, preferred_element_type=F32)
    outs, r0 = [], 0
    for w, _ in parts:
        outs.append(full[r0:r0 + w.shape[0]])
        r0 += w.shape[0]
    return outs


def _softplus(x):
    return jnp.maximum(x, 0.0) + jnp.log(1.0 + jnp.exp(-jnp.abs(x)))


GROUP_COLS = SSM_INNER // SSM_GROUPS


def _state_update(b_t, xw):
    return jnp.concatenate(
        [jnp.dot(b_t[g * SSM_STATE:(g + 1) * SSM_STATE], xw[:, g * GROUP_COLS:(g + 1) * GROUP_COLS],
                 preferred_element_type=F32) for g in range(SSM_GROUPS)], axis=0)


def _stack_decay(dec_row):
    return jnp.concatenate(
        [jnp.broadcast_to(dec_row[:, g * GROUP_COLS:(g + 1) * GROUP_COLS], (SSM_STATE, GROUP_COLS))
         for g in range(SSM_GROUPS)], axis=0)


def _conv_silu(win, shift_ref, cw_ref, cb_ref, rows):
    total = rows + 2 * HALO
    assert win.shape[0] == total and shift_ref.shape == (len(MXU_TAPS) * rows, total)
    shifted = jnp.dot(shift_ref[...], win, preferred_element_type=F32)
    win32 = win.astype(F32)
    acc = jnp.broadcast_to(cb_ref[...], (rows, SSM_CONV_DIM))
    for j in range(SSM_CONV):
        off = j - SSM_CONV // 2
        if j in MXU_TAPS:
            tap = shifted[MXU_TAPS.index(j) * rows:(MXU_TAPS.index(j) + 1) * rows]
        elif off == 0:
            tap = win32[HALO:HALO + rows]
        else:
            tap = pltpu.roll(win32, (total - off) % total, axis=0)[HALO:HALO + rows]
        acc = acc + cw_ref[j:j + 1, :] * tap
    return acc * jax.nn.sigmoid(acc)


def _shift_matrix(rows):
    offs = jnp.array([j - SSM_CONV // 2 for j in MXU_TAPS])
    t = jnp.arange(rows)
    src = HALO + t[None, :] + offs[:, None]
    return (src.reshape(-1)[:, None] == jnp.arange(rows + 2 * HALO)[None, :]).astype(BF16)


def _ssd_kernel(z_ref, xc_ref, xl_ref, xr_ref, dt_ref, mx_ref, mdt_ref, cw_ref, cb_ref, dtb_ref, alog_ref,
                dsk_ref, nw_ref, sel_ref, shc_ref, shm_ref, o_ref, xs_scr, dts_scr, cum_scr, hbs_scr, hf_scr, hb_scr,
                win_scr,
                *, cs, sub, nb):
    rows = cs * sub
    ph = pl.program_id(1)
    t = pl.program_id(2)
    fwd0, bwd0 = 0, SSM_HEADS
    a_row = -jnp.exp(alog_ref[...])

    def decay_terms(dt_raw):
        dt = _softplus(dt_raw + dtb_ref[...])
        return dt, _cumsum_rows(dt * a_row)

    def bcast8(row):
        return jnp.broadcast_to(row, (8, LANES))

    @pl.when(ph == 0)
    def _():
        blk = nb - 1 - t

        @pl.when(t == 0)
        def _():
            hb_scr[...] = jnp.zeros(hb_scr.shape, F32)

        left = jnp.where(blk == 0, mx_ref[...], xl_ref[...])
        right = jnp.where(blk == nb - 1, jnp.zeros_like(xr_ref[...]), xr_ref[...])
        win_scr[0:HALO, :] = left
        win_scr[HALO:HALO + rows, :] = xc_ref[...]
        win_scr[HALO + rows:HALO + rows + HALO, :] = right

        hb = hb_scr[...]
        for si in reversed(range(sub)):
            cc = blk * sub + si
            xbc = _conv_silu(win_scr[si * cs:si * cs + cs + 2 * HALO, :], shc_ref, cw_ref, cb_ref, cs)
            xs_scr[cc] = xbc.astype(BF16)
            dt, cum = decay_terms(dt_ref[si * cs:(si + 1) * cs, :])
            dts_scr[cc] = dt
            cum_scr[cc] = cum
            eb = cum - dt * a_row
            w_b, dec = _expand_rows([(jnp.exp(eb) * dt, bwd0), (bcast8(jnp.exp(cum[cs - 1:cs, :])), bwd0)],
                                    sel_ref)
            xw = (xbc[:, :SSM_INNER] * w_b).astype(BF16)
            bm_t = xbc[:, SSM_INNER:SSM_INNER + LANES].T.astype(BF16)
            hbs_scr[cc] = hb.astype(BF16)
            hb = hb * _stack_decay(dec[0:1]) + _state_update(bm_t, xw)
        hb_scr[...] = hb

    @pl.when(ph == 1)
    def _():
        @pl.when(t == 0)
        def _():
            wm = jnp.concatenate([jnp.zeros((HALO, SSM_CONV_DIM), BF16), mx_ref[...], xc_ref[0:HALO, :]], axis=0)
            xm = _conv_silu(wm, shm_ref, cw_ref, cb_ref, N_META)
            dtm, cumm = decay_terms(mdt_ref[...])
            (w_m,) = _expand_rows([(jnp.exp(cumm[N_META - 1:N_META, :] - cumm) * dtm, fwd0)], sel_ref)
            xwm = (xm[:, :SSM_INNER] * w_m).astype(BF16)
            bmm_t = xm[:, SSM_INNER:SSM_INNER + LANES].T.astype(BF16)
            hf_scr[...] = _state_update(bmm_t, xwm)

        lane = lax.broadcasted_iota(jnp.int32, (cs, LANES), 1)
        l_i = lax.broadcasted_iota(jnp.int32, (cs, cs), 0)
        s_i = lax.broadcasted_iota(jnp.int32, (cs, cs), 1)
        lower = s_i <= l_i
        diag = s_i == l_i
        hpg = SSM_HEADS // SSM_GROUPS
        zx = jnp.zeros((cs, LANES), BF16)
        nt_dims = (((1,), (1,)), ((), ()))

        hf = hf_scr[...]
        for si in range(sub):
            cc = t * sub + si
            xbc = xs_scr[cc]
            x_bf = xbc[:, :SSM_INNER]
            bm = xbc[:, SSM_INNER:SSM_INNER + LANES]
            cm = xbc[:, SSM_INNER + LANES:SSM_INNER + 2 * LANES]
            x = x_bf.astype(F32)

            dt = dts_scr[cc]
            cum = cum_scr[cc]
            eb = cum - dt * a_row
            dt_t, cum_t, eb_t = dt.T, cum.T, eb.T
            last = cum[cs - 1:cs, :]

            c_grp = [jnp.where(lane // SSM_STATE == g, cm, jnp.zeros_like(cm)) for g in range(SSM_GROUPS)]
            g_mats = [lax.dot_general(c_g, bm, nt_dims, preferred_element_type=F32) for c_g in c_grp]

            pieces = []
            for hp in range(SSM_HEADS // 2):
                w_pair = []
                for h in (2 * hp, 2 * hp + 1):
                    arg_f = cum[:, fwd0 + h:fwd0 + h + 1] - cum_t[fwd0 + h:fwd0 + h + 1, :]
                    arg_b = eb_t[bwd0 + h:bwd0 + h + 1, :] - eb[:, bwd0 + h:bwd0 + h + 1]
                    e = jnp.exp(jnp.minimum(jnp.where(lower, arg_f, arg_b), 0.0))
                    dt_f_row = dt_t[fwd0 + h:fwd0 + h + 1, :]
                    dt_b_row = dt_t[bwd0 + h:bwd0 + h + 1, :]
                    m = e * jnp.where(lower, dt_f_row, dt_b_row) + jnp.where(diag, dt_b_row, 0.0)
                    w_pair.append((g_mats[h // hpg] * m).astype(BF16))
                xp = x_bf[:, hp * LANES:(hp + 1) * LANES]
                rhs = jnp.concatenate([jnp.where(lane < SSM_HEADDIM, xp, zx),
                                       jnp.where(lane >= SSM_HEADDIM, xp, zx)], axis=0)
                pieces.append(jnp.dot(jnp.concatenate(w_pair, axis=1), rhs, preferred_element_type=F32))
            y = jnp.concatenate(pieces, axis=1)

            d_f, d_b, w_f, dec = _expand_rows(
                [(jnp.exp(cum), fwd0), (jnp.exp(last - eb), bwd0), (jnp.exp(last - cum) * dt, fwd0),
                 (bcast8(jnp.exp(last)), fwd0)], sel_ref)
            hf_bf = hf.astype(BF16)
            hb_bf = hbs_scr[cc]
            y = y + d_f * jnp.concatenate([jnp.dot(c_g, hf_bf, preferred_element_type=F32) for c_g in c_grp],
                                          axis=1)
            y = y + d_b * jnp.concatenate([jnp.dot(c_g, hb_bf, preferred_element_type=F32) for c_g in c_grp],
                                          axis=1)
            y = y + x * dsk_ref[...]

            xw = (x * w_f).astype(BF16)
            hf = hf * _stack_decay(dec[0:1]) + _state_update(bm.astype(F32).T.astype(BF16), xw)

            zf = z_ref[si * cs:(si + 1) * cs, :].astype(F32)
            y = y * (zf * jax.nn.sigmoid(zf))
            o_ref[si * cs:(si + 1) * cs, :] = _rmsnorm(y, nw_ref[...]).astype(o_ref.dtype)
        hf_scr[...] = hf


def _ssd_call(z, xbc, dt, mxbc, mdt, cw, cb, dtb, alog, dskip, nw, sel, shc, shm, batch, seq, cs, sub):
    n = z.shape[0]
    rows = cs * sub
    assert n == batch * seq and seq % rows == 0 and cs % HALO == 0
    nc = seq // cs
    nb = seq // rows
    hpb = rows // HALO
    n_halo = n // HALO

    def ph0_block(ph, t):
        return (1 - ph) * (nb - 1 - t)

    const2 = lambda shape: pl.BlockSpec(shape, lambda b, ph, t: (0, 0))
    return pl.pallas_call(
        functools.partial(_ssd_kernel, cs=cs, sub=sub, nb=nb),
        grid=(batch, 2, nb),
        in_specs=[
            pl.BlockSpec((rows, SSM_INNER), lambda b, ph, t: (b * nb + ph * t, 0)),
            pl.BlockSpec((rows, SSM_CONV_DIM), lambda b, ph, t: (b * nb + ph0_block(ph, t), 0)),
            pl.BlockSpec((HALO, SSM_CONV_DIM),
                         lambda b, ph, t: (jnp.maximum((b * nb + ph0_block(ph, t)) * hpb - 1, 0), 0)),
            pl.BlockSpec((HALO, SSM_CONV_DIM),
                         lambda b, ph, t: (jnp.minimum((b * nb + ph0_block(ph, t) + 1) * hpb, n_halo - 1), 0)),
            pl.BlockSpec((rows, DT_PAD), lambda b, ph, t: (b * nb + ph0_block(ph, t), 0)),
            const2(mxbc.shape), const2(mdt.shape), const2(cw.shape), const2(cb.shape), const2(dtb.shape),
            const2(alog.shape), const2(dskip.shape), const2(nw.shape), const2(sel.shape), const2(shc.shape),
            const2(shm.shape),
        ],
        out_specs=pl.BlockSpec((rows, SSM_INNER), lambda b, ph, t: (b * nb + ph * t, 0)),
        out_shape=jax.ShapeDtypeStruct((n, SSM_INNER), BF16),
        scratch_shapes=[
            pltpu.VMEM((nc, cs, SSM_CONV_DIM), BF16),
            pltpu.VMEM((nc, cs, DT_PAD), F32),
            pltpu.VMEM((nc, cs, DT_PAD), F32),
            pltpu.VMEM((nc, LANES, GROUP_COLS), BF16),
            pltpu.VMEM((LANES, GROUP_COLS), F32),
            pltpu.VMEM((LANES, GROUP_COLS), F32),
            pltpu.VMEM((rows + 2 * HALO, SSM_CONV_DIM), BF16),
        ],
        compiler_params=pltpu.CompilerParams(dimension_semantics=("arbitrary",) * 3, vmem_limit_bytes=VMEM_LIMIT),
        name="bi_ssd",
    )(z, xbc, xbc, xbc, dt, mxbc, mdt, cw, cb, dtb, alog, dskip, nw, sel, shc, shm)


def _head_selector():
    k = jnp.arange(LANES)[:, None]
    col = jnp.arange(SSM_INNER)[None, :]
    return ((k % SSM_HEADS == col // SSM_HEADDIM) & (k < 2 * SSM_HEADS)).astype(BF16)


def _prep_weights(ffn1_norm_w, ffn1_w_gate, ffn1_w_up, ffn1_w_down, mix_norm_w, w_in, lambda_q1, lambda_k1,
                  lambda_q2, lambda_k2, attn_subln_w, conv_w, conv_b, dt_bias_fwd, dt_bias_bwd, a_log_fwd,
                  a_log_bwd, ssm_d, ssm_norm_w, w_out, ffn2_norm_w, ffn2_w_gate, ffn2_w_up, ffn2_w_down,
                  final_norm_w):
    def ffn(norm_w, wg, wu, wd):
        return norm_w[0][None, :], wg[0].astype(BF16), wu[0].astype(BF16), wd[0].astype(BF16)

    pad_lanes = lambda v, width: jnp.pad(v, (0, width - v.shape[0]))[None, :]
    o_v, o_z = 2 * ATT_QK, 2 * ATT_QK + ATT_V
    wi = w_in[0].astype(BF16)
    win = jnp.pad(jnp.concatenate([wi[:, :o_v], wi[:, o_z:]], axis=1),
                  ((0, 0), (0, D_IN_PAD - (w_in.shape[2] - ATT_V))))
    return dict(
        ffn1=ffn(ffn1_norm_w, ffn1_w_gate, ffn1_w_up, ffn1_w_down),
        ffn2=ffn(ffn2_norm_w, ffn2_w_gate, ffn2_w_up, ffn2_w_down),
        mix_norm=mix_norm_w[0][None, :],
        win=win,
        wvt=wi[:, o_v:o_z].T,
        lamv=jnp.stack([lambda_q1[0], lambda_k1[0], lambda_q2[0], lambda_k2[0]]),
        subw_col=attn_subln_w[0][:, None],
        cw=jnp.pad(conv_w[0], ((0, 8 - SSM_CONV), (0, 0))),
        cb=conv_b[0][None, :],
        dtb=pad_lanes(jnp.concatenate([dt_bias_fwd[0], dt_bias_bwd[0]]), DT_PAD),
        alog=pad_lanes(jnp.concatenate([a_log_fwd[0], a_log_bwd[0]]), DT_PAD),
        dskip=jnp.repeat(ssm_d[0], SSM_HEADDIM)[None, :],
        ssm_norm=ssm_norm_w[0][None, :],
        wo=w_out[0].astype(BF16).reshape(2, ATT_V, D_MODEL),
        final=final_norm_w[None, :],
        sel=_head_selector(),
        nsel=(jnp.arange(ATT_QK)[:, None] // ATT_DH == jnp.arange(LANES)[None, :]).astype(BF16),
        shc=_shift_matrix(SSD_CHUNK),
        shm=_shift_matrix(N_META),
    )


def _encode(x, w, meta_proj, tab, mtab):
    batch, seq, _ = x.shape
    km, vmt, mxbc, mdt, knm = meta_proj
    h0 = x.reshape(batch * seq, D_MODEL)
    h1 = _ffn_call(h0, *w["ffn1"])
    q, k, vt, z, xbc, dt, qn, kn = _inproj_call(h1, w["mix_norm"], w["win"], w["wvt"], w["nsel"])
    att = _attn_call(w["lamv"], q, k, vt, km, vmt, tab, mtab, w["subw_col"], qn, kn, knm, batch, seq, ATT_TQ, ATT_TK)
    ssm = _ssd_call(z, xbc, dt, mxbc, mdt, w["cw"], w["cb"], w["dtb"], w["alog"], w["dskip"], w["ssm_norm"],
                    w["sel"], w["shc"], w["shm"], batch, seq, SSD_CHUNK, SSD_SUB)
    y = _ffn_call(h1, *w["ffn2"], mix=(att, ssm, w["wo"]), final_w=w["final"])
    return y.reshape(batch, seq, D_MODEL)


def kernel(x_prompt, x_sample, meta_tokens, ffn1_norm_w, ffn1_w_gate, ffn1_w_up, ffn1_w_down, mix_norm_w, w_in, rel_bias, lambda_q1, lambda_k1, lambda_q2, lambda_k2, attn_subln_w, conv_w, conv_b, dt_bias_fwd, dt_bias_bwd, a_log_fwd, a_log_bwd, ssm_d, ssm_norm_w, w_out, ffn2_norm_w, ffn2_w_gate, ffn2_w_up, ffn2_w_down, final_norm_w):
    w = _prep_weights(ffn1_norm_w, ffn1_w_gate, ffn1_w_up, ffn1_w_down, mix_norm_w, w_in, lambda_q1, lambda_k1,
                      lambda_q2, lambda_k2, attn_subln_w, conv_w, conv_b, dt_bias_fwd, dt_bias_bwd, a_log_fwd,
                      a_log_bwd, ssm_d, ssm_norm_w, w_out, ffn2_norm_w, ffn2_w_gate, ffn2_w_up, ffn2_w_down,
                      final_norm_w)
    hm = _ffn_call(meta_tokens, *w["ffn1"])
    _, km, vmt, _, mxbc, mdt, _, knm = _inproj_call(hm, w["mix_norm"], w["win"], w["wvt"], w["nsel"])
    meta_proj = (km, vmt[0], mxbc, mdt, knm)
    tab, mtab = _bias_call(rel_bias, ATT_TQ, ATT_TK)
    return (_encode(x_prompt, w, meta_proj, tab, mtab), _encode(x_sample, w, meta_proj, tab, mtab))
```

```python
import functools
import math

import jax
import jax.numpy as jnp
from jax import lax
from jax.experimental import pallas as pl
from jax.experimental.pallas import tpu as pltpu

F32 = jnp.float32
BF16 = jnp.bfloat16

D_MODEL = 1024
N_META = 16
N_ATT_HEADS = 8
ATT_DH = 64
ATT_DV = 128
ATT_QK = 1024
ATT_V = 1024
NUM_BUCKETS = 32
MAX_DISTANCE = 128
SSM_HEADS = 16
SSM_HEADDIM = 64
SSM_INNER = 1024
SSM_GROUPS = 2
SSM_STATE = 64
SSM_CONV = 7
SSM_CONV_DIM = 1280
D_FF = 2816
EPS = 1e-6
LAYER = 0
LAM_INIT = 0.8 - 0.6 * math.exp(-0.3 * LAYER)
LOG2E = math.log2(math.e)
Q_SCALE = ATT_DH ** -0.5 * LOG2E
NEG_BIG = -1e30
NORM_SLACK = 1.02
MAX_SHIFT_GAP = 100.0

LANES = 128
BF16_ROWS = 16
VMEM_LIMIT = 56 * 1024 * 1024

FF_TILE = 256
N_FF = D_FF // FF_TILE
DT_PAD = LANES
D_IN_PAD = 2 * ATT_QK + SSM_INNER + SSM_CONV_DIM + DT_PAD
T5_BAND = 91

ROW_TILE = 512
ATT_TQ = 1024
ATT_TK = 512
TAB_TQ = 512
ATT_COLS = 256
SSD_CHUNK = 128
MXU_TAPS = (0, 1, 5)
SSD_SUB = 4
HALO = BF16_ROWS
VT_ROWS = ATT_DV + BF16_ROWS


def _rmsnorm(x, w):
    ms = jnp.mean(x * x, axis=-1, keepdims=True)
    return x * lax.rsqrt(ms + EPS) * w


def _resident(shape):
    nd = len(shape)
    return pl.BlockSpec(shape, lambda *_: (0,) * nd, pipeline_mode=pl.Buffered(1))


def _ffn_kernel(*refs, has_mix, has_final):
    it = iter(refs)
    h_ref = next(it)
    if has_mix:
        att_ref, ssm_ref, wo_ref = next(it), next(it), next(it)
    nw_ref, wg_ref, wu_ref, wd_ref = next(it), next(it), next(it), next(it)
    fw_ref = next(it) if has_final else None
    o_ref = next(it)

    h = h_ref[...]
    if has_mix:
        h = (h + jnp.dot(att_ref[...], wo_ref[0], preferred_element_type=F32)
             + jnp.dot(ssm_ref[...], wo_ref[1], preferred_element_type=F32))
    u = _rmsnorm(h, nw_ref[...]).astype(BF16)
    acc = jnp.zeros_like(h)
    for j in range(N_FF):
        ff = slice(j * FF_TILE, (j + 1) * FF_TILE)
        g = jnp.dot(u, wg_ref[:, ff], preferred_element_type=F32)
        up = jnp.dot(u, wu_ref[:, ff], preferred_element_type=F32)
        a = (g * jax.nn.sigmoid(g) * up).astype(BF16)
        acc = acc + jnp.dot(a, wd_ref[ff, :], preferred_element_type=F32)
    h = h + 0.5 * acc
    if has_final:
        h = _rmsnorm(h, fw_ref[...])
    o_ref[...] = h


def _ffn_call(h, norm_w, wg, wu, wd, mix=None, final_w=None):
    n = h.shape[0]
    tm = min(ROW_TILE, n)
    assert n % tm == 0
    row = lambda width: pl.BlockSpec((tm, width), lambda i: (i, 0))
    args, specs = [h], [row(D_MODEL)]
    if mix is not None:
        att, ssm, wo = mix
        args += [att, ssm, wo]
        specs += [row(ATT_V), row(SSM_INNER), _resident(wo.shape)]
    args += [norm_w, wg, wu, wd]
    specs += [_resident(norm_w.shape), _resident(wg.shape), _resident(wu.shape), _resident(wd.shape)]
    if final_w is not None:
        args.append(final_w)
        specs.append(_resident(final_w.shape))
    return pl.pallas_call(
        functools.partial(_ffn_kernel, has_mix=mix is not None, has_final=final_w is not None),
        grid=(n // tm,),
        in_specs=specs,
        out_specs=row(D_MODEL),
        out_shape=jax.ShapeDtypeStruct((n, D_MODEL), F32),
        compiler_params=pltpu.CompilerParams(dimension_semantics=("arbitrary",), vmem_limit_bytes=VMEM_LIMIT),
        name="ffn_mix" if mix is not None else "ffn",
    )(*args)


_IN_SEGS = (("q", 0, ATT_QK), ("k", ATT_QK, ATT_QK), ("z", 2 * ATT_QK, SSM_INNER),
            ("xbc", 2 * ATT_QK + SSM_INNER, SSM_CONV_DIM), ("dt", D_IN_PAD - DT_PAD, DT_PAD))


def _inproj_kernel(h_ref, nw_ref, win_ref, wvt_ref, q_ref, k_ref, vt_ref, z_ref, xbc_ref, dt_ref):
    u = _rmsnorm(h_ref[...], nw_ref[...]).astype(BF16)
    outs = dict(q=q_ref, k=k_ref, z=z_ref, xbc=xbc_ref, dt=dt_ref)
    for name, c0, width in _IN_SEGS:
        o_ref = outs[name]
        step = 512 if width % 512 == 0 else (256 if width % 256 == 0 else LANES)
        for s in range(0, width, step):
            r = jnp.dot(u, win_ref[:, c0 + s:c0 + s + step], preferred_element_type=F32)
            if name == "q":
                r = r * Q_SCALE
            o_ref[:, s:s + step] = r.astype(o_ref.dtype)
    nt_dims = (((1,), (1,)), ((), ()))
    ones = jnp.ones((VT_ROWS - ATT_DV, u.shape[0]), vt_ref.dtype)
    for s in range(0, ATT_V, 256):
        r = lax.dot_general(wvt_ref[s:s + 256, :], u, nt_dims, preferred_element_type=F32).astype(vt_ref.dtype)
        for hh in range(256 // ATT_DV):
            head = s // ATT_DV + hh
            vt_ref[0, head * VT_ROWS:head * VT_ROWS + ATT_DV, :] = r[hh * ATT_DV:(hh + 1) * ATT_DV]
            vt_ref[0, head * VT_ROWS + ATT_DV:(head + 1) * VT_ROWS, :] = ones


def _inproj_call(h, norm_w, win, wvt):
    n = h.shape[0]
    tm = min(ATT_TK, n)
    assert n % tm == 0
    row = lambda width: pl.BlockSpec((tm, width), lambda i: (i, 0))
    widths = (ATT_QK, ATT_QK, SSM_INNER, SSM_CONV_DIM, DT_PAD)
    dtypes = (BF16, BF16, BF16, BF16, F32)
    shapes = [jax.ShapeDtypeStruct((n, w), dt) for w, dt in zip(widths, dtypes)]
    specs = [row(w) for w in widths]
    shapes.insert(2, jax.ShapeDtypeStruct((n // tm, N_ATT_HEADS * VT_ROWS, tm), BF16))
    specs.insert(2, pl.BlockSpec((1, N_ATT_HEADS * VT_ROWS, tm), lambda i: (i, 0, 0)))
    return pl.pallas_call(
        _inproj_kernel,
        grid=(n // tm,),
        in_specs=[row(D_MODEL), _resident(norm_w.shape), _resident(win.shape), _resident(wvt.shape)],
        out_specs=specs,
        out_shape=shapes,
        compiler_params=pltpu.CompilerParams(dimension_semantics=("arbitrary",), vmem_limit_bytes=VMEM_LIMIT),
        name="inproj",
    )(h, norm_w, win, wvt)


def _t5_bias(rel, rb_ref, head):
    half = NUM_BUCKETS // 2
    max_exact = half // 2
    ret = jnp.where(rel > 0, half, 0)
    n = jnp.abs(rel)
    nf = jnp.maximum(n, 1).astype(F32)
    large = max_exact + (jnp.log(nf / max_exact) / math.log(MAX_DISTANCE / max_exact)
                         * (half - max_exact)).astype(jnp.int32)
    large = jnp.minimum(large, half - 1)
    bucket = ret + jnp.where(n < max_exact, n, large)
    val = jnp.zeros(rel.shape, F32)
    for jb in range(NUM_BUCKETS):
        val = jnp.where(bucket == jb, rb_ref[jb, head], val)
    return val * LOG2E


def _bias_geometry(tq, tk):
    unit = min(tq, tk)
    assert tq % unit == 0 and tk % unit == 0 and unit >= T5_BAND + 1
    return unit, tk // unit, tq // unit


def _bias_kernel(rb_ref, tab_ref, mtab_ref, rng_ref, *, tq, tk):
    head = pl.program_id(0)
    unit, lo, hi = _bias_geometry(TAB_TQ, tk)
    n_near = lo + hi + 1
    top = rb_ref[0, head]
    bottom = rb_ref[0, head]
    for jb in range(1, NUM_BUCKETS):
        top = jnp.maximum(top, rb_ref[jb, head])
        bottom = jnp.minimum(bottom, rb_ref[jb, head])
    rng_ref[0, 0:1, :] = jnp.full((1, LANES), top * LOG2E, F32)
    rng_ref[0, 1:2, :] = jnp.full((1, LANES), bottom * LOG2E, F32)
    rng_ref[0, 2:8, :] = jnp.zeros((6, LANES), F32)
    far_left = rb_ref[NUM_BUCKETS // 2 - 1, head] * LOG2E
    far_right = rb_ref[NUM_BUCKETS - 1, head] * LOG2E
    krow = lax.broadcasted_iota(jnp.int32, (LANES, LANES), 0)
    qcol = lax.broadcasted_iota(jnp.int32, (LANES, LANES), 1)
    for t in range(n_near):
        for a in range(tk // LANES):
            for b in range(TAB_TQ // LANES):
                base = (a - b) * LANES + (t - lo) * unit
                blk = (slice(a * LANES, (a + 1) * LANES), slice(b * LANES, (b + 1) * LANES))
                if base + LANES - 1 <= -T5_BAND:
                    tab_ref[(0, t) + blk] = jnp.full((LANES, LANES), far_left, F32)
                elif base - LANES + 1 >= T5_BAND:
                    tab_ref[(0, t) + blk] = jnp.full((LANES, LANES), far_right, F32)
                else:
                    tab_ref[(0, t) + blk] = _t5_bias(krow - qcol + base, rb_ref, head)
    tab_ref[0, n_near] = jnp.full((tk, TAB_TQ), far_left, F32)
    tab_ref[0, n_near + 1] = jnp.full((tk, TAB_TQ), far_right, F32)
    mrow = lax.broadcasted_iota(jnp.int32, (N_META, tq), 0)
    mcol = lax.broadcasted_iota(jnp.int32, (N_META, tq), 1)
    mtab_ref[0, 0] = _t5_bias(mrow - N_META - mcol, rb_ref, head)
    mtab_ref[0, 1] = jnp.full((N_META, tq), far_left, F32)


def _bias_call(rel_bias, tq, tk):
    assert tq % TAB_TQ == 0
    _, lo, hi = _bias_geometry(TAB_TQ, tk)
    nt = lo + hi + 3
    return pl.pallas_call(
        functools.partial(_bias_kernel, tq=tq, tk=tk),
        grid=(N_ATT_HEADS,),
        in_specs=[pl.BlockSpec(memory_space=pltpu.SMEM)],
        out_specs=[pl.BlockSpec((1, nt, tk, TAB_TQ), lambda h: (h, 0, 0, 0)),
                   pl.BlockSpec((1, 2, N_META, tq), lambda h: (h, 0, 0, 0)),
                   pl.BlockSpec((1, 8, LANES), lambda h: (h, 0, 0))],
        out_shape=[jax.ShapeDtypeStruct((N_ATT_HEADS, nt, tk, TAB_TQ), F32),
                   jax.ShapeDtypeStruct((N_ATT_HEADS, 2, N_META, tq), F32),
                   jax.ShapeDtypeStruct((N_ATT_HEADS, 8, LANES), F32)],
        compiler_params=pltpu.CompilerParams(dimension_semantics=("arbitrary",)),
        name="t5_bias",
    )(rel_bias)


def _attn_kernel(lam_ref, qall_ref, k_ref, vt_ref, km_ref, vmt_ref, tab_ref, mtab_ref, rng_ref, sw_ref, o_ref,
                 sa_ref, sb_ref, mca_ref, mcb_ref, m_scr, acc_scr, accp_scr, shift_scr, flag_scr,
                 *, tq, tk, nkc, nq, n_steps):
    g = pl.program_id(0)
    qi = jnp.minimum(g, n_steps - 1) % nq
    unit, lo, hi = _bias_geometry(TAB_TQ, tk)
    n_near = lo + hi + 1
    nt_dims = (((1,), (1,)), ((), ()))
    n_col = 2 * tq // ATT_COLS

    def finalize_previous():
        acc = accp_scr[...]
        o = acc[:ATT_DV] / acc[ATT_DV:ATT_DV + 1]
        lv = lam_ref[...]
        lam = (jnp.exp(jnp.sum(lv[0:1] * lv[1:2], axis=1, keepdims=True))
               - jnp.exp(jnp.sum(lv[2:3] * lv[3:4], axis=1, keepdims=True)) + LAM_INIT)
        out = o[:, :tq] - lam * o[:, tq:]
        ms = jnp.mean(out * out, axis=0, keepdims=True)
        out = out * lax.rsqrt(ms + EPS) * sw_ref[...] * (1.0 - LAM_INIT)
        o_ref[...] = out.T.astype(o_ref.dtype)

    @pl.when(g == 0)
    def _():
        accp_scr[...] = jnp.ones(accp_scr.shape, F32)

    @pl.when(g < n_steps)
    def _():
        q = qall_ref[pl.ds(pl.multiple_of(qi * tq, tq), tq), :]
        lane = lax.broadcasted_iota(jnp.int32, (tq, LANES), 1)
        zero = jnp.zeros_like(q)
        q2 = jnp.concatenate([jnp.where(lane < ATT_DH, q, zero), jnp.where(lane >= ATT_DH, q, zero)], axis=0)

        def bias_tile(j, c):
            q0 = (c * ATT_COLS) % tq
            du = j * (tk // unit) - (qi * (tq // TAB_TQ) + q0 // TAB_TQ) * (TAB_TQ // unit)
            idx = jnp.where(du < -lo, n_near, jnp.where(du > hi, n_near + 1, du + lo))
            return tab_ref[0, idx, :, pl.ds(q0 % TAB_TQ, ATT_COLS)]

        def add_bias(s, b):
            return jnp.concatenate([s[:, :tq] + b, s[:, tq:] + b], axis=1)

        @pl.when(qi == 0)
        def _():
            half = jnp.where(lax.broadcasted_iota(jnp.int32, (LANES, LANES), 0) // ATT_DH
                             == lax.broadcasted_iota(jnp.int32, (LANES, LANES), 1), 1.0, 0.0).astype(BF16)

            def max_sq_norm(x_ref):
                x = x_ref[...]
                sq = jnp.dot(x * x, half, preferred_element_type=F32)
                return jnp.max(sq, axis=0, keepdims=True)

            bound2 = max_sq_norm(qall_ref) * jnp.maximum(max_sq_norm(k_ref), max_sq_norm(km_ref))
            lane_row = lax.broadcasted_iota(jnp.int32, (1, LANES), 1)
            qk_bound = [NORM_SLACK * jnp.sqrt(jnp.max(jnp.where(lane_row == mp, bound2, 0.0), axis=1, keepdims=True))
                        for mp in range(2)]
            hi_b = rng_ref[0, 0:1, 0:1]
            lo_b = rng_ref[0, 1:2, 0:1]
            col = lax.broadcasted_iota(jnp.int32, (1, 2 * tq), 1)
            shift_scr[...] = jnp.where(col < tq, qk_bound[0], qk_bound[1]) + hi_b
            worst_gap = 2.0 * jnp.maximum(qk_bound[0], qk_bound[1]) + (hi_b - lo_b)
            flag_scr[0] = (worst_gap[0, 0] <= MAX_SHIFT_GAP).astype(jnp.int32)

        shift = shift_scr[...]
        bounded = flag_scr[0] == 1

        @pl.when(bounded)
        def _():
            finalize_previous()
            sm = lax.dot_general(km_ref[...], q2, nt_dims, preferred_element_type=F32)
            sm = add_bias(sm, mtab_ref[0, jnp.minimum(qi, 1)])
            acc_scr[...] = jnp.dot(vmt_ref[...], jnp.exp2(sm - shift).astype(BF16), preferred_element_type=F32)

            def stage_logits(u, j, c):
                buf = (sa_ref, sb_ref)[u % 2]
                buf[:, pl.ds(c * ATT_COLS, ATT_COLS)] = lax.dot_general(
                    k_ref[pl.ds(j * tk, tk), :], q2[c * ATT_COLS:(c + 1) * ATT_COLS], nt_dims,
                    preferred_element_type=F32)

            units = [(j, c) for j in range(nkc) for c in range(n_col)]
            stage_logits(0, *units[0])
            for u, (j, c) in enumerate(units):
                if u + 1 < len(units):
                    stage_logits(u + 1, *units[u + 1])
                cols = pl.ds(c * ATT_COLS, ATT_COLS)
                s = (sa_ref, sb_ref)[u % 2][:, cols] + bias_tile(j, c)
                p = jnp.exp2(s - shift[:, c * ATT_COLS:(c + 1) * ATT_COLS]).astype(BF16)
                acc_scr[:, cols] += jnp.dot(vt_ref[j], p, preferred_element_type=F32)
            accp_scr[...] = acc_scr[...]

        @pl.when(jnp.logical_not(bounded))
        def _():
            finalize_previous()

            def produce(j, c, s_ref, mc_ref):
                cols = pl.ds(c * ATT_COLS, ATT_COLS)
                s = lax.dot_general(k_ref[pl.ds(j * tk, tk), :], q2[c * ATT_COLS:(c + 1) * ATT_COLS], nt_dims,
                                    preferred_element_type=F32)
                s = s + bias_tile(j, c)
                s_ref[:, cols] = s
                mc_ref[:, cols] = jnp.max(s, axis=0, keepdims=True)

            def consume(s, m_cur, vt, cols, first=False):
                if first:
                    m_new = m_cur
                else:
                    m_prev = m_scr[:, cols]
                    m_new = jnp.maximum(m_prev, m_cur)
                    alpha = jnp.exp2(m_prev - m_new)
                p = jnp.exp2(s - m_new).astype(BF16)
                pv = jnp.dot(vt, p, preferred_element_type=F32)
                acc_scr[:, cols] = pv if first else alpha * acc_scr[:, cols] + pv
                m_scr[:, cols] = m_new

            sm = lax.dot_general(km_ref[...], q2, nt_dims, preferred_element_type=F32)
            sm = add_bias(sm, mtab_ref[0, jnp.minimum(qi, 1)])
            consume(sm, jnp.max(sm, axis=0, keepdims=True), vmt_ref[...], pl.ds(0, 2 * tq), first=True)

            bufs = ((sa_ref, mca_ref), (sb_ref, mcb_ref))
            for c in range(n_col):
                produce(0, c, *bufs[0])
            for j in range(nkc):
                s_ref, mc_ref = bufs[j % 2]
                for c in range(n_col):
                    cols = pl.ds(c * ATT_COLS, ATT_COLS)
                    if j + 1 < nkc:
                        produce(j + 1, c, *bufs[(j + 1) % 2])
                    consume(s_ref[:, cols], mc_ref[:, cols], vt_ref[j], cols)
            accp_scr[...] = acc_scr[...]

    @pl.when(g == n_steps)
    def _():
        finalize_previous()


def _attn_call(lamv, q, k, vt, km, vmt, tab, mtab, rng, subw_col, batch, seq, tq, tk):
    n = q.shape[0]
    assert n == batch * seq and seq % (2 * tk) == 0 and seq % tq == 0 and vt.shape[2] == tk
    nq = seq // tq
    nkc = seq // tk
    nt = tab.shape[1]
    n_steps = N_ATT_HEADS * batch * nq

    def tile(g):
        g = jnp.minimum(g, n_steps - 1)
        return g // (batch * nq), (g // nq) % batch, g % nq

    def cur(f):
        return lambda g: f(*tile(g))

    def prev(f):
        return lambda g: f(*tile(jnp.maximum(g - 1, 0)))

    return pl.pallas_call(
        functools.partial(_attn_kernel, tq=tq, tk=tk, nkc=nkc, nq=nq, n_steps=n_steps),
        grid=(n_steps + 1,),
        in_specs=[
            pl.BlockSpec(lamv.shape, lambda g: (0, 0)),
            pl.BlockSpec((seq, LANES), cur(lambda h, b, i: (b, h))),
            pl.BlockSpec((seq, LANES), cur(lambda h, b, i: (b, h))),
            pl.BlockSpec((nkc, VT_ROWS, tk), cur(lambda h, b, i: (b, h, 0))),
            pl.BlockSpec((N_META, LANES), cur(lambda h, b, i: (0, h))),
            pl.BlockSpec((VT_ROWS, N_META), cur(lambda h, b, i: (h, 0))),
            pl.BlockSpec((1, nt, tk, TAB_TQ), cur(lambda h, b, i: (h, 0, 0, 0))),
            pl.BlockSpec((1, 2, N_META, tq), cur(lambda h, b, i: (h, 0, 0, 0))),
            pl.BlockSpec((1, 8, LANES), cur(lambda h, b, i: (h, 0, 0))),
            pl.BlockSpec(subw_col.shape, lambda g: (0, 0)),
        ],
        out_specs=pl.BlockSpec((tq, LANES), prev(lambda h, b, i: (b * nq + i, h))),
        out_shape=jax.ShapeDtypeStruct((n, ATT_V), BF16),
        scratch_shapes=[pltpu.VMEM((tk, 2 * tq), F32)] * 2 + [pltpu.VMEM((1, 2 * tq), F32)] * 3
        + [pltpu.VMEM((VT_ROWS, 2 * tq), F32)] * 2 + [pltpu.VMEM((1, 2 * tq), F32), pltpu.SMEM((1,), jnp.int32)],
        compiler_params=pltpu.CompilerParams(dimension_semantics=("arbitrary",), vmem_limit_bytes=VMEM_LIMIT),
        name="diff_attn",
    )(lamv, q, k, vt, km, vmt, tab, mtab, rng, subw_col)


def _split3(x):
    hi = x.astype(BF16)
    r1 = x - hi.astype(F32)
    mid = r1.astype(BF16)
    lo = (r1 - mid.astype(F32)).astype(BF16)
    return hi, mid, lo


def _cumsum_rows(a):
    rows = a.shape[0]
    r_i = lax.broadcasted_iota(jnp.int32, (rows, rows), 0)
    c_i = lax.broadcasted_iota(jnp.int32, (rows, rows), 1)
    tri = jnp.where(c_i <= r_i, 1.0, 0.0).astype(BF16)
    out = None
    for term in _split3(a):
        part = jnp.dot(tri, term, preferred_element_type=F32)
        out = part if out is None else out + part
    return out


def _expand_rows(parts, sel_ref):
    masked = []
    for w, first in parts:
        lane = lax.broadcasted_iota(jnp.int32, w.shape, 1)
        masked.append(jnp.where((lane >= first) & (lane < first + SSM_HEADS), w, 0.0))
    stacked = jnp.concatenate(masked, axis=0)
    hi = stacked.astype(BF16)
    lo = (stacked - hi.astype(F32)).astype(BF16)
    sel = sel_ref[...]
    full = jnp.dot(hi, sel, preferred_element_type=F32) + jnp.dot(lo, sel, preferred_element_type=F32)
    outs, r0 = [], 0
    for w, _ in parts:
        outs.append(full[r0:r0 + w.shape[0]])
        r0 += w.shape[0]
    return outs


def _softplus(x):
    return jnp.maximum(x, 0.0) + jnp.log(1.0 + jnp.exp(-jnp.abs(x)))


GROUP_COLS = SSM_INNER // SSM_GROUPS


def _state_update(b_t, xw):
    return jnp.concatenate(
        [jnp.dot(b_t[g * SSM_STATE:(g + 1) * SSM_STATE], xw[:, g * GROUP_COLS:(g + 1) * GROUP_COLS],
                 preferred_element_type=F32) for g in range(SSM_GROUPS)], axis=0)


def _stack_decay(dec_row):
    return jnp.concatenate(
        [jnp.broadcast_to(dec_row[:, g * GROUP_COLS:(g + 1) * GROUP_COLS], (SSM_STATE, GROUP_COLS))
         for g in range(SSM_GROUPS)], axis=0)


def _conv_silu(win, shift_ref, cw_ref, cb_ref, rows):
    total = rows + 2 * HALO
    assert win.shape[0] == total and shift_ref.shape == (len(MXU_TAPS) * rows, total)
    shifted = jnp.dot(shift_ref[...], win, preferred_element_type=F32)
    win32 = win.astype(F32)
    acc = jnp.broadcast_to(cb_ref[...], (rows, SSM_CONV_DIM))
    for j in range(SSM_CONV):
        off = j - SSM_CONV // 2
        if j in MXU_TAPS:
            tap = shifted[MXU_TAPS.index(j) * rows:(MXU_TAPS.index(j) + 1) * rows]
        elif off == 0:
            tap = win32[HALO:HALO + rows]
        else:
            tap = pltpu.roll(win32, (total - off) % total, axis=0)[HALO:HALO + rows]
        acc = acc + cw_ref[j:j + 1, :] * tap
    return acc * jax.nn.sigmoid(acc)


def _shift_matrix(rows):
    offs = jnp.array([j - SSM_CONV // 2 for j in MXU_TAPS])
    t = jnp.arange(rows)
    src = HALO + t[None, :] + offs[:, None]
    return (src.reshape(-1)[:, None] == jnp.arange(rows + 2 * HALO)[None, :]).astype(BF16)


def _ssd_kernel(z_ref, xc_ref, xl_ref, xr_ref, dt_ref, mx_ref, mdt_ref, cw_ref, cb_ref, dtb_ref, alog_ref,
                dsk_ref, nw_ref, sel_ref, shc_ref, shm_ref, o_ref, xs_scr, dts_scr, cum_scr, hbs_scr, hf_scr, hb_scr,
                win_scr,
                *, cs, sub, nb):
    rows = cs * sub
    ph = pl.program_id(1)
    t = pl.program_id(2)
    fwd0, bwd0 = 0, SSM_HEADS
    a_row = -jnp.exp(alog_ref[...])

    def decay_terms(dt_raw):
        dt = _softplus(dt_raw + dtb_ref[...])
        return dt, _cumsum_rows(dt * a_row)

    def bcast8(row):
        return jnp.broadcast_to(row, (8, LANES))

    @pl.when(ph == 0)
    def _():
        blk = nb - 1 - t

        @pl.when(t == 0)
        def _():
            hb_scr[...] = jnp.zeros(hb_scr.shape, F32)

        left = jnp.where(blk == 0, mx_ref[...], xl_ref[...])
        right = jnp.where(blk == nb - 1, jnp.zeros_like(xr_ref[...]), xr_ref[...])
        win_scr[0:HALO, :] = left
        win_scr[HALO:HALO + rows, :] = xc_ref[...]
        win_scr[HALO + rows:HALO + rows + HALO, :] = right

        hb = hb_scr[...]
        for si in reversed(range(sub)):
            cc = blk * sub + si
            xbc = _conv_silu(win_scr[si * cs:si * cs + cs + 2 * HALO, :], shc_ref, cw_ref, cb_ref, cs)
            xs_scr[cc] = xbc.astype(BF16)
            dt, cum = decay_terms(dt_ref[si * cs:(si + 1) * cs, :])
            dts_scr[cc] = dt
            cum_scr[cc] = cum
            eb = cum - dt * a_row
            w_b, dec = _expand_rows([(jnp.exp(eb) * dt, bwd0), (bcast8(jnp.exp(cum[cs - 1:cs, :])), bwd0)],
                                    sel_ref)
            xw = (xbc[:, :SSM_INNER] * w_b).astype(BF16)
            bm_t = xbc[:, SSM_INNER:SSM_INNER + LANES].T.astype(BF16)
            hbs_scr[cc] = hb.astype(BF16)
            hb = hb * _stack_decay(dec[0:1]) + _state_update(bm_t, xw)
        hb_scr[...] = hb

    @pl.when(ph == 1)
    def _():
        @pl.when(t == 0)
        def _():
            wm = jnp.concatenate([jnp.zeros((HALO, SSM_CONV_DIM), BF16), mx_ref[...], xc_ref[0:HALO, :]], axis=0)
            xm = _conv_silu(wm, shm_ref, cw_ref, cb_ref, N_META)
            dtm, cumm = decay_terms(mdt_ref[...])
            (w_m,) = _expand_rows([(jnp.exp(cumm[N_META - 1:N_META, :] - cumm) * dtm, fwd0)], sel_ref)
            xwm = (xm[:, :SSM_INNER] * w_m).astype(BF16)
            bmm_t = xm[:, SSM_INNER:SSM_INNER + LANES].T.astype(BF16)
            hf_scr[...] = _state_update(bmm_t, xwm)

        lane = lax.broadcasted_iota(jnp.int32, (cs, LANES), 1)
        l_i = lax.broadcasted_iota(jnp.int32, (cs, cs), 0)
        s_i = lax.broadcasted_iota(jnp.int32, (cs, cs), 1)
        lower = s_i <= l_i
        diag = s_i == l_i
        hpg = SSM_HEADS // SSM_GROUPS
        zx = jnp.zeros((cs, LANES), BF16)
        nt_dims = (((1,), (1,)), ((), ()))

        hf = hf_scr[...]
        for si in range(sub):
            cc = t * sub + si
            xbc = xs_scr[cc]
            x_bf = xbc[:, :SSM_INNER]
            bm = xbc[:, SSM_INNER:SSM_INNER + LANES]
            cm = xbc[:, SSM_INNER + LANES:SSM_INNER + 2 * LANES]
            x = x_bf.astype(F32)

            dt = dts_scr[cc]
            cum = cum_scr[cc]
            eb = cum - dt * a_row
            dt_t, cum_t, eb_t = dt.T, cum.T, eb.T
            last = cum[cs - 1:cs, :]

            c_grp = [jnp.where(lane // SSM_STATE == g, cm, jnp.zeros_like(cm)) for g in range(SSM_GROUPS)]
            g_mats = [lax.dot_general(c_g, bm, nt_dims, preferred_element_type=F32) for c_g in c_grp]

            pieces = []
            for hp in range(SSM_HEADS // 2):
                w_pair = []
                for h in (2 * hp, 2 * hp + 1):
                    arg_f = cum[:, fwd0 + h:fwd0 + h + 1] - cum_t[fwd0 + h:fwd0 + h + 1, :]
                    arg_b = eb_t[bwd0 + h:bwd0 + h + 1, :] - eb[:, bwd0 + h:bwd0 + h + 1]
                    e = jnp.exp(jnp.minimum(jnp.where(lower, arg_f, arg_b), 0.0))
                    dt_f_row = dt_t[fwd0 + h:fwd0 + h + 1, :]
                    dt_b_row = dt_t[bwd0 + h:bwd0 + h + 1, :]
                    m = e * jnp.where(lower, dt_f_row, dt_b_row) + jnp.where(diag, dt_b_row, 0.0)
                    w_pair.append((g_mats[h // hpg] * m).astype(BF16))
                xp = x_bf[:, hp * LANES:(hp + 1) * LANES]
                rhs = jnp.concatenate([jnp.where(lane < SSM_HEADDIM, xp, zx),
                                       jnp.where(lane >= SSM_HEADDIM, xp, zx)], axis=0)
                pieces.append(jnp.dot(jnp.concatenate(w_pair, axis=1), rhs, preferred_element_type=F32))
            y = jnp.concatenate(pieces, axis=1)

            d_f, d_b, w_f, dec = _expand_rows(
                [(jnp.exp(cum), fwd0), (jnp.exp(last - eb), bwd0), (jnp.exp(last - cum) * dt, fwd0),
                 (bcast8(jnp.exp(last)), fwd0)], sel_ref)
            hf_bf = hf.astype(BF16)
            hb_bf = hbs_scr[cc]
            y = y + d_f * jnp.concatenate([jnp.dot(c_g, hf_bf, preferred_element_type=F32) for c_g in c_grp],
                                          axis=1)
            y = y + d_b * jnp.concatenate([jnp.dot(c_g, hb_bf, preferred_element_type=F32) for c_g in c_grp],
                                          axis=1)
            y = y + x * dsk_ref[...]

            xw = (x * w_f).astype(BF16)
            hf = hf * _stack_decay(dec[0:1]) + _state_update(bm.astype(F32).T.astype(BF16), xw)

            zf = z_ref[si * cs:(si + 1) * cs, :].astype(F32)
            y = y * (zf * jax.nn.sigmoid(zf))
            o_ref[si * cs:(si + 1) * cs, :] = _rmsnorm(y, nw_ref[...]).astype(o_ref.dtype)
        hf_scr[...] = hf


def _ssd_call(z, xbc, dt, mxbc, mdt, cw, cb, dtb, alog, dskip, nw, sel, shc, shm, batch, seq, cs, sub):
    n = z.shape[0]
    rows = cs * sub
    assert n == batch * seq and seq % rows == 0 and cs % HALO == 0
    nc = seq // cs
    nb = seq // rows
    hpb = rows // HALO
    n_halo = n // HALO

    def ph0_block(ph, t):
        return (1 - ph) * (nb - 1 - t)

    const2 = lambda shape: pl.BlockSpec(shape, lambda b, ph, t: (0, 0))
    return pl.pallas_call(
        functools.partial(_ssd_kernel, cs=cs, sub=sub, nb=nb),
        grid=(batch, 2, nb),
        in_specs=[
            pl.BlockSpec((rows, SSM_INNER), lambda b, ph, t: (b * nb + ph * t, 0)),
            pl.BlockSpec((rows, SSM_CONV_DIM), lambda b, ph, t: (b * nb + ph0_block(ph, t), 0)),
            pl.BlockSpec((HALO, SSM_CONV_DIM),
                         lambda b, ph, t: (jnp.maximum((b * nb + ph0_block(ph, t)) * hpb - 1, 0), 0)),
            pl.BlockSpec((HALO, SSM_CONV_DIM),
                         lambda b, ph, t: (jnp.minimum((b * nb + ph0_block(ph, t) + 1) * hpb, n_halo - 1), 0)),
            pl.BlockSpec((rows, DT_PAD), lambda b, ph, t: (b * nb + ph0_block(ph, t), 0)),
            const2(mxbc.shape), const2(mdt.shape), const2(cw.shape), const2(cb.shape), const2(dtb.shape),
            const2(alog.shape), const2(dskip.shape), const2(nw.shape), const2(sel.shape), const2(shc.shape),
            const2(shm.shape),
        ],
        out_specs=pl.BlockSpec((rows, SSM_INNER), lambda b, ph, t: (b * nb + ph * t, 0)),
        out_shape=jax.ShapeDtypeStruct((n, SSM_INNER), BF16),
        scratch_shapes=[
            pltpu.VMEM((nc, cs, SSM_CONV_DIM), BF16),
            pltpu.VMEM((nc, cs, DT_PAD), F32),
            pltpu.VMEM((nc, cs, DT_PAD), F32),
            pltpu.VMEM((nc, LANES, GROUP_COLS), BF16),
            pltpu.VMEM((LANES, GROUP_COLS), F32),
            pltpu.VMEM((LANES, GROUP_COLS), F32),
            pltpu.VMEM((rows + 2 * HALO, SSM_CONV_DIM), BF16),
        ],
        compiler_params=pltpu.CompilerParams(dimension_semantics=("arbitrary",) * 3, vmem_limit_bytes=VMEM_LIMIT),
        name="bi_ssd",
    )(z, xbc, xbc, xbc, dt, mxbc, mdt, cw, cb, dtb, alog, dskip, nw, sel, shc, shm)


def _head_selector():
    k = jnp.arange(LANES)[:, None]
    col = jnp.arange(SSM_INNER)[None, :]
    return ((k % SSM_HEADS == col // SSM_HEADDIM) & (k < 2 * SSM_HEADS)).astype(BF16)


def _prep_weights(ffn1_norm_w, ffn1_w_gate, ffn1_w_up, ffn1_w_down, mix_norm_w, w_in, lambda_q1, lambda_k1,
                  lambda_q2, lambda_k2, attn_subln_w, conv_w, conv_b, dt_bias_fwd, dt_bias_bwd, a_log_fwd,
                  a_log_bwd, ssm_d, ssm_norm_w, w_out, ffn2_norm_w, ffn2_w_gate, ffn2_w_up, ffn2_w_down,
                  final_norm_w):
    def ffn(norm_w, wg, wu, wd):
        return norm_w[0][None, :], wg[0].astype(BF16), wu[0].astype(BF16), wd[0].astype(BF16)

    pad_lanes = lambda v, width: jnp.pad(v, (0, width - v.shape[0]))[None, :]
    o_v, o_z = 2 * ATT_QK, 2 * ATT_QK + ATT_V
    wi = w_in[0].astype(BF16)
    win = jnp.pad(jnp.concatenate([wi[:, :o_v], wi[:, o_z:]], axis=1),
                  ((0, 0), (0, D_IN_PAD - (w_in.shape[2] - ATT_V))))
    return dict(
        ffn1=ffn(ffn1_norm_w, ffn1_w_gate, ffn1_w_up, ffn1_w_down),
        ffn2=ffn(ffn2_norm_w, ffn2_w_gate, ffn2_w_up, ffn2_w_down),
        mix_norm=mix_norm_w[0][None, :],
        win=win,
        wvt=wi[:, o_v:o_z].T,
        lamv=jnp.stack([lambda_q1[0], lambda_k1[0], lambda_q2[0], lambda_k2[0]]),
        subw_col=attn_subln_w[0][:, None],
        cw=jnp.pad(conv_w[0], ((0, 8 - SSM_CONV), (0, 0))),
        cb=conv_b[0][None, :],
        dtb=pad_lanes(jnp.concatenate([dt_bias_fwd[0], dt_bias_bwd[0]]), DT_PAD),
        alog=pad_lanes(jnp.concatenate([a_log_fwd[0], a_log_bwd[0]]), DT_PAD),
        dskip=jnp.repeat(ssm_d[0], SSM_HEADDIM)[None, :],
        ssm_norm=ssm_norm_w[0][None, :],
        wo=w_out[0].astype(BF16).reshape(2, ATT_V, D_MODEL),
        final=final_norm_w[None, :],
        sel=_head_selector(),
        shc=_shift_matrix(SSD_CHUNK),
        shm=_shift_matrix(N_META),
    )


def _encode(x, w, meta_proj, bias_tabs):
    batch, seq, _ = x.shape
    km, vmt, mxbc, mdt = meta_proj
    h0 = x.reshape(batch * seq, D_MODEL)
    h1 = _ffn_call(h0, *w["ffn1"])
    q, k, vt, z, xbc, dt = _inproj_call(h1, w["mix_norm"], w["win"], w["wvt"])
    att = _attn_call(w["lamv"], q, k, vt, km, vmt, *bias_tabs, w["subw_col"], batch, seq, ATT_TQ, ATT_TK)
    ssm = _ssd_call(z, xbc, dt, mxbc, mdt, w["cw"], w["cb"], w["dtb"], w["alog"], w["dskip"], w["ssm_norm"],
                    w["sel"], w["shc"], w["shm"], batch, seq, SSD_CHUNK, SSD_SUB)
    y = _ffn_call(h1, *w["ffn2"], mix=(att, ssm, w["wo"]), final_w=w["final"])
    return y.reshape(batch, seq, D_MODEL)


def kernel(x_prompt, x_sample, meta_tokens, ffn1_norm_w, ffn1_w_gate, ffn1_w_up, ffn1_w_down, mix_norm_w, w_in, rel_bias, lambda_q1, lambda_k1, lambda_q2, lambda_k2, attn_subln_w, conv_w, conv_b, dt_bias_fwd, dt_bias_bwd, a_log_fwd, a_log_bwd, ssm_d, ssm_norm_w, w_out, ffn2_norm_w, ffn2_w_gate, ffn2_w_up, ffn2_w_down, final_norm_w):
    w = _prep_weights(ffn1_norm_w, ffn1_w_gate, ffn1_w_up, ffn1_w_down, mix_norm_w, w_in, lambda_q1, lambda_k1,
                      lambda_q2, lambda_k2, attn_subln_w, conv_w, conv_b, dt_bias_fwd, dt_bias_bwd, a_log_fwd,
                      a_log_bwd, ssm_d, ssm_norm_w, w_out, ffn2_norm_w, ffn2_w_gate, ffn2_w_up, ffn2_w_down,
                      final_norm_w)
    hm = _ffn_call(meta_tokens, *w["ffn1"])
    _, km, vmt, _, mxbc, mdt = _inproj_call(hm, w["mix_norm"], w["win"], w["wvt"])
    meta_proj = (km, vmt[0], mxbc, mdt)
    bias_tabs = _bias_call(rel_bias, ATT_TQ, ATT_TK)
    return (_encode(x_prompt, w, meta_proj, bias_tabs), _encode(x_sample, w, meta_proj, bias_tabs))
```

```python
import functools
import math

import jax
import jax.numpy as jnp
from jax import lax
from jax.experimental import pallas as pl
from jax.experimental.pallas import tpu as pltpu

F32 = jnp.float32
BF16 = jnp.bfloat16

D_MODEL = 1024
N_META = 16
N_ATT_HEADS = 8
ATT_DH = 64
ATT_DV = 128
ATT_QK = 1024
ATT_V = 1024
NUM_BUCKETS = 32
MAX_DISTANCE = 128
SSM_HEADS = 16
SSM_HEADDIM = 64
SSM_INNER = 1024
SSM_GROUPS = 2
SSM_STATE = 64
SSM_CONV = 7
SSM_CONV_DIM = 1280
D_FF = 2816
EPS = 1e-6
LAYER = 0
LAM_INIT = 0.8 - 0.6 * math.exp(-0.3 * LAYER)
LOG2E = math.log2(math.e)
Q_SCALE = ATT_DH ** -0.5 * LOG2E
NEG_BIG = -1e30
NORM_SLACK = 1.02
MAX_SHIFT_GAP = 100.0

LANES = 128
BF16_ROWS = 16
VMEM_LIMIT = 56 * 1024 * 1024

FF_TILE = 256
N_FF = D_FF // FF_TILE
DT_PAD = LANES
D_IN_PAD = 2 * ATT_QK + SSM_INNER + SSM_CONV_DIM + DT_PAD
T5_BAND = 91

ROW_TILE = 512
ATT_TQ = 1024
ATT_TK = 512
TAB_TQ = 512
ATT_COLS = 256
SSD_CHUNK = 128
MXU_TAPS = (0, 1, 5)
SSD_SUB = 8
HALO = BF16_ROWS
VT_ROWS = ATT_DV + BF16_ROWS


def _rmsnorm(x, w):
    ms = jnp.mean(x * x, axis=-1, keepdims=True)
    return x * lax.rsqrt(ms + EPS) * w


def _resident(shape):
    nd = len(shape)
    return pl.BlockSpec(shape, lambda *_: (0,) * nd, pipeline_mode=pl.Buffered(1))


def _ffn_kernel(*refs, has_mix, has_final):
    it = iter(refs)
    h_ref = next(it)
    if has_mix:
        att_ref, ssm_ref, wo_ref = next(it), next(it), next(it)
    nw_ref, wg_ref, wu_ref, wd_ref = next(it), next(it), next(it), next(it)
    fw_ref = next(it) if has_final else None
    o_ref = next(it)

    h = h_ref[...]
    if has_mix:
        h = (h + jnp.dot(att_ref[...], wo_ref[0], preferred_element_type=F32)
             + jnp.dot(ssm_ref[...], wo_ref[1], preferred_element_type=F32))
    u = _rmsnorm(h, nw_ref[...]).astype(BF16)
    acc = jnp.zeros_like(h)
    for j in range(N_FF):
        ff = slice(j * FF_TILE, (j + 1) * FF_TILE)
        g = jnp.dot(u, wg_ref[:, ff], preferred_element_type=F32)
        up = jnp.dot(u, wu_ref[:, ff], preferred_element_type=F32)
        a = (g * jax.nn.sigmoid(g) * up).astype(BF16)
        acc = acc + jnp.dot(a, wd_ref[ff, :], preferred_element_type=F32)
    h = h + 0.5 * acc
    if has_final:
        h = _rmsnorm(h, fw_ref[...])
    o_ref[...] = h


def _ffn_call(h, norm_w, wg, wu, wd, mix=None, final_w=None):
    n = h.shape[0]
    tm = min(ROW_TILE, n)
    assert n % tm == 0
    row = lambda width: pl.BlockSpec((tm, width), lambda i: (i, 0))
    args, specs = [h], [row(D_MODEL)]
    if mix is not None:
        att, ssm, wo = mix
        args += [att, ssm, wo]
        specs += [row(ATT_V), row(SSM_INNER), _resident(wo.shape)]
    args += [norm_w, wg, wu, wd]
    specs += [_resident(norm_w.shape), _resident(wg.shape), _resident(wu.shape), _resident(wd.shape)]
    if final_w is not None:
        args.append(final_w)
        specs.append(_resident(final_w.shape))
    return pl.pallas_call(
        functools.partial(_ffn_kernel, has_mix=mix is not None, has_final=final_w is not None),
        grid=(n // tm,),
        in_specs=specs,
        out_specs=row(D_MODEL),
        out_shape=jax.ShapeDtypeStruct((n, D_MODEL), F32),
        compiler_params=pltpu.CompilerParams(dimension_semantics=("arbitrary",), vmem_limit_bytes=VMEM_LIMIT),
        name="ffn_mix" if mix is not None else "ffn",
    )(*args)


_IN_SEGS = (("q", 0, ATT_QK), ("k", ATT_QK, ATT_QK), ("z", 2 * ATT_QK, SSM_INNER),
            ("xbc", 2 * ATT_QK + SSM_INNER, SSM_CONV_DIM), ("dt", D_IN_PAD - DT_PAD, DT_PAD))


def _inproj_kernel(h_ref, nw_ref, win_ref, wvt_ref, q_ref, k_ref, vt_ref, z_ref, xbc_ref, dt_ref):
    u = _rmsnorm(h_ref[...], nw_ref[...]).astype(BF16)
    outs = dict(q=q_ref, k=k_ref, z=z_ref, xbc=xbc_ref, dt=dt_ref)
    for name, c0, width in _IN_SEGS:
        o_ref = outs[name]
        step = 512 if width % 512 == 0 else (256 if width % 256 == 0 else LANES)
        for s in range(0, width, step):
            r = jnp.dot(u, win_ref[:, c0 + s:c0 + s + step], preferred_element_type=F32)
            if name == "q":
                r = r * Q_SCALE
            o_ref[:, s:s + step] = r.astype(o_ref.dtype)
    nt_dims = (((1,), (1,)), ((), ()))
    ones = jnp.ones((VT_ROWS - ATT_DV, u.shape[0]), vt_ref.dtype)
    for s in range(0, ATT_V, 256):
        r = lax.dot_general(wvt_ref[s:s + 256, :], u, nt_dims, preferred_element_type=F32).astype(vt_ref.dtype)
        for hh in range(256 // ATT_DV):
            head = s // ATT_DV + hh
            vt_ref[0, head * VT_ROWS:head * VT_ROWS + ATT_DV, :] = r[hh * ATT_DV:(hh + 1) * ATT_DV]
            vt_ref[0, head * VT_ROWS + ATT_DV:(head + 1) * VT_ROWS, :] = ones


def _inproj_call(h, norm_w, win, wvt):
    n = h.shape[0]
    tm = min(ATT_TK, n)
    assert n % tm == 0
    row = lambda width: pl.BlockSpec((tm, width), lambda i: (i, 0))
    widths = (ATT_QK, ATT_QK, SSM_INNER, SSM_CONV_DIM, DT_PAD)
    dtypes = (BF16, BF16, BF16, BF16, F32)
    shapes = [jax.ShapeDtypeStruct((n, w), dt) for w, dt in zip(widths, dtypes)]
    specs = [row(w) for w in widths]
    shapes.insert(2, jax.ShapeDtypeStruct((n // tm, N_ATT_HEADS * VT_ROWS, tm), BF16))
    specs.insert(2, pl.BlockSpec((1, N_ATT_HEADS * VT_ROWS, tm), lambda i: (i, 0, 0)))
    return pl.pallas_call(
        _inproj_kernel,
        grid=(n // tm,),
        in_specs=[row(D_MODEL), _resident(norm_w.shape), _resident(win.shape), _resident(wvt.shape)],
        out_specs=specs,
        out_shape=shapes,
        compiler_params=pltpu.CompilerParams(dimension_semantics=("arbitrary",), vmem_limit_bytes=VMEM_LIMIT),
        name="inproj",
    )(h, norm_w, win, wvt)


def _t5_bias(rel, rb_ref, head):
    half = NUM_BUCKETS // 2
    max_exact = half // 2
    ret = jnp.where(rel > 0, half, 0)
    n = jnp.abs(rel)
    nf = jnp.maximum(n, 1).astype(F32)
    large = max_exact + (jnp.log(nf / max_exact) / math.log(MAX_DISTANCE / max_exact)
                         * (half - max_exact)).astype(jnp.int32)
    large = jnp.minimum(large, half - 1)
    bucket = ret + jnp.where(n < max_exact, n, large)
    val = jnp.zeros(rel.shape, F32)
    for jb in range(NUM_BUCKETS):
        val = jnp.where(bucket == jb, rb_ref[jb, head], val)
    return val * LOG2E


def _bias_geometry(tq, tk):
    unit = min(tq, tk)
    assert tq % unit == 0 and tk % unit == 0 and unit >= T5_BAND + 1
    return unit, tk // unit, tq // unit


def _bias_kernel(rb_ref, tab_ref, mtab_ref, rng_ref, *, tq, tk):
    head = pl.program_id(0)
    unit, lo, hi = _bias_geometry(TAB_TQ, tk)
    n_near = lo + hi + 1
    top = rb_ref[0, head]
    bottom = rb_ref[0, head]
    for jb in range(1, NUM_BUCKETS):
        top = jnp.maximum(top, rb_ref[jb, head])
        bottom = jnp.minimum(bottom, rb_ref[jb, head])
    rng_ref[0, 0:1, :] = jnp.full((1, LANES), top * LOG2E, F32)
    rng_ref[0, 1:2, :] = jnp.full((1, LANES), bottom * LOG2E, F32)
    rng_ref[0, 2:8, :] = jnp.zeros((6, LANES), F32)
    far_left = rb_ref[NUM_BUCKETS // 2 - 1, head] * LOG2E
    far_right = rb_ref[NUM_BUCKETS - 1, head] * LOG2E
    krow = lax.broadcasted_iota(jnp.int32, (LANES, LANES), 0)
    qcol = lax.broadcasted_iota(jnp.int32, (LANES, LANES), 1)
    for t in range(n_near):
        for a in range(tk // LANES):
            for b in range(TAB_TQ // LANES):
                base = (a - b) * LANES + (t - lo) * unit
                blk = (slice(a * LANES, (a + 1) * LANES), slice(b * LANES, (b + 1) * LANES))
                if base + LANES - 1 <= -T5_BAND:
                    tab_ref[(0, t) + blk] = jnp.full((LANES, LANES), far_left, F32)
                elif base - LANES + 1 >= T5_BAND:
                    tab_ref[(0, t) + blk] = jnp.full((LANES, LANES), far_right, F32)
                else:
                    tab_ref[(0, t) + blk] = _t5_bias(krow - qcol + base, rb_ref, head)
    tab_ref[0, n_near] = jnp.full((tk, TAB_TQ), far_left, F32)
    tab_ref[0, n_near + 1] = jnp.full((tk, TAB_TQ), far_right, F32)
    mrow = lax.broadcasted_iota(jnp.int32, (N_META, tq), 0)
    mcol = lax.broadcasted_iota(jnp.int32, (N_META, tq), 1)
    mtab_ref[0, 0] = _t5_bias(mrow - N_META - mcol, rb_ref, head)
    mtab_ref[0, 1] = jnp.full((N_META, tq), far_left, F32)


def _bias_call(rel_bias, tq, tk):
    assert tq % TAB_TQ == 0
    _, lo, hi = _bias_geometry(TAB_TQ, tk)
    nt = lo + hi + 3
    return pl.pallas_call(
        functools.partial(_bias_kernel, tq=tq, tk=tk),
        grid=(N_ATT_HEADS,),
        in_specs=[pl.BlockSpec(memory_space=pltpu.SMEM)],
        out_specs=[pl.BlockSpec((1, nt, tk, TAB_TQ), lambda h: (h, 0, 0, 0)),
                   pl.BlockSpec((1, 2, N_META, tq), lambda h: (h, 0, 0, 0)),
                   pl.BlockSpec((1, 8, LANES), lambda h: (h, 0, 0))],
        out_shape=[jax.ShapeDtypeStruct((N_ATT_HEADS, nt, tk, TAB_TQ), F32),
                   jax.ShapeDtypeStruct((N_ATT_HEADS, 2, N_META, tq), F32),
                   jax.ShapeDtypeStruct((N_ATT_HEADS, 8, LANES), F32)],
        compiler_params=pltpu.CompilerParams(dimension_semantics=("arbitrary",)),
        name="t5_bias",
    )(rel_bias)


def _attn_kernel(lam_ref, qall_ref, k_ref, vt_ref, km_ref, vmt_ref, tab_ref, mtab_ref, rng_ref, sw_ref, o_ref,
                 sa_ref, sb_ref, mca_ref, mcb_ref, m_scr, acc_scr, accp_scr, shift_scr, flag_scr,
                 *, tq, tk, nkc, nq, n_steps):
    g = pl.program_id(0)
    qi = jnp.minimum(g, n_steps - 1) % nq
    unit, lo, hi = _bias_geometry(TAB_TQ, tk)
    n_near = lo + hi + 1
    nt_dims = (((1,), (1,)), ((), ()))
    n_col = 2 * tq // ATT_COLS

    def finalize_previous():
        acc = accp_scr[...]
        o = acc[:ATT_DV] / acc[ATT_DV:ATT_DV + 1]
        lv = lam_ref[...]
        lam = (jnp.exp(jnp.sum(lv[0:1] * lv[1:2], axis=1, keepdims=True))
               - jnp.exp(jnp.sum(lv[2:3] * lv[3:4], axis=1, keepdims=True)) + LAM_INIT)
        out = o[:, :tq] - lam * o[:, tq:]
        ms = jnp.mean(out * out, axis=0, keepdims=True)
        out = out * lax.rsqrt(ms + EPS) * sw_ref[...] * (1.0 - LAM_INIT)
        o_ref[...] = out.T.astype(o_ref.dtype)

    @pl.when(g == 0)
    def _():
        accp_scr[...] = jnp.ones(accp_scr.shape, F32)

    @pl.when(g < n_steps)
    def _():
        q = qall_ref[pl.ds(pl.multiple_of(qi * tq, tq), tq), :]
        lane = lax.broadcasted_iota(jnp.int32, (tq, LANES), 1)
        zero = jnp.zeros_like(q)
        q2 = jnp.concatenate([jnp.where(lane < ATT_DH, q, zero), jnp.where(lane >= ATT_DH, q, zero)], axis=0)

        def bias_tile(j, c):
            q0 = (c * ATT_COLS) % tq
            du = j * (tk // unit) - (qi * (tq // TAB_TQ) + q0 // TAB_TQ) * (TAB_TQ // unit)
            idx = jnp.where(du < -lo, n_near, jnp.where(du > hi, n_near + 1, du + lo))
            return tab_ref[0, idx, :, pl.ds(q0 % TAB_TQ, ATT_COLS)]

        def add_bias(s, b):
            return jnp.concatenate([s[:, :tq] + b, s[:, tq:] + b], axis=1)

        @pl.when(qi == 0)
        def _():
            half = jnp.where(lax.broadcasted_iota(jnp.int32, (LANES, LANES), 0) // ATT_DH
                             == lax.broadcasted_iota(jnp.int32, (LANES, LANES), 1), 1.0, 0.0).astype(BF16)

            def max_sq_norm(x_ref):
                x = x_ref[...]
                sq = jnp.dot(x * x, half, preferred_element_type=F32)
                return jnp.max(sq, axis=0, keepdims=True)

            bound2 = max_sq_norm(qall_ref) * jnp.maximum(max_sq_norm(k_ref), max_sq_norm(km_ref))
            lane_row = lax.broadcasted_iota(jnp.int32, (1, LANES), 1)
            qk_bound = [NORM_SLACK * jnp.sqrt(jnp.max(jnp.where(lane_row == mp, bound2, 0.0), axis=1, keepdims=True))
                        for mp in range(2)]
            hi_b = rng_ref[0, 0:1, 0:1]
            lo_b = rng_ref[0, 1:2, 0:1]
            col = lax.broadcasted_iota(jnp.int32, (1, 2 * tq), 1)
            shift_scr[...] = jnp.where(col < tq, qk_bound[0], qk_bound[1]) + hi_b
            worst_gap = 2.0 * jnp.maximum(qk_bound[0], qk_bound[1]) + (hi_b - lo_b)
            flag_scr[0] = (worst_gap[0, 0] <= MAX_SHIFT_GAP).astype(jnp.int32)

        shift = shift_scr[...]
        bounded = flag_scr[0] == 1

        @pl.when(bounded)
        def _():
            finalize_previous()
            sm = lax.dot_general(km_ref[...], q2, nt_dims, preferred_element_type=F32)
            sm = add_bias(sm, mtab_ref[0, jnp.minimum(qi, 1)])
            acc_scr[...] = jnp.dot(vmt_ref[...], jnp.exp2(sm - shift).astype(BF16), preferred_element_type=F32)

            def stage_logits(u, j, c):
                buf = (sa_ref, sb_ref)[u % 2]
                buf[:, pl.ds(c * ATT_COLS, ATT_COLS)] = lax.dot_general(
                    k_ref[pl.ds(j * tk, tk), :], q2[c * ATT_COLS:(c + 1) * ATT_COLS], nt_dims,
                    preferred_element_type=F32)

            units = [(j, c) for j in range(nkc) for c in range(n_col)]
            stage_logits(0, *units[0])
            for u, (j, c) in enumerate(units):
                if u + 1 < len(units):
                    stage_logits(u + 1, *units[u + 1])
                cols = pl.ds(c * ATT_COLS, ATT_COLS)
                s = (sa_ref, sb_ref)[u % 2][:, cols] + bias_tile(j, c)
                p = jnp.exp2(s - shift[:, c * ATT_COLS:(c + 1) * ATT_COLS]).astype(BF16)
                acc_scr[:, cols] += jnp.dot(vt_ref[j], p, preferred_element_type=F32)
            accp_scr[...] = acc_scr[...]

        @pl.when(jnp.logical_not(bounded))
        def _():
            finalize_previous()

            def produce(j, c, s_ref, mc_ref):
                cols = pl.ds(c * ATT_COLS, ATT_COLS)
                s = lax.dot_general(k_ref[pl.ds(j * tk, tk), :], q2[c * ATT_COLS:(c + 1) * ATT_COLS], nt_dims,
                                    preferred_element_type=F32)
                s = s + bias_tile(j, c)
                s_ref[:, cols] = s
                mc_ref[:, cols] = jnp.max(s, axis=0, keepdims=True)

            def consume(s, m_cur, vt, cols, first=False):
                if first:
                    m_new = m_cur
                else:
                    m_prev = m_scr[:, cols]
                    m_new = jnp.maximum(m_prev, m_cur)
                    alpha = jnp.exp2(m_prev - m_new)
                p = jnp.exp2(s - m_new).astype(BF16)
                pv = jnp.dot(vt, p, preferred_element_type=F32)
                acc_scr[:, cols] = pv if first else alpha * acc_scr[:, cols] + pv
                m_scr[:, cols] = m_new

            sm = lax.dot_general(km_ref[...], q2, nt_dims, preferred_element_type=F32)
            sm = add_bias(sm, mtab_ref[0, jnp.minimum(qi, 1)])
            consume(sm, jnp.max(sm, axis=0, keepdims=True), vmt_ref[...], pl.ds(0, 2 * tq), first=True)

            bufs = ((sa_ref, mca_ref), (sb_ref, mcb_ref))
            for c in range(n_col):
                produce(0, c, *bufs[0])
            for j in range(nkc):
                s_ref, mc_ref = bufs[j % 2]
                for c in range(n_col):
                    cols = pl.ds(c * ATT_COLS, ATT_COLS)
                    if j + 1 < nkc:
                        produce(j + 1, c, *bufs[(j + 1) % 2])
                    consume(s_ref[:, cols], mc_ref[:, cols], vt_ref[j], cols)
            accp_scr[...] = acc_scr[...]

    @pl.when(g == n_steps)
    def _():
        finalize_previous()


def _attn_call(lamv, q, k, vt, km, vmt, tab, mtab, rng, subw_col, batch, seq, tq, tk):
    n = q.shape[0]
    assert n == batch * seq and seq % (2 * tk) == 0 and seq % tq == 0 and vt.shape[2] == tk
    nq = seq // tq
    nkc = seq // tk
    nt = tab.shape[1]
    n_steps = N_ATT_HEADS * batch * nq

    def tile(g):
        g = jnp.minimum(g, n_steps - 1)
        return g // (batch * nq), (g // nq) % batch, g % nq

    def cur(f):
        return lambda g: f(*tile(g))

    def prev(f):
        return lambda g: f(*tile(jnp.maximum(g - 1, 0)))

    return pl.pallas_call(
        functools.partial(_attn_kernel, tq=tq, tk=tk, nkc=nkc, nq=nq, n_steps=n_steps),
        grid=(n_steps + 1,),
        in_specs=[
            pl.BlockSpec(lamv.shape, lambda g: (0, 0)),
            pl.BlockSpec((seq, LANES), cur(lambda h, b, i: (b, h))),
            pl.BlockSpec((seq, LANES), cur(lambda h, b, i: (b, h))),
            pl.BlockSpec((nkc, VT_ROWS, tk), cur(lambda h, b, i: (b, h, 0))),
            pl.BlockSpec((N_META, LANES), cur(lambda h, b, i: (0, h))),
            pl.BlockSpec((VT_ROWS, N_META), cur(lambda h, b, i: (h, 0))),
            pl.BlockSpec((1, nt, tk, TAB_TQ), cur(lambda h, b, i: (h, 0, 0, 0))),
            pl.BlockSpec((1, 2, N_META, tq), cur(lambda h, b, i: (h, 0, 0, 0))),
            pl.BlockSpec((1, 8, LANES), cur(lambda h, b, i: (h, 0, 0))),
            pl.BlockSpec(subw_col.shape, lambda g: (0, 0)),
        ],
        out_specs=pl.BlockSpec((tq, LANES), prev(lambda h, b, i: (b * nq + i, h))),
        out_shape=jax.ShapeDtypeStruct((n, ATT_V), BF16),
        scratch_shapes=[pltpu.VMEM((tk, 2 * tq), F32)] * 2 + [pltpu.VMEM((1, 2 * tq), F32)] * 3
        + [pltpu.VMEM((VT_ROWS, 2 * tq), F32)] * 2 + [pltpu.VMEM((1, 2 * tq), F32), pltpu.SMEM((1,), jnp.int32)],
        compiler_params=pltpu.CompilerParams(dimension_semantics=("arbitrary",), vmem_limit_bytes=VMEM_LIMIT),
        name="diff_attn",
    )(lamv, q, k, vt, km, vmt, tab, mtab, rng, subw_col)


def _split3(x):
    hi = x.astype(BF16)
    r1 = x - hi.astype(F32)
    mid = r1.astype(BF16)
    lo = (r1 - mid.astype(F32)).astype(BF16)
    return hi, mid, lo


def _cumsum_rows(a):
    rows = a.shape[0]
    r_i = lax.broadcasted_iota(jnp.int32, (rows, rows), 0)
    c_i = lax.broadcasted_iota(jnp.int32, (rows, rows), 1)
    tri = jnp.where(c_i <= r_i, 1.0, 0.0).astype(BF16)
    out = None
    for term in _split3(a):
        part = jnp.dot(tri, term, preferred_element_type=F32)
        out = part if out is None else out + part
    return out


def _expand_rows(parts, sel_ref):
    masked = []
    for w, first in parts:
        lane = lax.broadcasted_iota(jnp.int32, w.shape, 1)
        masked.append(jnp.where((lane >= first) & (lane < first + SSM_HEADS), w, 0.0))
    stacked = jnp.concatenate(masked, axis=0)
    hi = stacked.astype(BF16)
    lo = (stacked - hi.astype(F32)).astype(BF16)
    sel = sel_ref[...]
    full = jnp.dot(hi, sel, preferred_element_type=F32) + jnp.dot(lo, sel, preferred_element_type=F32)
    outs, r0 = [], 0
    for w, _ in parts:
        outs.append(full[r0:r0 + w.shape[0]])
        r0 += w.shape[0]
    return outs


def _softplus(x):
    return jnp.maximum(x, 0.0) + jnp.log(1.0 + jnp.exp(-jnp.abs(x)))


GROUP_COLS = SSM_INNER // SSM_GROUPS


def _state_update(b_t, xw):
    return jnp.concatenate(
        [jnp.dot(b_t[g * SSM_STATE:(g + 1) * SSM_STATE], xw[:, g * GROUP_COLS:(g + 1) * GROUP_COLS],
                 preferred_element_type=F32) for g in range(SSM_GROUPS)], axis=0)


def _stack_decay(dec_row):
    return jnp.concatenate(
        [jnp.broadcast_to(dec_row[:, g * GROUP_COLS:(g + 1) * GROUP_COLS], (SSM_STATE, GROUP_COLS))
         for g in range(SSM_GROUPS)], axis=0)


def _conv_silu(win, shift_ref, cw_ref, cb_ref, rows):
    total = rows + 2 * HALO
    assert win.shape[0] == total and shift_ref.shape == (len(MXU_TAPS) * rows, total)
    shifted = jnp.dot(shift_ref[...], win, preferred_element_type=F32)
    win32 = win.astype(F32)
    acc = jnp.broadcast_to(cb_ref[...], (rows, SSM_CONV_DIM))
    for j in range(SSM_CONV):
        off = j - SSM_CONV // 2
        if j in MXU_TAPS:
            tap = shifted[MXU_TAPS.index(j) * rows:(MXU_TAPS.index(j) + 1) * rows]
        elif off == 0:
            tap = win32[HALO:HALO + rows]
        else:
            tap = pltpu.roll(win32, (total - off) % total, axis=0)[HALO:HALO + rows]
        acc = acc + cw_ref[j:j + 1, :] * tap
    return acc * jax.nn.sigmoid(acc)


def _shift_matrix(rows):
    offs = jnp.array([j - SSM_CONV // 2 for j in MXU_TAPS])
    t = jnp.arange(rows)
    src = HALO + t[None, :] + offs[:, None]
    return (src.reshape(-1)[:, None] == jnp.arange(rows + 2 * HALO)[None, :]).astype(BF16)


def _ssd_kernel(z_ref, xc_ref, xl_ref, xr_ref, dt_ref, mx_ref, mdt_ref, cw_ref, cb_ref, dtb_ref, alog_ref,
                dsk_ref, nw_ref, sel_ref, shc_ref, shm_ref, o_ref, xs_scr, dts_scr, cum_scr, hbs_scr, hf_scr, hb_scr,
                win_scr,
                *, cs, sub, nb):
    rows = cs * sub
    ph = pl.program_id(1)
    t = pl.program_id(2)
    fwd0, bwd0 = 0, SSM_HEADS
    a_row = -jnp.exp(alog_ref[...])

    def decay_terms(dt_raw):
        dt = _softplus(dt_raw + dtb_ref[...])
        return dt, _cumsum_rows(dt * a_row)

    def bcast8(row):
        return jnp.broadcast_to(row, (8, LANES))

    @pl.when(ph == 0)
    def _():
        blk = nb - 1 - t

        @pl.when(t == 0)
        def _():
            hb_scr[...] = jnp.zeros(hb_scr.shape, F32)

        left = jnp.where(blk == 0, mx_ref[...], xl_ref[...])
        right = jnp.where(blk == nb - 1, jnp.zeros_like(xr_ref[...]), xr_ref[...])
        win_scr[0:HALO, :] = left
        win_scr[HALO:HALO + rows, :] = xc_ref[...]
        win_scr[HALO + rows:HALO + rows + HALO, :] = right

        hb = hb_scr[...]
        for si in reversed(range(sub)):
            cc = blk * sub + si
            xbc = _conv_silu(win_scr[si * cs:si * cs + cs + 2 * HALO, :], shc_ref, cw_ref, cb_ref, cs)
            xs_scr[cc] = xbc.astype(BF16)
            dt, cum = decay_terms(dt_ref[si * cs:(si + 1) * cs, :])
            dts_scr[cc] = dt
            cum_scr[cc] = cum
            eb = cum - dt * a_row
            w_b, dec = _expand_rows([(jnp.exp(eb) * dt, bwd0), (bcast8(jnp.exp(cum[cs - 1:cs, :])), bwd0)],
                                    sel_ref)
            xw = (xbc[:, :SSM_INNER] * w_b).astype(BF16)
            bm_t = xbc[:, SSM_INNER:SSM_INNER + LANES].T.astype(BF16)
            hbs_scr[cc] = hb.astype(BF16)
            hb = hb * _stack_decay(dec[0:1]) + _state_update(bm_t, xw)
        hb_scr[...] = hb

    @pl.when(ph == 1)
    def _():
        @pl.when(t == 0)
        def _():
            wm = jnp.concatenate([jnp.zeros((HALO, SSM_CONV_DIM), BF16), mx_ref[...], xc_ref[0:HALO, :]], axis=0)
            xm = _conv_silu(wm, shm_ref, cw_ref, cb_ref, N_META)
            dtm, cumm = decay_terms(mdt_ref[...])
            (w_m,) = _expand_rows([(jnp.exp(cumm[N_META - 1:N_META, :] - cumm) * dtm, fwd0)], sel_ref)
            xwm = (xm[:, :SSM_INNER] * w_m).astype(BF16)
            bmm_t = xm[:, SSM_INNER:SSM_INNER + LANES].T.astype(BF16)
            hf_scr[...] = _state_update(bmm_t, xwm)

        lane = lax.broadcasted_iota(jnp.int32, (cs, LANES), 1)
        l_i = lax.broadcasted_iota(jnp.int32, (cs, cs), 0)
        s_i = lax.broadcasted_iota(jnp.int32, (cs, cs), 1)
        lower = s_i <= l_i
        diag = s_i == l_i
        hpg = SSM_HEADS // SSM_GROUPS
        zx = jnp.zeros((cs, LANES), BF16)
        nt_dims = (((1,), (1,)), ((), ()))

        hf = hf_scr[...]
        for si in range(sub):
            cc = t * sub + si
            xbc = xs_scr[cc]
            x_bf = xbc[:, :SSM_INNER]
            bm = xbc[:, SSM_INNER:SSM_INNER + LANES]
            cm = xbc[:, SSM_INNER + LANES:SSM_INNER + 2 * LANES]
            x = x_bf.astype(F32)

            dt = dts_scr[cc]
            cum = cum_scr[cc]
            eb = cum - dt * a_row
            dt_t, cum_t, eb_t = dt.T, cum.T, eb.T
            last = cum[cs - 1:cs, :]

            c_grp = [jnp.where(lane // SSM_STATE == g, cm, jnp.zeros_like(cm)) for g in range(SSM_GROUPS)]
            g_mats = [lax.dot_general(c_g, bm, nt_dims, preferred_element_type=F32) for c_g in c_grp]

            pieces = []
            for hp in range(SSM_HEADS // 2):
                w_pair = []
                for h in (2 * hp, 2 * hp + 1):
                    arg_f = cum[:, fwd0 + h:fwd0 + h + 1] - cum_t[fwd0 + h:fwd0 + h + 1, :]
                    arg_b = eb_t[bwd0 + h:bwd0 + h + 1, :] - eb[:, bwd0 + h:bwd0 + h + 1]
                    e = jnp.exp(jnp.minimum(jnp.where(lower, arg_f, arg_b), 0.0))
                    dt_f_row = dt_t[fwd0 + h:fwd0 + h + 1, :]
                    dt_b_row = dt_t[bwd0 + h:bwd0 + h + 1, :]
                    m = e * jnp.where(lower, dt_f_row, dt_b_row) + jnp.where(diag, dt_b_row, 0.0)
                    w_pair.append((g_mats[h // hpg] * m).astype(BF16))
                xp = x_bf[:, hp * LANES:(hp + 1) * LANES]
                rhs = jnp.concatenate([jnp.where(lane < SSM_HEADDIM, xp, zx),
                                       jnp.where(lane >= SSM_HEADDIM, xp, zx)], axis=0)
                pieces.append(jnp.dot(jnp.concatenate(w_pair, axis=1), rhs, preferred_element_type=F32))
            y = jnp.concatenate(pieces, axis=1)

            d_f, d_b, w_f, dec = _expand_rows(
                [(jnp.exp(cum), fwd0), (jnp.exp(last - eb), bwd0), (jnp.exp(last - cum) * dt, fwd0),
                 (bcast8(jnp.exp(last)), fwd0)], sel_ref)
            hf_bf = hf.astype(BF16)
            hb_bf = hbs_scr[cc]
            y = y + d_f * jnp.concatenate([jnp.dot(c_g, hf_bf, preferred_element_type=F32) for c_g in c_grp],
                                          axis=1)
            y = y + d_b * jnp.concatenate([jnp.dot(c_g, hb_bf, preferred_element_type=F32) for c_g in c_grp],
                                          axis=1)
            y = y + x * dsk_ref[...]

            xw = (x * w_f).astype(BF16)
            hf = hf * _stack_decay(dec[0:1]) + _state_update(bm.astype(F32).T.astype(BF16), xw)

            zf = z_ref[si * cs:(si + 1) * cs, :].astype(F32)
            y = y * (zf * jax.nn.sigmoid(zf))
            o_ref[si * cs:(si + 1) * cs, :] = _rmsnorm(y, nw_ref[...]).astype(o_ref.dtype)
        hf_scr[...] = hf


def _ssd_call(z, xbc, dt, mxbc, mdt, cw, cb, dtb, alog, dskip, nw, sel, shc, shm, batch, seq, cs, sub):
    n = z.shape[0]
    rows = cs * sub
    assert n == batch * seq and seq % rows == 0 and cs % HALO == 0
    nc = seq // cs
    nb = seq // rows
    hpb = rows // HALO
    n_halo = n // HALO

    def ph0_block(ph, t):
        return (1 - ph) * (nb - 1 - t)

    const2 = lambda shape: pl.BlockSpec(shape, lambda b, ph, t: (0, 0))
    return pl.pallas_call(
        functools.partial(_ssd_kernel, cs=cs, sub=sub, nb=nb),
        grid=(batch, 2, nb),
        in_specs=[
            pl.BlockSpec((rows, SSM_INNER), lambda b, ph, t: (b * nb + ph * t, 0)),
            pl.BlockSpec((rows, SSM_CONV_DIM), lambda b, ph, t: (b * nb + ph0_block(ph, t), 0)),
            pl.BlockSpec((HALO, SSM_CONV_DIM),
                         lambda b, ph, t: (jnp.maximum((b * nb + ph0_block(ph, t)) * hpb - 1, 0), 0)),
            pl.BlockSpec((HALO, SSM_CONV_DIM),
                         lambda b, ph, t: (jnp.minimum((b * nb + ph0_block(ph, t) + 1) * hpb, n_halo - 1), 0)),
            pl.BlockSpec((rows, DT_PAD), lambda b, ph, t: (b * nb + ph0_block(ph, t), 0)),
            const2(mxbc.shape), const2(mdt.shape), const2(cw.shape), const2(cb.shape), const2(dtb.shape),
            const2(alog.shape), const2(dskip.shape), const2(nw.shape), const2(sel.shape), const2(shc.shape),
            const2(shm.shape),
        ],
        out_specs=pl.BlockSpec((rows, SSM_INNER), lambda b, ph, t: (b * nb + ph * t, 0)),
        out_shape=jax.ShapeDtypeStruct((n, SSM_INNER), BF16),
        scratch_shapes=[
            pltpu.VMEM((nc, cs, SSM_CONV_DIM), BF16),
            pltpu.VMEM((nc, cs, DT_PAD), F32),
            pltpu.VMEM((nc, cs, DT_PAD), F32),
            pltpu.VMEM((nc, LANES, GROUP_COLS), BF16),
            pltpu.VMEM((LANES, GROUP_COLS), F32),
            pltpu.VMEM((LANES, GROUP_COLS), F32),
            pltpu.VMEM((rows + 2 * HALO, SSM_CONV_DIM), BF16),
        ],
        compiler_params=pltpu.CompilerParams(dimension_semantics=("arbitrary",) * 3, vmem_limit_bytes=VMEM_LIMIT),
        name="bi_ssd",
    )(z, xbc, xbc, xbc, dt, mxbc, mdt, cw, cb, dtb, alog, dskip, nw, sel, shc, shm)


def _head_selector():
    k = jnp.arange(LANES)[:, None]
    col = jnp.arange(SSM_INNER)[None, :]
    return ((k % SSM_HEADS == col // SSM_HEADDIM) & (k < 2 * SSM_HEADS)).astype(BF16)


def _prep_weights(ffn1_norm_w, ffn1_w_gate, ffn1_w_up, ffn1_w_down, mix_norm_w, w_in, lambda_q1, lambda_k1,
                  lambda_q2, lambda_k2, attn_subln_w, conv_w, conv_b, dt_bias_fwd, dt_bias_bwd, a_log_fwd,
                  a_log_bwd, ssm_d, ssm_norm_w, w_out, ffn2_norm_w, ffn2_w_gate, ffn2_w_up, ffn2_w_down,
                  final_norm_w):
    def ffn(norm_w, wg, wu, wd):
        return norm_w[0][None, :], wg[0].astype(BF16), wu[0].astype(BF16), wd[0].astype(BF16)

    pad_lanes = lambda v, width: jnp.pad(v, (0, width - v.shape[0]))[None, :]
    o_v, o_z = 2 * ATT_QK, 2 * ATT_QK + ATT_V
    wi = w_in[0].astype(BF16)
    win = jnp.pad(jnp.concatenate([wi[:, :o_v], wi[:, o_z:]], axis=1),
                  ((0, 0), (0, D_IN_PAD - (w_in.shape[2] - ATT_V))))
    return dict(
        ffn1=ffn(ffn1_norm_w, ffn1_w_gate, ffn1_w_up, ffn1_w_down),
        ffn2=ffn(ffn2_norm_w, ffn2_w_gate, ffn2_w_up, ffn2_w_down),
        mix_norm=mix_norm_w[0][None, :],
        win=win,
        wvt=wi[:, o_v:o_z].T,
        lamv=jnp.stack([lambda_q1[0], lambda_k1[0], lambda_q2[0], lambda_k2[0]]),
        subw_col=attn_subln_w[0][:, None],
        cw=jnp.pad(conv_w[0], ((0, 8 - SSM_CONV), (0, 0))),
        cb=conv_b[0][None, :],
        dtb=pad_lanes(jnp.concatenate([dt_bias_fwd[0], dt_bias_bwd[0]]), DT_PAD),
        alog=pad_lanes(jnp.concatenate([a_log_fwd[0], a_log_bwd[0]]), DT_PAD),
        dskip=jnp.repeat(ssm_d[0], SSM_HEADDIM)[None, :],
        ssm_norm=ssm_norm_w[0][None, :],
        wo=w_out[0].astype(BF16).reshape(2, ATT_V, D_MODEL),
        final=final_norm_w[None, :],
        sel=_head_selector(),
        shc=_shift_matrix(SSD_CHUNK),
        shm=_shift_matrix(N_META),
    )


def _encode(x, w, meta_proj, bias_tabs):
    batch, seq, _ = x.shape
    km, vmt, mxbc, mdt = meta_proj
    h0 = x.reshape(batch * seq, D_MODEL)
    h1 = _ffn_call(h0, *w["ffn1"])
    q, k, vt, z, xbc, dt = _inproj_call(h1, w["mix_norm"], w["win"], w["wvt"])
    att = _attn_call(w["lamv"], q, k, vt, km, vmt, *bias_tabs, w["subw_col"], batch, seq, ATT_TQ, ATT_TK)
    ssm = _ssd_call(z, xbc, dt, mxbc, mdt, w["cw"], w["cb"], w["dtb"], w["alog"], w["dskip"], w["ssm_norm"],
                    w["sel"], w["shc"], w["shm"], batch, seq, SSD_CHUNK, SSD_SUB)
    y = _ffn_call(h1, *w["ffn2"], mix=(att, ssm, w["wo"]), final_w=w["final"])
    return y.reshape(batch, seq, D_MODEL)


def kernel(x_prompt, x_sample, meta_tokens, ffn1_norm_w, ffn1_w_gate, ffn1_w_up, ffn1_w_down, mix_norm_w, w_in, rel_bias, lambda_q1, lambda_k1, lambda_q2, lambda_k2, attn_subln_w, conv_w, conv_b, dt_bias_fwd, dt_bias_bwd, a_log_fwd, a_log_bwd, ssm_d, ssm_norm_w, w_out, ffn2_norm_w, ffn2_w_gate, ffn2_w_up, ffn2_w_down, final_norm_w):
    w = _prep_weights(ffn1_norm_w, ffn1_w_gate, ffn1_w_up, ffn1_w_down, mix_norm_w, w_in, lambda_q1, lambda_k1,
                      lambda_q2, lambda_k2, attn_subln_w, conv_w, conv_b, dt_bias_fwd, dt_bias_bwd, a_log_fwd,
                      a_log_bwd, ssm_d, ssm_norm_w, w_out, ffn2_norm_w, ffn2_w_gate, ffn2_w_up, ffn2_w_down,
                      final_norm_w)
    hm = _ffn_call(meta_tokens, *w["ffn1"])
    _, km, vmt, _, mxbc, mdt = _inproj_call(hm, w["mix_norm"], w["win"], w["wvt"])
    meta_proj = (km, vmt[0], mxbc, mdt)
    bias_tabs = _bias_call(rel_bias, ATT_TQ, ATT_TK)
    return (_encode(x_prompt, w, meta_proj, bias_tabs), _encode(x_sample, w, meta_proj, bias_tabs))
```

```python
import functools
import math

import jax
import jax.numpy as jnp
from jax import lax
from jax.experimental import pallas as pl
from jax.experimental.pallas import tpu as pltpu

F32 = jnp.float32
BF16 = jnp.bfloat16

D_MODEL = 1024
N_META = 16
N_ATT_HEADS = 8
ATT_DH = 64
ATT_DV = 128
ATT_QK = 1024
ATT_V = 1024
NUM_BUCKETS = 32
MAX_DISTANCE = 128
SSM_HEADS = 16
SSM_HEADDIM = 64
SSM_INNER = 1024
SSM_GROUPS = 2
SSM_STATE = 64
SSM_CONV = 7
SSM_CONV_DIM = 1280
D_FF = 2816
EPS = 1e-6
LAYER = 0
LAM_INIT = 0.8 - 0.6 * math.exp(-0.3 * LAYER)
LOG2E = math.log2(math.e)
Q_SCALE = ATT_DH ** -0.5 * LOG2E
NORM_SLACK = 1.02
MAX_SHIFT_GAP = 100.0

LANES = 128
BF16_ROWS = 16
VMEM_LIMIT = 56 * 1024 * 1024

FF_TILE = 256
N_FF = D_FF // FF_TILE
DT_PAD = LANES
D_IN_PAD = 2 * ATT_QK + SSM_INNER + SSM_CONV_DIM + DT_PAD
T5_BAND = 91

ROW_TILE = 512
ATT_TQ = 1024
ATT_TK = 512
TAB_TQ = 512
ATT_COLS = 256
SSD_CHUNK = 128
MXU_TAPS = (0, 1, 5)
SSD_SUB = 8
HALO = BF16_ROWS
VT_ROWS = ATT_DV + BF16_ROWS


def _rmsnorm(x, w):
    ms = jnp.mean(x * x, axis=-1, keepdims=True)
    return x * lax.rsqrt(ms + EPS) * w


def _resident(shape):
    nd = len(shape)
    return pl.BlockSpec(shape, lambda *_: (0,) * nd, pipeline_mode=pl.Buffered(1))


def _ffn_kernel(*refs, has_mix, has_final):
    it = iter(refs)
    h_ref = next(it)
    if has_mix:
        att_ref, ssm_ref, wo_ref = next(it), next(it), next(it)
    nw_ref, wg_ref, wu_ref, wd_ref = next(it), next(it), next(it), next(it)
    fw_ref = next(it) if has_final else None
    o_ref = next(it)

    h = h_ref[...]
    if has_mix:
        h = (h + jnp.dot(att_ref[...], wo_ref[0], preferred_element_type=F32)
             + jnp.dot(ssm_ref[...], wo_ref[1], preferred_element_type=F32))
    u = _rmsnorm(h, nw_ref[...]).astype(BF16)
    acc = jnp.zeros_like(h)
    for j in range(N_FF):
        ff = slice(j * FF_TILE, (j + 1) * FF_TILE)
        g = jnp.dot(u, wg_ref[:, ff], preferred_element_type=F32)
        up = jnp.dot(u, wu_ref[:, ff], preferred_element_type=F32)
        a = (g * jax.nn.sigmoid(g) * up).astype(BF16)
        acc = acc + jnp.dot(a, wd_ref[ff, :], preferred_element_type=F32)
    h = h + 0.5 * acc
    if has_final:
        h = _rmsnorm(h, fw_ref[...])
    o_ref[...] = h


def _ffn_call(h, norm_w, wg, wu, wd, mix=None, final_w=None):
    n = h.shape[0]
    tm = min(ROW_TILE, n)
    assert n % tm == 0
    row = lambda width: pl.BlockSpec((tm, width), lambda i: (i, 0))
    args, specs = [h], [row(D_MODEL)]
    if mix is not None:
        att, ssm, wo = mix
        args += [att, ssm, wo]
        specs += [row(ATT_V), row(SSM_INNER), _resident(wo.shape)]
    args += [norm_w, wg, wu, wd]
    specs += [_resident(norm_w.shape), _resident(wg.shape), _resident(wu.shape), _resident(wd.shape)]
    if final_w is not None:
        args.append(final_w)
        specs.append(_resident(final_w.shape))
    return pl.pallas_call(
        functools.partial(_ffn_kernel, has_mix=mix is not None, has_final=final_w is not None),
        grid=(n // tm,),
        in_specs=specs,
        out_specs=row(D_MODEL),
        out_shape=jax.ShapeDtypeStruct((n, D_MODEL), F32),
        compiler_params=pltpu.CompilerParams(dimension_semantics=("arbitrary",), vmem_limit_bytes=VMEM_LIMIT),
        name="ffn_mix" if mix is not None else "ffn",
    )(*args)


_IN_SEGS = (("q", 0, ATT_QK), ("k", ATT_QK, ATT_QK), ("z", 2 * ATT_QK, SSM_INNER),
            ("xbc", 2 * ATT_QK + SSM_INNER, SSM_CONV_DIM), ("dt", D_IN_PAD - DT_PAD, DT_PAD))


def _inproj_kernel(h_ref, nw_ref, win_ref, wvt_ref, q_ref, k_ref, vt_ref, z_ref, xbc_ref, dt_ref):
    u = _rmsnorm(h_ref[...], nw_ref[...]).astype(BF16)
    outs = dict(q=q_ref, k=k_ref, z=z_ref, xbc=xbc_ref, dt=dt_ref)
    for name, c0, width in _IN_SEGS:
        o_ref = outs[name]
        step = 512 if width % 512 == 0 else (256 if width % 256 == 0 else LANES)
        for s in range(0, width, step):
            r = jnp.dot(u, win_ref[:, c0 + s:c0 + s + step], preferred_element_type=F32)
            if name == "q":
                r = r * Q_SCALE
            o_ref[:, s:s + step] = r.astype(o_ref.dtype)
    nt_dims = (((1,), (1,)), ((), ()))
    ones = jnp.ones((VT_ROWS - ATT_DV, u.shape[0]), vt_ref.dtype)
    for s in range(0, ATT_V, 256):
        r = lax.dot_general(wvt_ref[s:s + 256, :], u, nt_dims, preferred_element_type=F32).astype(vt_ref.dtype)
        for hh in range(256 // ATT_DV):
            head = s // ATT_DV + hh
            vt_ref[0, head * VT_ROWS:head * VT_ROWS + ATT_DV, :] = r[hh * ATT_DV:(hh + 1) * ATT_DV]
            vt_ref[0, head * VT_ROWS + ATT_DV:(head + 1) * VT_ROWS, :] = ones


def _inproj_call(h, norm_w, win, wvt):
    n = h.shape[0]
    tm = min(ATT_TK, n)
    assert n % tm == 0
    row = lambda width: pl.BlockSpec((tm, width), lambda i: (i, 0))
    widths = (ATT_QK, ATT_QK, SSM_INNER, SSM_CONV_DIM, DT_PAD)
    dtypes = (BF16, BF16, BF16, BF16, F32)
    shapes = [jax.ShapeDtypeStruct((n, w), dt) for w, dt in zip(widths, dtypes)]
    specs = [row(w) for w in widths]
    shapes.insert(2, jax.ShapeDtypeStruct((n // tm, N_ATT_HEADS * VT_ROWS, tm), BF16))
    specs.insert(2, pl.BlockSpec((1, N_ATT_HEADS * VT_ROWS, tm), lambda i: (i, 0, 0)))
    return pl.pallas_call(
        _inproj_kernel,
        grid=(n // tm,),
        in_specs=[row(D_MODEL), _resident(norm_w.shape), _resident(win.shape), _resident(wvt.shape)],
        out_specs=specs,
        out_shape=shapes,
        compiler_params=pltpu.CompilerParams(dimension_semantics=("arbitrary",), vmem_limit_bytes=VMEM_LIMIT),
        name="inproj",
    )(h, norm_w, win, wvt)


def _t5_bias(rel, rb_ref, head):
    half = NUM_BUCKETS // 2
    max_exact = half // 2
    ret = jnp.where(rel > 0, half, 0)
    n = jnp.abs(rel)
    nf = jnp.maximum(n, 1).astype(F32)
    large = max_exact + (jnp.log(nf / max_exact) / math.log(MAX_DISTANCE / max_exact)
                         * (half - max_exact)).astype(jnp.int32)
    large = jnp.minimum(large, half - 1)
    bucket = ret + jnp.where(n < max_exact, n, large)
    val = jnp.zeros(rel.shape, F32)
    for jb in range(NUM_BUCKETS):
        val = jnp.where(bucket == jb, rb_ref[jb, head], val)
    return val * LOG2E


def _bias_geometry(tq, tk):
    unit = min(tq, tk)
    assert tq % unit == 0 and tk % unit == 0 and unit >= T5_BAND + 1
    return unit, tk // unit, tq // unit


def _bias_kernel(rb_ref, tab_ref, mtab_ref, rng_ref, *, tq, tk):
    head = pl.program_id(0)
    unit, lo, hi = _bias_geometry(TAB_TQ, tk)
    n_near = lo + hi + 1
    top = rb_ref[0, head]
    bottom = rb_ref[0, head]
    for jb in range(1, NUM_BUCKETS):
        top = jnp.maximum(top, rb_ref[jb, head])
        bottom = jnp.minimum(bottom, rb_ref[jb, head])
    rng_ref[0, 0:1, :] = jnp.full((1, LANES), top * LOG2E, F32)
    rng_ref[0, 1:2, :] = jnp.full((1, LANES), bottom * LOG2E, F32)
    rng_ref[0, 2:8, :] = jnp.zeros((6, LANES), F32)
    far_left = rb_ref[NUM_BUCKETS // 2 - 1, head] * LOG2E
    far_right = rb_ref[NUM_BUCKETS - 1, head] * LOG2E
    krow = lax.broadcasted_iota(jnp.int32, (LANES, LANES), 0)
    qcol = lax.broadcasted_iota(jnp.int32, (LANES, LANES), 1)
    for t in range(n_near):
        for a in range(tk // LANES):
            for b in range(TAB_TQ // LANES):
                base = (a - b) * LANES + (t - lo) * unit
                blk = (slice(a * LANES, (a + 1) * LANES), slice(b * LANES, (b + 1) * LANES))
                if base + LANES - 1 <= -T5_BAND:
                    tab_ref[(0, t) + blk] = jnp.full((LANES, LANES), far_left, F32)
                elif base - LANES + 1 >= T5_BAND:
                    tab_ref[(0, t) + blk] = jnp.full((LANES, LANES), far_right, F32)
                else:
                    tab_ref[(0, t) + blk] = _t5_bias(krow - qcol + base, rb_ref, head)
    tab_ref[0, n_near] = jnp.full((tk, TAB_TQ), far_left, F32)
    tab_ref[0, n_near + 1] = jnp.full((tk, TAB_TQ), far_right, F32)
    mrow = lax.broadcasted_iota(jnp.int32, (N_META, tq), 0)
    mcol = lax.broadcasted_iota(jnp.int32, (N_META, tq), 1)
    mtab_ref[0, 0] = _t5_bias(mrow - N_META - mcol, rb_ref, head)
    mtab_ref[0, 1] = jnp.full((N_META, tq), far_left, F32)


def _bias_call(rel_bias, tq, tk):
    assert tq % TAB_TQ == 0
    _, lo, hi = _bias_geometry(TAB_TQ, tk)
    nt = lo + hi + 3
    return pl.pallas_call(
        functools.partial(_bias_kernel, tq=tq, tk=tk),
        grid=(N_ATT_HEADS,),
        in_specs=[pl.BlockSpec(memory_space=pltpu.SMEM)],
        out_specs=[pl.BlockSpec((1, nt, tk, TAB_TQ), lambda h: (h, 0, 0, 0)),
                   pl.BlockSpec((1, 2, N_META, tq), lambda h: (h, 0, 0, 0)),
                   pl.BlockSpec((1, 8, LANES), lambda h: (h, 0, 0))],
        out_shape=[jax.ShapeDtypeStruct((N_ATT_HEADS, nt, tk, TAB_TQ), F32),
                   jax.ShapeDtypeStruct((N_ATT_HEADS, 2, N_META, tq), F32),
                   jax.ShapeDtypeStruct((N_ATT_HEADS, 8, LANES), F32)],
        compiler_params=pltpu.CompilerParams(dimension_semantics=("arbitrary",)),
        name="t5_bias",
    )(rel_bias)


def _attn_kernel(lam_ref, qall_ref, k_ref, vt_ref, km_ref, vmt_ref, tab_ref, mtab_ref, rng_ref, sw_ref, o_ref,
                 sa_ref, sb_ref, mca_ref, mcb_ref, m_scr, acc_scr, accp_scr, shift_scr, flag_scr,
                 *, tq, tk, nkc, nq, n_steps):
    g = pl.program_id(0)
    qi = jnp.minimum(g, n_steps - 1) % nq
    unit, lo, hi = _bias_geometry(TAB_TQ, tk)
    n_near = lo + hi + 1
    nt_dims = (((1,), (1,)), ((), ()))
    n_col = 2 * tq // ATT_COLS

    def finalize_previous():
        acc = accp_scr[...]
        o = acc[:ATT_DV] / acc[ATT_DV:ATT_DV + 1]
        lv = lam_ref[...]
        lam = (jnp.exp(jnp.sum(lv[0:1] * lv[1:2], axis=1, keepdims=True))
               - jnp.exp(jnp.sum(lv[2:3] * lv[3:4], axis=1, keepdims=True)) + LAM_INIT)
        out = o[:, :tq] - lam * o[:, tq:]
        ms = jnp.mean(out * out, axis=0, keepdims=True)
        out = out * lax.rsqrt(ms + EPS) * sw_ref[...] * (1.0 - LAM_INIT)
        o_ref[...] = out.T.astype(o_ref.dtype)

    @pl.when(g == 0)
    def _():
        accp_scr[...] = jnp.ones(accp_scr.shape, F32)

    @pl.when(g < n_steps)
    def _():
        q = qall_ref[pl.ds(pl.multiple_of(qi * tq, tq), tq), :]
        lane = lax.broadcasted_iota(jnp.int32, (tq, LANES), 1)
        zero = jnp.zeros_like(q)
        q2 = jnp.concatenate([jnp.where(lane < ATT_DH, q, zero), jnp.where(lane >= ATT_DH, q, zero)], axis=0)

        def bias_tile(j, c):
            q0 = (c * ATT_COLS) % tq
            du = j * (tk // unit) - (qi * (tq // TAB_TQ) + q0 // TAB_TQ) * (TAB_TQ // unit)
            idx = jnp.where(du < -lo, n_near, jnp.where(du > hi, n_near + 1, du + lo))
            return tab_ref[0, idx, :, pl.ds(q0 % TAB_TQ, ATT_COLS)]

        def add_bias(s, b):
            return jnp.concatenate([s[:, :tq] + b, s[:, tq:] + b], axis=1)

        @pl.when(qi == 0)
        def _():
            half = jnp.where(lax.broadcasted_iota(jnp.int32, (LANES, LANES), 0) // ATT_DH
                             == lax.broadcasted_iota(jnp.int32, (LANES, LANES), 1), 1.0, 0.0).astype(BF16)

            def max_sq_norm(x_ref):
                x = x_ref[...]
                sq = jnp.dot(x * x, half, preferred_element_type=F32)
                return jnp.max(sq, axis=0, keepdims=True)

            bound2 = max_sq_norm(qall_ref) * jnp.maximum(max_sq_norm(k_ref), max_sq_norm(km_ref))
            lane_row = lax.broadcasted_iota(jnp.int32, (1, LANES), 1)
            qk_bound = [NORM_SLACK * jnp.sqrt(jnp.max(jnp.where(lane_row == mp, bound2, 0.0), axis=1, keepdims=True))
                        for mp in range(2)]
            hi_b = rng_ref[0, 0:1, 0:1]
            lo_b = rng_ref[0, 1:2, 0:1]
            col = lax.broadcasted_iota(jnp.int32, (1, 2 * tq), 1)
            shift_scr[...] = jnp.where(col < tq, qk_bound[0], qk_bound[1]) + hi_b
            worst_gap = 2.0 * jnp.maximum(qk_bound[0], qk_bound[1]) + (hi_b - lo_b)
            flag_scr[0] = (worst_gap[0, 0] <= MAX_SHIFT_GAP).astype(jnp.int32)

        shift = shift_scr[...]
        bounded = flag_scr[0] == 1

        @pl.when(bounded)
        def _():
            finalize_previous()
            sm = lax.dot_general(km_ref[...], q2, nt_dims, preferred_element_type=F32)
            sm = add_bias(sm, mtab_ref[0, jnp.minimum(qi, 1)])
            acc_scr[...] = jnp.dot(vmt_ref[...], jnp.exp2(sm - shift).astype(BF16), preferred_element_type=F32)

            def stage_logits(u, j, c):
                buf = (sa_ref, sb_ref)[u % 2]
                buf[:, pl.ds(c * ATT_COLS, ATT_COLS)] = lax.dot_general(
                    k_ref[pl.ds(j * tk, tk), :], q2[c * ATT_COLS:(c + 1) * ATT_COLS], nt_dims,
                    preferred_element_type=F32)

            units = [(j, c) for j in range(nkc) for c in range(n_col)]
            stage_logits(0, *units[0])
            for u, (j, c) in enumerate(units):
                if u + 1 < len(units):
                    stage_logits(u + 1, *units[u + 1])
                cols = pl.ds(c * ATT_COLS, ATT_COLS)
                s = (sa_ref, sb_ref)[u % 2][:, cols] + bias_tile(j, c)
                p = jnp.exp2(s - shift[:, c * ATT_COLS:(c + 1) * ATT_COLS]).astype(BF16)
                acc_scr[:, cols] += jnp.dot(vt_ref[j], p, preferred_element_type=F32)
            accp_scr[...] = acc_scr[...]

        @pl.when(jnp.logical_not(bounded))
        def _():
            finalize_previous()

            def produce(j, c, s_ref, mc_ref):
                cols = pl.ds(c * ATT_COLS, ATT_COLS)
                s = lax.dot_general(k_ref[pl.ds(j * tk, tk), :], q2[c * ATT_COLS:(c + 1) * ATT_COLS], nt_dims,
                                    preferred_element_type=F32)
                s = s + bias_tile(j, c)
                s_ref[:, cols] = s
                mc_ref[:, cols] = jnp.max(s, axis=0, keepdims=True)

            def consume(s, m_cur, vt, cols, first=False):
                if first:
                    m_new = m_cur
                else:
                    m_prev = m_scr[:, cols]
                    m_new = jnp.maximum(m_prev, m_cur)
                    alpha = jnp.exp2(m_prev - m_new)
                p = jnp.exp2(s - m_new).astype(BF16)
                pv = jnp.dot(vt, p, preferred_element_type=F32)
                acc_scr[:, cols] = pv if first else alpha * acc_scr[:, cols] + pv
                m_scr[:, cols] = m_new

            sm = lax.dot_general(km_ref[...], q2, nt_dims, preferred_element_type=F32)
            sm = add_bias(sm, mtab_ref[0, jnp.minimum(qi, 1)])
            consume(sm, jnp.max(sm, axis=0, keepdims=True), vmt_ref[...], pl.ds(0, 2 * tq), first=True)

            bufs = ((sa_ref, mca_ref), (sb_ref, mcb_ref))
            for c in range(n_col):
                produce(0, c, *bufs[0])
            for j in range(nkc):
                s_ref, mc_ref = bufs[j % 2]
                for c in range(n_col):
                    cols = pl.ds(c * ATT_COLS, ATT_COLS)
                    if j + 1 < nkc:
                        produce(j + 1, c, *bufs[(j + 1) % 2])
                    consume(s_ref[:, cols], mc_ref[:, cols], vt_ref[j], cols)
            accp_scr[...] = acc_scr[...]

    @pl.when(g == n_steps)
    def _():
        finalize_previous()


def _attn_call(lamv, q, k, vt, km, vmt, tab, mtab, rng, subw_col, batch, seq, tq, tk):
    n = q.shape[0]
    assert n == batch * seq and seq % (2 * tk) == 0 and seq % tq == 0 and vt.shape[2] == tk
    nq = seq // tq
    nkc = seq // tk
    nt = tab.shape[1]
    n_steps = N_ATT_HEADS * batch * nq

    def tile(g):
        g = jnp.minimum(g, n_steps - 1)
        return g // (batch * nq), (g // nq) % batch, g % nq

    def cur(f):
        return lambda g: f(*tile(g))

    def prev(f):
        return lambda g: f(*tile(jnp.maximum(g - 1, 0)))

    return pl.pallas_call(
        functools.partial(_attn_kernel, tq=tq, tk=tk, nkc=nkc, nq=nq, n_steps=n_steps),
        grid=(n_steps + 1,),
        in_specs=[
            pl.BlockSpec(lamv.shape, lambda g: (0, 0)),
            pl.BlockSpec((seq, LANES), cur(lambda h, b, i: (b, h))),
            pl.BlockSpec((seq, LANES), cur(lambda h, b, i: (b, h))),
            pl.BlockSpec((nkc, VT_ROWS, tk), cur(lambda h, b, i: (b, h, 0))),
            pl.BlockSpec((N_META, LANES), cur(lambda h, b, i: (0, h))),
            pl.BlockSpec((VT_ROWS, N_META), cur(lambda h, b, i: (h, 0))),
            pl.BlockSpec((1, nt, tk, TAB_TQ), cur(lambda h, b, i: (h, 0, 0, 0))),
            pl.BlockSpec((1, 2, N_META, tq), cur(lambda h, b, i: (h, 0, 0, 0))),
            pl.BlockSpec((1, 8, LANES), cur(lambda h, b, i: (h, 0, 0))),
            pl.BlockSpec(subw_col.shape, lambda g: (0, 0)),
        ],
        out_specs=pl.BlockSpec((tq, LANES), prev(lambda h, b, i: (b * nq + i, h))),
        out_shape=jax.ShapeDtypeStruct((n, ATT_V), BF16),
        scratch_shapes=[pltpu.VMEM((tk, 2 * tq), F32)] * 2 + [pltpu.VMEM((1, 2 * tq), F32)] * 3
        + [pltpu.VMEM((VT_ROWS, 2 * tq), F32)] * 2 + [pltpu.VMEM((1, 2 * tq), F32), pltpu.SMEM((1,), jnp.int32)],
        compiler_params=pltpu.CompilerParams(dimension_semantics=("arbitrary",), vmem_limit_bytes=VMEM_LIMIT),
        name="diff_attn",
    )(lamv, q, k, vt, km, vmt, tab, mtab, rng, subw_col)


def _split3(x):
    hi = x.astype(BF16)
    r1 = x - hi.astype(F32)
    mid = r1.astype(BF16)
    lo = (r1 - mid.astype(F32)).astype(BF16)
    return hi, mid, lo


def _cumsum_rows(a):
    rows = a.shape[0]
    r_i = lax.broadcasted_iota(jnp.int32, (rows, rows), 0)
    c_i = lax.broadcasted_iota(jnp.int32, (rows, rows), 1)
    tri = jnp.where(c_i <= r_i, 1.0, 0.0).astype(BF16)
    out = None
    for term in _split3(a):
        part = jnp.dot(tri, term, preferred_element_type=F32)
        out = part if out is None else out + part
    return out


def _expand_rows(parts, sel_ref):
    masked = []
    for w, first in parts:
        lane = lax.broadcasted_iota(jnp.int32, w.shape, 1)
        masked.append(jnp.where((lane >= first) & (lane < first + SSM_HEADS), w, 0.0))
    stacked = jnp.concatenate(masked, axis=0)
    hi = stacked.astype(BF16)
    lo = (stacked - hi.astype(F32)).astype(BF16)
    sel = sel_ref[...]
    full = jnp.dot(hi, sel, preferred_element_type=F32) + jnp.dot(lo, sel, preferred_element_type=F32)
    outs, r0 = [], 0
    for w, _ in parts:
        outs.append(full[r0:r0 + w.shape[0]])
        r0 += w.shape[0]
    return outs


def _softplus(x):
    return jnp.maximum(x, 0.0) + jnp.log(1.0 + jnp.exp(-jnp.abs(x)))


GROUP_COLS = SSM_INNER // SSM_GROUPS


def _state_update(b_t, xw):
    return jnp.concatenate(
        [jnp.dot(b_t[g * SSM_STATE:(g + 1) * SSM_STATE], xw[:, g * GROUP_COLS:(g + 1) * GROUP_COLS],
                 preferred_element_type=F32) for g in range(SSM_GROUPS)], axis=0)


def _stack_decay(dec_row):
    return jnp.concatenate(
        [jnp.broadcast_to(dec_row[:, g * GROUP_COLS:(g + 1) * GROUP_COLS], (SSM_STATE, GROUP_COLS))
         for g in range(SSM_GROUPS)], axis=0)


def _conv_silu(win, shift_ref, cw_ref, cb_ref, rows):
    total = rows + 2 * HALO
    assert win.shape[0] == total and shift_ref.shape == (len(MXU_TAPS) * rows, total)
    shifted = jnp.dot(shift_ref[...], win, preferred_element_type=F32)
    win32 = win.astype(F32)
    acc = jnp.broadcast_to(cb_ref[...], (rows, SSM_CONV_DIM))
    for j in range(SSM_CONV):
        off = j - SSM_CONV // 2
        if j in MXU_TAPS:
            tap = shifted[MXU_TAPS.index(j) * rows:(MXU_TAPS.index(j) + 1) * rows]
        elif off == 0:
            tap = win32[HALO:HALO + rows]
        else:
            tap = pltpu.roll(win32, (total - off) % total, axis=0)[HALO:HALO + rows]
        acc = acc + cw_ref[j:j + 1, :] * tap
    return acc * jax.nn.sigmoid(acc)


def _shift_matrix(rows):
    offs = jnp.array([j - SSM_CONV // 2 for j in MXU_TAPS])
    t = jnp.arange(rows)
    src = HALO + t[None, :] + offs[:, None]
    return (src.reshape(-1)[:, None] == jnp.arange(rows + 2 * HALO)[None, :]).astype(BF16)


def _ssd_kernel(z_ref, xc_ref, xl_ref, xr_ref, dt_ref, mx_ref, mdt_ref, cw_ref, cb_ref, dtb_ref, alog_ref,
                dsk_ref, nw_ref, sel_ref, shc_ref, shm_ref, o_ref, xs_scr, dts_scr, cum_scr, hbs_scr, hf_scr, hb_scr,
                win_scr,
                *, cs, sub, nb):
    rows = cs * sub
    ph = pl.program_id(1)
    t = pl.program_id(2)
    fwd0, bwd0 = 0, SSM_HEADS
    a_row = -jnp.exp(alog_ref[...])

    def decay_terms(dt_raw):
        dt = _softplus(dt_raw + dtb_ref[...])
        return dt, _cumsum_rows(dt * a_row)

    def bcast8(row):
        return jnp.broadcast_to(row, (8, LANES))

    @pl.when(ph == 0)
    def _():
        blk = nb - 1 - t

        @pl.when(t == 0)
        def _():
            hb_scr[...] = jnp.zeros(hb_scr.shape, F32)

        left = jnp.where(blk == 0, mx_ref[...], xl_ref[...])
        right = jnp.where(blk == nb - 1, jnp.zeros_like(xr_ref[...]), xr_ref[...])
        win_scr[0:HALO, :] = left
        win_scr[HALO:HALO + rows, :] = xc_ref[...]
        win_scr[HALO + rows:HALO + rows + HALO, :] = right

        hb = hb_scr[...]
        for si in reversed(range(sub)):
            cc = blk * sub + si
            xbc = _conv_silu(win_scr[si * cs:si * cs + cs + 2 * HALO, :], shc_ref, cw_ref, cb_ref, cs)
            xs_scr[cc] = xbc.astype(BF16)
            dt, cum = decay_terms(dt_ref[si * cs:(si + 1) * cs, :])
            dts_scr[cc] = dt
            cum_scr[cc] = cum
            eb = cum - dt * a_row
            w_b, dec = _expand_rows([(jnp.exp(eb) * dt, bwd0), (bcast8(jnp.exp(cum[cs - 1:cs, :])), bwd0)],
                                    sel_ref)
            xw = (xbc[:, :SSM_INNER] * w_b).astype(BF16)
            bm_t = xbc[:, SSM_INNER:SSM_INNER + LANES].T.astype(BF16)
            hbs_scr[cc] = hb.astype(BF16)
            hb = hb * _stack_decay(dec[0:1]) + _state_update(bm_t, xw)
        hb_scr[...] = hb

    @pl.when(ph == 1)
    def _():
        @pl.when(t == 0)
        def _():
            wm = jnp.concatenate([jnp.zeros((HALO, SSM_CONV_DIM), BF16), mx_ref[...], xc_ref[0:HALO, :]], axis=0)
            xm = _conv_silu(wm, shm_ref, cw_ref, cb_ref, N_META)
            dtm, cumm = decay_terms(mdt_ref[...])
            (w_m,) = _expand_rows([(jnp.exp(cumm[N_META - 1:N_META, :] - cumm) * dtm, fwd0)], sel_ref)
            xwm = (xm[:, :SSM_INNER] * w_m).astype(BF16)
            bmm_t = xm[:, SSM_INNER:SSM_INNER + LANES].T.astype(BF16)
            hf_scr[...] = _state_update(bmm_t, xwm)

        lane = lax.broadcasted_iota(jnp.int32, (cs, LANES), 1)
        l_i = lax.broadcasted_iota(jnp.int32, (cs, cs), 0)
        s_i = lax.broadcasted_iota(jnp.int32, (cs, cs), 1)
        lower = s_i <= l_i
        diag = s_i == l_i
        hpg = SSM_HEADS // SSM_GROUPS
        zx = jnp.zeros((cs, LANES), BF16)
        nt_dims = (((1,), (1,)), ((), ()))

        hf = hf_scr[...]
        for si in range(sub):
            cc = t * sub + si
            xbc = xs_scr[cc]
            x_bf = xbc[:, :SSM_INNER]
            bm = xbc[:, SSM_INNER:SSM_INNER + LANES]
            cm = xbc[:, SSM_INNER + LANES:SSM_INNER + 2 * LANES]
            x = x_bf.astype(F32)

            dt = dts_scr[cc]
            cum = cum_scr[cc]
            eb = cum - dt * a_row
            dt_t, cum_t, eb_t = dt.T, cum.T, eb.T
            last = cum[cs - 1:cs, :]

            c_grp = [jnp.where(lane // SSM_STATE == g, cm, jnp.zeros_like(cm)) for g in range(SSM_GROUPS)]
            g_mats = [lax.dot_general(c_g, bm, nt_dims, preferred_element_type=F32) for c_g in c_grp]

            pieces = []
            for hp in range(SSM_HEADS // 2):
                w_pair = []
                for h in (2 * hp, 2 * hp + 1):
                    arg_f = cum[:, fwd0 + h:fwd0 + h + 1] - cum_t[fwd0 + h:fwd0 + h + 1, :]
                    arg_b = eb_t[bwd0 + h:bwd0 + h + 1, :] - eb[:, bwd0 + h:bwd0 + h + 1]
                    e = jnp.exp(jnp.minimum(jnp.where(lower, arg_f, arg_b), 0.0))
                    dt_f_row = dt_t[fwd0 + h:fwd0 + h + 1, :]
                    dt_b_row = dt_t[bwd0 + h:bwd0 + h + 1, :]
                    m = e * jnp.where(lower, dt_f_row, dt_b_row) + jnp.where(diag, dt_b_row, 0.0)
                    w_pair.append((g_mats[h // hpg] * m).astype(BF16))
                xp = x_bf[:, hp * LANES:(hp + 1) * LANES]
                rhs = jnp.concatenate([jnp.where(lane < SSM_HEADDIM, xp, zx),
                                       jnp.where(lane >= SSM_HEADDIM, xp, zx)], axis=0)
                pieces.append(jnp.dot(jnp.concatenate(w_pair, axis=1), rhs, preferred_element_type=F32))
            y = jnp.concatenate(pieces, axis=1)

            d_f, d_b, w_f, dec = _expand_rows(
                [(jnp.exp(cum), fwd0), (jnp.exp(last - eb), bwd0), (jnp.exp(last - cum) * dt, fwd0),
                 (bcast8(jnp.exp(last)), fwd0)], sel_ref)
            hf_bf = hf.astype(BF16)
            hb_bf = hbs_scr[cc]
            y = y + d_f * jnp.concatenate([jnp.dot(c_g, hf_bf, preferred_element_type=F32) for c_g in c_grp],
                                          axis=1)
            y = y + d_b * jnp.concatenate([jnp.dot(c_g, hb_bf, preferred_element_type=F32) for c_g in c_grp],
                                          axis=1)
            y = y + x * dsk_ref[...]

            xw = (x * w_f).astype(BF16)
            hf = hf * _stack_decay(dec[0:1]) + _state_update(bm.astype(F32).T.astype(BF16), xw)

            zf = z_ref[si * cs:(si + 1) * cs, :].astype(F32)
            y = y * (zf * jax.nn.sigmoid(zf))
            o_ref[si * cs:(si + 1) * cs, :] = _rmsnorm(y, nw_ref[...]).astype(o_ref.dtype)
        hf_scr[...] = hf


def _ssd_call(z, xbc, dt, mxbc, mdt, cw, cb, dtb, alog, dskip, nw, sel, shc, shm, batch, seq, cs, sub):
    n = z.shape[0]
    rows = cs * sub
    assert n == batch * seq and seq % rows == 0 and cs % HALO == 0
    nc = seq // cs
    nb = seq // rows
    hpb = rows // HALO
    n_halo = n // HALO

    def ph0_block(ph, t):
        return (1 - ph) * (nb - 1 - t)

    const2 = lambda shape: pl.BlockSpec(shape, lambda b, ph, t: (0, 0))
    return pl.pallas_call(
        functools.partial(_ssd_kernel, cs=cs, sub=sub, nb=nb),
        grid=(batch, 2, nb),
        in_specs=[
            pl.BlockSpec((rows, SSM_INNER), lambda b, ph, t: (b * nb + ph * t, 0)),
            pl.BlockSpec((rows, SSM_CONV_DIM), lambda b, ph, t: (b * nb + ph0_block(ph, t), 0)),
            pl.BlockSpec((HALO, SSM_CONV_DIM),
                         lambda b, ph, t: (jnp.maximum((b * nb + ph0_block(ph, t)) * hpb - 1, 0), 0)),
            pl.BlockSpec((HALO, SSM_CONV_DIM),
                         lambda b, ph, t: (jnp.minimum((b * nb + ph0_block(ph, t) + 1) * hpb, n_halo - 1), 0)),
            pl.BlockSpec((rows, DT_PAD), lambda b, ph, t: (b * nb + ph0_block(ph, t), 0)),
            const2(mxbc.shape), const2(mdt.shape), const2(cw.shape), const2(cb.shape), const2(dtb.shape),
            const2(alog.shape), const2(dskip.shape), const2(nw.shape), const2(sel.shape), const2(shc.shape),
            const2(shm.shape),
        ],
        out_specs=pl.BlockSpec((rows, SSM_INNER), lambda b, ph, t: (b * nb + ph * t, 0)),
        out_shape=jax.ShapeDtypeStruct((n, SSM_INNER), BF16),
        scratch_shapes=[
            pltpu.VMEM((nc, cs, SSM_CONV_DIM), BF16),
            pltpu.VMEM((nc, cs, DT_PAD), F32),
            pltpu.VMEM((nc, cs, DT_PAD), F32),
            pltpu.VMEM((nc, LANES, GROUP_COLS), BF16),
            pltpu.VMEM((LANES, GROUP_COLS), F32),
            pltpu.VMEM((LANES, GROUP_COLS), F32),
            pltpu.VMEM((rows + 2 * HALO, SSM_CONV_DIM), BF16),
        ],
        compiler_params=pltpu.CompilerParams(dimension_semantics=("arbitrary",) * 3, vmem_limit_bytes=VMEM_LIMIT),
        name="bi_ssd",
    )(z, xbc, xbc, xbc, dt, mxbc, mdt, cw, cb, dtb, alog, dskip, nw, sel, shc, shm)


def _head_selector():
    k = jnp.arange(LANES)[:, None]
    col = jnp.arange(SSM_INNER)[None, :]
    return ((k % SSM_HEADS == col // SSM_HEADDIM) & (k < 2 * SSM_HEADS)).astype(BF16)


def _prep_weights(ffn1_norm_w, ffn1_w_gate, ffn1_w_up, ffn1_w_down, mix_norm_w, w_in, lambda_q1, lambda_k1,
                  lambda_q2, lambda_k2, attn_subln_w, conv_w, conv_b, dt_bias_fwd, dt_bias_bwd, a_log_fwd,
                  a_log_bwd, ssm_d, ssm_norm_w, w_out, ffn2_norm_w, ffn2_w_gate, ffn2_w_up, ffn2_w_down,
                  final_norm_w):
    def ffn(norm_w, wg, wu, wd):
        return norm_w[0][None, :], wg[0].astype(BF16), wu[0].astype(BF16), wd[0].astype(BF16)

    pad_lanes = lambda v, width: jnp.pad(v, (0, width - v.shape[0]))[None, :]
    o_v, o_z = 2 * ATT_QK, 2 * ATT_QK + ATT_V
    wi = w_in[0].astype(BF16)
    win = jnp.pad(jnp.concatenate([wi[:, :o_v], wi[:, o_z:]], axis=1),
                  ((0, 0), (0, D_IN_PAD - (w_in.shape[2] - ATT_V))))
    return dict(
        ffn1=ffn(ffn1_norm_w, ffn1_w_gate, ffn1_w_up, ffn1_w_down),
        ffn2=ffn(ffn2_norm_w, ffn2_w_gate, ffn2_w_up, ffn2_w_down),
        mix_norm=mix_norm_w[0][None, :],
        win=win,
        wvt=wi[:, o_v:o_z].T,
        lamv=jnp.stack([lambda_q1[0], lambda_k1[0], lambda_q2[0], lambda_k2[0]]),
        subw_col=attn_subln_w[0][:, None],
        cw=jnp.pad(conv_w[0], ((0, 8 - SSM_CONV), (0, 0))),
        cb=conv_b[0][None, :],
        dtb=pad_lanes(jnp.concatenate([dt_bias_fwd[0], dt_bias_bwd[0]]), DT_PAD),
        alog=pad_lanes(jnp.concatenate([a_log_fwd[0], a_log_bwd[0]]), DT_PAD),
        dskip=jnp.repeat(ssm_d[0], SSM_HEADDIM)[None, :],
        ssm_norm=ssm_norm_w[0][None, :],
        wo=w_out[0].astype(BF16).reshape(2, ATT_V, D_MODEL),
        final=final_norm_w[None, :],
        sel=_head_selector(),
        shc=_shift_matrix(SSD_CHUNK),
        shm=_shift_matrix(N_META),
    )


def _encode(x, w, meta_proj, bias_tabs):
    batch, seq, _ = x.shape
    km, vmt, mxbc, mdt = meta_proj
    h0 = x.reshape(batch * seq, D_MODEL)
    h1 = _ffn_call(h0, *w["ffn1"])
    q, k, vt, z, xbc, dt = _inproj_call(h1, w["mix_norm"], w["win"], w["wvt"])
    att = _attn_call(w["lamv"], q, k, vt, km, vmt, *bias_tabs, w["subw_col"], batch, seq, ATT_TQ, ATT_TK)
    ssm = _ssd_call(z, xbc, dt, mxbc, mdt, w["cw"], w["cb"], w["dtb"], w["alog"], w["dskip"], w["ssm_norm"],
                    w["sel"], w["shc"], w["shm"], batch, seq, SSD_CHUNK, SSD_SUB)
    y = _ffn_call(h1, *w["ffn2"], mix=(att, ssm, w["wo"]), final_w=w["final"])
    return y.reshape(batch, seq, D_MODEL)


def kernel(x_prompt, x_sample, meta_tokens, ffn1_norm_w, ffn1_w_gate, ffn1_w_up, ffn1_w_down, mix_norm_w, w_in, rel_bias, lambda_q1, lambda_k1, lambda_q2, lambda_k2, attn_subln_w, conv_w, conv_b, dt_bias_fwd, dt_bias_bwd, a_log_fwd, a_log_bwd, ssm_d, ssm_norm_w, w_out, ffn2_norm_w, ffn2_w_gate, ffn2_w_up, ffn2_w_down, final_norm_w):
    w = _prep_weights(ffn1_norm_w, ffn1_w_gate, ffn1_w_up, ffn1_w_down, mix_norm_w, w_in, lambda_q1, lambda_k1,
                      lambda_q2, lambda_k2, attn_subln_w, conv_w, conv_b, dt_bias_fwd, dt_bias_bwd, a_log_fwd,
                      a_log_bwd, ssm_d, ssm_norm_w, w_out, ffn2_norm_w, ffn2_w_gate, ffn2_w_up, ffn2_w_down,
                      final_norm_w)
    hm = _ffn_call(meta_tokens, *w["ffn1"])
    _, km, vmt, _, mxbc, mdt = _inproj_call(hm, w["mix_norm"], w["win"], w["wvt"])
    meta_proj = (km, vmt[0], mxbc, mdt)
    bias_tabs = _bias_call(rel_bias, ATT_TQ, ATT_TK)
    return (_encode(x_prompt, w, meta_proj, bias_tabs), _encode(x_sample, w, meta_proj, bias_tabs))
```

```python
import functools
import math

import jax
import jax.numpy as jnp
import numpy as np
from jax import lax
from jax.experimental import pallas as pl
from jax.experimental.pallas import tpu as pltpu

F32 = jnp.float32
BF16 = jnp.bfloat16

D_MODEL = 1024
N_META = 16
N_ATT_HEADS = 8
ATT_DH = 64
ATT_DV = 128
ATT_QK = 1024
ATT_V = 1024
NUM_BUCKETS = 32
MAX_DISTANCE = 128
SSM_HEADS = 16
SSM_HEADDIM = 64
SSM_INNER = 1024
SSM_GROUPS = 2
SSM_STATE = 64
SSM_CONV = 7
SSM_CONV_DIM = 1280
D_FF = 2816
EPS = 1e-6
LAYER = 0
LAM_INIT = 0.8 - 0.6 * math.exp(-0.3 * LAYER)
LOG2E = math.log2(math.e)
Q_SCALE = ATT_DH ** -0.5 * LOG2E
NORM_SLACK = 1.02
MAX_SHIFT_GAP = 100.0

LANES = 128
BF16_ROWS = 16
VMEM_LIMIT = 56 * 1024 * 1024

FF_TILE = 256
N_FF = D_FF // FF_TILE
DT_PAD = LANES
DT_COLS = 2 * SSM_HEADS
T5_BAND = 91

ROW_TILE = 512
ATT_TQ = 1024
ATT_TK = 512
TAB_TQ = 512
ATT_COLS = 256
SSD_CHUNK = 128
MXU_TAPS = (0, 1, 5)
SSD_SUB = 8
HALO = BF16_ROWS
VT_ROWS = ATT_DV + BF16_ROWS


def _rmsnorm(x, w):
    ms = jnp.mean(x * x, axis=-1, keepdims=True)
    return x * lax.rsqrt(ms + EPS) * w


def _resident(shape):
    nd = len(shape)
    return pl.BlockSpec(shape, lambda *_: (0,) * nd, pipeline_mode=pl.Buffered(1))


def _ffn_kernel(*refs, has_mix, has_final):
    it = iter(refs)
    h_ref = next(it)
    if has_mix:
        att_ref, ssm_ref, wo_ref = next(it), next(it), next(it)
    nw_ref, wg_ref, wu_ref, wd_ref = next(it), next(it), next(it), next(it)
    fw_ref = next(it) if has_final else None
    o_ref = next(it)

    h = h_ref[...]
    if has_mix:
        h = (h + jnp.dot(att_ref[...], wo_ref[0], preferred_element_type=F32)
             + jnp.dot(ssm_ref[...], wo_ref[1], preferred_element_type=F32))
    u = _rmsnorm(h, nw_ref[...]).astype(BF16)
    acc = jnp.zeros_like(h)
    for j in range(N_FF):
        ff = slice(j * FF_TILE, (j + 1) * FF_TILE)
        g = jnp.dot(u, wg_ref[:, ff], preferred_element_type=F32)
        up = jnp.dot(u, wu_ref[:, ff], preferred_element_type=F32)
        a = (g * jax.nn.sigmoid(g) * up).astype(BF16)
        acc = acc + jnp.dot(a, wd_ref[ff, :], preferred_element_type=F32)
    h = h + 0.5 * acc
    if has_final:
        h = _rmsnorm(h, fw_ref[...])
    o_ref[...] = h


def _ffn_call(h, norm_w, wg, wu, wd, mix=None, final_w=None):
    n = h.shape[0]
    tm = min(ROW_TILE, n)
    assert n % tm == 0
    row = lambda width: pl.BlockSpec((tm, width), lambda i: (i, 0))
    args, specs = [h], [row(D_MODEL)]
    if mix is not None:
        att, ssm, wo = mix
        args += [att, ssm, wo]
        specs += [row(ATT_V), row(SSM_INNER), _resident(wo.shape)]
    args += [norm_w, wg, wu, wd]
    specs += [_resident(norm_w.shape), _resident(wg.shape), _resident(wu.shape), _resident(wd.shape)]
    if final_w is not None:
        args.append(final_w)
        specs.append(_resident(final_w.shape))
    return pl.pallas_call(
        functools.partial(_ffn_kernel, has_mix=mix is not None, has_final=final_w is not None),
        grid=(n // tm,),
        in_specs=specs,
        out_specs=row(D_MODEL),
        out_shape=jax.ShapeDtypeStruct((n, D_MODEL), F32),
        compiler_params=pltpu.CompilerParams(dimension_semantics=("arbitrary",), vmem_limit_bytes=VMEM_LIMIT),
        name="ffn_mix" if mix is not None else "ffn",
    )(*args)


_IN_SEGS = (("q", 0, ATT_QK), ("k", ATT_QK, ATT_QK), ("z", 2 * ATT_QK + ATT_V, SSM_INNER),
            ("xbc", 2 * ATT_QK + ATT_V + SSM_INNER, SSM_CONV_DIM),
            ("dt", 2 * ATT_QK + ATT_V + SSM_INNER + SSM_CONV_DIM, DT_COLS))


def _inproj_kernel(h_ref, nw_ref, win_ref, wvt_ref, q_ref, k_ref, vt_ref, z_ref, xbc_ref, dt_ref):
    u = _rmsnorm(h_ref[...], nw_ref[...]).astype(BF16)
    outs = dict(q=q_ref, k=k_ref, z=z_ref, xbc=xbc_ref, dt=dt_ref)
    for name, c0, width in _IN_SEGS:
        o_ref = outs[name]
        step = 512 if width % 512 == 0 else (256 if width % 256 == 0 else width)
        for s in range(0, width, step):
            r = jnp.dot(u, win_ref[:, c0 + s:c0 + s + step], preferred_element_type=F32)
            if name == "q":
                r = r * Q_SCALE
            o_ref[:, s:s + step] = r.astype(o_ref.dtype)
    dt_ref[:, DT_COLS:] = jnp.zeros((dt_ref.shape[0], DT_PAD - DT_COLS), dt_ref.dtype)
    nt_dims = (((1,), (1,)), ((), ()))
    ones = jnp.ones((VT_ROWS - ATT_DV, u.shape[0]), vt_ref.dtype)
    for s in range(0, ATT_V, 256):
        r = lax.dot_general(wvt_ref[s:s + 256, :], u, nt_dims, preferred_element_type=F32).astype(vt_ref.dtype)
        for hh in range(256 // ATT_DV):
            head = s // ATT_DV + hh
            vt_ref[0, head * VT_ROWS:head * VT_ROWS + ATT_DV, :] = r[hh * ATT_DV:(hh + 1) * ATT_DV]
            vt_ref[0, head * VT_ROWS + ATT_DV:(head + 1) * VT_ROWS, :] = ones


def _inproj_call(h, norm_w, win, wvt):
    n = h.shape[0]
    tm = min(ATT_TK, n)
    assert n % tm == 0
    row = lambda width: pl.BlockSpec((tm, width), lambda i: (i, 0))
    widths = (ATT_QK, ATT_QK, SSM_INNER, SSM_CONV_DIM, DT_PAD)
    dtypes = (BF16, BF16, BF16, BF16, F32)
    shapes = [jax.ShapeDtypeStruct((n, w), dt) for w, dt in zip(widths, dtypes)]
    specs = [row(w) for w in widths]
    shapes.insert(2, jax.ShapeDtypeStruct((n // tm, N_ATT_HEADS * VT_ROWS, tm), BF16))
    specs.insert(2, pl.BlockSpec((1, N_ATT_HEADS * VT_ROWS, tm), lambda i: (i, 0, 0)))
    return pl.pallas_call(
        _inproj_kernel,
        grid=(n // tm,),
        in_specs=[row(D_MODEL), _resident(norm_w.shape), _resident(win.shape), _resident(wvt.shape)],
        out_specs=specs,
        out_shape=shapes,
        compiler_params=pltpu.CompilerParams(dimension_semantics=("arbitrary",), vmem_limit_bytes=VMEM_LIMIT),
        name="inproj",
    )(h, norm_w, win, wvt)


def _t5_bias(rel, rb_ref, head):
    half = NUM_BUCKETS // 2
    max_exact = half // 2
    ret = jnp.where(rel > 0, half, 0)
    n = jnp.abs(rel)
    nf = jnp.maximum(n, 1).astype(F32)
    large = max_exact + (jnp.log(nf / max_exact) / math.log(MAX_DISTANCE / max_exact)
                         * (half - max_exact)).astype(jnp.int32)
    large = jnp.minimum(large, half - 1)
    bucket = ret + jnp.where(n < max_exact, n, large)
    val = jnp.zeros(rel.shape, F32)
    for jb in range(NUM_BUCKETS):
        val = jnp.where(bucket == jb, rb_ref[jb, head], val)
    return val * LOG2E


def _bias_geometry(tq, tk):
    unit = min(tq, tk)
    assert tq % unit == 0 and tk % unit == 0 and unit >= T5_BAND + 1
    return unit, tk // unit, tq // unit


def _bias_kernel(rb_ref, tab_ref, mtab_ref, rng_ref, *, tq, tk):
    head = pl.program_id(0)
    unit, lo, hi = _bias_geometry(TAB_TQ, tk)
    n_near = lo + hi + 1
    top = rb_ref[0, head]
    bottom = rb_ref[0, head]
    for jb in range(1, NUM_BUCKETS):
        top = jnp.maximum(top, rb_ref[jb, head])
        bottom = jnp.minimum(bottom, rb_ref[jb, head])
    rng_ref[0, 0:1, :] = jnp.full((1, LANES), top * LOG2E, F32)
    rng_ref[0, 1:2, :] = jnp.full((1, LANES), bottom * LOG2E, F32)
    rng_ref[0, 2:8, :] = jnp.zeros((6, LANES), F32)
    far_left = rb_ref[NUM_BUCKETS // 2 - 1, head] * LOG2E
    far_right = rb_ref[NUM_BUCKETS - 1, head] * LOG2E
    krow = lax.broadcasted_iota(jnp.int32, (LANES, LANES), 0)
    qcol = lax.broadcasted_iota(jnp.int32, (LANES, LANES), 1)
    for t in range(n_near):
        for a in range(tk // LANES):
            for b in range(TAB_TQ // LANES):
                base = (a - b) * LANES + (t - lo) * unit
                blk = (slice(a * LANES, (a + 1) * LANES), slice(b * LANES, (b + 1) * LANES))
                if base + LANES - 1 <= -T5_BAND:
                    tab_ref[(0, t) + blk] = jnp.full((LANES, LANES), far_left, F32)
                elif base - LANES + 1 >= T5_BAND:
                    tab_ref[(0, t) + blk] = jnp.full((LANES, LANES), far_right, F32)
                else:
                    tab_ref[(0, t) + blk] = _t5_bias(krow - qcol + base, rb_ref, head)
    tab_ref[0, n_near] = jnp.full((tk, TAB_TQ), far_left, F32)
    tab_ref[0, n_near + 1] = jnp.full((tk, TAB_TQ), far_right, F32)
    mrow = lax.broadcasted_iota(jnp.int32, (N_META, tq), 0)
    mcol = lax.broadcasted_iota(jnp.int32, (N_META, tq), 1)
    mtab_ref[0, 0] = _t5_bias(mrow - N_META - mcol, rb_ref, head)
    mtab_ref[0, 1] = jnp.full((N_META, tq), far_left, F32)


def _bias_call(rel_bias, tq, tk):
    assert tq % TAB_TQ == 0
    _, lo, hi = _bias_geometry(TAB_TQ, tk)
    nt = lo + hi + 3
    return pl.pallas_call(
        functools.partial(_bias_kernel, tq=tq, tk=tk),
        grid=(N_ATT_HEADS,),
        in_specs=[pl.BlockSpec(memory_space=pltpu.SMEM)],
        out_specs=[pl.BlockSpec((1, nt, tk, TAB_TQ), lambda h: (h, 0, 0, 0)),
                   pl.BlockSpec((1, 2, N_META, tq), lambda h: (h, 0, 0, 0)),
                   pl.BlockSpec((1, 8, LANES), lambda h: (h, 0, 0))],
        out_shape=[jax.ShapeDtypeStruct((N_ATT_HEADS, nt, tk, TAB_TQ), F32),
                   jax.ShapeDtypeStruct((N_ATT_HEADS, 2, N_META, tq), F32),
                   jax.ShapeDtypeStruct((N_ATT_HEADS, 8, LANES), F32)],
        compiler_params=pltpu.CompilerParams(dimension_semantics=("arbitrary",)),
        name="t5_bias",
    )(rel_bias)


def _attn_kernel(lam_ref, qall_ref, k_ref, vt_ref, km_ref, vmt_ref, tab_ref, mtab_ref, rng_ref, sw_ref, o_ref,
                 sa_ref, sb_ref, mca_ref, mcb_ref, m_scr, acc_scr, accp_scr, shift_scr, flag_scr,
                 *, tq, tk, nkc, nq, n_steps):
    g = pl.program_id(0)
    qi = jnp.minimum(g, n_steps - 1) % nq
    unit, lo, hi = _bias_geometry(TAB_TQ, tk)
    n_near = lo + hi + 1
    nt_dims = (((1,), (1,)), ((), ()))
    n_col = 2 * tq // ATT_COLS

    def finalize_previous():
        acc = accp_scr[...]
        o = acc[:ATT_DV] / acc[ATT_DV:ATT_DV + 1]
        lv = lam_ref[...]
        lam = (jnp.exp(jnp.sum(lv[0:1] * lv[1:2], axis=1, keepdims=True))
               - jnp.exp(jnp.sum(lv[2:3] * lv[3:4], axis=1, keepdims=True)) + LAM_INIT)
        out = o[:, :tq] - lam * o[:, tq:]
        ms = jnp.mean(out * out, axis=0, keepdims=True)
        out = out * lax.rsqrt(ms + EPS) * sw_ref[...] * (1.0 - LAM_INIT)
        o_ref[...] = out.T.astype(o_ref.dtype)

    @pl.when(g == 0)
    def _():
        accp_scr[...] = jnp.ones(accp_scr.shape, F32)

    @pl.when(g < n_steps)
    def _():
        q = qall_ref[pl.ds(pl.multiple_of(qi * tq, tq), tq), :]
        lane = lax.broadcasted_iota(jnp.int32, (tq, LANES), 1)
        zero = jnp.zeros_like(q)
        q2 = jnp.concatenate([jnp.where(lane < ATT_DH, q, zero), jnp.where(lane >= ATT_DH, q, zero)], axis=0)

        def bias_tile(j, c):
            q0 = (c * ATT_COLS) % tq
            du = j * (tk // unit) - (qi * (tq // TAB_TQ) + q0 // TAB_TQ) * (TAB_TQ // unit)
            idx = jnp.where(du < -lo, n_near, jnp.where(du > hi, n_near + 1, du + lo))
            return tab_ref[0, idx, :, pl.ds(q0 % TAB_TQ, ATT_COLS)]

        def add_bias(s, b):
            return jnp.concatenate([s[:, :tq] + b, s[:, tq:] + b], axis=1)

        @pl.when(qi == 0)
        def _():
            half = jnp.where(lax.broadcasted_iota(jnp.int32, (LANES, LANES), 0) // ATT_DH
                             == lax.broadcasted_iota(jnp.int32, (LANES, LANES), 1), 1.0, 0.0).astype(BF16)

            def max_sq_norm(x_ref):
                x = x_ref[...]
                sq = jnp.dot(x * x, half, preferred_element_type=F32)
                return jnp.max(sq, axis=0, keepdims=True)

            bound2 = max_sq_norm(qall_ref) * jnp.maximum(max_sq_norm(k_ref), max_sq_norm(km_ref))
            lane_row = lax.broadcasted_iota(jnp.int32, (1, LANES), 1)
            qk_bound = [NORM_SLACK * jnp.sqrt(jnp.max(jnp.where(lane_row == mp, bound2, 0.0), axis=1, keepdims=True))
                        for mp in range(2)]
            hi_b = rng_ref[0, 0:1, 0:1]
            lo_b = rng_ref[0, 1:2, 0:1]
            col = lax.broadcasted_iota(jnp.int32, (1, 2 * tq), 1)
            shift_scr[...] = jnp.where(col < tq, qk_bound[0], qk_bound[1]) + hi_b
            worst_gap = 2.0 * jnp.maximum(qk_bound[0], qk_bound[1]) + (hi_b - lo_b)
            flag_scr[0] = (worst_gap[0, 0] <= MAX_SHIFT_GAP).astype(jnp.int32)

        shift = shift_scr[...]
        bounded = flag_scr[0] == 1

        @pl.when(bounded)
        def _():
            finalize_previous()
            sm = lax.dot_general(km_ref[...], q2, nt_dims, preferred_element_type=F32)
            sm = add_bias(sm, mtab_ref[0, jnp.minimum(qi, 1)])
            acc_scr[...] = jnp.dot(vmt_ref[...], jnp.exp2(sm - shift).astype(BF16), preferred_element_type=F32)

            def stage_logits(u, j, c):
                buf = (sa_ref, sb_ref)[u % 2]
                buf[:, pl.ds(c * ATT_COLS, ATT_COLS)] = lax.dot_general(
                    k_ref[pl.ds(j * tk, tk), :], q2[c * ATT_COLS:(c + 1) * ATT_COLS], nt_dims,
                    preferred_element_type=F32)

            units = [(j, c) for j in range(nkc) for c in range(n_col)]
            stage_logits(0, *units[0])
            for u, (j, c) in enumerate(units):
                if u + 1 < len(units):
                    stage_logits(u + 1, *units[u + 1])
                cols = pl.ds(c * ATT_COLS, ATT_COLS)
                s = (sa_ref, sb_ref)[u % 2][:, cols] + bias_tile(j, c)
                p = jnp.exp2(s - shift[:, c * ATT_COLS:(c + 1) * ATT_COLS]).astype(BF16)
                acc_scr[:, cols] += jnp.dot(vt_ref[j], p, preferred_element_type=F32)
            accp_scr[...] = acc_scr[...]

        @pl.when(jnp.logical_not(bounded))
        def _():
            finalize_previous()

            def produce(j, c, s_ref, mc_ref):
                cols = pl.ds(c * ATT_COLS, ATT_COLS)
                s = lax.dot_general(k_ref[pl.ds(j * tk, tk), :], q2[c * ATT_COLS:(c + 1) * ATT_COLS], nt_dims,
                                    preferred_element_type=F32)
                s = s + bias_tile(j, c)
                s_ref[:, cols] = s
                mc_ref[:, cols] = jnp.max(s, axis=0, keepdims=True)

            def consume(s, m_cur, vt, cols, first=False):
                if first:
                    m_new = m_cur
                else:
                    m_prev = m_scr[:, cols]
                    m_new = jnp.maximum(m_prev, m_cur)
                    alpha = jnp.exp2(m_prev - m_new)
                p = jnp.exp2(s - m_new).astype(BF16)
                pv = jnp.dot(vt, p, preferred_element_type=F32)
                acc_scr[:, cols] = pv if first else alpha * acc_scr[:, cols] + pv
                m_scr[:, cols] = m_new

            sm = lax.dot_general(km_ref[...], q2, nt_dims, preferred_element_type=F32)
            sm = add_bias(sm, mtab_ref[0, jnp.minimum(qi, 1)])
            consume(sm, jnp.max(sm, axis=0, keepdims=True), vmt_ref[...], pl.ds(0, 2 * tq), first=True)

            bufs = ((sa_ref, mca_ref), (sb_ref, mcb_ref))
            for c in range(n_col):
                produce(0, c, *bufs[0])
            for j in range(nkc):
                s_ref, mc_ref = bufs[j % 2]
                for c in range(n_col):
                    cols = pl.ds(c * ATT_COLS, ATT_COLS)
                    if j + 1 < nkc:
                        produce(j + 1, c, *bufs[(j + 1) % 2])
                    consume(s_ref[:, cols], mc_ref[:, cols], vt_ref[j], cols)
            accp_scr[...] = acc_scr[...]

    @pl.when(g == n_steps)
    def _():
        finalize_previous()


def _attn_call(lamv, q, k, vt, km, vmt, tab, mtab, rng, subw_col, batch, seq, tq, tk):
    n = q.shape[0]
    assert n == batch * seq and seq % (2 * tk) == 0 and seq % tq == 0 and vt.shape[2] == tk
    nq = seq // tq
    nkc = seq // tk
    nt = tab.shape[1]
    n_steps = N_ATT_HEADS * batch * nq

    def tile(g):
        g = jnp.minimum(g, n_steps - 1)
        return g // (batch * nq), (g // nq) % batch, g % nq

    def cur(f):
        return lambda g: f(*tile(g))

    def prev(f):
        return lambda g: f(*tile(jnp.maximum(g - 1, 0)))

    return pl.pallas_call(
        functools.partial(_attn_kernel, tq=tq, tk=tk, nkc=nkc, nq=nq, n_steps=n_steps),
        grid=(n_steps + 1,),
        in_specs=[
            pl.BlockSpec(lamv.shape, lambda g: (0, 0)),
            pl.BlockSpec((seq, LANES), cur(lambda h, b, i: (b, h))),
            pl.BlockSpec((seq, LANES), cur(lambda h, b, i: (b, h))),
            pl.BlockSpec((nkc, VT_ROWS, tk), cur(lambda h, b, i: (b, h, 0))),
            pl.BlockSpec((N_META, LANES), cur(lambda h, b, i: (0, h))),
            pl.BlockSpec((VT_ROWS, N_META), cur(lambda h, b, i: (h, 0))),
            pl.BlockSpec((1, nt, tk, TAB_TQ), cur(lambda h, b, i: (h, 0, 0, 0))),
            pl.BlockSpec((1, 2, N_META, tq), cur(lambda h, b, i: (h, 0, 0, 0))),
            pl.BlockSpec((1, 8, LANES), cur(lambda h, b, i: (h, 0, 0))),
            pl.BlockSpec(subw_col.shape, lambda g: (0, 0)),
        ],
        out_specs=pl.BlockSpec((tq, LANES), prev(lambda h, b, i: (b * nq + i, h))),
        out_shape=jax.ShapeDtypeStruct((n, ATT_V), BF16),
        scratch_shapes=[pltpu.VMEM((tk, 2 * tq), F32)] * 2 + [pltpu.VMEM((1, 2 * tq), F32)] * 3
        + [pltpu.VMEM((VT_ROWS, 2 * tq), F32)] * 2 + [pltpu.VMEM((1, 2 * tq), F32), pltpu.SMEM((1,), jnp.int32)],
        compiler_params=pltpu.CompilerParams(dimension_semantics=("arbitrary",), vmem_limit_bytes=VMEM_LIMIT),
        name="diff_attn",
    )(lamv, q, k, vt, km, vmt, tab, mtab, rng, subw_col)


def _split3(x):
    hi = x.astype(BF16)
    r1 = x - hi.astype(F32)
    mid = r1.astype(BF16)
    lo = (r1 - mid.astype(F32)).astype(BF16)
    return hi, mid, lo


def _cumsum_rows(a):
    rows = a.shape[0]
    r_i = lax.broadcasted_iota(jnp.int32, (rows, rows), 0)
    c_i = lax.broadcasted_iota(jnp.int32, (rows, rows), 1)
    tri = jnp.where(c_i <= r_i, 1.0, 0.0).astype(BF16)
    out = None
    for term in _split3(a):
        part = jnp.dot(tri, term, preferred_element_type=F32)
        out = part if out is None else out + part
    return out


def _expand_rows(parts, sel_ref):
    masked = []
    for w, first in parts:
        lane = lax.broadcasted_iota(jnp.int32, w.shape, 1)
        masked.append(jnp.where((lane >= first) & (lane < first + SSM_HEADS), w, 0.0))
    stacked = jnp.concatenate(masked, axis=0)
    hi = stacked.astype(BF16)
    lo = (stacked - hi.astype(F32)).astype(BF16)
    sel = sel_ref[...]
    full = jnp.dot(hi, sel, preferred_element_type=F32) + jnp.dot(lo, sel, preferred_element_type=F32)
    outs, r0 = [], 0
    for w, _ in parts:
        outs.append(full[r0:r0 + w.shape[0]])
        r0 += w.shape[0]
    return outs


def _softplus(x):
    return jnp.maximum(x, 0.0) + jnp.log(1.0 + jnp.exp(-jnp.abs(x)))


GROUP_COLS = SSM_INNER // SSM_GROUPS


def _state_update(b_t, xw):
    return jnp.concatenate(
        [jnp.dot(b_t[g * SSM_STATE:(g + 1) * SSM_STATE], xw[:, g * GROUP_COLS:(g + 1) * GROUP_COLS],
                 preferred_element_type=F32) for g in range(SSM_GROUPS)], axis=0)


def _stack_decay(dec_row):
    return jnp.concatenate(
        [jnp.broadcast_to(dec_row[:, g * GROUP_COLS:(g + 1) * GROUP_COLS], (SSM_STATE, GROUP_COLS))
         for g in range(SSM_GROUPS)], axis=0)


def _conv_silu(win, shift_ref, cw_ref, cb_ref, rows):
    total = rows + 2 * HALO
    assert win.shape[0] == total and shift_ref.shape == (len(MXU_TAPS) * rows, total)
    shifted = jnp.dot(shift_ref[...], win, preferred_element_type=F32)
    win32 = win.astype(F32)
    acc = jnp.broadcast_to(cb_ref[...], (rows, SSM_CONV_DIM))
    for j in range(SSM_CONV):
        off = j - SSM_CONV // 2
        if j in MXU_TAPS:
            tap = shifted[MXU_TAPS.index(j) * rows:(MXU_TAPS.index(j) + 1) * rows]
        elif off == 0:
            tap = win32[HALO:HALO + rows]
        else:
            tap = pltpu.roll(win32, (total - off) % total, axis=0)[HALO:HALO + rows]
        acc = acc + cw_ref[j:j + 1, :] * tap
    return acc * jax.nn.sigmoid(acc)


def _shift_matrix(rows):
    offs = np.array([j - SSM_CONV // 2 for j in MXU_TAPS])
    src = HALO + np.arange(rows)[None, :] + offs[:, None]
    return jnp.asarray(src.reshape(-1)[:, None] == np.arange(rows + 2 * HALO)[None, :], BF16)


def _ssd_kernel(z_ref, xc_ref, xl_ref, xr_ref, dt_ref, mx_ref, mdt_ref, cw_ref, cb_ref, dtb_ref, alog_ref,
                dsk_ref, nw_ref, sel_ref, shc_ref, shm_ref, o_ref, xs_scr, dts_scr, cum_scr, hbs_scr, hf_scr, hb_scr,
                win_scr,
                *, cs, sub, nb):
    rows = cs * sub
    ph = pl.program_id(1)
    t = pl.program_id(2)
    fwd0, bwd0 = 0, SSM_HEADS
    a_row = -jnp.exp(alog_ref[...])

    def decay_terms(dt_raw):
        dt = _softplus(dt_raw + dtb_ref[...])
        return dt, _cumsum_rows(dt * a_row)

    def bcast8(row):
        return jnp.broadcast_to(row, (8, LANES))

    @pl.when(ph == 0)
    def _():
        blk = nb - 1 - t

        @pl.when(t == 0)
        def _():
            hb_scr[...] = jnp.zeros(hb_scr.shape, F32)

        left = jnp.where(blk == 0, mx_ref[...], xl_ref[...])
        right = jnp.where(blk == nb - 1, jnp.zeros_like(xr_ref[...]), xr_ref[...])
        win_scr[0:HALO, :] = left
        win_scr[HALO:HALO + rows, :] = xc_ref[...]
        win_scr[HALO + rows:HALO + rows + HALO, :] = right

        hb = hb_scr[...]
        for si in reversed(range(sub)):
            cc = blk * sub + si
            xbc = _conv_silu(win_scr[si * cs:si * cs + cs + 2 * HALO, :], shc_ref, cw_ref, cb_ref, cs)
            xs_scr[cc] = xbc.astype(BF16)
            dt, cum = decay_terms(dt_ref[si * cs:(si + 1) * cs, :])
            dts_scr[cc] = dt
            cum_scr[cc] = cum
            eb = cum - dt * a_row
            w_b, dec = _expand_rows([(jnp.exp(eb) * dt, bwd0), (bcast8(jnp.exp(cum[cs - 1:cs, :])), bwd0)],
                                    sel_ref)
            xw = (xbc[:, :SSM_INNER] * w_b).astype(BF16)
            bm_t = xbc[:, SSM_INNER:SSM_INNER + LANES].T.astype(BF16)
            hbs_scr[cc] = hb.astype(BF16)
            hb = hb * _stack_decay(dec[0:1]) + _state_update(bm_t, xw)
        hb_scr[...] = hb

    @pl.when(ph == 1)
    def _():
        @pl.when(t == 0)
        def _():
            wm = jnp.concatenate([jnp.zeros((HALO, SSM_CONV_DIM), BF16), mx_ref[...], xc_ref[0:HALO, :]], axis=0)
            xm = _conv_silu(wm, shm_ref, cw_ref, cb_ref, N_META)
            dtm, cumm = decay_terms(mdt_ref[...])
            (w_m,) = _expand_rows([(jnp.exp(cumm[N_META - 1:N_META, :] - cumm) * dtm, fwd0)], sel_ref)
            xwm = (xm[:, :SSM_INNER] * w_m).astype(BF16)
            bmm_t = xm[:, SSM_INNER:SSM_INNER + LANES].T.astype(BF16)
            hf_scr[...] = _state_update(bmm_t, xwm)

        lane = lax.broadcasted_iota(jnp.int32, (cs, LANES), 1)
        l_i = lax.broadcasted_iota(jnp.int32, (cs, cs), 0)
        s_i = lax.broadcasted_iota(jnp.int32, (cs, cs), 1)
        lower = s_i <= l_i
        diag = s_i == l_i
        hpg = SSM_HEADS // SSM_GROUPS
        zx = jnp.zeros((cs, LANES), BF16)
        nt_dims = (((1,), (1,)), ((), ()))

        hf = hf_scr[...]
        for si in range(sub):
            cc = t * sub + si
            xbc = xs_scr[cc]
            x_bf = xbc[:, :SSM_INNER]
            bm = xbc[:, SSM_INNER:SSM_INNER + LANES]
            cm = xbc[:, SSM_INNER + LANES:SSM_INNER + 2 * LANES]
            x = x_bf.astype(F32)

            dt = dts_scr[cc]
            cum = cum_scr[cc]
            eb = cum - dt * a_row
            dt_t, cum_t, eb_t = dt.T, cum.T, eb.T
            last = cum[cs - 1:cs, :]

            c_grp = [jnp.where(lane // SSM_STATE == g, cm, jnp.zeros_like(cm)) for g in range(SSM_GROUPS)]
            g_mats = [lax.dot_general(c_g, bm, nt_dims, preferred_element_type=F32) for c_g in c_grp]

            pieces = []
            for hp in range(SSM_HEADS // 2):
                w_pair = []
                for h in (2 * hp, 2 * hp + 1):
                    arg_f = cum[:, fwd0 + h:fwd0 + h + 1] - cum_t[fwd0 + h:fwd0 + h + 1, :]
                    arg_b = eb_t[bwd0 + h:bwd0 + h + 1, :] - eb[:, bwd0 + h:bwd0 + h + 1]
                    e = jnp.exp(jnp.minimum(jnp.where(lower, arg_f, arg_b), 0.0))
                    dt_f_row = dt_t[fwd0 + h:fwd0 + h + 1, :]
                    dt_b_row = dt_t[bwd0 + h:bwd0 + h + 1, :]
                    m = e * jnp.where(lower, dt_f_row, dt_b_row) + jnp.where(diag, dt_b_row, 0.0)
                    w_pair.append((g_mats[h // hpg] * m).astype(BF16))
                xp = x_bf[:, hp * LANES:(hp + 1) * LANES]
                rhs = jnp.concatenate([jnp.where(lane < SSM_HEADDIM, xp, zx),
                                       jnp.where(lane >= SSM_HEADDIM, xp, zx)], axis=0)
                pieces.append(jnp.dot(jnp.concatenate(w_pair, axis=1), rhs, preferred_element_type=F32))
            y = jnp.concatenate(pieces, axis=1)

            d_f, d_b, w_f, dec = _expand_rows(
                [(jnp.exp(cum), fwd0), (jnp.exp(last - eb), bwd0), (jnp.exp(last - cum) * dt, fwd0),
                 (bcast8(jnp.exp(last)), fwd0)], sel_ref)
            hf_bf = hf.astype(BF16)
            hb_bf = hbs_scr[cc]
            y = y + d_f * jnp.concatenate([jnp.dot(c_g, hf_bf, preferred_element_type=F32) for c_g in c_grp],
                                          axis=1)
            y = y + d_b * jnp.concatenate([jnp.dot(c_g, hb_bf, preferred_element_type=F32) for c_g in c_grp],
                                          axis=1)
            y = y + x * dsk_ref[...]

            xw = (x * w_f).astype(BF16)
            hf = hf * _stack_decay(dec[0:1]) + _state_update(bm.astype(F32).T.astype(BF16), xw)

            zf = z_ref[si * cs:(si + 1) * cs, :].astype(F32)
            y = y * (zf * jax.nn.sigmoid(zf))
            o_ref[si * cs:(si + 1) * cs, :] = _rmsnorm(y, nw_ref[...]).astype(o_ref.dtype)
        hf_scr[...] = hf


def _ssd_call(z, xbc, dt, mxbc, mdt, cw, cb, dtb, alog, dskip, nw, sel, shc, shm, batch, seq, cs, sub):
    n = z.shape[0]
    rows = cs * sub
    assert n == batch * seq and seq % rows == 0 and cs % HALO == 0
    nc = seq // cs
    nb = seq // rows
    hpb = rows // HALO
    n_halo = n // HALO

    def ph0_block(ph, t):
        return (1 - ph) * (nb - 1 - t)

    const2 = lambda shape: pl.BlockSpec(shape, lambda b, ph, t: (0, 0))
    return pl.pallas_call(
        functools.partial(_ssd_kernel, cs=cs, sub=sub, nb=nb),
        grid=(batch, 2, nb),
        in_specs=[
            pl.BlockSpec((rows, SSM_INNER), lambda b, ph, t: (b * nb + ph * t, 0)),
            pl.BlockSpec((rows, SSM_CONV_DIM), lambda b, ph, t: (b * nb + ph0_block(ph, t), 0)),
            pl.BlockSpec((HALO, SSM_CONV_DIM),
                         lambda b, ph, t: (jnp.maximum((b * nb + ph0_block(ph, t)) * hpb - 1, 0), 0)),
            pl.BlockSpec((HALO, SSM_CONV_DIM),
                         lambda b, ph, t: (jnp.minimum((b * nb + ph0_block(ph, t) + 1) * hpb, n_halo - 1), 0)),
            pl.BlockSpec((rows, DT_PAD), lambda b, ph, t: (b * nb + ph0_block(ph, t), 0)),
            const2(mxbc.shape), const2(mdt.shape), const2(cw.shape), const2(cb.shape), const2(dtb.shape),
            const2(alog.shape), const2(dskip.shape), const2(nw.shape), const2(sel.shape), const2(shc.shape),
            const2(shm.shape),
        ],
        out_specs=pl.BlockSpec((rows, SSM_INNER), lambda b, ph, t: (b * nb + ph * t, 0)),
        out_shape=jax.ShapeDtypeStruct((n, SSM_INNER), BF16),
        scratch_shapes=[
            pltpu.VMEM((nc, cs, SSM_CONV_DIM), BF16),
            pltpu.VMEM((nc, cs, DT_PAD), F32),
            pltpu.VMEM((nc, cs, DT_PAD), F32),
            pltpu.VMEM((nc, LANES, GROUP_COLS), BF16),
            pltpu.VMEM((LANES, GROUP_COLS), F32),
            pltpu.VMEM((LANES, GROUP_COLS), F32),
            pltpu.VMEM((rows + 2 * HALO, SSM_CONV_DIM), BF16),
        ],
        compiler_params=pltpu.CompilerParams(dimension_semantics=("arbitrary",) * 3, vmem_limit_bytes=VMEM_LIMIT),
        name="bi_ssd",
    )(z, xbc, xbc, xbc, dt, mxbc, mdt, cw, cb, dtb, alog, dskip, nw, sel, shc, shm)


def _head_selector():
    k = np.arange(LANES)[:, None]
    col = np.arange(SSM_INNER)[None, :]
    return jnp.asarray((k % SSM_HEADS == col // SSM_HEADDIM) & (k < 2 * SSM_HEADS), BF16)


def _prep_weights(ffn1_norm_w, ffn1_w_gate, ffn1_w_up, ffn1_w_down, mix_norm_w, w_in, lambda_q1, lambda_k1,
                  lambda_q2, lambda_k2, attn_subln_w, conv_w, conv_b, dt_bias_fwd, dt_bias_bwd, a_log_fwd,
                  a_log_bwd, ssm_d, ssm_norm_w, w_out, ffn2_norm_w, ffn2_w_gate, ffn2_w_up, ffn2_w_down,
                  final_norm_w):
    def ffn(norm_w, wg, wu, wd):
        return norm_w[0][None, :], wg[0].astype(BF16), wu[0].astype(BF16), wd[0].astype(BF16)

    pad_lanes = lambda v, width: jnp.pad(v, (0, width - v.shape[0]))[None, :]
    o_v, o_z = 2 * ATT_QK, 2 * ATT_QK + ATT_V
    wi = w_in[0].astype(BF16)
    return dict(
        ffn1=ffn(ffn1_norm_w, ffn1_w_gate, ffn1_w_up, ffn1_w_down),
        ffn2=ffn(ffn2_norm_w, ffn2_w_gate, ffn2_w_up, ffn2_w_down),
        mix_norm=mix_norm_w[0][None, :],
        win=wi,
        wvt=wi[:, o_v:o_z].T,
        lamv=jnp.stack([lambda_q1[0], lambda_k1[0], lambda_q2[0], lambda_k2[0]]),
        subw_col=attn_subln_w[0][:, None],
        cw=jnp.pad(conv_w[0], ((0, 8 - SSM_CONV), (0, 0))),
        cb=conv_b[0][None, :],
        dtb=pad_lanes(jnp.concatenate([dt_bias_fwd[0], dt_bias_bwd[0]]), DT_PAD),
        alog=pad_lanes(jnp.concatenate([a_log_fwd[0], a_log_bwd[0]]), DT_PAD),
        dskip=jnp.repeat(ssm_d[0], SSM_HEADDIM)[None, :],
        ssm_norm=ssm_norm_w[0][None, :],
        wo=w_out[0].astype(BF16).reshape(2, ATT_V, D_MODEL),
        final=final_norm_w[None, :],
        sel=_head_selector(),
        shc=_shift_matrix(SSD_CHUNK),
        shm=_shift_matrix(N_META),
    )


def _encode(x, w, meta_proj, bias_tabs):
    batch, seq, _ = x.shape
    km, vmt, mxbc, mdt = meta_proj
    h0 = x.reshape(batch * seq, D_MODEL)
    h1 = _ffn_call(h0, *w["ffn1"])
    q, k, vt, z, xbc, dt = _inproj_call(h1, w["mix_norm"], w["win"], w["wvt"])
    att = _attn_call(w["lamv"], q, k, vt, km, vmt, *bias_tabs, w["subw_col"], batch, seq, ATT_TQ, ATT_TK)
    ssm = _ssd_call(z, xbc, dt, mxbc, mdt, w["cw"], w["cb"], w["dtb"], w["alog"], w["dskip"], w["ssm_norm"],
                    w["sel"], w["shc"], w["shm"], batch, seq, SSD_CHUNK, SSD_SUB)
    y = _ffn_call(h1, *w["ffn2"], mix=(att, ssm, w["wo"]), final_w=w["final"])
    return y.reshape(batch, seq, D_MODEL)


def kernel(x_prompt, x_sample, meta_tokens, ffn1_norm_w, ffn1_w_gate, ffn1_w_up, ffn1_w_down, mix_norm_w, w_in, rel_bias, lambda_q1, lambda_k1, lambda_q2, lambda_k2, attn_subln_w, conv_w, conv_b, dt_bias_fwd, dt_bias_bwd, a_log_fwd, a_log_bwd, ssm_d, ssm_norm_w, w_out, ffn2_norm_w, ffn2_w_gate, ffn2_w_up, ffn2_w_down, final_norm_w):
    w = _prep_weights(ffn1_norm_w, ffn1_w_gate, ffn1_w_up, ffn1_w_down, mix_norm_w, w_in, lambda_q1, lambda_k1,
                      lambda_q2, lambda_k2, attn_subln_w, conv_w, conv_b, dt_bias_fwd, dt_bias_bwd, a_log_fwd,
                      a_log_bwd, ssm_d, ssm_norm_w, w_out, ffn2_norm_w, ffn2_w_gate, ffn2_w_up, ffn2_w_down,
                      final_norm_w)
    hm = _ffn_call(meta_tokens, *w["ffn1"])
    _, km, vmt, _, mxbc, mdt = _inproj_call(hm, w["mix_norm"], w["win"], w["wvt"])
    meta_proj = (km, vmt[0], mxbc, mdt)
    bias_tabs = _bias_call(rel_bias, ATT_TQ, ATT_TK)
    return (_encode(x_prompt, w, meta_proj, bias_tabs), _encode(x_sample, w, meta_proj, bias_tabs))
```

```python
import functools
import math

import jax
import jax.numpy as jnp
import numpy as np
from jax import lax
from jax.experimental import pallas as pl
from jax.experimental.pallas import tpu as pltpu

F32 = jnp.float32
BF16 = jnp.bfloat16

D_MODEL = 1024
N_META = 16
N_ATT_HEADS = 8
ATT_DH = 64
ATT_DV = 128
ATT_QK = 1024
ATT_V = 1024
NUM_BUCKETS = 32
MAX_DISTANCE = 128
SSM_HEADS = 16
SSM_HEADDIM = 64
SSM_INNER = 1024
SSM_GROUPS = 2
SSM_STATE = 64
SSM_CONV = 7
SSM_CONV_DIM = 1280
D_FF = 2816
EPS = 1e-6
LAYER = 0
LAM_INIT = 0.8 - 0.6 * math.exp(-0.3 * LAYER)
LOG2E = math.log2(math.e)
Q_SCALE = ATT_DH ** -0.5 * LOG2E
NORM_SLACK = 1.02
MAX_SHIFT_GAP = 100.0

LANES = 128
BF16_ROWS = 16
VMEM_LIMIT = 56 * 1024 * 1024

FF_TILE = 256
N_FF = D_FF // FF_TILE
DT_PAD = LANES
DT_COLS = 2 * SSM_HEADS
T5_BAND = 91

ROW_TILE = 512
ATT_TQ = 1024
ATT_TK = 512
TAB_TQ = 512
ATT_COLS = 256
SSD_CHUNK = 128
MXU_TAPS = (0, 1, 5)
SSD_SUB = 8
HALO = BF16_ROWS
VT_ROWS = ATT_DV + BF16_ROWS


def _rmsnorm(x, w):
    ms = jnp.mean(x * x, axis=-1, keepdims=True)
    return x * lax.rsqrt(ms + EPS) * w


def _resident(shape):
    nd = len(shape)
    return pl.BlockSpec(shape, lambda *_: (0,) * nd, pipeline_mode=pl.Buffered(1))


def _ffn_kernel(*refs, has_mix, has_final):
    it = iter(refs)
    h_ref = next(it)
    if has_mix:
        att_ref, ssm_ref, wo_ref = next(it), next(it), next(it)
    nw_ref, wg_ref, wu_ref, wd_ref = next(it), next(it), next(it), next(it)
    fw_ref = next(it) if has_final else None
    o_ref = next(it)

    h = h_ref[...]
    if has_mix:
        h = (h + jnp.dot(att_ref[...], wo_ref[0], preferred_element_type=F32)
             + jnp.dot(ssm_ref[...], wo_ref[1], preferred_element_type=F32))
    u = _rmsnorm(h, nw_ref[...]).astype(BF16)
    acc = jnp.zeros_like(h)
    for j in range(N_FF):
        ff = slice(j * FF_TILE, (j + 1) * FF_TILE)
        g = jnp.dot(u, wg_ref[:, ff], preferred_element_type=F32)
        up = jnp.dot(u, wu_ref[:, ff], preferred_element_type=F32)
        a = (g * jax.nn.sigmoid(g) * up).astype(BF16)
        acc = acc + jnp.dot(a, wd_ref[ff, :], preferred_element_type=F32)
    h = h + 0.5 * acc
    if has_final:
        h = _rmsnorm(h, fw_ref[...])
    o_ref[...] = h


def _ffn_call(h, norm_w, wg, wu, wd, mix=None, final_w=None):
    n = h.shape[0]
    tm = min(ROW_TILE, n)
    assert n % tm == 0
    row = lambda width: pl.BlockSpec((tm, width), lambda i: (i, 0))
    args, specs = [h], [row(D_MODEL)]
    if mix is not None:
        att, ssm, wo = mix
        args += [att, ssm, wo]
        specs += [row(ATT_V), row(SSM_INNER), _resident(wo.shape)]
    args += [norm_w, wg, wu, wd]
    specs += [_resident(norm_w.shape), _resident(wg.shape), _resident(wu.shape), _resident(wd.shape)]
    if final_w is not None:
        args.append(final_w)
        specs.append(_resident(final_w.shape))
    return pl.pallas_call(
        functools.partial(_ffn_kernel, has_mix=mix is not None, has_final=final_w is not None),
        grid=(n // tm,),
        in_specs=specs,
        out_specs=row(D_MODEL),
        out_shape=jax.ShapeDtypeStruct((n, D_MODEL), F32),
        compiler_params=pltpu.CompilerParams(dimension_semantics=("arbitrary",), vmem_limit_bytes=VMEM_LIMIT),
        name="ffn_mix" if mix is not None else "ffn",
    )(*args)


_IN_SEGS = (("q", 0, ATT_QK), ("k", ATT_QK, ATT_QK), ("z", 2 * ATT_QK + ATT_V, SSM_INNER),
            ("xbc", 2 * ATT_QK + ATT_V + SSM_INNER, SSM_CONV_DIM),
            ("dt", 2 * ATT_QK + ATT_V + SSM_INNER + SSM_CONV_DIM, DT_COLS))


def _inproj_kernel(h_ref, nw_ref, win_ref, wdt_ref, wvt_ref, q_ref, k_ref, vt_ref, z_ref, xbc_ref, dt_ref):
    u = _rmsnorm(h_ref[...], nw_ref[...]).astype(BF16)
    outs = dict(q=q_ref, k=k_ref, z=z_ref, xbc=xbc_ref, dt=dt_ref)
    for name, c0, width in _IN_SEGS:
        o_ref = outs[name]
        step = 512 if width % 512 == 0 else (256 if width % 256 == 0 else width)
        for s in range(0, width, step):
            w_cols = wdt_ref[...] if name == "dt" else win_ref[:, c0 + s:c0 + s + step]
            r = jnp.dot(u, w_cols, preferred_element_type=F32)
            if name == "q":
                r = r * Q_SCALE
            o_ref[:, s:s + step] = r.astype(o_ref.dtype)
    dt_ref[:, DT_COLS:] = jnp.zeros((dt_ref.shape[0], DT_PAD - DT_COLS), dt_ref.dtype)
    nt_dims = (((1,), (1,)), ((), ()))
    ones = jnp.ones((VT_ROWS - ATT_DV, u.shape[0]), vt_ref.dtype)
    for s in range(0, ATT_V, 256):
        r = lax.dot_general(wvt_ref[s:s + 256, :], u, nt_dims, preferred_element_type=F32).astype(vt_ref.dtype)
        for hh in range(256 // ATT_DV):
            head = s // ATT_DV + hh
            vt_ref[0, head * VT_ROWS:head * VT_ROWS + ATT_DV, :] = r[hh * ATT_DV:(hh + 1) * ATT_DV]
            vt_ref[0, head * VT_ROWS + ATT_DV:(head + 1) * VT_ROWS, :] = ones


def _inproj_call(h, norm_w, win, wdt, wvt):
    n = h.shape[0]
    tm = min(ATT_TK, n)
    assert n % tm == 0
    row = lambda width: pl.BlockSpec((tm, width), lambda i: (i, 0))
    widths = (ATT_QK, ATT_QK, SSM_INNER, SSM_CONV_DIM, DT_PAD)
    dtypes = (BF16, BF16, BF16, BF16, F32)
    shapes = [jax.ShapeDtypeStruct((n, w), dt) for w, dt in zip(widths, dtypes)]
    specs = [row(w) for w in widths]
    shapes.insert(2, jax.ShapeDtypeStruct((n // tm, N_ATT_HEADS * VT_ROWS, tm), BF16))
    specs.insert(2, pl.BlockSpec((1, N_ATT_HEADS * VT_ROWS, tm), lambda i: (i, 0, 0)))
    return pl.pallas_call(
        _inproj_kernel,
        grid=(n // tm,),
        in_specs=[row(D_MODEL), _resident(norm_w.shape), _resident(win.shape), _resident(wdt.shape),
                  _resident(wvt.shape)],
        out_specs=specs,
        out_shape=shapes,
        compiler_params=pltpu.CompilerParams(dimension_semantics=("arbitrary",), vmem_limit_bytes=VMEM_LIMIT),
        name="inproj",
    )(h, norm_w, win, wdt, wvt)


def _t5_bias(rel, rb_ref, head):
    half = NUM_BUCKETS // 2
    max_exact = half // 2
    ret = jnp.where(rel > 0, half, 0)
    n = jnp.abs(rel)
    nf = jnp.maximum(n, 1).astype(F32)
    large = max_exact + (jnp.log(nf / max_exact) / math.log(MAX_DISTANCE / max_exact)
                         * (half - max_exact)).astype(jnp.int32)
    large = jnp.minimum(large, half - 1)
    bucket = ret + jnp.where(n < max_exact, n, large)
    val = jnp.zeros(rel.shape, F32)
    for jb in range(NUM_BUCKETS):
        val = jnp.where(bucket == jb, rb_ref[jb, head], val)
    return val * LOG2E


def _bias_geometry(tq, tk):
    unit = min(tq, tk)
    assert tq % unit == 0 and tk % unit == 0 and unit >= T5_BAND + 1
    return unit, tk // unit, tq // unit


def _bias_kernel(rb_ref, tab_ref, mtab_ref, rng_ref, *, tq, tk):
    head = pl.program_id(0)
    unit, lo, hi = _bias_geometry(TAB_TQ, tk)
    n_near = lo + hi + 1
    top = rb_ref[0, head]
    bottom = rb_ref[0, head]
    for jb in range(1, NUM_BUCKETS):
        top = jnp.maximum(top, rb_ref[jb, head])
        bottom = jnp.minimum(bottom, rb_ref[jb, head])
    rng_ref[0, 0:1, :] = jnp.full((1, LANES), top * LOG2E, F32)
    rng_ref[0, 1:2, :] = jnp.full((1, LANES), bottom * LOG2E, F32)
    rng_ref[0, 2:8, :] = jnp.zeros((6, LANES), F32)
    far_left = rb_ref[NUM_BUCKETS // 2 - 1, head] * LOG2E
    far_right = rb_ref[NUM_BUCKETS - 1, head] * LOG2E
    krow = lax.broadcasted_iota(jnp.int32, (LANES, LANES), 0)
    qcol = lax.broadcasted_iota(jnp.int32, (LANES, LANES), 1)
    for t in range(n_near):
        for a in range(tk // LANES):
            for b in range(TAB_TQ // LANES):
                base = (a - b) * LANES + (t - lo) * unit
                blk = (slice(a * LANES, (a + 1) * LANES), slice(b * LANES, (b + 1) * LANES))
                if base + LANES - 1 <= -T5_BAND:
                    tab_ref[(0, t) + blk] = jnp.full((LANES, LANES), far_left, F32)
                elif base - LANES + 1 >= T5_BAND:
                    tab_ref[(0, t) + blk] = jnp.full((LANES, LANES), far_right, F32)
                else:
                    tab_ref[(0, t) + blk] = _t5_bias(krow - qcol + base, rb_ref, head)
    tab_ref[0, n_near] = jnp.full((tk, TAB_TQ), far_left, F32)
    tab_ref[0, n_near + 1] = jnp.full((tk, TAB_TQ), far_right, F32)
    mrow = lax.broadcasted_iota(jnp.int32, (N_META, tq), 0)
    mcol = lax.broadcasted_iota(jnp.int32, (N_META, tq), 1)
    mtab_ref[0, 0] = _t5_bias(mrow - N_META - mcol, rb_ref, head)
    mtab_ref[0, 1] = jnp.full((N_META, tq), far_left, F32)


def _bias_call(rel_bias, tq, tk):
    assert tq % TAB_TQ == 0
    _, lo, hi = _bias_geometry(TAB_TQ, tk)
    nt = lo + hi + 3
    return pl.pallas_call(
        functools.partial(_bias_kernel, tq=tq, tk=tk),
        grid=(N_ATT_HEADS,),
        in_specs=[pl.BlockSpec(memory_space=pltpu.SMEM)],
        out_specs=[pl.BlockSpec((1, nt, tk, TAB_TQ), lambda h: (h, 0, 0, 0)),
                   pl.BlockSpec((1, 2, N_META, tq), lambda h: (h, 0, 0, 0)),
                   pl.BlockSpec((1, 8, LANES), lambda h: (h, 0, 0))],
        out_shape=[jax.ShapeDtypeStruct((N_ATT_HEADS, nt, tk, TAB_TQ), F32),
                   jax.ShapeDtypeStruct((N_ATT_HEADS, 2, N_META, tq), F32),
                   jax.ShapeDtypeStruct((N_ATT_HEADS, 8, LANES), F32)],
        compiler_params=pltpu.CompilerParams(dimension_semantics=("arbitrary",)),
        name="t5_bias",
    )(rel_bias)


def _attn_kernel(lam_ref, qall_ref, k_ref, vt_ref, km_ref, vmt_ref, tab_ref, mtab_ref, rng_ref, sw_ref, o_ref,
                 sa_ref, sb_ref, mca_ref, mcb_ref, m_scr, acc_scr, accp_scr, shift_scr, flag_scr,
                 *, tq, tk, nkc, nq, n_steps):
    g = pl.program_id(0)
    qi = jnp.minimum(g, n_steps - 1) % nq
    unit, lo, hi = _bias_geometry(TAB_TQ, tk)
    n_near = lo + hi + 1
    nt_dims = (((1,), (1,)), ((), ()))
    n_col = 2 * tq // ATT_COLS

    def finalize_previous():
        acc = accp_scr[...]
        o = acc[:ATT_DV] / acc[ATT_DV:ATT_DV + 1]
        lv = lam_ref[...]
        lam = (jnp.exp(jnp.sum(lv[0:1] * lv[1:2], axis=1, keepdims=True))
               - jnp.exp(jnp.sum(lv[2:3] * lv[3:4], axis=1, keepdims=True)) + LAM_INIT)
        out = o[:, :tq] - lam * o[:, tq:]
        ms = jnp.mean(out * out, axis=0, keepdims=True)
        out = out * lax.rsqrt(ms + EPS) * sw_ref[...] * (1.0 - LAM_INIT)
        o_ref[...] = out.T.astype(o_ref.dtype)

    @pl.when(g == 0)
    def _():
        accp_scr[...] = jnp.ones(accp_scr.shape, F32)

    @pl.when(g < n_steps)
    def _():
        q = qall_ref[pl.ds(pl.multiple_of(qi * tq, tq), tq), :]
        lane = lax.broadcasted_iota(jnp.int32, (tq, LANES), 1)
        zero = jnp.zeros_like(q)
        q2 = jnp.concatenate([jnp.where(lane < ATT_DH, q, zero), jnp.where(lane >= ATT_DH, q, zero)], axis=0)

        def bias_tile(j, c):
            q0 = (c * ATT_COLS) % tq
            du = j * (tk // unit) - (qi * (tq // TAB_TQ) + q0 // TAB_TQ) * (TAB_TQ // unit)
            idx = jnp.where(du < -lo, n_near, jnp.where(du > hi, n_near + 1, du + lo))
            return tab_ref[0, idx, :, pl.ds(q0 % TAB_TQ, ATT_COLS)]

        def add_bias(s, b):
            return jnp.concatenate([s[:, :tq] + b, s[:, tq:] + b], axis=1)

        @pl.when(qi == 0)
        def _():
            half = jnp.where(lax.broadcasted_iota(jnp.int32, (LANES, LANES), 0) // ATT_DH
                             == lax.broadcasted_iota(jnp.int32, (LANES, LANES), 1), 1.0, 0.0).astype(BF16)

            def max_sq_norm(x_ref):
                x = x_ref[...]
                sq = jnp.dot(x * x, half, preferred_element_type=F32)
                return jnp.max(sq, axis=0, keepdims=True)

            bound2 = max_sq_norm(qall_ref) * jnp.maximum(max_sq_norm(k_ref), max_sq_norm(km_ref))
            lane_row = lax.broadcasted_iota(jnp.int32, (1, LANES), 1)
            qk_bound = [NORM_SLACK * jnp.sqrt(jnp.max(jnp.where(lane_row == mp, bound2, 0.0), axis=1, keepdims=True))
                        for mp in range(2)]
            hi_b = rng_ref[0, 0:1, 0:1]
            lo_b = rng_ref[0, 1:2, 0:1]
            col = lax.broadcasted_iota(jnp.int32, (1, 2 * tq), 1)
            shift_scr[...] = jnp.where(col < tq, qk_bound[0], qk_bound[1]) + hi_b
            worst_gap = 2.0 * jnp.maximum(qk_bound[0], qk_bound[1]) + (hi_b - lo_b)
            flag_scr[0] = (worst_gap[0, 0] <= MAX_SHIFT_GAP).astype(jnp.int32)

        shift = shift_scr[...]
        bounded = flag_scr[0] == 1

        @pl.when(bounded)
        def _():
            finalize_previous()
            sm = lax.dot_general(km_ref[...], q2, nt_dims, preferred_element_type=F32)
            sm = add_bias(sm, mtab_ref[0, jnp.minimum(qi, 1)])
            acc_scr[...] = jnp.dot(vmt_ref[...], jnp.exp2(sm - shift).astype(BF16), preferred_element_type=F32)

            def stage_logits(u, j, c):
                buf = (sa_ref, sb_ref)[u % 2]
                buf[:, pl.ds(c * ATT_COLS, ATT_COLS)] = lax.dot_general(
                    k_ref[pl.ds(j * tk, tk), :], q2[c * ATT_COLS:(c + 1) * ATT_COLS], nt_dims,
                    preferred_element_type=F32)

            units = [(j, c) for j in range(nkc) for c in range(n_col)]
            stage_logits(0, *units[0])
            for u, (j, c) in enumerate(units):
                if u + 1 < len(units):
                    stage_logits(u + 1, *units[u + 1])
                cols = pl.ds(c * ATT_COLS, ATT_COLS)
                s = (sa_ref, sb_ref)[u % 2][:, cols] + bias_tile(j, c)
                p = jnp.exp2(s - shift[:, c * ATT_COLS:(c + 1) * ATT_COLS]).astype(BF16)
                acc_scr[:, cols] += jnp.dot(vt_ref[j], p, preferred_element_type=F32)
            accp_scr[...] = acc_scr[...]

        @pl.when(jnp.logical_not(bounded))
        def _():
            finalize_previous()

            def produce(j, c, s_ref, mc_ref):
                cols = pl.ds(c * ATT_COLS, ATT_COLS)
                s = lax.dot_general(k_ref[pl.ds(j * tk, tk), :], q2[c * ATT_COLS:(c + 1) * ATT_COLS], nt_dims,
                                    preferred_element_type=F32)
                s = s + bias_tile(j, c)
                s_ref[:, cols] = s
                mc_ref[:, cols] = jnp.max(s, axis=0, keepdims=True)

            def consume(s, m_cur, vt, cols, first=False):
                if first:
                    m_new = m_cur
                else:
                    m_prev = m_scr[:, cols]
                    m_new = jnp.maximum(m_prev, m_cur)
                    alpha = jnp.exp2(m_prev - m_new)
                p = jnp.exp2(s - m_new).astype(BF16)
                pv = jnp.dot(vt, p, preferred_element_type=F32)
                acc_scr[:, cols] = pv if first else alpha * acc_scr[:, cols] + pv
                m_scr[:, cols] = m_new

            sm = lax.dot_general(km_ref[...], q2, nt_dims, preferred_element_type=F32)
            sm = add_bias(sm, mtab_ref[0, jnp.minimum(qi, 1)])
            consume(sm, jnp.max(sm, axis=0, keepdims=True), vmt_ref[...], pl.ds(0, 2 * tq), first=True)

            bufs = ((sa_ref, mca_ref), (sb_ref, mcb_ref))
            for c in range(n_col):
                produce(0, c, *bufs[0])
            for j in range(nkc):
                s_ref, mc_ref = bufs[j % 2]
                for c in range(n_col):
                    cols = pl.ds(c * ATT_COLS, ATT_COLS)
                    if j + 1 < nkc:
                        produce(j + 1, c, *bufs[(j + 1) % 2])
                    consume(s_ref[:, cols], mc_ref[:, cols], vt_ref[j], cols)
            accp_scr[...] = acc_scr[...]

    @pl.when(g == n_steps)
    def _():
        finalize_previous()


def _attn_call(lamv, q, k, vt, km, vmt, tab, mtab, rng, subw_col, batch, seq, tq, tk):
    n = q.shape[0]
    assert n == batch * seq and seq % (2 * tk) == 0 and seq % tq == 0 and vt.shape[2] == tk
    nq = seq // tq
    nkc = seq // tk
    nt = tab.shape[1]
    n_steps = N_ATT_HEADS * batch * nq

    def tile(g):
        g = jnp.minimum(g, n_steps - 1)
        return g // (batch * nq), (g // nq) % batch, g % nq

    def cur(f):
        return lambda g: f(*tile(g))

    def prev(f):
        return lambda g: f(*tile(jnp.maximum(g - 1, 0)))

    return pl.pallas_call(
        functools.partial(_attn_kernel, tq=tq, tk=tk, nkc=nkc, nq=nq, n_steps=n_steps),
        grid=(n_steps + 1,),
        in_specs=[
            pl.BlockSpec(lamv.shape, lambda g: (0, 0)),
            pl.BlockSpec((seq, LANES), cur(lambda h, b, i: (b, h))),
            pl.BlockSpec((seq, LANES), cur(lambda h, b, i: (b, h))),
            pl.BlockSpec((nkc, VT_ROWS, tk), cur(lambda h, b, i: (b, h, 0))),
            pl.BlockSpec((N_META, LANES), cur(lambda h, b, i: (0, h))),
            pl.BlockSpec((VT_ROWS, N_META), cur(lambda h, b, i: (h, 0))),
            pl.BlockSpec((1, nt, tk, TAB_TQ), cur(lambda h, b, i: (h, 0, 0, 0))),
            pl.BlockSpec((1, 2, N_META, tq), cur(lambda h, b, i: (h, 0, 0, 0))),
            pl.BlockSpec((1, 8, LANES), cur(lambda h, b, i: (h, 0, 0))),
            pl.BlockSpec(subw_col.shape, lambda g: (0, 0)),
        ],
        out_specs=pl.BlockSpec((tq, LANES), prev(lambda h, b, i: (b * nq + i, h))),
        out_shape=jax.ShapeDtypeStruct((n, ATT_V), BF16),
        scratch_shapes=[pltpu.VMEM((tk, 2 * tq), F32)] * 2 + [pltpu.VMEM((1, 2 * tq), F32)] * 3
        + [pltpu.VMEM((VT_ROWS, 2 * tq), F32)] * 2 + [pltpu.VMEM((1, 2 * tq), F32), pltpu.SMEM((1,), jnp.int32)],
        compiler_params=pltpu.CompilerParams(dimension_semantics=("arbitrary",), vmem_limit_bytes=VMEM_LIMIT),
        name="diff_attn",
    )(lamv, q, k, vt, km, vmt, tab, mtab, rng, subw_col)


def _split3(x):
    hi = x.astype(BF16)
    r1 = x - hi.astype(F32)
    mid = r1.astype(BF16)
    lo = (r1 - mid.astype(F32)).astype(BF16)
    return hi, mid, lo


def _cumsum_rows(a):
    rows = a.shape[0]
    r_i = lax.broadcasted_iota(jnp.int32, (rows, rows), 0)
    c_i = lax.broadcasted_iota(jnp.int32, (rows, rows), 1)
    tri = jnp.where(c_i <= r_i, 1.0, 0.0).astype(BF16)
    out = None
    for term in _split3(a):
        part = jnp.dot(tri, term, preferred_element_type=F32)
        out = part if out is None else out + part
    return out


def _expand_rows(parts, sel_ref):
    masked = []
    for w, first in parts:
        lane = lax.broadcasted_iota(jnp.int32, w.shape, 1)
        masked.append(jnp.where((lane >= first) & (lane < first + SSM_HEADS), w, 0.0))
    stacked = jnp.concatenate(masked, axis=0)
    hi = stacked.astype(BF16)
    lo = (stacked - hi.astype(F32)).astype(BF16)
    sel = sel_ref[...]
    full = jnp.dot(hi, sel, preferred_element_type=F32) + jnp.dot(lo, sel, preferred_element_type=F32)
    outs, r0 = [], 0
    for w, _ in parts:
        outs.append(full[r0:r0 + w.shape[0]])
        r0 += w.shape[0]
    return outs


def _softplus(x):
    return jnp.maximum(x, 0.0) + jnp.log(1.0 + jnp.exp(-jnp.abs(x)))


GROUP_COLS = SSM_INNER // SSM_GROUPS


def _state_update(b_t, xw):
    return jnp.concatenate(
        [jnp.dot(b_t[g * SSM_STATE:(g + 1) * SSM_STATE], xw[:, g * GROUP_COLS:(g + 1) * GROUP_COLS],
                 preferred_element_type=F32) for g in range(SSM_GROUPS)], axis=0)


def _stack_decay(dec_row):
    return jnp.concatenate(
        [jnp.broadcast_to(dec_row[:, g * GROUP_COLS:(g + 1) * GROUP_COLS], (SSM_STATE, GROUP_COLS))
         for g in range(SSM_GROUPS)], axis=0)


def _conv_silu(win, shift_ref, cw_ref, cb_ref, rows):
    total = rows + 2 * HALO
    assert win.shape[0] == total and shift_ref.shape == (len(MXU_TAPS) * rows, total)
    shifted = jnp.dot(shift_ref[...], win, preferred_element_type=F32)
    win32 = win.astype(F32)
    acc = jnp.broadcast_to(cb_ref[...], (rows, SSM_CONV_DIM))
    for j in range(SSM_CONV):
        off = j - SSM_CONV // 2
        if j in MXU_TAPS:
            tap = shifted[MXU_TAPS.index(j) * rows:(MXU_TAPS.index(j) + 1) * rows]
        elif off == 0:
            tap = win32[HALO:HALO + rows]
        else:
            tap = pltpu.roll(win32, (total - off) % total, axis=0)[HALO:HALO + rows]
        acc = acc + cw_ref[j:j + 1, :] * tap
    return acc * jax.nn.sigmoid(acc)


def _shift_matrix(rows):
    offs = np.array([j - SSM_CONV // 2 for j in MXU_TAPS])
    src = HALO + np.arange(rows)[None, :] + offs[:, None]
    return jnp.asarray(src.reshape(-1)[:, None] == np.arange(rows + 2 * HALO)[None, :], BF16)


def _ssd_kernel(z_ref, xc_ref, xl_ref, xr_ref, dt_ref, mx_ref, mdt_ref, cw_ref, cb_ref, dtb_ref, alog_ref,
                dsk_ref, nw_ref, sel_ref, shc_ref, shm_ref, o_ref, xs_scr, dts_scr, cum_scr, hbs_scr, hf_scr, hb_scr,
                win_scr,
                *, cs, sub, nb):
    rows = cs * sub
    ph = pl.program_id(1)
    t = pl.program_id(2)
    fwd0, bwd0 = 0, SSM_HEADS
    a_row = -jnp.exp(alog_ref[...])

    def decay_terms(dt_raw):
        dt = _softplus(dt_raw + dtb_ref[...])
        return dt, _cumsum_rows(dt * a_row)

    def bcast8(row):
        return jnp.broadcast_to(row, (8, LANES))

    @pl.when(ph == 0)
    def _():
        blk = nb - 1 - t

        @pl.when(t == 0)
        def _():
            hb_scr[...] = jnp.zeros(hb_scr.shape, F32)

        left = jnp.where(blk == 0, mx_ref[...], xl_ref[...])
        right = jnp.where(blk == nb - 1, jnp.zeros_like(xr_ref[...]), xr_ref[...])
        win_scr[0:HALO, :] = left
        win_scr[HALO:HALO + rows, :] = xc_ref[...]
        win_scr[HALO + rows:HALO + rows + HALO, :] = right

        hb = hb_scr[...]
        for si in reversed(range(sub)):
            cc = blk * sub + si
            xbc = _conv_silu(win_scr[si * cs:si * cs + cs + 2 * HALO, :], shc_ref, cw_ref, cb_ref, cs)
            xs_scr[cc] = xbc.astype(BF16)
            dt, cum = decay_terms(dt_ref[si * cs:(si + 1) * cs, :])
            dts_scr[cc] = dt
            cum_scr[cc] = cum
            eb = cum - dt * a_row
            w_b, dec = _expand_rows([(jnp.exp(eb) * dt, bwd0), (bcast8(jnp.exp(cum[cs - 1:cs, :])), bwd0)],
                                    sel_ref)
            xw = (xbc[:, :SSM_INNER] * w_b).astype(BF16)
            bm_t = xbc[:, SSM_INNER:SSM_INNER + LANES].T.astype(BF16)
            hbs_scr[cc] = hb.astype(BF16)
            hb = hb * _stack_decay(dec[0:1]) + _state_update(bm_t, xw)
        hb_scr[...] = hb

    @pl.when(ph == 1)
    def _():
        @pl.when(t == 0)
        def _():
            wm = jnp.concatenate([jnp.zeros((HALO, SSM_CONV_DIM), BF16), mx_ref[...], xc_ref[0:HALO, :]], axis=0)
            xm = _conv_silu(wm, shm_ref, cw_ref, cb_ref, N_META)
            dtm, cumm = decay_terms(mdt_ref[...])
            (w_m,) = _expand_rows([(jnp.exp(cumm[N_META - 1:N_META, :] - cumm) * dtm, fwd0)], sel_ref)
            xwm = (xm[:, :SSM_INNER] * w_m).astype(BF16)
            bmm_t = xm[:, SSM_INNER:SSM_INNER + LANES].T.astype(BF16)
            hf_scr[...] = _state_update(bmm_t, xwm)

        lane = lax.broadcasted_iota(jnp.int32, (cs, LANES), 1)
        l_i = lax.broadcasted_iota(jnp.int32, (cs, cs), 0)
        s_i = lax.broadcasted_iota(jnp.int32, (cs, cs), 1)
        lower = s_i <= l_i
        diag = s_i == l_i
        hpg = SSM_HEADS // SSM_GROUPS
        zx = jnp.zeros((cs, LANES), BF16)
        nt_dims = (((1,), (1,)), ((), ()))

        hf = hf_scr[...]
        for si in range(sub):
            cc = t * sub + si
            xbc = xs_scr[cc]
            x_bf = xbc[:, :SSM_INNER]
            bm = xbc[:, SSM_INNER:SSM_INNER + LANES]
            cm = xbc[:, SSM_INNER + LANES:SSM_INNER + 2 * LANES]
            x = x_bf.astype(F32)

            dt = dts_scr[cc]
            cum = cum_scr[cc]
            eb = cum - dt * a_row
            dt_t, cum_t, eb_t = dt.T, cum.T, eb.T
            last = cum[cs - 1:cs, :]

            c_grp = [jnp.where(lane // SSM_STATE == g, cm, jnp.zeros_like(cm)) for g in range(SSM_GROUPS)]
            g_mats = [lax.dot_general(c_g, bm, nt_dims, preferred_element_type=F32) for c_g in c_grp]

            pieces = []
            for hp in range(SSM_HEADS // 2):
                w_pair = []
                for h in (2 * hp, 2 * hp + 1):
                    arg_f = cum[:, fwd0 + h:fwd0 + h + 1] - cum_t[fwd0 + h:fwd0 + h + 1, :]
                    arg_b = eb_t[bwd0 + h:bwd0 + h + 1, :] - eb[:, bwd0 + h:bwd0 + h + 1]
                    e = jnp.exp(jnp.minimum(jnp.where(lower, arg_f, arg_b), 0.0))
                    dt_f_row = dt_t[fwd0 + h:fwd0 + h + 1, :]
                    dt_b_row = dt_t[bwd0 + h:bwd0 + h + 1, :]
                    m = e * jnp.where(lower, dt_f_row, dt_b_row) + jnp.where(diag, dt_b_row, 0.0)
                    w_pair.append((g_mats[h // hpg] * m).astype(BF16))
                xp = x_bf[:, hp * LANES:(hp + 1) * LANES]
                rhs = jnp.concatenate([jnp.where(lane < SSM_HEADDIM, xp, zx),
                                       jnp.where(lane >= SSM_HEADDIM, xp, zx)], axis=0)
                pieces.append(jnp.dot(jnp.concatenate(w_pair, axis=1), rhs, preferred_element_type=F32))
            y = jnp.concatenate(pieces, axis=1)

            d_f, d_b, w_f, dec = _expand_rows(
                [(jnp.exp(cum), fwd0), (jnp.exp(last - eb), bwd0), (jnp.exp(last - cum) * dt, fwd0),
                 (bcast8(jnp.exp(last)), fwd0)], sel_ref)
            hf_bf = hf.astype(BF16)
            hb_bf = hbs_scr[cc]
            y = y + d_f * jnp.concatenate([jnp.dot(c_g, hf_bf, preferred_element_type=F32) for c_g in c_grp],
                                          axis=1)
            y = y + d_b * jnp.concatenate([jnp.dot(c_g, hb_bf, preferred_element_type=F32) for c_g in c_grp],
                                          axis=1)
            y = y + x * dsk_ref[...]

            xw = (x * w_f).astype(BF16)
            hf = hf * _stack_decay(dec[0:1]) + _state_update(bm.astype(F32).T.astype(BF16), xw)

            zf = z_ref[si * cs:(si + 1) * cs, :].astype(F32)
            y = y * (zf * jax.nn.sigmoid(zf))
            o_ref[si * cs:(si + 1) * cs, :] = _rmsnorm(y, nw_ref[...]).astype(o_ref.dtype)
        hf_scr[...] = hf


def _ssd_call(z, xbc, dt, mxbc, mdt, cw, cb, dtb, alog, dskip, nw, sel, shc, shm, batch, seq, cs, sub):
    n = z.shape[0]
    rows = cs * sub
    assert n == batch * seq and seq % rows == 0 and cs % HALO == 0
    nc = seq // cs
    nb = seq // rows
    hpb = rows // HALO
    n_halo = n // HALO

    def ph0_block(ph, t):
        return (1 - ph) * (nb - 1 - t)

    const2 = lambda shape: pl.BlockSpec(shape, lambda b, ph, t: (0, 0))
    return pl.pallas_call(
        functools.partial(_ssd_kernel, cs=cs, sub=sub, nb=nb),
        grid=(batch, 2, nb),
        in_specs=[
            pl.BlockSpec((rows, SSM_INNER), lambda b, ph, t: (b * nb + ph * t, 0)),
            pl.BlockSpec((rows, SSM_CONV_DIM), lambda b, ph, t: (b * nb + ph0_block(ph, t), 0)),
            pl.BlockSpec((HALO, SSM_CONV_DIM),
                         lambda b, ph, t: (jnp.maximum((b * nb + ph0_block(ph, t)) * hpb - 1, 0), 0)),
            pl.BlockSpec((HALO, SSM_CONV_DIM),
                         lambda b, ph, t: (jnp.minimum((b * nb + ph0_block(ph, t) + 1) * hpb, n_halo - 1), 0)),
            pl.BlockSpec((rows, DT_PAD), lambda b, ph, t: (b * nb + ph0_block(ph, t), 0)),
            const2(mxbc.shape), const2(mdt.shape), const2(cw.shape), const2(cb.shape), const2(dtb.shape),
            const2(alog.shape), const2(dskip.shape), const2(nw.shape), const2(sel.shape), const2(shc.shape),
            const2(shm.shape),
        ],
        out_specs=pl.BlockSpec((rows, SSM_INNER), lambda b, ph, t: (b * nb + ph * t, 0)),
        out_shape=jax.ShapeDtypeStruct((n, SSM_INNER), BF16),
        scratch_shapes=[
            pltpu.VMEM((nc, cs, SSM_CONV_DIM), BF16),
            pltpu.VMEM((nc, cs, DT_PAD), F32),
            pltpu.VMEM((nc, cs, DT_PAD), F32),
            pltpu.VMEM((nc, LANES, GROUP_COLS), BF16),
            pltpu.VMEM((LANES, GROUP_COLS), F32),
            pltpu.VMEM((LANES, GROUP_COLS), F32),
            pltpu.VMEM((rows + 2 * HALO, SSM_CONV_DIM), BF16),
        ],
        compiler_params=pltpu.CompilerParams(dimension_semantics=("arbitrary",) * 3, vmem_limit_bytes=VMEM_LIMIT),
        name="bi_ssd",
    )(z, xbc, xbc, xbc, dt, mxbc, mdt, cw, cb, dtb, alog, dskip, nw, sel, shc, shm)


def _head_selector():
    k = np.arange(LANES)[:, None]
    col = np.arange(SSM_INNER)[None, :]
    return jnp.asarray((k % SSM_HEADS == col // SSM_HEADDIM) & (k < 2 * SSM_HEADS), BF16)


def _prep_weights(ffn1_norm_w, ffn1_w_gate, ffn1_w_up, ffn1_w_down, mix_norm_w, w_in, lambda_q1, lambda_k1,
                  lambda_q2, lambda_k2, attn_subln_w, conv_w, conv_b, dt_bias_fwd, dt_bias_bwd, a_log_fwd,
                  a_log_bwd, ssm_d, ssm_norm_w, w_out, ffn2_norm_w, ffn2_w_gate, ffn2_w_up, ffn2_w_down,
                  final_norm_w):
    def ffn(norm_w, wg, wu, wd):
        return norm_w[0][None, :], wg[0].astype(BF16), wu[0].astype(BF16), wd[0].astype(BF16)

    pad_lanes = lambda v, width: jnp.pad(v, (0, width - v.shape[0]))[None, :]
    o_v, o_z = 2 * ATT_QK, 2 * ATT_QK + ATT_V
    wi = w_in[0].astype(BF16)
    return dict(
        ffn1=ffn(ffn1_norm_w, ffn1_w_gate, ffn1_w_up, ffn1_w_down),
        ffn2=ffn(ffn2_norm_w, ffn2_w_gate, ffn2_w_up, ffn2_w_down),
        mix_norm=mix_norm_w[0][None, :],
        win=wi[:, :_IN_SEGS[-1][1]],
        wdt=wi[:, _IN_SEGS[-1][1]:],
        wvt=wi[:, o_v:o_z].T,
        lamv=jnp.stack([lambda_q1[0], lambda_k1[0], lambda_q2[0], lambda_k2[0]]),
        subw_col=attn_subln_w[0][:, None],
        cw=jnp.pad(conv_w[0], ((0, 8 - SSM_CONV), (0, 0))),
        cb=conv_b[0][None, :],
        dtb=pad_lanes(jnp.concatenate([dt_bias_fwd[0], dt_bias_bwd[0]]), DT_PAD),
        alog=pad_lanes(jnp.concatenate([a_log_fwd[0], a_log_bwd[0]]), DT_PAD),
        dskip=jnp.repeat(ssm_d[0], SSM_HEADDIM)[None, :],
        ssm_norm=ssm_norm_w[0][None, :],
        wo=w_out[0].astype(BF16).reshape(2, ATT_V, D_MODEL),
        final=final_norm_w[None, :],
        sel=_head_selector(),
        shc=_shift_matrix(SSD_CHUNK),
        shm=_shift_matrix(N_META),
    )


def _encode(x, w, meta_proj, bias_tabs):
    batch, seq, _ = x.shape
    km, vmt, mxbc, mdt = meta_proj
    h0 = x.reshape(batch * seq, D_MODEL)
    h1 = _ffn_call(h0, *w["ffn1"])
    q, k, vt, z, xbc, dt = _inproj_call(h1, w["mix_norm"], w["win"], w["wdt"], w["wvt"])
    att = _attn_call(w["lamv"], q, k, vt, km, vmt, *bias_tabs, w["subw_col"], batch, seq, ATT_TQ, ATT_TK)
    ssm = _ssd_call(z, xbc, dt, mxbc, mdt, w["cw"], w["cb"], w["dtb"], w["alog"], w["dskip"], w["ssm_norm"],
                    w["sel"], w["shc"], w["shm"], batch, seq, SSD_CHUNK, SSD_SUB)
    y = _ffn_call(h1, *w["ffn2"], mix=(att, ssm, w["wo"]), final_w=w["final"])
    return y.reshape(batch, seq, D_MODEL)


def kernel(x_prompt, x_sample, meta_tokens, ffn1_norm_w, ffn1_w_gate, ffn1_w_up, ffn1_w_down, mix_norm_w, w_in, rel_bias, lambda_q1, lambda_k1, lambda_q2, lambda_k2, attn_subln_w, conv_w, conv_b, dt_bias_fwd, dt_bias_bwd, a_log_fwd, a_log_bwd, ssm_d, ssm_norm_w, w_out, ffn2_norm_w, ffn2_w_gate, ffn2_w_up, ffn2_w_down, final_norm_w):
    w = _prep_weights(ffn1_norm_w, ffn1_w_gate, ffn1_w_up, ffn1_w_down, mix_norm_w, w_in, lambda_q1, lambda_k1,
                      lambda_q2, lambda_k2, attn_subln_w, conv_w, conv_b, dt_bias_fwd, dt_bias_bwd, a_log_fwd,
                      a_log_bwd, ssm_d, ssm_norm_w, w_out, ffn2_norm_w, ffn2_w_gate, ffn2_w_up, ffn2_w_down,
                      final_norm_w)
    hm = _ffn_call(meta_tokens, *w["ffn1"])
    _, km, vmt, _, mxbc, mdt = _inproj_call(hm, w["mix_norm"], w["win"], w["wdt"], w["wvt"])
    meta_proj = (km, vmt[0], mxbc, mdt)
    bias_tabs = _bias_call(rel_bias, ATT_TQ, ATT_TK)
    return (_encode(x_prompt, w, meta_proj, bias_tabs), _encode(x_sample, w, meta_proj, bias_tabs))
```

```python
import functools
import math

import jax
import jax.numpy as jnp
import numpy as np
from jax import lax
from jax.experimental import pallas as pl
from jax.experimental.pallas import tpu as pltpu

F32 = jnp.float32
BF16 = jnp.bfloat16

D_MODEL = 1024
N_META = 16
N_ATT_HEADS = 8
ATT_DH = 64
ATT_DV = 128
ATT_QK = 1024
ATT_V = 1024
NUM_BUCKETS = 32
MAX_DISTANCE = 128
SSM_HEADS = 16
SSM_HEADDIM = 64
SSM_INNER = 1024
SSM_GROUPS = 2
SSM_STATE = 64
SSM_CONV = 7
SSM_CONV_DIM = 1280
D_FF = 2816
EPS = 1e-6
LAYER = 0
LAM_INIT = 0.8 - 0.6 * math.exp(-0.3 * LAYER)
LOG2E = math.log2(math.e)
Q_SCALE = ATT_DH ** -0.5 * LOG2E
NORM_SLACK = 1.02
MAX_SHIFT_GAP = 100.0

LANES = 128
BF16_ROWS = 16
VMEM_LIMIT = 56 * 1024 * 1024

FF_TILE = 256
N_FF = D_FF // FF_TILE
DT_PAD = LANES
DT_COLS = 2 * SSM_HEADS
T5_BAND = 91

ROW_TILE = 512
ATT_TQ = 1024
ATT_TK = 512
TAB_TQ = 512
ATT_COLS = 256
SSD_CHUNK = 128
MXU_TAPS = (0, 1, 5)
SSD_SUB = 8
HALO = BF16_ROWS
VT_ROWS = ATT_DV + BF16_ROWS


def _rmsnorm(x, w):
    ms = jnp.mean(x * x, axis=-1, keepdims=True)
    return x * lax.rsqrt(ms + EPS) * w


def _resident(shape):
    nd = len(shape)
    return pl.BlockSpec(shape, lambda *_: (0,) * nd, pipeline_mode=pl.Buffered(1))


def _ffn_kernel(*refs, has_mix, has_final):
    it = iter(refs)
    h_ref = next(it)
    if has_mix:
        att_ref, ssm_ref, wo_ref = next(it), next(it), next(it)
    nw_ref, wg_ref, wu_ref, wd_ref = next(it), next(it), next(it), next(it)
    fw_ref = next(it) if has_final else None
    o_ref = next(it)

    h = h_ref[...]
    if has_mix:
        h = (h + jnp.dot(att_ref[...], wo_ref[0], preferred_element_type=F32)
             + jnp.dot(ssm_ref[...], wo_ref[1], preferred_element_type=F32))
    u = _rmsnorm(h, nw_ref[...]).astype(BF16)
    acc = jnp.zeros_like(h)
    for j in range(N_FF):
        ff = slice(j * FF_TILE, (j + 1) * FF_TILE)
        g = jnp.dot(u, wg_ref[:, ff], preferred_element_type=F32)
        up = jnp.dot(u, wu_ref[:, ff], preferred_element_type=F32)
        a = (g * jax.nn.sigmoid(g) * up).astype(BF16)
        acc = acc + jnp.dot(a, wd_ref[ff, :], preferred_element_type=F32)
    h = h + 0.5 * acc
    if has_final:
        h = _rmsnorm(h, fw_ref[...])
    o_ref[...] = h


def _ffn_call(h, norm_w, wg, wu, wd, mix=None, final_w=None):
    n = h.shape[0]
    tm = min(ROW_TILE, n)
    assert n % tm == 0
    row = lambda width: pl.BlockSpec((tm, width), lambda i: (i, 0))
    args, specs = [h], [row(D_MODEL)]
    if mix is not None:
        att, ssm, wo = mix
        args += [att, ssm, wo]
        specs += [row(ATT_V), row(SSM_INNER), _resident(wo.shape)]
    args += [norm_w, wg, wu, wd]
    specs += [_resident(norm_w.shape), _resident(wg.shape), _resident(wu.shape), _resident(wd.shape)]
    if final_w is not None:
        args.append(final_w)
        specs.append(_resident(final_w.shape))
    return pl.pallas_call(
        functools.partial(_ffn_kernel, has_mix=mix is not None, has_final=final_w is not None),
        grid=(n // tm,),
        in_specs=specs,
        out_specs=row(D_MODEL),
        out_shape=jax.ShapeDtypeStruct((n, D_MODEL), F32),
        compiler_params=pltpu.CompilerParams(dimension_semantics=("arbitrary",), vmem_limit_bytes=VMEM_LIMIT),
        name="ffn_mix" if mix is not None else "ffn",
    )(*args)


_IN_SEGS = (("q", 0, ATT_QK), ("k", ATT_QK, ATT_QK), ("z", 2 * ATT_QK + ATT_V, SSM_INNER),
            ("xbc", 2 * ATT_QK + ATT_V + SSM_INNER, SSM_CONV_DIM),
            ("dt", 2 * ATT_QK + ATT_V + SSM_INNER + SSM_CONV_DIM, DT_COLS))


def _inproj_kernel(h_ref, nw_ref, win_ref, wvt_ref, q_ref, k_ref, vt_ref, z_ref, xbc_ref, dt_ref):
    u = _rmsnorm(h_ref[...], nw_ref[...]).astype(BF16)
    outs = dict(q=q_ref, k=k_ref, z=z_ref, xbc=xbc_ref, dt=dt_ref)
    for name, c0, width in _IN_SEGS:
        o_ref = outs[name]
        step = 512 if width % 512 == 0 else (256 if width % 256 == 0 else width)
        for s in range(0, width, step):
            r = jnp.dot(u, win_ref[:, c0 + s:c0 + s + step], preferred_element_type=F32)
            if name == "q":
                r = r * Q_SCALE
            o_ref[:, s:s + step] = r.astype(o_ref.dtype)
    dt_ref[:, DT_COLS:] = jnp.zeros((dt_ref.shape[0], DT_PAD - DT_COLS), dt_ref.dtype)
    nt_dims = (((1,), (1,)), ((), ()))
    ones = jnp.ones((VT_ROWS - ATT_DV, u.shape[0]), vt_ref.dtype)
    for s in range(0, ATT_V, 256):
        r = lax.dot_general(wvt_ref[s:s + 256, :], u, nt_dims, preferred_element_type=F32).astype(vt_ref.dtype)
        for hh in range(256 // ATT_DV):
            head = s // ATT_DV + hh
            vt_ref[0, head * VT_ROWS:head * VT_ROWS + ATT_DV, :] = r[hh * ATT_DV:(hh + 1) * ATT_DV]
            vt_ref[0, head * VT_ROWS + ATT_DV:(head + 1) * VT_ROWS, :] = ones


def _inproj_call(h, norm_w, win, wvt):
    n = h.shape[0]
    tm = min(ATT_TK, n)
    assert n % tm == 0
    row = lambda width: pl.BlockSpec((tm, width), lambda i: (i, 0))
    widths = (ATT_QK, ATT_QK, SSM_INNER, SSM_CONV_DIM, DT_PAD)
    dtypes = (BF16, BF16, BF16, BF16, F32)
    shapes = [jax.ShapeDtypeStruct((n, w), dt) for w, dt in zip(widths, dtypes)]
    specs = [row(w) for w in widths]
    shapes.insert(2, jax.ShapeDtypeStruct((n // tm, N_ATT_HEADS * VT_ROWS, tm), BF16))
    specs.insert(2, pl.BlockSpec((1, N_ATT_HEADS * VT_ROWS, tm), lambda i: (i, 0, 0)))
    return pl.pallas_call(
        _inproj_kernel,
        grid=(n // tm,),
        in_specs=[row(D_MODEL), _resident(norm_w.shape), _resident(win.shape), _resident(wvt.shape)],
        out_specs=specs,
        out_shape=shapes,
        compiler_params=pltpu.CompilerParams(dimension_semantics=("arbitrary",), vmem_limit_bytes=VMEM_LIMIT),
        name="inproj",
    )(h, norm_w, win, wvt)


def _t5_bias(rel, rb_ref, head):
    half = NUM_BUCKETS // 2
    max_exact = half // 2
    ret = jnp.where(rel > 0, half, 0)
    n = jnp.abs(rel)
    nf = jnp.maximum(n, 1).astype(F32)
    large = max_exact + (jnp.log(nf / max_exact) / math.log(MAX_DISTANCE / max_exact)
                         * (half - max_exact)).astype(jnp.int32)
    large = jnp.minimum(large, half - 1)
    bucket = ret + jnp.where(n < max_exact, n, large)
    val = jnp.zeros(rel.shape, F32)
    for jb in range(NUM_BUCKETS):
        val = jnp.where(bucket == jb, rb_ref[jb, head], val)
    return val * LOG2E


def _bias_geometry(tq, tk):
    unit = min(tq, tk)
    assert tq % unit == 0 and tk % unit == 0 and unit >= T5_BAND + 1
    return unit, tk // unit, tq // unit


def _bias_kernel(rb_ref, tab_ref, mtab_ref, rng_ref, *, tq, tk):
    head = pl.program_id(0)
    unit, lo, hi = _bias_geometry(TAB_TQ, tk)
    n_near = lo + hi + 1
    top = rb_ref[0, head]
    bottom = rb_ref[0, head]
    for jb in range(1, NUM_BUCKETS):
        top = jnp.maximum(top, rb_ref[jb, head])
        bottom = jnp.minimum(bottom, rb_ref[jb, head])
    rng_ref[0, 0:1, :] = jnp.full((1, LANES), top * LOG2E, F32)
    rng_ref[0, 1:2, :] = jnp.full((1, LANES), bottom * LOG2E, F32)
    rng_ref[0, 2:8, :] = jnp.zeros((6, LANES), F32)
    far_left = rb_ref[NUM_BUCKETS // 2 - 1, head] * LOG2E
    far_right = rb_ref[NUM_BUCKETS - 1, head] * LOG2E
    krow = lax.broadcasted_iota(jnp.int32, (LANES, LANES), 0)
    qcol = lax.broadcasted_iota(jnp.int32, (LANES, LANES), 1)
    for t in range(n_near):
        for a in range(tk // LANES):
            for b in range(TAB_TQ // LANES):
                base = (a - b) * LANES + (t - lo) * unit
                blk = (slice(a * LANES, (a + 1) * LANES), slice(b * LANES, (b + 1) * LANES))
                if base + LANES - 1 <= -T5_BAND:
                    tab_ref[(0, t) + blk] = jnp.full((LANES, LANES), far_left, F32)
                elif base - LANES + 1 >= T5_BAND:
                    tab_ref[(0, t) + blk] = jnp.full((LANES, LANES), far_right, F32)
                else:
                    tab_ref[(0, t) + blk] = _t5_bias(krow - qcol + base, rb_ref, head)
    tab_ref[0, n_near] = jnp.full((tk, TAB_TQ), far_left, F32)
    tab_ref[0, n_near + 1] = jnp.full((tk, TAB_TQ), far_right, F32)
    mrow = lax.broadcasted_iota(jnp.int32, (N_META, tq), 0)
    mcol = lax.broadcasted_iota(jnp.int32, (N_META, tq), 1)
    mtab_ref[0, 0] = _t5_bias(mrow - N_META - mcol, rb_ref, head)
    mtab_ref[0, 1] = jnp.full((N_META, tq), far_left, F32)


def _bias_call(rel_bias, tq, tk):
    assert tq % TAB_TQ == 0
    _, lo, hi = _bias_geometry(TAB_TQ, tk)
    nt = lo + hi + 3
    return pl.pallas_call(
        functools.partial(_bias_kernel, tq=tq, tk=tk),
        grid=(N_ATT_HEADS,),
        in_specs=[pl.BlockSpec(memory_space=pltpu.SMEM)],
        out_specs=[pl.BlockSpec((1, nt, tk, TAB_TQ), lambda h: (h, 0, 0, 0)),
                   pl.BlockSpec((1, 2, N_META, tq), lambda h: (h, 0, 0, 0)),
                   pl.BlockSpec((1, 8, LANES), lambda h: (h, 0, 0))],
        out_shape=[jax.ShapeDtypeStruct((N_ATT_HEADS, nt, tk, TAB_TQ), F32),
                   jax.ShapeDtypeStruct((N_ATT_HEADS, 2, N_META, tq), F32),
                   jax.ShapeDtypeStruct((N_ATT_HEADS, 8, LANES), F32)],
        compiler_params=pltpu.CompilerParams(dimension_semantics=("arbitrary",)),
        name="t5_bias",
    )(rel_bias)


def _attn_kernel(lam_ref, qall_ref, k_ref, vt_ref, km_ref, vmt_ref, tab_ref, mtab_ref, rng_ref, sw_ref, o_ref,
                 sa_ref, sb_ref, mca_ref, mcb_ref, m_scr, acc_scr, accp_scr, shift_scr, flag_scr,
                 *, tq, tk, nkc, nq, n_steps):
    g = pl.program_id(0)
    qi = jnp.minimum(g, n_steps - 1) % nq
    unit, lo, hi = _bias_geometry(TAB_TQ, tk)
    n_near = lo + hi + 1
    nt_dims = (((1,), (1,)), ((), ()))
    n_col = 2 * tq // ATT_COLS

    def finalize_previous():
        acc = accp_scr[...]
        o = acc[:ATT_DV] / acc[ATT_DV:ATT_DV + 1]
        lv = lam_ref[...]
        lam = (jnp.exp(jnp.sum(lv[0:1] * lv[1:2], axis=1, keepdims=True))
               - jnp.exp(jnp.sum(lv[2:3] * lv[3:4], axis=1, keepdims=True)) + LAM_INIT)
        out = o[:, :tq] - lam * o[:, tq:]
        ms = jnp.mean(out * out, axis=0, keepdims=True)
        out = out * lax.rsqrt(ms + EPS) * sw_ref[...] * (1.0 - LAM_INIT)
        o_ref[...] = out.T.astype(o_ref.dtype)

    @pl.when(g == 0)
    def _():
        accp_scr[...] = jnp.ones(accp_scr.shape, F32)

    @pl.when(g < n_steps)
    def _():
        q = qall_ref[pl.ds(pl.multiple_of(qi * tq, tq), tq), :]
        lane = lax.broadcasted_iota(jnp.int32, (tq, LANES), 1)
        zero = jnp.zeros_like(q)
        q2 = jnp.concatenate([jnp.where(lane < ATT_DH, q, zero), jnp.where(lane >= ATT_DH, q, zero)], axis=0)

        def bias_tile(j, c):
            q0 = (c * ATT_COLS) % tq
            du = j * (tk // unit) - (qi * (tq // TAB_TQ) + q0 // TAB_TQ) * (TAB_TQ // unit)
            idx = jnp.where(du < -lo, n_near, jnp.where(du > hi, n_near + 1, du + lo))
            return tab_ref[0, idx, :, pl.ds(q0 % TAB_TQ, ATT_COLS)]

        def add_bias(s, b):
            return jnp.concatenate([s[:, :tq] + b, s[:, tq:] + b], axis=1)

        @pl.when(qi == 0)
        def _():
            half = jnp.where(lax.broadcasted_iota(jnp.int32, (LANES, LANES), 0) // ATT_DH
                             == lax.broadcasted_iota(jnp.int32, (LANES, LANES), 1), 1.0, 0.0).astype(BF16)

            def max_sq_norm(x_ref):
                x = x_ref[...]
                sq = jnp.dot(x * x, half, preferred_element_type=F32)
                return jnp.max(sq, axis=0, keepdims=True)

            bound2 = max_sq_norm(qall_ref) * jnp.maximum(max_sq_norm(k_ref), max_sq_norm(km_ref))
            lane_row = lax.broadcasted_iota(jnp.int32, (1, LANES), 1)
            qk_bound = [NORM_SLACK * jnp.sqrt(jnp.max(jnp.where(lane_row == mp, bound2, 0.0), axis=1, keepdims=True))
                        for mp in range(2)]
            hi_b = rng_ref[0, 0:1, 0:1]
            lo_b = rng_ref[0, 1:2, 0:1]
            col = lax.broadcasted_iota(jnp.int32, (1, 2 * tq), 1)
            shift_scr[...] = jnp.where(col < tq, qk_bound[0], qk_bound[1]) + hi_b
            worst_gap = 2.0 * jnp.maximum(qk_bound[0], qk_bound[1]) + (hi_b - lo_b)
            flag_scr[0] = (worst_gap[0, 0] <= MAX_SHIFT_GAP).astype(jnp.int32)

        shift = shift_scr[...]
        bounded = flag_scr[0] == 1

        @pl.when(bounded)
        def _():
            finalize_previous()
            sm = lax.dot_general(km_ref[...], q2, nt_dims, preferred_element_type=F32)
            sm = add_bias(sm, mtab_ref[0, jnp.minimum(qi, 1)])
            acc_scr[...] = jnp.dot(vmt_ref[...], jnp.exp2(sm - shift).astype(BF16), preferred_element_type=F32)

            def stage_logits(u, j, c):
                buf = (sa_ref, sb_ref)[u % 2]
                buf[:, pl.ds(c * ATT_COLS, ATT_COLS)] = lax.dot_general(
                    k_ref[pl.ds(j * tk, tk), :], q2[c * ATT_COLS:(c + 1) * ATT_COLS], nt_dims,
                    preferred_element_type=F32)

            units = [(j, c) for j in range(nkc) for c in range(n_col)]
            stage_logits(0, *units[0])
            for u, (j, c) in enumerate(units):
                if u + 1 < len(units):
                    stage_logits(u + 1, *units[u + 1])
                cols = pl.ds(c * ATT_COLS, ATT_COLS)
                s = (sa_ref, sb_ref)[u % 2][:, cols] + bias_tile(j, c)
                p = jnp.exp2(s - shift[:, c * ATT_COLS:(c + 1) * ATT_COLS]).astype(BF16)
                acc_scr[:, cols] += jnp.dot(vt_ref[j], p, preferred_element_type=F32)
            accp_scr[...] = acc_scr[...]

        @pl.when(jnp.logical_not(bounded))
        def _():
            finalize_previous()

            def produce(j, c, s_ref, mc_ref):
                cols = pl.ds(c * ATT_COLS, ATT_COLS)
                s = lax.dot_general(k_ref[pl.ds(j * tk, tk), :], q2[c * ATT_COLS:(c + 1) * ATT_COLS], nt_dims,
                                    preferred_element_type=F32)
                s = s + bias_tile(j, c)
                s_ref[:, cols] = s
                mc_ref[:, cols] = jnp.max(s, axis=0, keepdims=True)

            def consume(s, m_cur, vt, cols, first=False):
                if first:
                    m_new = m_cur
                else:
                    m_prev = m_scr[:, cols]
                    m_new = jnp.maximum(m_prev, m_cur)
                    alpha = jnp.exp2(m_prev - m_new)
                p = jnp.exp2(s - m_new).astype(BF16)
                pv = jnp.dot(vt, p, preferred_element_type=F32)
                acc_scr[:, cols] = pv if first else alpha * acc_scr[:, cols] + pv
                m_scr[:, cols] = m_new

            sm = lax.dot_general(km_ref[...], q2, nt_dims, preferred_element_type=F32)
            sm = add_bias(sm, mtab_ref[0, jnp.minimum(qi, 1)])
            consume(sm, jnp.max(sm, axis=0, keepdims=True), vmt_ref[...], pl.ds(0, 2 * tq), first=True)

            bufs = ((sa_ref, mca_ref), (sb_ref, mcb_ref))
            for c in range(n_col):
                produce(0, c, *bufs[0])
            for j in range(nkc):
                s_ref, mc_ref = bufs[j % 2]
                for c in range(n_col):
                    cols = pl.ds(c * ATT_COLS, ATT_COLS)
                    if j + 1 < nkc:
                        produce(j + 1, c, *bufs[(j + 1) % 2])
                    consume(s_ref[:, cols], mc_ref[:, cols], vt_ref[j], cols)
            accp_scr[...] = acc_scr[...]

    @pl.when(g == n_steps)
    def _():
        finalize_previous()


def _attn_call(lamv, q, k, vt, km, vmt, tab, mtab, rng, subw_col, batch, seq, tq, tk):
    n = q.shape[0]
    assert n == batch * seq and seq % (2 * tk) == 0 and seq % tq == 0 and vt.shape[2] == tk
    nq = seq // tq
    nkc = seq // tk
    nt = tab.shape[1]
    n_steps = N_ATT_HEADS * batch * nq

    def tile(g):
        g = jnp.minimum(g, n_steps - 1)
        return g // (batch * nq), (g // nq) % batch, g % nq

    def cur(f):
        return lambda g: f(*tile(g))

    def prev(f):
        return lambda g: f(*tile(jnp.maximum(g - 1, 0)))

    return pl.pallas_call(
        functools.partial(_attn_kernel, tq=tq, tk=tk, nkc=nkc, nq=nq, n_steps=n_steps),
        grid=(n_steps + 1,),
        in_specs=[
            pl.BlockSpec(lamv.shape, lambda g: (0, 0)),
            pl.BlockSpec((seq, LANES), cur(lambda h, b, i: (b, h))),
            pl.BlockSpec((seq, LANES), cur(lambda h, b, i: (b, h))),
            pl.BlockSpec((nkc, VT_ROWS, tk), cur(lambda h, b, i: (b, h, 0))),
            pl.BlockSpec((N_META, LANES), cur(lambda h, b, i: (0, h))),
            pl.BlockSpec((VT_ROWS, N_META), cur(lambda h, b, i: (h, 0))),
            pl.BlockSpec((1, nt, tk, TAB_TQ), cur(lambda h, b, i: (h, 0, 0, 0))),
            pl.BlockSpec((1, 2, N_META, tq), cur(lambda h, b, i: (h, 0, 0, 0))),
            pl.BlockSpec((1, 8, LANES), cur(lambda h, b, i: (h, 0, 0))),
            pl.BlockSpec(subw_col.shape, lambda g: (0, 0)),
        ],
        out_specs=pl.BlockSpec((tq, LANES), prev(lambda h, b, i: (b * nq + i, h))),
        out_shape=jax.ShapeDtypeStruct((n, ATT_V), BF16),
        scratch_shapes=[pltpu.VMEM((tk, 2 * tq), F32)] * 2 + [pltpu.VMEM((1, 2 * tq), F32)] * 3
        + [pltpu.VMEM((VT_ROWS, 2 * tq), F32)] * 2 + [pltpu.VMEM((1, 2 * tq), F32), pltpu.SMEM((1,), jnp.int32)],
        compiler_params=pltpu.CompilerParams(dimension_semantics=("arbitrary",), vmem_limit_bytes=VMEM_LIMIT),
        name="diff_attn",
    )(lamv, q, k, vt, km, vmt, tab, mtab, rng, subw_col)


def _split3(x):
    hi = x.astype(BF16)
    r1 = x - hi.astype(F32)
    mid = r1.astype(BF16)
    lo = (r1 - mid.astype(F32)).astype(BF16)
    return hi, mid, lo


def _cumsum_rows(a):
    rows = a.shape[0]
    r_i = lax.broadcasted_iota(jnp.int32, (rows, rows), 0)
    c_i = lax.broadcasted_iota(jnp.int32, (rows, rows), 1)
    tri = jnp.where(c_i <= r_i, 1.0, 0.0).astype(BF16)
    out = None
    for term in _split3(a):
        part = jnp.dot(tri, term, preferred_element_type=F32)
        out = part if out is None else out + part
    return out


def _expand_rows(parts, sel_ref):
    masked = []
    for w, first in parts:
        lane = lax.broadcasted_iota(jnp.int32, w.shape, 1)
        masked.append(jnp.where((lane >= first) & (lane < first + SSM_HEADS), w, 0.0))
    stacked = jnp.concatenate(masked, axis=0)
    hi = stacked.astype(BF16)
    lo = (stacked - hi.astype(F32)).astype(BF16)
    sel = sel_ref[...]
    full = jnp.dot(hi, sel, preferred_element_type=F32) + jnp.dot(lo, sel, preferred_element_type=F32)
    outs, r0 = [], 0
    for w, _ in parts:
        outs.append(full[r0:r0 + w.shape[0]])
        r0 += w.shape[0]
    return outs


def _softplus(x):
    return jnp.maximum(x, 0.0) + jnp.log(1.0 + jnp.exp(-jnp.abs(x)))


GROUP_COLS = SSM_INNER // SSM_GROUPS


def _state_update(b_t, xw):
    return jnp.concatenate(
        [jnp.dot(b_t[g * SSM_STATE:(g + 1) * SSM_STATE], xw[:, g * GROUP_COLS:(g + 1) * GROUP_COLS],
                 preferred_element_type=F32) for g in range(SSM_GROUPS)], axis=0)


def _stack_decay(dec_row):
    return jnp.concatenate(
        [jnp.broadcast_to(dec_row[:, g * GROUP_COLS:(g + 1) * GROUP_COLS], (SSM_STATE, GROUP_COLS))
         for g in range(SSM_GROUPS)], axis=0)


def _conv_silu(win, shift_ref, cw_ref, cb_ref, rows):
    total = rows + 2 * HALO
    assert win.shape[0] == total and shift_ref.shape == (len(MXU_TAPS) * rows, total)
    shifted = jnp.dot(shift_ref[...], win, preferred_element_type=F32)
    win32 = win.astype(F32)
    acc = jnp.broadcast_to(cb_ref[...], (rows, SSM_CONV_DIM))
    for j in range(SSM_CONV):
        off = j - SSM_CONV // 2
        if j in MXU_TAPS:
            tap = shifted[MXU_TAPS.index(j) * rows:(MXU_TAPS.index(j) + 1) * rows]
        elif off == 0:
            tap = win32[HALO:HALO + rows]
        else:
            tap = pltpu.roll(win32, (total - off) % total, axis=0)[HALO:HALO + rows]
        acc = acc + cw_ref[j:j + 1, :] * tap
    return acc * jax.nn.sigmoid(acc)


def _shift_matrix(rows):
    offs = np.array([j - SSM_CONV // 2 for j in MXU_TAPS])
    src = HALO + np.arange(rows)[None, :] + offs[:, None]
    return jnp.asarray(src.reshape(-1)[:, None] == np.arange(rows + 2 * HALO)[None, :], BF16)


def _ssd_kernel(z_ref, xc_ref, xl_ref, xr_ref, dt_ref, mx_ref, mdt_ref, cw_ref, cb_ref, dtb_ref, alog_ref,
                dsk_ref, nw_ref, sel_ref, shc_ref, shm_ref, o_ref, xs_scr, dts_scr, cum_scr, hbs_scr, hf_scr, hb_scr,
                win_scr,
                *, cs, sub, nb):
    rows = cs * sub
    ph = pl.program_id(1)
    t = pl.program_id(2)
    fwd0, bwd0 = 0, SSM_HEADS
    a_row = -jnp.exp(alog_ref[...])

    def decay_terms(dt_raw):
        dt = _softplus(dt_raw + dtb_ref[...])
        return dt, _cumsum_rows(dt * a_row)

    def bcast8(row):
        return jnp.broadcast_to(row, (8, LANES))

    @pl.when(ph == 0)
    def _():
        blk = nb - 1 - t

        @pl.when(t == 0)
        def _():
            hb_scr[...] = jnp.zeros(hb_scr.shape, F32)

        left = jnp.where(blk == 0, mx_ref[...], xl_ref[...])
        right = jnp.where(blk == nb - 1, jnp.zeros_like(xr_ref[...]), xr_ref[...])
        win_scr[0:HALO, :] = left
        win_scr[HALO:HALO + rows, :] = xc_ref[...]
        win_scr[HALO + rows:HALO + rows + HALO, :] = right

        hb = hb_scr[...]
        for si in reversed(range(sub)):
            cc = blk * sub + si
            xbc = _conv_silu(win_scr[si * cs:si * cs + cs + 2 * HALO, :], shc_ref, cw_ref, cb_ref, cs)
            xs_scr[cc] = xbc.astype(BF16)
            dt, cum = decay_terms(dt_ref[si * cs:(si + 1) * cs, :])
            dts_scr[cc] = dt
            cum_scr[cc] = cum
            eb = cum - dt * a_row
            w_b, dec = _expand_rows([(jnp.exp(eb) * dt, bwd0), (bcast8(jnp.exp(cum[cs - 1:cs, :])), bwd0)],
                                    sel_ref)
            xw = (xbc[:, :SSM_INNER] * w_b).astype(BF16)
            bm_t = xbc[:, SSM_INNER:SSM_INNER + LANES].T.astype(BF16)
            hbs_scr[cc] = hb.astype(BF16)
            hb = hb * _stack_decay(dec[0:1]) + _state_update(bm_t, xw)
        hb_scr[...] = hb

    @pl.when(ph == 1)
    def _():
        @pl.when(t == 0)
        def _():
            wm = jnp.concatenate([jnp.zeros((HALO, SSM_CONV_DIM), BF16), mx_ref[...], xc_ref[0:HALO, :]], axis=0)
            xm = _conv_silu(wm, shm_ref, cw_ref, cb_ref, N_META)
            dtm, cumm = decay_terms(mdt_ref[...])
            (w_m,) = _expand_rows([(jnp.exp(cumm[N_META - 1:N_META, :] - cumm) * dtm, fwd0)], sel_ref)
            xwm = (xm[:, :SSM_INNER] * w_m).astype(BF16)
            bmm_t = xm[:, SSM_INNER:SSM_INNER + LANES].T.astype(BF16)
            hf_scr[...] = _state_update(bmm_t, xwm)

        lane = lax.broadcasted_iota(jnp.int32, (cs, LANES), 1)
        l_i = lax.broadcasted_iota(jnp.int32, (cs, cs), 0)
        s_i = lax.broadcasted_iota(jnp.int32, (cs, cs), 1)
        lower = s_i <= l_i
        diag = s_i == l_i
        hpg = SSM_HEADS // SSM_GROUPS
        zx = jnp.zeros((cs, LANES), BF16)
        nt_dims = (((1,), (1,)), ((), ()))

        hf = hf_scr[...]
        for si in range(sub):
            cc = t * sub + si
            xbc = xs_scr[cc]
            x_bf = xbc[:, :SSM_INNER]
            bm = xbc[:, SSM_INNER:SSM_INNER + LANES]
            cm = xbc[:, SSM_INNER + LANES:SSM_INNER + 2 * LANES]
            x = x_bf.astype(F32)

            dt = dts_scr[cc]
            cum = cum_scr[cc]
            eb = cum - dt * a_row
            dt_t, cum_t, eb_t = dt.T, cum.T, eb.T
            last = cum[cs - 1:cs, :]

            c_grp = [jnp.where(lane // SSM_STATE == g, cm, jnp.zeros_like(cm)) for g in range(SSM_GROUPS)]
            g_mats = [lax.dot_general(c_g, bm, nt_dims, preferred_element_type=F32) for c_g in c_grp]

            pieces = []
            for hp in range(SSM_HEADS // 2):
                w_pair = []
                for h in (2 * hp, 2 * hp + 1):
                    arg_f = cum[:, fwd0 + h:fwd0 + h + 1] - cum_t[fwd0 + h:fwd0 + h + 1, :]
                    arg_b = eb_t[bwd0 + h:bwd0 + h + 1, :] - eb[:, bwd0 + h:bwd0 + h + 1]
                    e = jnp.exp(jnp.minimum(jnp.where(lower, arg_f, arg_b), 0.0))
                    dt_f_row = dt_t[fwd0 + h:fwd0 + h + 1, :]
                    dt_b_row = dt_t[bwd0 + h:bwd0 + h + 1, :]
                    m = e * jnp.where(lower, dt_f_row, dt_b_row) + jnp.where(diag, dt_b_row, 0.0)
                    w_pair.append((g_mats[h // hpg] * m).astype(BF16))
                xp = x_bf[:, hp * LANES:(hp + 1) * LANES]
                rhs = jnp.concatenate([jnp.where(lane < SSM_HEADDIM, xp, zx),
                                       jnp.where(lane >= SSM_HEADDIM, xp, zx)], axis=0)
                pieces.append(jnp.dot(jnp.concatenate(w_pair, axis=1), rhs, preferred_element_type=F32))
            y = jnp.concatenate(pieces, axis=1)

            d_f, d_b, w_f, dec = _expand_rows(
                [(jnp.exp(cum), fwd0), (jnp.exp(last - eb), bwd0), (jnp.exp(last - cum) * dt, fwd0),
                 (bcast8(jnp.exp(last)), fwd0)], sel_ref)
            hf_bf = hf.astype(BF16)
            hb_bf = hbs_scr[cc]
            y = y + d_f * jnp.concatenate([jnp.dot(c_g, hf_bf, preferred_element_type=F32) for c_g in c_grp],
                                          axis=1)
            y = y + d_b * jnp.concatenate([jnp.dot(c_g, hb_bf, preferred_element_type=F32) for c_g in c_grp],
                                          axis=1)
            y = y + x * dsk_ref[...]

            xw = (x * w_f).astype(BF16)
            hf = hf * _stack_decay(dec[0:1]) + _state_update(bm.astype(F32).T.astype(BF16), xw)

            zf = z_ref[si * cs:(si + 1) * cs, :].astype(F32)
            y = y * (zf * jax.nn.sigmoid(zf))
            o_ref[si * cs:(si + 1) * cs, :] = _rmsnorm(y, nw_ref[...]).astype(o_ref.dtype)
        hf_scr[...] = hf


def _ssd_call(z, xbc, dt, mxbc, mdt, cw, cb, dtb, alog, dskip, nw, sel, shc, shm, batch, seq, cs, sub):
    n = z.shape[0]
    rows = cs * sub
    assert n == batch * seq and seq % rows == 0 and cs % HALO == 0
    nc = seq // cs
    nb = seq // rows
    hpb = rows // HALO
    n_halo = n // HALO

    def ph0_block(ph, t):
        return (1 - ph) * (nb - 1 - t)

    const2 = lambda shape: pl.BlockSpec(shape, lambda b, ph, t: (0, 0))
    return pl.pallas_call(
        functools.partial(_ssd_kernel, cs=cs, sub=sub, nb=nb),
        grid=(batch, 2, nb),
        in_specs=[
            pl.BlockSpec((rows, SSM_INNER), lambda b, ph, t: (b * nb + ph * t, 0)),
            pl.BlockSpec((rows, SSM_CONV_DIM), lambda b, ph, t: (b * nb + ph0_block(ph, t), 0)),
            pl.BlockSpec((HALO, SSM_CONV_DIM),
                         lambda b, ph, t: (jnp.maximum((b * nb + ph0_block(ph, t)) * hpb - 1, 0), 0)),
            pl.BlockSpec((HALO, SSM_CONV_DIM),
                         lambda b, ph, t: (jnp.minimum((b * nb + ph0_block(ph, t) + 1) * hpb, n_halo - 1), 0)),
            pl.BlockSpec((rows, DT_PAD), lambda b, ph, t: (b * nb + ph0_block(ph, t), 0)),
            const2(mxbc.shape), const2(mdt.shape), const2(cw.shape), const2(cb.shape), const2(dtb.shape),
            const2(alog.shape), const2(dskip.shape), const2(nw.shape), const2(sel.shape), const2(shc.shape),
            const2(shm.shape),
        ],
        out_specs=pl.BlockSpec((rows, SSM_INNER), lambda b, ph, t: (b * nb + ph * t, 0)),
        out_shape=jax.ShapeDtypeStruct((n, SSM_INNER), BF16),
        scratch_shapes=[
            pltpu.VMEM((nc, cs, SSM_CONV_DIM), BF16),
            pltpu.VMEM((nc, cs, DT_PAD), F32),
            pltpu.VMEM((nc, cs, DT_PAD), F32),
            pltpu.VMEM((nc, LANES, GROUP_COLS), BF16),
            pltpu.VMEM((LANES, GROUP_COLS), F32),
            pltpu.VMEM((LANES, GROUP_COLS), F32),
            pltpu.VMEM((rows + 2 * HALO, SSM_CONV_DIM), BF16),
        ],
        compiler_params=pltpu.CompilerParams(dimension_semantics=("arbitrary",) * 3, vmem_limit_bytes=VMEM_LIMIT),
        name="bi_ssd",
    )(z, xbc, xbc, xbc, dt, mxbc, mdt, cw, cb, dtb, alog, dskip, nw, sel, shc, shm)


def _head_selector():
    k = np.arange(LANES)[:, None]
    col = np.arange(SSM_INNER)[None, :]
    return jnp.asarray((k % SSM_HEADS == col // SSM_HEADDIM) & (k < 2 * SSM_HEADS), BF16)


def _prep_weights(ffn1_norm_w, ffn1_w_gate, ffn1_w_up, ffn1_w_down, mix_norm_w, w_in, lambda_q1, lambda_k1,
                  lambda_q2, lambda_k2, attn_subln_w, conv_w, conv_b, dt_bias_fwd, dt_bias_bwd, a_log_fwd,
                  a_log_bwd, ssm_d, ssm_norm_w, w_out, ffn2_norm_w, ffn2_w_gate, ffn2_w_up, ffn2_w_down,
                  final_norm_w):
    def ffn(norm_w, wg, wu, wd):
        return norm_w[0][None, :], wg[0].astype(BF16), wu[0].astype(BF16), wd[0].astype(BF16)

    pad_lanes = lambda v, width: jnp.pad(v, (0, width - v.shape[0]))[None, :]
    o_v, o_z = 2 * ATT_QK, 2 * ATT_QK + ATT_V
    wi = w_in[0].astype(BF16)
    return dict(
        ffn1=ffn(ffn1_norm_w, ffn1_w_gate, ffn1_w_up, ffn1_w_down),
        ffn2=ffn(ffn2_norm_w, ffn2_w_gate, ffn2_w_up, ffn2_w_down),
        mix_norm=mix_norm_w[0][None, :],
        win=wi,
        wvt=wi[:, o_v:o_z].T,
        lamv=jnp.stack([lambda_q1[0], lambda_k1[0], lambda_q2[0], lambda_k2[0]]),
        subw_col=attn_subln_w[0][:, None],
        cw=jnp.pad(conv_w[0], ((0, 8 - SSM_CONV), (0, 0))),
        cb=conv_b[0][None, :],
        dtb=pad_lanes(jnp.concatenate([dt_bias_fwd[0], dt_bias_bwd[0]]), DT_PAD),
        alog=pad_lanes(jnp.concatenate([a_log_fwd[0], a_log_bwd[0]]), DT_PAD),
        dskip=jnp.repeat(ssm_d[0], SSM_HEADDIM)[None, :],
        ssm_norm=ssm_norm_w[0][None, :],
        wo=w_out[0].astype(BF16).reshape(2, ATT_V, D_MODEL),
        final=final_norm_w[None, :],
        sel=_head_selector(),
        shc=_shift_matrix(SSD_CHUNK),
        shm=_shift_matrix(N_META),
    )


def _q_tile(seq):
    return seq if seq <= 2 * ATT_TQ else ATT_TQ


def _encode(x, w, meta_proj, bias_tabs):
    batch, seq, _ = x.shape
    km, vmt, mxbc, mdt = meta_proj
    h0 = x.reshape(batch * seq, D_MODEL)
    h1 = _ffn_call(h0, *w["ffn1"])
    q, k, vt, z, xbc, dt = _inproj_call(h1, w["mix_norm"], w["win"], w["wvt"])
    att = _attn_call(w["lamv"], q, k, vt, km, vmt, *bias_tabs, w["subw_col"], batch, seq, _q_tile(seq), ATT_TK)
    ssm = _ssd_call(z, xbc, dt, mxbc, mdt, w["cw"], w["cb"], w["dtb"], w["alog"], w["dskip"], w["ssm_norm"],
                    w["sel"], w["shc"], w["shm"], batch, seq, SSD_CHUNK, SSD_SUB)
    y = _ffn_call(h1, *w["ffn2"], mix=(att, ssm, w["wo"]), final_w=w["final"])
    return y.reshape(batch, seq, D_MODEL)


def kernel(x_prompt, x_sample, meta_tokens, ffn1_norm_w, ffn1_w_gate, ffn1_w_up, ffn1_w_down, mix_norm_w, w_in, rel_bias, lambda_q1, lambda_k1, lambda_q2, lambda_k2, attn_subln_w, conv_w, conv_b, dt_bias_fwd, dt_bias_bwd, a_log_fwd, a_log_bwd, ssm_d, ssm_norm_w, w_out, ffn2_norm_w, ffn2_w_gate, ffn2_w_up, ffn2_w_down, final_norm_w):
    w = _prep_weights(ffn1_norm_w, ffn1_w_gate, ffn1_w_up, ffn1_w_down, mix_norm_w, w_in, lambda_q1, lambda_k1,
                      lambda_q2, lambda_k2, attn_subln_w, conv_w, conv_b, dt_bias_fwd, dt_bias_bwd, a_log_fwd,
                      a_log_bwd, ssm_d, ssm_norm_w, w_out, ffn2_norm_w, ffn2_w_gate, ffn2_w_up, ffn2_w_down,
                      final_norm_w)
    hm = _ffn_call(meta_tokens, *w["ffn1"])
    _, km, vmt, _, mxbc, mdt = _inproj_call(hm, w["mix_norm"], w["win"], w["wvt"])
    meta_proj = (km, vmt[0], mxbc, mdt)
    bias_tabs = _bias_call(rel_bias, max(_q_tile(x_prompt.shape[1]), _q_tile(x_sample.shape[1])), ATT_TK)
    return (_encode(x_prompt, w, meta_proj, bias_tabs), _encode(x_sample, w, meta_proj, bias_tabs))
```

```python
import functools
import math

import jax
import jax.numpy as jnp
import numpy as np
from jax import lax
from jax.experimental import pallas as pl
from jax.experimental.pallas import tpu as pltpu

F32 = jnp.float32
BF16 = jnp.bfloat16

D_MODEL = 1024
N_META = 16
N_ATT_HEADS = 8
ATT_DH = 64
ATT_DV = 128
ATT_QK = 1024
ATT_V = 1024
NUM_BUCKETS = 32
MAX_DISTANCE = 128
SSM_HEADS = 16
SSM_HEADDIM = 64
SSM_INNER = 1024
SSM_GROUPS = 2
SSM_STATE = 64
SSM_CONV = 7
SSM_CONV_DIM = 1280
D_FF = 2816
EPS = 1e-6
LAYER = 0
LAM_INIT = 0.8 - 0.6 * math.exp(-0.3 * LAYER)
LOG2E = math.log2(math.e)
Q_SCALE = ATT_DH ** -0.5 * LOG2E
NORM_SLACK = 1.02
MAX_SHIFT_GAP = 100.0

LANES = 128
BF16_ROWS = 16
VMEM_LIMIT = 56 * 1024 * 1024

FF_TILE = 256
N_FF = D_FF // FF_TILE
DT_PAD = LANES
DT_COLS = 2 * SSM_HEADS
T5_BAND = 91

ROW_TILE = 1024
ATT_TQ = 1024
ATT_TK = 512
TAB_TQ = 512
ATT_COLS = 256
SSD_CHUNK = 128
MXU_TAPS = (0, 1, 5)
SSD_SUB = 8
HALO = BF16_ROWS
VT_ROWS = ATT_DV + BF16_ROWS


def _rmsnorm(x, w):
    ms = jnp.mean(x * x, axis=-1, keepdims=True)
    return x * lax.rsqrt(ms + EPS) * w


def _resident(shape):
    nd = len(shape)
    return pl.BlockSpec(shape, lambda *_: (0,) * nd, pipeline_mode=pl.Buffered(1))


def _ffn_kernel(*refs, has_mix, has_final):
    it = iter(refs)
    h_ref = next(it)
    if has_mix:
        att_ref, ssm_ref, wo_ref = next(it), next(it), next(it)
    nw_ref, wg_ref, wu_ref, wd_ref = next(it), next(it), next(it), next(it)
    fw_ref = next(it) if has_final else None
    o_ref = next(it)

    rows = h_ref.shape[0]
    sub = 2 if rows % 1024 == 0 else 1
    sr = rows // sub

    def prologue(r):
        h = h_ref[r, :]
        if has_mix:
            h = (h + jnp.dot(att_ref[r, :], wo_ref[0], preferred_element_type=F32)
                 + jnp.dot(ssm_ref[r, :], wo_ref[1], preferred_element_type=F32))
        return h, _rmsnorm(h, nw_ref[...]).astype(BF16)

    def epilogue(r, h, acc):
        h = h + 0.5 * acc
        if has_final:
            h = _rmsnorm(h, fw_ref[...])
        o_ref[r, :] = h

    slices = [slice(t * sr, (t + 1) * sr) for t in range(sub)]
    state = {0: prologue(slices[0])}
    for t in range(sub):
        h, u = state.pop(t)
        acc = jnp.zeros_like(h)
        for j in range(N_FF):
            ff = slice(j * FF_TILE, (j + 1) * FF_TILE)
            g = jnp.dot(u, wg_ref[:, ff], preferred_element_type=F32)
            up = jnp.dot(u, wu_ref[:, ff], preferred_element_type=F32)
            a = (g * jax.nn.sigmoid(g) * up).astype(BF16)
            acc = acc + jnp.dot(a, wd_ref[ff, :], preferred_element_type=F32)
            if j == 1 and t + 1 < sub:
                state[t + 1] = prologue(slices[t + 1])
            if j == 1 and t > 0:
                epilogue(*pending)
        pending = (slices[t], h, acc)
    epilogue(*pending)


def _ffn_call(h, norm_w, wg, wu, wd, mix=None, final_w=None):
    n = h.shape[0]
    tm = min(ROW_TILE, n)
    assert n % tm == 0
    row = lambda width: pl.BlockSpec((tm, width), lambda i: (i, 0))
    args, specs = [h], [row(D_MODEL)]
    if mix is not None:
        att, ssm, wo = mix
        args += [att, ssm, wo]
        specs += [row(ATT_V), row(SSM_INNER), _resident(wo.shape)]
    args += [norm_w, wg, wu, wd]
    specs += [_resident(norm_w.shape), _resident(wg.shape), _resident(wu.shape), _resident(wd.shape)]
    if final_w is not None:
        args.append(final_w)
        specs.append(_resident(final_w.shape))
    return pl.pallas_call(
        functools.partial(_ffn_kernel, has_mix=mix is not None, has_final=final_w is not None),
        grid=(n // tm,),
        in_specs=specs,
        out_specs=row(D_MODEL),
        out_shape=jax.ShapeDtypeStruct((n, D_MODEL), F32),
        compiler_params=pltpu.CompilerParams(dimension_semantics=("arbitrary",), vmem_limit_bytes=VMEM_LIMIT),
        name="ffn_mix" if mix is not None else "ffn",
    )(*args)


_IN_SEGS = (("q", 0, ATT_QK), ("k", ATT_QK, ATT_QK), ("z", 2 * ATT_QK + ATT_V, SSM_INNER),
            ("xbc", 2 * ATT_QK + ATT_V + SSM_INNER, SSM_CONV_DIM),
            ("dt", 2 * ATT_QK + ATT_V + SSM_INNER + SSM_CONV_DIM, DT_COLS))


def _inproj_kernel(h_ref, nw_ref, win_ref, wvt_ref, q_ref, k_ref, vt_ref, z_ref, xbc_ref, dt_ref):
    u = _rmsnorm(h_ref[...], nw_ref[...]).astype(BF16)
    outs = dict(q=q_ref, k=k_ref, z=z_ref, xbc=xbc_ref, dt=dt_ref)
    for name, c0, width in _IN_SEGS:
        o_ref = outs[name]
        step = 512 if width % 512 == 0 else (256 if width % 256 == 0 else width)
        for s in range(0, width, step):
            r = jnp.dot(u, win_ref[:, c0 + s:c0 + s + step], preferred_element_type=F32)
            if name == "q":
                r = r * Q_SCALE
            o_ref[:, s:s + step] = r.astype(o_ref.dtype)
    dt_ref[:, DT_COLS:] = jnp.zeros((dt_ref.shape[0], DT_PAD - DT_COLS), dt_ref.dtype)
    nt_dims = (((1,), (1,)), ((), ()))
    ones = jnp.ones((VT_ROWS - ATT_DV, u.shape[0]), vt_ref.dtype)
    for s in range(0, ATT_V, 256):
        r = lax.dot_general(wvt_ref[s:s + 256, :], u, nt_dims, preferred_element_type=F32).astype(vt_ref.dtype)
        for hh in range(256 // ATT_DV):
            head = s // ATT_DV + hh
            vt_ref[0, head * VT_ROWS:head * VT_ROWS + ATT_DV, :] = r[hh * ATT_DV:(hh + 1) * ATT_DV]
            vt_ref[0, head * VT_ROWS + ATT_DV:(head + 1) * VT_ROWS, :] = ones


def _inproj_call(h, norm_w, win, wvt):
    n = h.shape[0]
    tm = min(ATT_TK, n)
    assert n % tm == 0
    row = lambda width: pl.BlockSpec((tm, width), lambda i: (i, 0))
    widths = (ATT_QK, ATT_QK, SSM_INNER, SSM_CONV_DIM, DT_PAD)
    dtypes = (BF16, BF16, BF16, BF16, F32)
    shapes = [jax.ShapeDtypeStruct((n, w), dt) for w, dt in zip(widths, dtypes)]
    specs = [row(w) for w in widths]
    shapes.insert(2, jax.ShapeDtypeStruct((n // tm, N_ATT_HEADS * VT_ROWS, tm), BF16))
    specs.insert(2, pl.BlockSpec((1, N_ATT_HEADS * VT_ROWS, tm), lambda i: (i, 0, 0)))
    return pl.pallas_call(
        _inproj_kernel,
        grid=(n // tm,),
        in_specs=[row(D_MODEL), _resident(norm_w.shape), _resident(win.shape), _resident(wvt.shape)],
        out_specs=specs,
        out_shape=shapes,
        compiler_params=pltpu.CompilerParams(dimension_semantics=("arbitrary",), vmem_limit_bytes=VMEM_LIMIT),
        name="inproj",
    )(h, norm_w, win, wvt)


def _t5_bias(rel, rb_ref, head):
    half = NUM_BUCKETS // 2
    max_exact = half // 2
    ret = jnp.where(rel > 0, half, 0)
    n = jnp.abs(rel)
    nf = jnp.maximum(n, 1).astype(F32)
    large = max_exact + (jnp.log(nf / max_exact) / math.log(MAX_DISTANCE / max_exact)
                         * (half - max_exact)).astype(jnp.int32)
    large = jnp.minimum(large, half - 1)
    bucket = ret + jnp.where(n < max_exact, n, large)
    val = jnp.zeros(rel.shape, F32)
    for jb in range(NUM_BUCKETS):
        val = jnp.where(bucket == jb, rb_ref[jb, head], val)
    return val * LOG2E


def _bias_geometry(tq, tk):
    unit = min(tq, tk)
    assert tq % unit == 0 and tk % unit == 0 and unit >= T5_BAND + 1
    return unit, tk // unit, tq // unit


def _bias_kernel(rb_ref, tab_ref, mtab_ref, rng_ref, *, tq, tk):
    head = pl.program_id(0)
    unit, lo, hi = _bias_geometry(TAB_TQ, tk)
    n_near = lo + hi + 1
    top = rb_ref[0, head]
    bottom = rb_ref[0, head]
    for jb in range(1, NUM_BUCKETS):
        top = jnp.maximum(top, rb_ref[jb, head])
        bottom = jnp.minimum(bottom, rb_ref[jb, head])
    rng_ref[0, 0:1, :] = jnp.full((1, LANES), top * LOG2E, F32)
    rng_ref[0, 1:2, :] = jnp.full((1, LANES), bottom * LOG2E, F32)
    rng_ref[0, 2:8, :] = jnp.zeros((6, LANES), F32)
    far_left = rb_ref[NUM_BUCKETS // 2 - 1, head] * LOG2E
    far_right = rb_ref[NUM_BUCKETS - 1, head] * LOG2E
    krow = lax.broadcasted_iota(jnp.int32, (LANES, LANES), 0)
    qcol = lax.broadcasted_iota(jnp.int32, (LANES, LANES), 1)
    for t in range(n_near):
        for a in range(tk // LANES):
            for b in range(TAB_TQ // LANES):
                base = (a - b) * LANES + (t - lo) * unit
                blk = (slice(a * LANES, (a + 1) * LANES), slice(b * LANES, (b + 1) * LANES))
                if base + LANES - 1 <= -T5_BAND:
                    tab_ref[(0, t) + blk] = jnp.full((LANES, LANES), far_left, F32)
                elif base - LANES + 1 >= T5_BAND:
                    tab_ref[(0, t) + blk] = jnp.full((LANES, LANES), far_right, F32)
                else:
                    tab_ref[(0, t) + blk] = _t5_bias(krow - qcol + base, rb_ref, head)
    tab_ref[0, n_near] = jnp.full((tk, TAB_TQ), far_left, F32)
    tab_ref[0, n_near + 1] = jnp.full((tk, TAB_TQ), far_right, F32)
    mrow = lax.broadcasted_iota(jnp.int32, (N_META, tq), 0)
    mcol = lax.broadcasted_iota(jnp.int32, (N_META, tq), 1)
    mtab_ref[0, 0] = _t5_bias(mrow - N_META - mcol, rb_ref, head)
    mtab_ref[0, 1] = jnp.full((N_META, tq), far_left, F32)


def _bias_call(rel_bias, tq, tk):
    assert tq % TAB_TQ == 0
    _, lo, hi = _bias_geometry(TAB_TQ, tk)
    nt = lo + hi + 3
    return pl.pallas_call(
        functools.partial(_bias_kernel, tq=tq, tk=tk),
        grid=(N_ATT_HEADS,),
        in_specs=[pl.BlockSpec(memory_space=pltpu.SMEM)],
        out_specs=[pl.BlockSpec((1, nt, tk, TAB_TQ), lambda h: (h, 0, 0, 0)),
                   pl.BlockSpec((1, 2, N_META, tq), lambda h: (h, 0, 0, 0)),
                   pl.BlockSpec((1, 8, LANES), lambda h: (h, 0, 0))],
        out_shape=[jax.ShapeDtypeStruct((N_ATT_HEADS, nt, tk, TAB_TQ), F32),
                   jax.ShapeDtypeStruct((N_ATT_HEADS, 2, N_META, tq), F32),
                   jax.ShapeDtypeStruct((N_ATT_HEADS, 8, LANES), F32)],
        compiler_params=pltpu.CompilerParams(dimension_semantics=("arbitrary",)),
        name="t5_bias",
    )(rel_bias)


def _attn_kernel(lam_ref, qall_ref, k_ref, vt_ref, km_ref, vmt_ref, tab_ref, mtab_ref, rng_ref, sw_ref, o_ref,
                 sa_ref, sb_ref, mca_ref, mcb_ref, m_scr, acc_scr, accp_scr, shift_scr, flag_scr,
                 *, tq, tk, nkc, nq, n_steps):
    g = pl.program_id(0)
    qi = jnp.minimum(g, n_steps - 1) % nq
    unit, lo, hi = _bias_geometry(TAB_TQ, tk)
    n_near = lo + hi + 1
    nt_dims = (((1,), (1,)), ((), ()))
    n_col = 2 * tq // ATT_COLS

    def finalize_previous():
        acc = accp_scr[...]
        o = acc[:ATT_DV] / acc[ATT_DV:ATT_DV + 1]
        lv = lam_ref[...]
        lam = (jnp.exp(jnp.sum(lv[0:1] * lv[1:2], axis=1, keepdims=True))
               - jnp.exp(jnp.sum(lv[2:3] * lv[3:4], axis=1, keepdims=True)) + LAM_INIT)
        out = o[:, :tq] - lam * o[:, tq:]
        ms = jnp.mean(out * out, axis=0, keepdims=True)
        out = out * lax.rsqrt(ms + EPS) * sw_ref[...] * (1.0 - LAM_INIT)
        o_ref[...] = out.T.astype(o_ref.dtype)

    @pl.when(g == 0)
    def _():
        accp_scr[...] = jnp.ones(accp_scr.shape, F32)

    @pl.when(g < n_steps)
    def _():
        q = qall_ref[pl.ds(pl.multiple_of(qi * tq, tq), tq), :]
        lane = lax.broadcasted_iota(jnp.int32, (tq, LANES), 1)
        zero = jnp.zeros_like(q)
        q2 = jnp.concatenate([jnp.where(lane < ATT_DH, q, zero), jnp.where(lane >= ATT_DH, q, zero)], axis=0)

        def bias_tile(j, c):
            q0 = (c * ATT_COLS) % tq
            du = j * (tk // unit) - (qi * (tq // TAB_TQ) + q0 // TAB_TQ) * (TAB_TQ // unit)
            idx = jnp.where(du < -lo, n_near, jnp.where(du > hi, n_near + 1, du + lo))
            return tab_ref[0, idx, :, pl.ds(q0 % TAB_TQ, ATT_COLS)]

        def add_bias(s, b):
            return jnp.concatenate([s[:, :tq] + b, s[:, tq:] + b], axis=1)

        @pl.when(qi == 0)
        def _():
            half = jnp.where(lax.broadcasted_iota(jnp.int32, (LANES, LANES), 0) // ATT_DH
                             == lax.broadcasted_iota(jnp.int32, (LANES, LANES), 1), 1.0, 0.0).astype(BF16)

            def max_sq_norm(x_ref):
                x = x_ref[...]
                sq = jnp.dot(x * x, half, preferred_element_type=F32)
                return jnp.max(sq, axis=0, keepdims=True)

            bound2 = max_sq_norm(qall_ref) * jnp.maximum(max_sq_norm(k_ref), max_sq_norm(km_ref))
            lane_row = lax.broadcasted_iota(jnp.int32, (1, LANES), 1)
            qk_bound = [NORM_SLACK * jnp.sqrt(jnp.max(jnp.where(lane_row == mp, bound2, 0.0), axis=1, keepdims=True))
                        for mp in range(2)]
            hi_b = rng_ref[0, 0:1, 0:1]
            lo_b = rng_ref[0, 1:2, 0:1]
            col = lax.broadcasted_iota(jnp.int32, (1, 2 * tq), 1)
            shift_scr[...] = jnp.where(col < tq, qk_bound[0], qk_bound[1]) + hi_b
            worst_gap = 2.0 * jnp.maximum(qk_bound[0], qk_bound[1]) + (hi_b - lo_b)
            flag_scr[0] = (worst_gap[0, 0] <= MAX_SHIFT_GAP).astype(jnp.int32)

        shift = shift_scr[...]
        bounded = flag_scr[0] == 1

        @pl.when(bounded)
        def _():
            finalize_previous()
            sm = lax.dot_general(km_ref[...], q2, nt_dims, preferred_element_type=F32)
            sm = add_bias(sm, mtab_ref[0, jnp.minimum(qi, 1)])
            acc_scr[...] = jnp.dot(vmt_ref[...], jnp.exp2(sm - shift).astype(BF16), preferred_element_type=F32)

            def stage_logits(u, j, c):
                buf = (sa_ref, sb_ref)[u % 2]
                buf[:, pl.ds(c * ATT_COLS, ATT_COLS)] = lax.dot_general(
                    k_ref[pl.ds(j * tk, tk), :], q2[c * ATT_COLS:(c + 1) * ATT_COLS], nt_dims,
                    preferred_element_type=F32)

            units = [(j, c) for j in range(nkc) for c in range(n_col)]
            stage_logits(0, *units[0])
            for u, (j, c) in enumerate(units):
                if u + 1 < len(units):
                    stage_logits(u + 1, *units[u + 1])
                cols = pl.ds(c * ATT_COLS, ATT_COLS)
                s = (sa_ref, sb_ref)[u % 2][:, cols] + bias_tile(j, c)
                p = jnp.exp2(s - shift[:, c * ATT_COLS:(c + 1) * ATT_COLS]).astype(BF16)
                acc_scr[:, cols] += jnp.dot(vt_ref[j], p, preferred_element_type=F32)
            accp_scr[...] = acc_scr[...]

        @pl.when(jnp.logical_not(bounded))
        def _():
            finalize_previous()

            def produce(j, c, s_ref, mc_ref):
                cols = pl.ds(c * ATT_COLS, ATT_COLS)
                s = lax.dot_general(k_ref[pl.ds(j * tk, tk), :], q2[c * ATT_COLS:(c + 1) * ATT_COLS], nt_dims,
                                    preferred_element_type=F32)
                s = s + bias_tile(j, c)
                s_ref[:, cols] = s
                mc_ref[:, cols] = jnp.max(s, axis=0, keepdims=True)

            def consume(s, m_cur, vt, cols, first=False):
                if first:
                    m_new = m_cur
                else:
                    m_prev = m_scr[:, cols]
                    m_new = jnp.maximum(m_prev, m_cur)
                    alpha = jnp.exp2(m_prev - m_new)
                p = jnp.exp2(s - m_new).astype(BF16)
                pv = jnp.dot(vt, p, preferred_element_type=F32)
                acc_scr[:, cols] = pv if first else alpha * acc_scr[:, cols] + pv
                m_scr[:, cols] = m_new

            sm = lax.dot_general(km_ref[...], q2, nt_dims, preferred_element_type=F32)
            sm = add_bias(sm, mtab_ref[0, jnp.minimum(qi, 1)])
            consume(sm, jnp.max(sm, axis=0, keepdims=True), vmt_ref[...], pl.ds(0, 2 * tq), first=True)

            bufs = ((sa_ref, mca_ref), (sb_ref, mcb_ref))
            for c in range(n_col):
                produce(0, c, *bufs[0])
            for j in range(nkc):
                s_ref, mc_ref = bufs[j % 2]
                for c in range(n_col):
                    cols = pl.ds(c * ATT_COLS, ATT_COLS)
                    if j + 1 < nkc:
                        produce(j + 1, c, *bufs[(j + 1) % 2])
                    consume(s_ref[:, cols], mc_ref[:, cols], vt_ref[j], cols)
            accp_scr[...] = acc_scr[...]

    @pl.when(g == n_steps)
    def _():
        finalize_previous()


def _attn_call(lamv, q, k, vt, km, vmt, tab, mtab, rng, subw_col, batch, seq, tq, tk):
    n = q.shape[0]
    assert n == batch * seq and seq % (2 * tk) == 0 and seq % tq == 0 and vt.shape[2] == tk
    nq = seq // tq
    nkc = seq // tk
    nt = tab.shape[1]
    n_steps = N_ATT_HEADS * batch * nq

    def tile(g):
        g = jnp.minimum(g, n_steps - 1)
        return g // (batch * nq), (g // nq) % batch, g % nq

    def cur(f):
        return lambda g: f(*tile(g))

    def prev(f):
        return lambda g: f(*tile(jnp.maximum(g - 1, 0)))

    return pl.pallas_call(
        functools.partial(_attn_kernel, tq=tq, tk=tk, nkc=nkc, nq=nq, n_steps=n_steps),
        grid=(n_steps + 1,),
        in_specs=[
            pl.BlockSpec(lamv.shape, lambda g: (0, 0)),
            pl.BlockSpec((seq, LANES), cur(lambda h, b, i: (b, h))),
            pl.BlockSpec((seq, LANES), cur(lambda h, b, i: (b, h))),
            pl.BlockSpec((nkc, VT_ROWS, tk), cur(lambda h, b, i: (b, h, 0))),
            pl.BlockSpec((N_META, LANES), cur(lambda h, b, i: (0, h))),
            pl.BlockSpec((VT_ROWS, N_META), cur(lambda h, b, i: (h, 0))),
            pl.BlockSpec((1, nt, tk, TAB_TQ), cur(lambda h, b, i: (h, 0, 0, 0))),
            pl.BlockSpec((1, 2, N_META, tq), cur(lambda h, b, i: (h, 0, 0, 0))),
            pl.BlockSpec((1, 8, LANES), cur(lambda h, b, i: (h, 0, 0))),
            pl.BlockSpec(subw_col.shape, lambda g: (0, 0)),
        ],
        out_specs=pl.BlockSpec((tq, LANES), prev(lambda h, b, i: (b * nq + i, h))),
        out_shape=jax.ShapeDtypeStruct((n, ATT_V), BF16),
        scratch_shapes=[pltpu.VMEM((tk, 2 * tq), F32)] * 2 + [pltpu.VMEM((1, 2 * tq), F32)] * 3
        + [pltpu.VMEM((VT_ROWS, 2 * tq), F32)] * 2 + [pltpu.VMEM((1, 2 * tq), F32), pltpu.SMEM((1,), jnp.int32)],
        compiler_params=pltpu.CompilerParams(dimension_semantics=("arbitrary",), vmem_limit_bytes=VMEM_LIMIT),
        name="diff_attn",
    )(lamv, q, k, vt, km, vmt, tab, mtab, rng, subw_col)


def _split3(x):
    hi = x.astype(BF16)
    r1 = x - hi.astype(F32)
    mid = r1.astype(BF16)
    lo = (r1 - mid.astype(F32)).astype(BF16)
    return hi, mid, lo


def _cumsum_rows(a):
    rows = a.shape[0]
    r_i = lax.broadcasted_iota(jnp.int32, (rows, rows), 0)
    c_i = lax.broadcasted_iota(jnp.int32, (rows, rows), 1)
    tri = jnp.where(c_i <= r_i, 1.0, 0.0).astype(BF16)
    out = None
    for term in _split3(a):
        part = jnp.dot(tri, term, preferred_element_type=F32)
        out = part if out is None else out + part
    return out


def _expand_rows(parts, sel_ref):
    masked = []
    for w, first in parts:
        lane = lax.broadcasted_iota(jnp.int32, w.shape, 1)
        masked.append(jnp.where((lane >= first) & (lane < first + SSM_HEADS), w, 0.0))
    stacked = jnp.concatenate(masked, axis=0)
    hi = stacked.astype(BF16)
    lo = (stacked - hi.astype(F32)).astype(BF16)
    sel = sel_ref[...]
    full = jnp.dot(hi, sel, preferred_element_type=F32) + jnp.dot(lo, sel, preferred_element_type=F32)
    outs, r0 = [], 0
    for w, _ in parts:
        outs.append(full[r0:r0 + w.shape[0]])
        r0 += w.shape[0]
    return outs


def _softplus(x):
    return jnp.maximum(x, 0.0) + jnp.log(1.0 + jnp.exp(-jnp.abs(x)))


GROUP_COLS = SSM_INNER // SSM_GROUPS


def _state_update(b_t, xw):
    return jnp.concatenate(
        [jnp.dot(b_t[g * SSM_STATE:(g + 1) * SSM_STATE], xw[:, g * GROUP_COLS:(g + 1) * GROUP_COLS],
                 preferred_element_type=F32) for g in range(SSM_GROUPS)], axis=0)


def _stack_decay(dec_row):
    return jnp.concatenate(
        [jnp.broadcast_to(dec_row[:, g * GROUP_COLS:(g + 1) * GROUP_COLS], (SSM_STATE, GROUP_COLS))
         for g in range(SSM_GROUPS)], axis=0)


def _conv_silu(win, shift_ref, cw_ref, cb_ref, rows):
    total = rows + 2 * HALO
    assert win.shape[0] == total and shift_ref.shape == (len(MXU_TAPS) * rows, total)
    shifted = jnp.dot(shift_ref[...], win, preferred_element_type=F32)
    win32 = win.astype(F32)
    acc = jnp.broadcast_to(cb_ref[...], (rows, SSM_CONV_DIM))
    for j in range(SSM_CONV):
        off = j - SSM_CONV // 2
        if j in MXU_TAPS:
            tap = shifted[MXU_TAPS.index(j) * rows:(MXU_TAPS.index(j) + 1) * rows]
        elif off == 0:
            tap = win32[HALO:HALO + rows]
        else:
            tap = pltpu.roll(win32, (total - off) % total, axis=0)[HALO:HALO + rows]
        acc = acc + cw_ref[j:j + 1, :] * tap
    return acc * jax.nn.sigmoid(acc)


def _shift_matrix(rows):
    offs = np.array([j - SSM_CONV // 2 for j in MXU_TAPS])
    src = HALO + np.arange(rows)[None, :] + offs[:, None]
    return jnp.asarray(src.reshape(-1)[:, None] == np.arange(rows + 2 * HALO)[None, :], BF16)


def _ssd_kernel(z_ref, xc_ref, xl_ref, xr_ref, dt_ref, mx_ref, mdt_ref, cw_ref, cb_ref, dtb_ref, alog_ref,
                dsk_ref, nw_ref, sel_ref, shc_ref, shm_ref, o_ref, xs_scr, dts_scr, cum_scr, hbs_scr, hf_scr, hb_scr,
                win_scr,
                *, cs, sub, nb):
    rows = cs * sub
    ph = pl.program_id(1)
    t = pl.program_id(2)
    fwd0, bwd0 = 0, SSM_HEADS
    a_row = -jnp.exp(alog_ref[...])

    def decay_terms(dt_raw):
        dt = _softplus(dt_raw + dtb_ref[...])
        return dt, _cumsum_rows(dt * a_row)

    def bcast8(row):
        return jnp.broadcast_to(row, (8, LANES))

    @pl.when(ph == 0)
    def _():
        blk = nb - 1 - t

        @pl.when(t == 0)
        def _():
            hb_scr[...] = jnp.zeros(hb_scr.shape, F32)

        left = jnp.where(blk == 0, mx_ref[...], xl_ref[...])
        right = jnp.where(blk == nb - 1, jnp.zeros_like(xr_ref[...]), xr_ref[...])
        win_scr[0:HALO, :] = left
        win_scr[HALO:HALO + rows, :] = xc_ref[...]
        win_scr[HALO + rows:HALO + rows + HALO, :] = right

        hb = hb_scr[...]
        for si in reversed(range(sub)):
            cc = blk * sub + si
            xbc = _conv_silu(win_scr[si * cs:si * cs + cs + 2 * HALO, :], shc_ref, cw_ref, cb_ref, cs)
            xs_scr[cc] = xbc.astype(BF16)
            dt, cum = decay_terms(dt_ref[si * cs:(si + 1) * cs, :])
            dts_scr[cc] = dt
            cum_scr[cc] = cum
            eb = cum - dt * a_row
            w_b, dec = _expand_rows([(jnp.exp(eb) * dt, bwd0), (bcast8(jnp.exp(cum[cs - 1:cs, :])), bwd0)],
                                    sel_ref)
            xw = (xbc[:, :SSM_INNER] * w_b).astype(BF16)
            bm_t = xbc[:, SSM_INNER:SSM_INNER + LANES].T.astype(BF16)
            hbs_scr[cc] = hb.astype(BF16)
            hb = hb * _stack_decay(dec[0:1]) + _state_update(bm_t, xw)
        hb_scr[...] = hb

    @pl.when(ph == 1)
    def _():
        @pl.when(t == 0)
        def _():
            wm = jnp.concatenate([jnp.zeros((HALO, SSM_CONV_DIM), BF16), mx_ref[...], xc_ref[0:HALO, :]], axis=0)
            xm = _conv_silu(wm, shm_ref, cw_ref, cb_ref, N_META)
            dtm, cumm = decay_terms(mdt_ref[...])
            (w_m,) = _expand_rows([(jnp.exp(cumm[N_META - 1:N_META, :] - cumm) * dtm, fwd0)], sel_ref)
            xwm = (xm[:, :SSM_INNER] * w_m).astype(BF16)
            bmm_t = xm[:, SSM_INNER:SSM_INNER + LANES].T.astype(BF16)
            hf_scr[...] = _state_update(bmm_t, xwm)

        lane = lax.broadcasted_iota(jnp.int32, (cs, LANES), 1)
        l_i = lax.broadcasted_iota(jnp.int32, (cs, cs), 0)
        s_i = lax.broadcasted_iota(jnp.int32, (cs, cs), 1)
        lower = s_i <= l_i
        diag = s_i == l_i
        hpg = SSM_HEADS // SSM_GROUPS
        zx = jnp.zeros((cs, LANES), BF16)
        nt_dims = (((1,), (1,)), ((), ()))

        hf = hf_scr[...]
        for si in range(sub):
            cc = t * sub + si
            xbc = xs_scr[cc]
            x_bf = xbc[:, :SSM_INNER]
            bm = xbc[:, SSM_INNER:SSM_INNER + LANES]
            cm = xbc[:, SSM_INNER + LANES:SSM_INNER + 2 * LANES]
            x = x_bf.astype(F32)

            dt = dts_scr[cc]
            cum = cum_scr[cc]
            eb = cum - dt * a_row
            dt_t, cum_t, eb_t = dt.T, cum.T, eb.T
            last = cum[cs - 1:cs, :]

            c_grp = [jnp.where(lane // SSM_STATE == g, cm, jnp.zeros_like(cm)) for g in range(SSM_GROUPS)]
            g_mats = [lax.dot_general(c_g, bm, nt_dims, preferred_element_type=F32) for c_g in c_grp]

            pieces = []
            for hp in range(SSM_HEADS // 2):
                w_pair = []
                for h in (2 * hp, 2 * hp + 1):
                    arg_f = cum[:, fwd0 + h:fwd0 + h + 1] - cum_t[fwd0 + h:fwd0 + h + 1, :]
                    arg_b = eb_t[bwd0 + h:bwd0 + h + 1, :] - eb[:, bwd0 + h:bwd0 + h + 1]
                    e = jnp.exp(jnp.minimum(jnp.where(lower, arg_f, arg_b), 0.0))
                    dt_f_row = dt_t[fwd0 + h:fwd0 + h + 1, :]
                    dt_b_row = dt_t[bwd0 + h:bwd0 + h + 1, :]
                    m = e * jnp.where(lower, dt_f_row, dt_b_row) + jnp.where(diag, dt_b_row, 0.0)
                    w_pair.append((g_mats[h // hpg] * m).astype(BF16))
                xp = x_bf[:, hp * LANES:(hp + 1) * LANES]
                rhs = jnp.concatenate([jnp.where(lane < SSM_HEADDIM, xp, zx),
                                       jnp.where(lane >= SSM_HEADDIM, xp, zx)], axis=0)
                pieces.append(jnp.dot(jnp.concatenate(w_pair, axis=1), rhs, preferred_element_type=F32))
            y = jnp.concatenate(pieces, axis=1)

            d_f, d_b, w_f, dec = _expand_rows(
                [(jnp.exp(cum), fwd0), (jnp.exp(last - eb), bwd0), (jnp.exp(last - cum) * dt, fwd0),
                 (bcast8(jnp.exp(last)), fwd0)], sel_ref)
            hf_bf = hf.astype(BF16)
            hb_bf = hbs_scr[cc]
            y = y + d_f * jnp.concatenate([jnp.dot(c_g, hf_bf, preferred_element_type=F32) for c_g in c_grp],
                                          axis=1)
            y = y + d_b * jnp.concatenate([jnp.dot(c_g, hb_bf, preferred_element_type=F32) for c_g in c_grp],
                                          axis=1)
            y = y + x * dsk_ref[...]

            xw = (x * w_f).astype(BF16)
            hf = hf * _stack_decay(dec[0:1]) + _state_update(bm.astype(F32).T.astype(BF16), xw)

            zf = z_ref[si * cs:(si + 1) * cs, :].astype(F32)
            y = y * (zf * jax.nn.sigmoid(zf))
            o_ref[si * cs:(si + 1) * cs, :] = _rmsnorm(y, nw_ref[...]).astype(o_ref.dtype)
        hf_scr[...] = hf


def _ssd_call(z, xbc, dt, mxbc, mdt, cw, cb, dtb, alog, dskip, nw, sel, shc, shm, batch, seq, cs, sub):
    n = z.shape[0]
    rows = cs * sub
    assert n == batch * seq and seq % rows == 0 and cs % HALO == 0
    nc = seq // cs
    nb = seq // rows
    hpb = rows // HALO
    n_halo = n // HALO

    def ph0_block(ph, t):
        return (1 - ph) * (nb - 1 - t)

    const2 = lambda shape: pl.BlockSpec(shape, lambda b, ph, t: (0, 0))
    return pl.pallas_call(
        functools.partial(_ssd_kernel, cs=cs, sub=sub, nb=nb),
        grid=(batch, 2, nb),
        in_specs=[
            pl.BlockSpec((rows, SSM_INNER), lambda b, ph, t: (b * nb + ph * t, 0)),
            pl.BlockSpec((rows, SSM_CONV_DIM), lambda b, ph, t: (b * nb + ph0_block(ph, t), 0)),
            pl.BlockSpec((HALO, SSM_CONV_DIM),
                         lambda b, ph, t: (jnp.maximum((b * nb + ph0_block(ph, t)) * hpb - 1, 0), 0)),
            pl.BlockSpec((HALO, SSM_CONV_DIM),
                         lambda b, ph, t: (jnp.minimum((b * nb + ph0_block(ph, t) + 1) * hpb, n_halo - 1), 0)),
            pl.BlockSpec((rows, DT_PAD), lambda b, ph, t: (b * nb + ph0_block(ph, t), 0)),
            const2(mxbc.shape), const2(mdt.shape), const2(cw.shape), const2(cb.shape), const2(dtb.shape),
            const2(alog.shape), const2(dskip.shape), const2(nw.shape), const2(sel.shape), const2(shc.shape),
            const2(shm.shape),
        ],
        out_specs=pl.BlockSpec((rows, SSM_INNER), lambda b, ph, t: (b * nb + ph * t, 0)),
        out_shape=jax.ShapeDtypeStruct((n, SSM_INNER), BF16),
        scratch_shapes=[
            pltpu.VMEM((nc, cs, SSM_CONV_DIM), BF16),
            pltpu.VMEM((nc, cs, DT_PAD), F32),
            pltpu.VMEM((nc, cs, DT_PAD), F32),
            pltpu.VMEM((nc, LANES, GROUP_COLS), BF16),
            pltpu.VMEM((LANES, GROUP_COLS), F32),
            pltpu.VMEM((LANES, GROUP_COLS), F32),
            pltpu.VMEM((rows + 2 * HALO, SSM_CONV_DIM), BF16),
        ],
        compiler_params=pltpu.CompilerParams(dimension_semantics=("arbitrary",) * 3, vmem_limit_bytes=VMEM_LIMIT),
        name="bi_ssd",
    )(z, xbc, xbc, xbc, dt, mxbc, mdt, cw, cb, dtb, alog, dskip, nw, sel, shc, shm)


def _head_selector():
    k = np.arange(LANES)[:, None]
    col = np.arange(SSM_INNER)[None, :]
    return jnp.asarray((k % SSM_HEADS == col // SSM_HEADDIM) & (k < 2 * SSM_HEADS), BF16)


def _prep_weights(ffn1_norm_w, ffn1_w_gate, ffn1_w_up, ffn1_w_down, mix_norm_w, w_in, lambda_q1, lambda_k1,
                  lambda_q2, lambda_k2, attn_subln_w, conv_w, conv_b, dt_bias_fwd, dt_bias_bwd, a_log_fwd,
                  a_log_bwd, ssm_d, ssm_norm_w, w_out, ffn2_norm_w, ffn2_w_gate, ffn2_w_up, ffn2_w_down,
                  final_norm_w):
    def ffn(norm_w, wg, wu, wd):
        return norm_w[0][None, :], wg[0].astype(BF16), wu[0].astype(BF16), wd[0].astype(BF16)

    pad_lanes = lambda v, width: jnp.pad(v, (0, width - v.shape[0]))[None, :]
    o_v, o_z = 2 * ATT_QK, 2 * ATT_QK + ATT_V
    wi = w_in[0].astype(BF16)
    return dict(
        ffn1=ffn(ffn1_norm_w, ffn1_w_gate, ffn1_w_up, ffn1_w_down),
        ffn2=ffn(ffn2_norm_w, ffn2_w_gate, ffn2_w_up, ffn2_w_down),
        mix_norm=mix_norm_w[0][None, :],
        win=wi,
        wvt=wi[:, o_v:o_z].T,
        lamv=jnp.stack([lambda_q1[0], lambda_k1[0], lambda_q2[0], lambda_k2[0]]),
        subw_col=attn_subln_w[0][:, None],
        cw=jnp.pad(conv_w[0], ((0, 8 - SSM_CONV), (0, 0))),
        cb=conv_b[0][None, :],
        dtb=pad_lanes(jnp.concatenate([dt_bias_fwd[0], dt_bias_bwd[0]]), DT_PAD),
        alog=pad_lanes(jnp.concatenate([a_log_fwd[0], a_log_bwd[0]]), DT_PAD),
        dskip=jnp.repeat(ssm_d[0], SSM_HEADDIM)[None, :],
        ssm_norm=ssm_norm_w[0][None, :],
        wo=w_out[0].astype(BF16).reshape(2, ATT_V, D_MODEL),
        final=final_norm_w[None, :],
        sel=_head_selector(),
        shc=_shift_matrix(SSD_CHUNK),
        shm=_shift_matrix(N_META),
    )


def _q_tile(seq):
    return seq if seq <= 2 * ATT_TQ else ATT_TQ


def _encode(x, w, meta_proj, bias_tabs):
    batch, seq, _ = x.shape
    km, vmt, mxbc, mdt = meta_proj
    h0 = x.reshape(batch * seq, D_MODEL)
    h1 = _ffn_call(h0, *w["ffn1"])
    q, k, vt, z, xbc, dt = _inproj_call(h1, w["mix_norm"], w["win"], w["wvt"])
    att = _attn_call(w["lamv"], q, k, vt, km, vmt, *bias_tabs, w["subw_col"], batch, seq, _q_tile(seq), ATT_TK)
    ssm = _ssd_call(z, xbc, dt, mxbc, mdt, w["cw"], w["cb"], w["dtb"], w["alog"], w["dskip"], w["ssm_norm"],
                    w["sel"], w["shc"], w["shm"], batch, seq, SSD_CHUNK, SSD_SUB)
    y = _ffn_call(h1, *w["ffn2"], mix=(att, ssm, w["wo"]), final_w=w["final"])
    return y.reshape(batch, seq, D_MODEL)


def kernel(x_prompt, x_sample, meta_tokens, ffn1_norm_w, ffn1_w_gate, ffn1_w_up, ffn1_w_down, mix_norm_w, w_in, rel_bias, lambda_q1, lambda_k1, lambda_q2, lambda_k2, attn_subln_w, conv_w, conv_b, dt_bias_fwd, dt_bias_bwd, a_log_fwd, a_log_bwd, ssm_d, ssm_norm_w, w_out, ffn2_norm_w, ffn2_w_gate, ffn2_w_up, ffn2_w_down, final_norm_w):
    w = _prep_weights(ffn1_norm_w, ffn1_w_gate, ffn1_w_up, ffn1_w_down, mix_norm_w, w_in, lambda_q1, lambda_k1,
                      lambda_q2, lambda_k2, attn_subln_w, conv_w, conv_b, dt_bias_fwd, dt_bias_bwd, a_log_fwd,
                      a_log_bwd, ssm_d, ssm_norm_w, w_out, ffn2_norm_w, ffn2_w_gate, ffn2_w_up, ffn2_w_down,
                      final_norm_w)
    hm = _ffn_call(meta_tokens, *w["ffn1"])
    _, km, vmt, _, mxbc, mdt = _inproj_call(hm, w["mix_norm"], w["win"], w["wvt"])
    meta_proj = (km, vmt[0], mxbc, mdt)
    bias_tabs = _bias_call(rel_bias, max(_q_tile(x_prompt.shape[1]), _q_tile(x_sample.shape[1])), ATT_TK)
    return (_encode(x_prompt, w, meta_proj, bias_tabs), _encode(x_sample, w, meta_proj, bias_tabs))
```
